```python
import jax, jax.numpy as jnp
from jax import lax
import numpy as np


D_MODEL = 2048
BATCH = 2
SEQ = 8192
DEPTH = 4

CTX_LEN = 256
GRID_W = 64
N_MIXERS = 3
MIXER_NA = 0
MIXER_GMLP = 1
MIXER_CONV = 2
NA_HEADS = 16
HEAD_DIM = D_MODEL // NA_HEADS
WIN_H = 8
WIN_W = 16
QROWS = 2
GM_WIDTH = 2 * D_MODEL
GM_GROUPS = 16
GM_GROUP_DIM = GM_WIDTH // GM_GROUPS
GM_CHUNK = 128
FFN_DIM = 5504
N_MOD = 6
EPS = 1e-6

kernel_name = 'hybrid_na_gmlp_shortconv_dit_block'


def rms_norm(x, g):
    x32 = x.astype(jnp.float32)
    y = x32 * lax.rsqrt(jnp.mean(x32 * x32, axis=-1, keepdims=True) + EPS)
    return (y * g.astype(jnp.float32)).astype(x.dtype)


def adaln(cond, w, b):
    return jnp.split(jax.nn.silu(cond) @ w + b, N_MOD, axis=-1)


def modulate(x, shift, scale):
    return x * (1.0 + scale) + shift


def dwconv3(z, w, b=None):
    zp = jnp.pad(z, ((0, 0), (1, 1), (0, 0)))
    y = zp[:, :-2] * w[0] + zp[:, 1:-1] * w[1] + zp[:, 2:] * w[2]
    return y if b is None else y + b


def neighbourhood_attention(h, hc, w_qkv, q_g, k_g, rpb, w_o, ctx_out):
    bn, n, d = h.shape
    rows = n // GRID_W
    kh = min(WIN_H, rows)
    nb = min(kh + QROWS - 1, rows)
    n_blk = rows // QROWS
    qw = QROWS * GRID_W
    nk = nb * GRID_W
    scale = HEAD_DIM ** -0.5

    def project(t):
        q, k, v = jnp.split(t @ w_qkv, 3, axis=-1)
        sh = (t.shape[0], t.shape[1], NA_HEADS, HEAD_DIM)
        return rms_norm(q.reshape(sh), q_g), rms_norm(k.reshape(sh), k_g), v.reshape(sh)

    q, k, v = project(h)
    qc, kc, vc = project(hc)
    kg = k.reshape(bn, rows, GRID_W, NA_HEADS, HEAD_DIM)
    vg = v.reshape(bn, rows, GRID_W, NA_HEADS, HEAD_DIM)

    q_row_off = jnp.repeat(jnp.arange(QROWS), GRID_W)
    q_col = jnp.tile(jnp.arange(GRID_W), QROWS)
    k_row_off = jnp.repeat(jnp.arange(nb), GRID_W)
    k_col = jnp.tile(jnp.arange(GRID_W), nb)
    c_start = jnp.clip(q_col - WIN_W // 2, 0, GRID_W - WIN_W)
    col_ok = (k_col[None, :] >= c_start[:, None]) & (k_col[None, :] < c_start[:, None] + WIN_W)
    dc_idx = jnp.clip(k_col[None, :] - q_col[:, None], -(WIN_W - 1), WIN_W - 1) + WIN_W - 1

    def block(args):
        blk, qb = args
        r0 = blk * QROWS
        q_row = r0 + q_row_off
        r_start = jnp.clip(q_row - kh // 2, 0, rows - kh)
        band0 = jnp.minimum(jnp.clip(r0 - kh // 2, 0, rows - kh), rows - nb)
        k_row = band0 + k_row_off
        kb = lax.dynamic_slice_in_dim(kg, band0, nb, axis=1).reshape(bn, nk, NA_HEADS, HEAD_DIM)
        vb = lax.dynamic_slice_in_dim(vg, band0, nb, axis=1).reshape(bn, nk, NA_HEADS, HEAD_DIM)
        ok = col_ok & (k_row[None, :] >= r_start[:, None]) & (k_row[None, :] < r_start[:, None] + kh)
        dr_idx = jnp.clip(k_row[None, :] - q_row[:, None], -(WIN_H - 1), WIN_H - 1) + WIN_H - 1
        bias = rpb[:, dr_idx, dc_idx].astype(jnp.float32)
        s_lat = jnp.einsum('bqhd,bkhd->bhqk', qb, kb).astype(jnp.float32) * scale + bias
        s_lat = jnp.where(ok, s_lat, -jnp.inf)
        s_ctx = jnp.einsum('bqhd,bchd->bhqc', qb, kc).astype(jnp.float32) * scale
        p = jax.nn.softmax(jnp.concatenate([s_lat, s_ctx], axis=-1), axis=-1).astype(vb.dtype)
        return (jnp.einsum('bhqk,bkhd->bqhd', p[..., :nk], vb)
                + jnp.einsum('bhqc,bchd->bqhd', p[..., nk:], vc))

    q_blocks = q.reshape(bn, n_blk, qw, NA_HEADS, HEAD_DIM).transpose(1, 0, 2, 3, 4)
    o = lax.map(block, (jnp.arange(n_blk), q_blocks))
    y = o.transpose(1, 0, 2, 3, 4).reshape(bn, n, d) @ w_o
    yc = None
    if ctx_out:
        s = jnp.einsum('bqhd,bkhd->bhqk', qc, kc).astype(jnp.float32) * scale
        p = jax.nn.softmax(s, axis=-1).astype(vc.dtype)
        yc = jnp.einsum('bhqk,bkhd->bqhd', p, vc).reshape(bn, hc.shape[1], d) @ w_o
    return y, yc


def chunk_gmlp(h, w_in, v_g, w_s, b_s, w_out):
    bn, n, _ = h.shape
    u, v = jnp.split(jax.nn.gelu(h @ w_in), 2, axis=-1)
    v = rms_norm(v, v_g)
    vch = v.reshape(bn, n // GM_CHUNK, GM_CHUNK, GM_GROUPS, GM_GROUP_DIM)
    sv = jnp.einsum('gij,bnjgd->bnigd', w_s, vch) + b_s.T[None, None, :, :, None]
    return (u * sv.reshape(bn, n, GM_WIDTH)) @ w_out


def short_conv(h, w_in, conv_w, w_out):
    b_gate, c_gate, xv = jnp.split(h @ w_in, 3, axis=-1)
    return (b_gate * dwconv3(c_gate * xv, conv_w)) @ w_out


def conv_ffn(h, w_up, conv_w, conv_b, w_down):
    gate, up = jnp.split(dwconv3(h @ w_up, conv_w, conv_b), 2, axis=-1)
    return (jax.nn.silu(gate) * up) @ w_down


def setup_inputs(seed: int = 0) -> dict:
    key = jax.random.key(seed)
    keys = iter(jax.random.split(key, 40))

    def nrm(shape, s):
        return jax.random.normal(next(keys), shape, jnp.float32) * s

    d = D_MODEL
    n_na = len(range(MIXER_NA, DEPTH, N_MIXERS))
    n_gm = len(range(MIXER_GMLP, DEPTH, N_MIXERS))
    n_sc = len(range(MIXER_CONV, DEPTH, N_MIXERS))
    return {
        'x': nrm((BATCH, SEQ, d), 1.0),
        'c': nrm((BATCH, d), 1.0),
        'ctx': nrm((BATCH, CTX_LEN, d), 1.0),
        'c_ctx': nrm((d,), 1.0),
        'norm_mix_g': 1.0 + nrm((DEPTH, d), 0.1),
        'norm_ffn_g': 1.0 + nrm((DEPTH, d), 0.1),
        'w_ada': nrm((DEPTH, d, N_MOD * d), 0.5 * d ** -0.5),
        'b_ada': nrm((DEPTH, N_MOD * d), 0.02),
        'na_w_qkv': nrm((n_na, d, 3 * d), d ** -0.5),
        'na_q_g': 1.0 + nrm((n_na, HEAD_DIM), 0.1),
        'na_k_g': 1.0 + nrm((n_na, HEAD_DIM), 0.1),
        'na_rpb': nrm((n_na, NA_HEADS, 2 * WIN_H - 1, 2 * WIN_W - 1), 0.2),
        'na_w_o': nrm((n_na, d, d), d ** -0.5),
        'gm_w_in': nrm((n_gm, d, 2 * GM_WIDTH), d ** -0.5),
        'gm_v_g': 1.0 + nrm((n_gm, GM_WIDTH), 0.1),
        'gm_w_s': nrm((n_gm, GM_GROUPS, GM_CHUNK, GM_CHUNK), GM_CHUNK ** -0.5),
        'gm_b_s': 1.0 + nrm((n_gm, GM_GROUPS, GM_CHUNK), 0.1),
        'gm_w_out': nrm((n_gm, GM_WIDTH, d), GM_WIDTH ** -0.5),
        'sc_w_in': nrm((n_sc, d, 3 * d), d ** -0.5),
        'sc_conv_w': nrm((n_sc, 3, d), 3 ** -0.5),
        'sc_w_out': nrm((n_sc, d, d), d ** -0.5),
        'ffn_w_up': nrm((DEPTH, d, 2 * FFN_DIM), d ** -0.5),
        'ffn_conv_w': nrm((DEPTH, 3, 2 * FFN_DIM), 3 ** -0.5),
        'ffn_conv_b': nrm((DEPTH, 2 * FFN_DIM), 0.02),
        'ffn_w_down': nrm((DEPTH, FFN_DIM, d), FFN_DIM ** -0.5),
    }


def reference(x, c, ctx, c_ctx, norm_mix_g, norm_ffn_g, w_ada, b_ada,
              na_w_qkv, na_q_g, na_k_g, na_rpb, na_w_o,
              gm_w_in, gm_v_g, gm_w_s, gm_b_s, gm_w_out,
              sc_w_in, sc_conv_w, sc_w_out,
              ffn_w_up, ffn_conv_w, ffn_conv_b, ffn_w_down):
    cond_lat = c[:, None, :]
    cond_ctx = c_ctx[None, None, :]
    for i in range(DEPTH):
        last = i == DEPTH - 1
        m, j = i % N_MIXERS, i // N_MIXERS
        sh1, sc1, g1, sh2, sc2, g2 = adaln(cond_lat, w_ada[i], b_ada[i])
        h = modulate(rms_norm(x, norm_mix_g[i]), sh1, sc1)
        hc = None
        if (not last) or m == MIXER_NA:
            csh1, csc1, cg1, csh2, csc2, cg2 = adaln(cond_ctx, w_ada[i], b_ada[i])
            hc = modulate(rms_norm(ctx, norm_mix_g[i]), csh1, csc1)
        if m == MIXER_NA:
            y, yc = neighbourhood_attention(h, hc, na_w_qkv[j], na_q_g[j], na_k_g[j],
                                            na_rpb[j], na_w_o[j], not last)
        elif m == MIXER_GMLP:
            y = chunk_gmlp(h, gm_w_in[j], gm_v_g[j], gm_w_s[j], gm_b_s[j], gm_w_out[j])
            yc = None if last else chunk_gmlp(hc, gm_w_in[j], gm_v_g[j], gm_w_s[j], gm_b_s[j], gm_w_out[j])
        else:
            y = short_conv(h, sc_w_in[j], sc_conv_w[j], sc_w_out[j])
            yc = None if last else short_conv(hc, sc_w_in[j], sc_conv_w[j], sc_w_out[j])
        x = x + g1 * y
        hf = modulate(rms_norm(x, norm_ffn_g[i]), sh2, sc2)
        x = x + g2 * conv_ffn(hf, ffn_w_up[i], ffn_conv_w[i], ffn_conv_b[i], ffn_w_down[i])
        if not last:
            ctx = ctx + cg1 * yc
            hcf = modulate(rms_norm(ctx, norm_ffn_g[i]), csh2, csc2)
            ctx = ctx + cg2 * conv_ffn(hcf, ffn_w_up[i], ffn_conv_w[i], ffn_conv_b[i], ffn_w_down[i])
    return x
```

```python
import functools

import numpy as np
import jax
import jax.numpy as jnp
from jax import lax
from jax.experimental import pallas as pl
from jax.experimental.pallas import tpu as pltpu

GRID_W = 64
NA_HEADS = 16
WIN_H = 8
WIN_W = 16
GM_GROUPS = 16
GM_CHUNK = 128
N_MOD = 6
N_MIXERS = 3
EPS = 1e-6

V7X_VMEM_BYTES = 64 * 1024 * 1024
V7X_SUBLANES_F32 = 8
V7X_SUBLANES_BF16 = 16
V7X_LANES = 128
V7X_MXU_DIM = 256

F32 = jnp.float32
BF16 = jnp.bfloat16

NA_QROWS = 2
COND_ROWS = V7X_SUBLANES_F32


def _compiler_params(semantics, vmem_bytes):
    return pltpu.CompilerParams(dimension_semantics=semantics,
                                vmem_limit_bytes=int(min(vmem_bytes, V7X_VMEM_BYTES - (4 << 20))))


def _tile(n, pref):
    t = min(n, pref)
    while n % t:
        t -= 1
    return t


def _norm_mod(x, g, shift, scale):
    ms = jnp.mean(x * x, axis=-1, keepdims=True)
    y = x * lax.rsqrt(ms + EPS) * g
    return y * (1.0 + scale) + shift


def _adaln_kernel(c_ref, w_ref, b_ref, o_ref):
    c = c_ref[...]
    a = (c * jax.nn.sigmoid(c)).astype(BF16)
    o_ref[...] = jnp.dot(a, w_ref[...].astype(BF16), preferred_element_type=F32) + b_ref[...]


def _adaln(cond, w_ada, b_ada):
    depth, d, n = w_ada.shape
    tn = _tile(n, 1024)
    return pl.pallas_call(
        _adaln_kernel,
        grid=(depth, n // tn),
        in_specs=[
            pl.BlockSpec((COND_ROWS, d), lambda l, j: (0, 0)),
            pl.BlockSpec((None, d, tn), lambda l, j: (l, 0, j)),
            pl.BlockSpec((None, 1, tn), lambda l, j: (l, 0, j)),
        ],
        out_specs=pl.BlockSpec((None, COND_ROWS, tn), lambda l, j: (l, 0, j)),
        out_shape=jax.ShapeDtypeStruct((depth, COND_ROWS, n), F32),
        compiler_params=_compiler_params(("arbitrary", "arbitrary"), 2 * d * tn * 4 + (8 << 20)),
        name="adaln",
    )(cond, w_ada, b_ada.reshape(depth, 1, n))


def _mod_spec(layer, row_fn, width, col_fn=None):
    if col_fn is None:
        return pl.BlockSpec((None, None, N_MOD, width), lambda i, j: (layer, row_fn(i), 0, 0))
    return pl.BlockSpec((None, None, N_MOD, width), lambda i, j: (layer, row_fn(i), 0, col_fn(j)))


def _proj_res_kernel(a_ref, w_ref, x_ref, mod_ref, o_ref, *, gate_row):
    y = jnp.dot(a_ref[...], w_ref[...], preferred_element_type=F32)
    o_ref[...] = x_ref[...] + mod_ref[gate_row:gate_row + 1, :] * y


def _proj_res(a, w, x, mods, layer, row_fn, gate_row, tm_pref=1024, tn_pref=512):
    m, k = a.shape
    n = w.shape[1]
    tm, tn = _tile(m, tm_pref), _tile(n, tn_pref)
    vmem = 2 * (tm * k * 2 + k * tn * 2 + 2 * tm * tn * 4) + (8 << 20)
    return pl.pallas_call(
        functools.partial(_proj_res_kernel, gate_row=gate_row),
        grid=(m // tm, n // tn),
        in_specs=[
            pl.BlockSpec((tm, k), lambda i, j: (i, 0)),
            pl.BlockSpec((k, tn), lambda i, j: (0, j)),
            pl.BlockSpec((tm, tn), lambda i, j: (i, j)),
            _mod_spec(layer, lambda i: row_fn(i, tm), tn, lambda j: j),
        ],
        out_specs=pl.BlockSpec((tm, tn), lambda i, j: (i, j)),
        out_shape=jax.ShapeDtypeStruct((m, n), F32),
        compiler_params=_compiler_params(("parallel", "arbitrary"), vmem),
        name="proj_res",
    )(a, w, x, mods)


def _qkv_kernel(x_ref, g_ref, mod_ref, w_ref, hg_ref, o_ref, hs_ref, *, n_norm_tiles, head_dim):
    j = pl.program_id(1)

    @pl.when(j == 0)
    def _():
        h = _norm_mod(x_ref[...], g_ref[...], mod_ref[0:1, :], mod_ref[1:2, :])
        hs_ref[...] = h.astype(BF16)

    y = jnp.dot(hs_ref[...], w_ref[...], preferred_element_type=F32)
    tn = y.shape[1]

    @pl.when(j < n_norm_tiles)
    def _():
        for hh in range(tn // head_dim):
            sl = slice(hh * head_dim, (hh + 1) * head_dim)
            t = y[:, sl]
            r = lax.rsqrt(jnp.mean(t * t, axis=-1, keepdims=True) + EPS)
            o_ref[:, sl] = (t * r * hg_ref[:, sl]).astype(o_ref.dtype)

    @pl.when(j >= n_norm_tiles)
    def _():
        o_ref[...] = y.astype(o_ref.dtype)


def _qkv(x, norm_g, mods, layer, row_fn, w_qkv, head_gain, head_dim, tm_pref=1024, tn_pref=512):
    m, d = x.shape
    n = w_qkv.shape[1]
    tm = _tile(m, tm_pref)
    tn = _tile(n // 3, tn_pref)
    assert tn % head_dim == 0
    vmem = 2 * (tm * d * 4 + d * tn * 2 + tm * tn * 2) + tm * d * 2 + 3 * tm * tn * 4 + (8 << 20)
    return pl.pallas_call(
        functools.partial(_qkv_kernel, n_norm_tiles=2 * (n // 3) // tn, head_dim=head_dim),
        grid=(m // tm, n // tn),
        in_specs=[
            pl.BlockSpec((tm, d), lambda i, j: (i, 0)),
            pl.BlockSpec((1, d), lambda i, j: (0, 0)),
            _mod_spec(layer, lambda i: row_fn(i, tm), d),
            pl.BlockSpec((d, tn), lambda i, j: (0, j)),
            pl.BlockSpec((1, tn), lambda i, j: (0, j)),
        ],
        out_specs=pl.BlockSpec((tm, tn), lambda i, j: (i, j)),
        out_shape=jax.ShapeDtypeStruct((m, n), BF16),
        scratch_shapes=[pltpu.VMEM((tm, d), BF16)],
        compiler_params=_compiler_params(("parallel", "arbitrary"), vmem),
        name="qkv",
    )(x, norm_g, mods, w_qkv, head_gain)


def _na_kernel(*refs, nb, heads, head_dim, scale):
    refs = refs[2:]
    q_ref = refs[0]
    k_refs = refs[1:1 + nb]
    v_refs = refs[1 + nb:1 + 2 * nb]
    kc_ref, vc_ref, tbl_ref, o_ref = refs[1 + 2 * nb:]
    nt = (((1,), (1,)), ((), ()))
    for h in range(heads):
        sl = slice(h * head_dim, (h + 1) * head_dim)
        q = q_ref[:, sl]
        k = jnp.concatenate([r[:, sl] for r in k_refs], axis=0)
        v = jnp.concatenate([r[:, sl] for r in v_refs], axis=0)
        s = lax.dot_general(q, k, nt, preferred_element_type=F32) * scale + tbl_ref[h]
        sc = lax.dot_general(q, kc_ref[:, sl], nt, preferred_element_type=F32) * scale
        m = jnp.maximum(jnp.max(s, axis=-1, keepdims=True), jnp.max(sc, axis=-1, keepdims=True))
        p = jnp.exp(s - m)
        pc = jnp.exp(sc - m)
        l = jnp.sum(p, axis=-1, keepdims=True) + jnp.sum(pc, axis=-1, keepdims=True)
        o = (jnp.dot(p.astype(BF16), v, preferred_element_type=F32)
             + jnp.dot(pc.astype(BF16), vc_ref[:, sl], preferred_element_type=F32))
        o_ref[:, sl] = (o * (1.0 / l)).astype(o_ref.dtype)


def _na_geometry(rows, qrows):
    kh = min(WIN_H, rows)
    nb = min(kh + qrows - 1, rows)
    n_blk = rows // qrows
    q_row_off = np.repeat(np.arange(qrows), GRID_W)
    q_col = np.tile(np.arange(GRID_W), qrows)
    k_row_off = np.repeat(np.arange(nb), GRID_W)
    k_col = np.tile(np.arange(GRID_W), nb)
    c_start = np.clip(q_col - WIN_W // 2, 0, GRID_W - WIN_W)
    col_ok = (k_col[None, :] >= c_start[:, None]) & (k_col[None, :] < c_start[:, None] + WIN_W)
    dc_idx = np.clip(k_col[None, :] - q_col[:, None], -(WIN_W - 1), WIN_W - 1) + WIN_W - 1
    band0s, keys, types = [], {}, []
    oks, drs = [], []
    for blk in range(n_blk):
        r0 = blk * qrows
        q_row = r0 + q_row_off
        r_start = np.clip(q_row - kh // 2, 0, rows - kh)
        band0 = min(int(np.clip(r0 - kh // 2, 0, rows - kh)), rows - nb)
        k_row = band0 + k_row_off
        ok = col_ok & (k_row[None, :] >= r_start[:, None]) & (k_row[None, :] < r_start[:, None] + kh)
        dr_idx = np.clip(k_row[None, :] - q_row[:, None], -(WIN_H - 1), WIN_H - 1) + WIN_H - 1
        key = (ok.tobytes(), dr_idx.tobytes())
        if key not in keys:
            keys[key] = len(oks)
            oks.append(ok)
            drs.append(dr_idx)
        types.append(keys[key])
        band0s.append(band0)
    return nb, n_blk, np.array(band0s, np.int32), np.array(types, np.int32), np.stack(oks), np.stack(drs), dc_idx


def _na_attention(qkv, qkv_ctx, rpb, bn, rows, d, head_dim):
    qrows = min(NA_QROWS, rows)
    nb, n_blk, band0s, types, oks, drs, dc_idx = _na_geometry(rows, qrows)
    qw, nk = qrows * GRID_W, nb * GRID_W
    ctx_len = qkv_ctx.shape[1]
    tbl = jnp.where(oks[:, None], jnp.transpose(rpb[:, drs, dc_idx[None]], (1, 0, 2, 3)), -jnp.inf).astype(F32)
    band_tbl = jnp.asarray(band0s)
    type_tbl = jnp.asarray(types)
    kv_view = qkv.reshape(bn * rows, GRID_W, 3 * d)

    def kv_spec(t, col):
        return pl.BlockSpec((None, GRID_W, d), lambda b, i, band, typ: (b * rows + band[i] + t, 0, col))

    in_specs = ([pl.BlockSpec((qw, d), lambda b, i, band, typ: (b * n_blk + i, 0))]
                + [kv_spec(t, 1) for t in range(nb)]
                + [kv_spec(t, 2) for t in range(nb)]
                + [pl.BlockSpec((None, ctx_len, d), lambda b, i, band, typ: (b, 0, 1)),
                   pl.BlockSpec((None, ctx_len, d), lambda b, i, band, typ: (b, 0, 2)),
                   pl.BlockSpec((None, NA_HEADS, qw, nk), lambda b, i, band, typ: (typ[i], 0, 0, 0))])
    vmem = 2 * (2 * qw * d * 2 + 2 * nk * d * 2 + 2 * ctx_len * d * 2 + NA_HEADS * qw * nk * 4) + (12 << 20)
    grid_spec = pltpu.PrefetchScalarGridSpec(
        num_scalar_prefetch=2,
        grid=(bn, n_blk),
        in_specs=in_specs,
        out_specs=pl.BlockSpec((qw, d), lambda b, i, band, typ: (b * n_blk + i, 0)),
    )
    return pl.pallas_call(
        functools.partial(_na_kernel, nb=nb, heads=NA_HEADS, head_dim=head_dim, scale=head_dim ** -0.5),
        grid_spec=grid_spec,
        out_shape=jax.ShapeDtypeStruct((bn * rows * GRID_W, d), BF16),
        compiler_params=_compiler_params(("parallel", "arbitrary"), vmem),
        name="na_attention",
    )(band_tbl, type_tbl, qkv, *([kv_view] * (2 * nb)), qkv_ctx, qkv_ctx, tbl)


def _ctx_attn_kernel(q_ref, k_ref, v_ref, o_ref, *, heads, head_dim, scale):
    nt = (((1,), (1,)), ((), ()))
    for h in range(heads):
        sl = slice(h * head_dim, (h + 1) * head_dim)
        s = lax.dot_general(q_ref[:, sl], k_ref[:, sl], nt, preferred_element_type=F32) * scale
        m = jnp.max(s, axis=-1, keepdims=True)
        p = jnp.exp(s - m)
        l = jnp.sum(p, axis=-1, keepdims=True)
        o = jnp.dot(p.astype(BF16), v_ref[:, sl], preferred_element_type=F32)
        o_ref[:, sl] = (o * (1.0 / l)).astype(o_ref.dtype)


def _ctx_attention(qkv_ctx, d, head_dim):
    bn, ctx_len, _ = qkv_ctx.shape
    spec = lambda col: pl.BlockSpec((None, ctx_len, d), lambda b: (b, 0, col))
    return pl.pallas_call(
        functools.partial(_ctx_attn_kernel, heads=NA_HEADS, head_dim=head_dim, scale=head_dim ** -0.5),
        grid=(bn,),
        in_specs=[spec(0), spec(1), spec(2)],
        out_specs=pl.BlockSpec((None, ctx_len, d), lambda b: (b, 0, 0)),
        out_shape=jax.ShapeDtypeStruct((bn, ctx_len, d), BF16),
        compiler_params=_compiler_params(("arbitrary",), 8 * ctx_len * d * 2 + (12 << 20)),
        name="ctx_attention",
    )(qkv_ctx, qkv_ctx, qkv_ctx)


HALO = V7X_SUBLANES_F32


def _shift_rows(u, tm):
    n = u.shape[0]
    prev = pltpu.roll(u, 1, 0)[HALO:HALO + tm]
    nxt = pltpu.roll(u, n - 1, 0)[HALO:HALO + tm]
    return prev, nxt


def _gated_conv_kernel(x_ref, xp_ref, xn_ref, g_ref, mod_ref, w1_ref, cw_ref, cb_ref, w2_ref, o_ref, hs_ref,
                       *, kind, tiles_per_seq, mod_rows):
    i = pl.program_id(0)
    j = pl.program_id(1)
    nj = pl.num_programs(1)
    tm = x_ref.shape[0]
    r_shift, r_scale, r_gate = mod_rows

    @pl.when(j == 0)
    def _():
        g = g_ref[...]
        shift = mod_ref[r_shift:r_shift + 1, :]
        scale = mod_ref[r_scale:r_scale + 1, :]
        pos = i % tiles_per_seq
        hp = jnp.where(pos == 0, 0.0, _norm_mod(xp_ref[...], g, shift, scale))
        hn = jnp.where(pos == tiles_per_seq - 1, 0.0, _norm_mod(xn_ref[...], g, shift, scale))
        h = _norm_mod(x_ref[...], g, shift, scale)
        hs_ref[...] = jnp.concatenate([hp, h, hn], axis=0).astype(BF16)

    u = jnp.dot(hs_ref[...], w1_ref[...], preferred_element_type=F32)
    cw = cw_ref[...]
    if kind == "ffn":
        tc = u.shape[1] // 2
        prev, nxt = _shift_rows(u, tm)
        y = prev * cw[0:1] + u[HALO:HALO + tm] * cw[1:2] + nxt * cw[2:3] + cb_ref[...]
        gate, up = y[:, :tc], y[:, tc:]
        a = gate * jax.nn.sigmoid(gate) * up
    else:
        tc = u.shape[1] // 3
        z = u[:, tc:2 * tc] * u[:, 2 * tc:]
        prev, nxt = _shift_rows(z, tm)
        zc = prev * cw[0:1] + z[HALO:HALO + tm] * cw[1:2] + nxt * cw[2:3]
        a = u[HALO:HALO + tm, :tc] * zc
    contrib = jnp.dot(a.astype(BF16), w2_ref[...], preferred_element_type=F32)

    @pl.when(j == 0)
    def _():
        o_ref[...] = contrib

    @pl.when(j > 0)
    def _():
        o_ref[...] += contrib

    @pl.when(j == nj - 1)
    def _():
        o_ref[...] = x_ref[...] + mod_ref[r_gate:r_gate + 1, :] * o_ref[...]


def _gated_conv(kind, x, seq_len, norm_g, mods, layer, row_fn, mod_rows, w1, cw, cb, w2, tc, tm_pref):
    m, d = x.shape
    parts = 2 if kind == "ffn" else 3
    n_chunks = w2.shape[0] // tc
    tm = _tile(seq_len, tm_pref)
    tiles_per_seq = seq_len // tm
    hb = tm // HALO
    n_hblk = m // HALO
    cparts = cw.shape[1] // (n_chunks * tc)
    vmem = (2 * (2 * tm * d * 4 + d * parts * tc * 2 + tc * d * 2) + (tm + 2 * HALO) * d * 2
            + 5 * (tm + 2 * HALO) * parts * tc * 4 + (6 << 20))
    return pl.pallas_call(
        functools.partial(_gated_conv_kernel, kind=kind, tiles_per_seq=tiles_per_seq, mod_rows=mod_rows),
        grid=(m // tm, n_chunks),
        in_specs=[
            pl.BlockSpec((tm, d), lambda i, j: (i, 0)),
            pl.BlockSpec((HALO, d), lambda i, j: (jnp.maximum(i * hb - 1, 0), 0)),
            pl.BlockSpec((HALO, d), lambda i, j: (jnp.minimum((i + 1) * hb, n_hblk - 1), 0)),
            pl.BlockSpec((1, d), lambda i, j: (0, 0)),
            _mod_spec(layer, lambda i: row_fn(i, tm), d),
            pl.BlockSpec((d, parts * tc), lambda i, j: (0, j)),
            pl.BlockSpec((3, cparts * tc), lambda i, j: (0, j)),
            pl.BlockSpec((1, cparts * tc), lambda i, j: (0, j)),
            pl.BlockSpec((tc, d), lambda i, j: (j, 0)),
        ],
        out_specs=pl.BlockSpec((tm, d), lambda i, j: (i, 0)),
        out_shape=jax.ShapeDtypeStruct((m, d), F32),
        scratch_shapes=[pltpu.VMEM((tm + 2 * HALO, d), BF16)],
        compiler_params=_compiler_params(("parallel", "arbitrary"), vmem),
        name=kind,
    )(x, x, x, norm_g, mods, w1, cw, cb, w2)


def _gmlp_kernel(x_ref, g_ref, mod_ref, win_ref, vg_ref, ws_ref, bs_ref, wout_ref, o_ref,
                 hs_ref, v_ref, ssq_ref, *, n_chunks, group_dim, width):
    j = pl.program_id(1)
    tm = x_ref.shape[0]
    tc = win_ref.shape[1]

    @pl.when(j == 0)
    def _():
        h = _norm_mod(x_ref[...], g_ref[...], mod_ref[0:1, :], mod_ref[1:2, :])
        hs_ref[...] = h.astype(BF16)
        ssq_ref[...] = jnp.zeros_like(ssq_ref)

    t = jax.nn.gelu(jnp.dot(hs_ref[...], win_ref[...], preferred_element_type=F32))

    @pl.when(j < n_chunks)
    def _():
        v_ref[j] = t
        ssq_ref[...] += jnp.sum(t * t, axis=-1, keepdims=True)

    @pl.when(j >= n_chunks)
    def _():
        jj = j - n_chunks
        r = lax.rsqrt(ssq_ref[...] * (1.0 / width) + EPS)
        vn = (v_ref[jj] * r * vg_ref[...]).astype(BF16)
        bs = bs_ref[...]
        gpc = tc // group_dim
        cols = []
        for gi in range(gpc):
            w = ws_ref[jj * gpc + gi]
            csl = slice(gi * group_dim, (gi + 1) * group_dim)
            rows_out = []
            for c in range(tm // GM_CHUNK):
                rsl = slice(c * GM_CHUNK, (c + 1) * GM_CHUNK)
                rows_out.append(jnp.dot(w, vn[rsl, csl], preferred_element_type=F32) + bs[:, csl])
            cols.append(jnp.concatenate(rows_out, axis=0))
        sv = jnp.concatenate(cols, axis=1)
        contrib = jnp.dot((t * sv).astype(BF16), wout_ref[...], preferred_element_type=F32)

        @pl.when(jj == 0)
        def _():
            o_ref[...] = contrib

        @pl.when(jj > 0)
        def _():
            o_ref[...] += contrib

        @pl.when(jj == n_chunks - 1)
        def _():
            o_ref[...] = x_ref[...] + mod_ref[2:3, :] * o_ref[...]


def _gmlp(x, norm_g, mods, layer, row_fn, w_in, v_g, w_s, b_s_cols, w_out, tm_pref, tc=512):
    m, d = x.shape
    width = w_out.shape[0]
    group_dim = width // GM_GROUPS
    tc = max(group_dim, _tile(width, tc))
    n_chunks = width // tc
    tm = _tile(m, tm_pref)
    assert tm % GM_CHUNK == 0
    vmem = (2 * (2 * tm * d * 4 + d * tc * 2 + tc * d * 2 + GM_CHUNK * tc * 4) + tm * d * 2 + tm * width * 4
            + GM_GROUPS * GM_CHUNK * GM_CHUNK * 2 * 2 + 6 * tm * tc * 4 + (6 << 20))
    return pl.pallas_call(
        functools.partial(_gmlp_kernel, n_chunks=n_chunks, group_dim=group_dim, width=width),
        grid=(m // tm, 2 * n_chunks),
        in_specs=[
            pl.BlockSpec((tm, d), lambda i, j: (i, 0)),
            pl.BlockSpec((1, d), lambda i, j: (0, 0)),
            _mod_spec(layer, lambda i: row_fn(i, tm), d),
            pl.BlockSpec((d, tc), lambda i, j: (0, jnp.where(j < n_chunks, j + n_chunks, j - n_chunks))),
            pl.BlockSpec((1, tc), lambda i, j: (0, jnp.maximum(j - n_chunks, 0))),
            pl.BlockSpec((GM_GROUPS, GM_CHUNK, GM_CHUNK), lambda i, j: (0, 0, 0)),
            pl.BlockSpec((GM_CHUNK, tc), lambda i, j: (0, jnp.maximum(j - n_chunks, 0))),
            pl.BlockSpec((tc, d), lambda i, j: (jnp.maximum(j - n_chunks, 0), 0)),
        ],
        out_specs=pl.BlockSpec((tm, d), lambda i, j: (i, 0)),
        out_shape=jax.ShapeDtypeStruct((m, d), F32),
        scratch_shapes=[pltpu.VMEM((tm, d), BF16), pltpu.VMEM((n_chunks, tm, tc), F32), pltpu.VMEM((tm, 1), F32)],
        compiler_params=_compiler_params(("parallel", "arbitrary"), vmem),
        name="gmlp",
    )(x, norm_g, mods, w_in, v_g, w_s, b_s_cols, w_out)


def _interleave_cols(parts, tc):
    f = parts[0].shape[1]
    fp = -(-f // tc) * tc
    parts = [jnp.pad(p, ((0, 0), (0, fp - f))).reshape(p.shape[0], fp // tc, 1, tc) for p in parts]
    return jnp.concatenate(parts, axis=2).reshape(parts[0].shape[0], -1)


def _prep_ffn(w_up, conv_w, conv_b, w_down, tc):
    f = w_down.shape[0]
    fp = -(-f // tc) * tc
    w1 = _interleave_cols([w_up[:, :f], w_up[:, f:]], tc).astype(BF16)
    cw = _interleave_cols([conv_w[:, :f], conv_w[:, f:]], tc)
    cb = _interleave_cols([conv_b[None, :f], conv_b[None, f:]], tc)
    w2 = jnp.pad(w_down, ((0, fp - f), (0, 0))).astype(BF16)
    return w1, cw, cb, w2


def _prep_sc(w_in, conv_w, w_out, tc):
    d = w_out.shape[0]
    w1 = _interleave_cols([w_in[:, :d], w_in[:, d:2 * d], w_in[:, 2 * d:]], tc).astype(BF16)
    return w1, conv_w, jnp.zeros((1, d), F32), w_out.astype(BF16)


def kernel(x, c, ctx, c_ctx, norm_mix_g, norm_ffn_g, w_ada, b_ada, na_w_qkv, na_q_g, na_k_g, na_rpb, na_w_o,
           gm_w_in, gm_v_g, gm_w_s, gm_b_s, gm_w_out, sc_w_in, sc_conv_w, sc_w_out,
           ffn_w_up, ffn_conv_w, ffn_conv_b, ffn_w_down):
    bn, seq, d = x.shape
    ctx_len = ctx.shape[1]
    depth = w_ada.shape[0]
    head_dim = d // NA_HEADS
    rows = seq // GRID_W
    assert bn + 1 <= COND_ROWS and seq % GRID_W == 0 and ctx_len % GM_CHUNK == 0

    cond = jnp.concatenate([c, c_ctx[None, :], jnp.zeros((COND_ROWS - bn - 1, d), F32)], axis=0)
    mods = _adaln(cond, w_ada, b_ada).reshape(depth, COND_ROWS, N_MOD, d)

    lat_row = lambda i, tm: i // (seq // tm)
    ctx_row = lambda i, tm: bn

    ffn_tc = 512 if ffn_w_down.shape[1] >= 512 else 128
    sc_tc = _tile(d, 512)
    lat_tm = 512

    xl = x.reshape(bn * seq, d)
    xc = ctx.reshape(bn * ctx_len, d)
    for i in range(depth):
        last = i == depth - 1
        mixer, jx = i % N_MIXERS, i // N_MIXERS
        g_mix = norm_mix_g[i][None, :]
        g_ffn = norm_ffn_g[i][None, :]
        do_ctx = (not last) or mixer == 0
        if mixer == 0:
            w_qkv = na_w_qkv[jx].astype(BF16)
            w_o = na_w_o[jx].astype(BF16)
            head_gain = jnp.concatenate([jnp.tile(na_q_g[jx], NA_HEADS), jnp.tile(na_k_g[jx], NA_HEADS),
                                         jnp.ones((d,), F32)])[None, :]
            qkv = _qkv(xl, g_mix, mods, i, lat_row, w_qkv, head_gain, head_dim)
            qkv_c = _qkv(xc, g_mix, mods, i, ctx_row, w_qkv, head_gain, head_dim).reshape(bn, ctx_len, 3 * d)
            att = _na_attention(qkv, qkv_c, na_rpb[jx], bn, rows, d, head_dim)
            xl_new = _proj_res(att, w_o, xl, mods, i, lat_row, 2)
            if not last:
                att_c = _ctx_attention(qkv_c, d, head_dim).reshape(bn * ctx_len, d)
                xc = _proj_res(att_c, w_o, xc, mods, i, ctx_row, 2)
            xl = xl_new
        elif mixer == 1:
            w_in = gm_w_in[jx].astype(BF16)
            w_out = gm_w_out[jx].astype(BF16)
            w_s = gm_w_s[jx].astype(BF16)
            width = w_out.shape[0]
            v_g = gm_v_g[jx][None, :]
            b_s_cols = jnp.repeat(gm_b_s[jx].T, width // GM_GROUPS, axis=1)
            xl = _gmlp(xl, g_mix, mods, i, lat_row, w_in, v_g, w_s, b_s_cols, w_out, lat_tm)
            if not last:
                xc = _gmlp(xc, g_mix, mods, i, ctx_row, w_in, v_g, w_s, b_s_cols, w_out, ctx_len)
        else:
            w1, cw, cb, w2 = _prep_sc(sc_w_in[jx], sc_conv_w[jx], sc_w_out[jx], sc_tc)
            xl = _gated_conv("sc", xl, seq, g_mix, mods, i, lat_row, (0, 1, 2), w1, cw, cb, w2, sc_tc, lat_tm)
            if not last:
                xc = _gated_conv("sc", xc, ctx_len, g_mix, mods, i, ctx_row, (0, 1, 2), w1, cw, cb, w2, sc_tc,
                                 ctx_len)
        w1, cw, cb, w2 = _prep_ffn(ffn_w_up[i], ffn_conv_w[i], ffn_conv_b[i], ffn_w_down[i], ffn_tc)
        xl = _gated_conv("ffn", xl, seq, g_ffn, mods, i, lat_row, (3, 4, 5), w1, cw, cb, w2, ffn_tc, lat_tm)
        if not last:
            xc = _gated_conv("ffn", xc, ctx_len, g_ffn, mods, i, ctx_row, (3, 4, 5), w1, cw, cb, w2, ffn_tc,
                             ctx_len)
    return xl.reshape(bn, seq, d)
```

```python
import functools

import numpy as np
import jax
import jax.numpy as jnp
from jax import lax
from jax.experimental import pallas as pl
from jax.experimental.pallas import tpu as pltpu

GRID_W = 64
NA_HEADS = 16
WIN_H = 8
WIN_W = 16
GM_GROUPS = 16
GM_CHUNK = 128
N_MOD = 6
N_MIXERS = 3
EPS = 1e-6

V7X_VMEM_BYTES = 64 * 1024 * 1024
V7X_SUBLANES_F32 = 8
V7X_SUBLANES_BF16 = 16
V7X_LANES = 128
V7X_MXU_DIM = 256

F32 = jnp.float32
BF16 = jnp.bfloat16

NA_QROWS = 2
COND_ROWS = V7X_SUBLANES_F32


def _compiler_params(semantics, vmem_bytes):
    return pltpu.CompilerParams(dimension_semantics=semantics,
                                vmem_limit_bytes=int(min(vmem_bytes, V7X_VMEM_BYTES - (4 << 20))))


def _tile(n, pref):
    t = min(n, pref)
    while n % t:
        t -= 1
    return t


def _norm_mod(x, g, shift, scale):
    ms = jnp.mean(x * x, axis=-1, keepdims=True)
    y = x * lax.rsqrt(ms + EPS) * g
    return y * (1.0 + scale) + shift


NORM_CHUNK = 32


def _store_norm_mod(hs_ref, sources, g, shift, scale):
    bounds = np.cumsum([0] + [ref.shape[0] for ref, _ in sources])
    total = int(bounds[-1])
    assert hs_ref.shape[0] == total
    for c0 in range(0, total, NORM_CHUNK):
        c1 = min(c0 + NORM_CHUNK, total)
        pieces = []
        for (ref, zero_pred), b0, b1 in zip(sources, bounds[:-1], bounds[1:]):
            lo, hi = max(c0, int(b0)), min(c1, int(b1))
            if lo >= hi:
                continue
            h = _norm_mod(ref[lo - int(b0):hi - int(b0), :], g, shift, scale)
            pieces.append(h if zero_pred is None else jnp.where(zero_pred, 0.0, h))
        h = pieces[0] if len(pieces) == 1 else jnp.concatenate(pieces, axis=0)
        hs_ref[c0:c1, :] = h.astype(BF16)


def _adaln_kernel(c_ref, w_ref, b_ref, o_ref):
    c = c_ref[...]
    a = (c * jax.nn.sigmoid(c)).astype(BF16)
    o_ref[...] = jnp.dot(a, w_ref[...].astype(BF16), preferred_element_type=F32) + b_ref[...]


def _adaln(cond, w_ada, b_ada):
    depth, d, n = w_ada.shape
    tn = _tile(n, 1024)
    return pl.pallas_call(
        _adaln_kernel,
        grid=(depth, n // tn),
        in_specs=[
            pl.BlockSpec((COND_ROWS, d), lambda l, j: (0, 0)),
            pl.BlockSpec((None, d, tn), lambda l, j: (l, 0, j)),
            pl.BlockSpec((None, 1, tn), lambda l, j: (l, 0, j)),
        ],
        out_specs=pl.BlockSpec((None, COND_ROWS, tn), lambda l, j: (l, 0, j)),
        out_shape=jax.ShapeDtypeStruct((depth, COND_ROWS, n), F32),
        compiler_params=_compiler_params(("arbitrary", "arbitrary"), 2 * d * tn * 4 + (8 << 20)),
        name="adaln",
    )(cond, w_ada, b_ada.reshape(depth, 1, n))


def _mod_spec(layer, row_fn, width, col_fn=None):
    if col_fn is None:
        return pl.BlockSpec((None, None, N_MOD, width), lambda i, j: (layer, row_fn(i), 0, 0))
    return pl.BlockSpec((None, None, N_MOD, width), lambda i, j: (layer, row_fn(i), 0, col_fn(j)))


SUB_ROWS = 256


def _skewed_subtiles(n_sub, first, second):
    nxt = first(0)
    for r in range(n_sub):
        cur = nxt
        if r + 1 < n_sub:
            nxt = first(r + 1)
        second(r, cur)


def _proj_res_kernel(a_ref, w_ref, x_ref, mod_ref, o_ref, *, gate_row, sub_rows):
    gate = mod_ref[gate_row:gate_row + 1, :]

    def matmul(r):
        return jnp.dot(a_ref[r * sub_rows:(r + 1) * sub_rows, :], w_ref[...], preferred_element_type=F32)

    def residual(r, y):
        rows = slice(r * sub_rows, (r + 1) * sub_rows)
        o_ref[rows, :] = x_ref[rows, :] + gate * y

    _skewed_subtiles(o_ref.shape[0] // sub_rows, matmul, residual)


def _proj_res(a, w, x, mods, layer, row_fn, gate_row, tm_pref=1024, tn_pref=512):
    m, k = a.shape
    n = w.shape[1]
    tm, tn = _tile(m, tm_pref), _tile(n, tn_pref)
    vmem = 2 * (tm * k * 2 + k * tn * 2 + 2 * tm * tn * 4) + (8 << 20)
    return pl.pallas_call(
        functools.partial(_proj_res_kernel, gate_row=gate_row, sub_rows=_tile(tm, SUB_ROWS)),
        grid=(m // tm, n // tn),
        in_specs=[
            pl.BlockSpec((tm, k), lambda i, j: (i, 0)),
            pl.BlockSpec((k, tn), lambda i, j: (0, j)),
            pl.BlockSpec((tm, tn), lambda i, j: (i, j)),
            _mod_spec(layer, lambda i: row_fn(i, tm), tn, lambda j: j),
        ],
        out_specs=pl.BlockSpec((tm, tn), lambda i, j: (i, j)),
        out_shape=jax.ShapeDtypeStruct((m, n), F32),
        compiler_params=_compiler_params(("parallel", "arbitrary"), vmem),
        name="proj_res",
    )(a, w, x, mods)


def _qkv_kernel(x_ref, g_ref, mod_ref, w_ref, hg_ref, o_ref, hs_ref, *, n_norm_tiles, head_dim, sub_rows):
    j = pl.program_id(1)

    @pl.when(j == 0)
    def _():
        _store_norm_mod(hs_ref, [(x_ref, None)], g_ref[...], mod_ref[0:1, :], mod_ref[1:2, :])

    tm, tn = o_ref.shape
    is_norm_tile = j < n_norm_tiles
    hg = hg_ref[...]

    def matmul(r):
        return jnp.dot(hs_ref[r * sub_rows:(r + 1) * sub_rows, :], w_ref[...], preferred_element_type=F32)

    def head_norm(r, y):
        rows = slice(r * sub_rows, (r + 1) * sub_rows)
        for hh in range(tn // head_dim):
            sl = slice(hh * head_dim, (hh + 1) * head_dim)
            t = y[:, sl]
            inv = lax.rsqrt(jnp.mean(t * t, axis=-1, keepdims=True) + EPS)
            o_ref[rows, sl] = (t * jnp.where(is_norm_tile, inv, 1.0) * hg[:, sl]).astype(o_ref.dtype)

    _skewed_subtiles(tm // sub_rows, matmul, head_norm)


def _qkv(x, norm_g, mods, layer, row_fn, w_qkv, head_gain, head_dim, tm_pref=1024, tn_pref=512):
    m, d = x.shape
    n = w_qkv.shape[1]
    tm = _tile(m, tm_pref)
    tn = _tile(n // 3, tn_pref)
    assert tn % head_dim == 0
    vmem = 2 * (tm * d * 4 + d * tn * 2 + tm * tn * 2) + tm * d * 2 + 3 * tm * tn * 4 + (8 << 20)
    return pl.pallas_call(
        functools.partial(_qkv_kernel, n_norm_tiles=2 * (n // 3) // tn, head_dim=head_dim,
                          sub_rows=_tile(tm, SUB_ROWS)),
        grid=(m // tm, n // tn),
        in_specs=[
            pl.BlockSpec((tm, d), lambda i, j: (i, 0)),
            pl.BlockSpec((1, d), lambda i, j: (0, 0)),
            _mod_spec(layer, lambda i: row_fn(i, tm), d),
            pl.BlockSpec((d, tn), lambda i, j: (0, j)),
            pl.BlockSpec((1, tn), lambda i, j: (0, j)),
        ],
        out_specs=pl.BlockSpec((tm, tn), lambda i, j: (i, j)),
        out_shape=jax.ShapeDtypeStruct((m, n), BF16),
        scratch_shapes=[pltpu.VMEM((tm, d), BF16)],
        compiler_params=_compiler_params(("parallel", "arbitrary"), vmem),
        name="qkv",
    )(x, norm_g, mods, w_qkv, head_gain)


def _na_kernel(*refs, nb, heads, head_dim, scale):
    refs = refs[2:]
    q_ref = refs[0]
    k_refs = refs[1:1 + nb]
    v_refs = refs[1 + nb:1 + 2 * nb]
    kc_ref, vc_ref, tbl_ref, o_ref = refs[1 + 2 * nb:]
    nt = (((1,), (1,)), ((), ()))
    for h in range(heads):
        sl = slice(h * head_dim, (h + 1) * head_dim)
        q = q_ref[:, sl]
        k = jnp.concatenate([r[:, sl] for r in k_refs], axis=0)
        v = jnp.concatenate([r[:, sl] for r in v_refs], axis=0)
        s = lax.dot_general(q, k, nt, preferred_element_type=F32) * scale + tbl_ref[h]
        sc = lax.dot_general(q, kc_ref[:, sl], nt, preferred_element_type=F32) * scale
        m = jnp.maximum(jnp.max(s, axis=-1, keepdims=True), jnp.max(sc, axis=-1, keepdims=True))
        p = jnp.exp(s - m)
        pc = jnp.exp(sc - m)
        l = jnp.sum(p, axis=-1, keepdims=True) + jnp.sum(pc, axis=-1, keepdims=True)
        o = (jnp.dot(p.astype(BF16), v, preferred_element_type=F32)
             + jnp.dot(pc.astype(BF16), vc_ref[:, sl], preferred_element_type=F32))
        o_ref[:, sl] = (o * (1.0 / l)).astype(o_ref.dtype)


def _na_geometry(rows, qrows):
    kh = min(WIN_H, rows)
    nb = min(kh + qrows - 1, rows)
    n_blk = rows // qrows
    band0s, keys, types, row_rel = [], {}, [], []
    for blk in range(n_blk):
        r0 = blk * qrows
        q_row = r0 + np.arange(qrows)
        r_start = np.clip(q_row - kh // 2, 0, rows - kh)
        band0 = min(int(np.clip(r0 - kh // 2, 0, rows - kh)), rows - nb)
        k_row = band0 + np.arange(nb)
        row_ok = (k_row[None, :] >= r_start[:, None]) & (k_row[None, :] < r_start[:, None] + kh)
        rel = np.where(row_ok, k_row[None, :] - q_row[:, None] + WIN_H - 1, -1)
        key = rel.tobytes()
        if key not in keys:
            keys[key] = len(row_rel)
            row_rel.append(rel)
        types.append(keys[key])
        band0s.append(band0)
    return nb, n_blk, np.array(band0s, np.int32), np.array(types, np.int32), row_rel


def _na_bias_table(rpb, row_rel):
    heads, n_dr, n_dc = rpb.shape
    period = 2 * GRID_W
    assert n_dc <= period
    r = jnp.pad(rpb, ((0, 0), (0, 0), (0, period - n_dc)))
    skew = jnp.tile(r, (1, 1, GRID_W))[:, :, :GRID_W * (period - 1)].reshape(heads, n_dr, GRID_W, period - 1)
    toep = skew[:, :, :, WIN_W - 1:WIN_W - 1 + GRID_W]
    q_col = np.arange(GRID_W)
    c_start = np.clip(q_col - WIN_W // 2, 0, GRID_W - WIN_W)
    col_ok = (q_col[None, :] >= c_start[:, None]) & (q_col[None, :] < c_start[:, None] + WIN_W)
    toep = jnp.where(col_ok, toep, -jnp.inf)
    masked = jnp.full((heads, GRID_W, GRID_W), -jnp.inf, F32)
    tables = []
    for rel in row_rel:
        q_blocks = []
        for qr in range(rel.shape[0]):
            q_blocks.append(jnp.concatenate(
                [toep[:, rel[qr, kr]] if rel[qr, kr] >= 0 else masked for kr in range(rel.shape[1])], axis=-1))
        tables.append(jnp.concatenate(q_blocks, axis=-2))
    return jnp.stack(tables).astype(F32)


def _na_attention(qkv, qkv_ctx, rpb, bn, rows, d, head_dim):
    qrows = min(NA_QROWS, rows)
    nb, n_blk, band0s, types, row_rel = _na_geometry(rows, qrows)
    qw, nk = qrows * GRID_W, nb * GRID_W
    ctx_len = qkv_ctx.shape[1]
    tbl = _na_bias_table(rpb, row_rel)
    band_tbl = jnp.asarray(band0s)
    type_tbl = jnp.asarray(types)
    kv_view = qkv.reshape(bn * rows, GRID_W, 3 * d)

    def kv_spec(t, col):
        return pl.BlockSpec((None, GRID_W, d), lambda b, i, band, typ: (b * rows + band[i] + t, 0, col))

    in_specs = ([pl.BlockSpec((qw, d), lambda b, i, band, typ: (b * n_blk + i, 0))]
                + [kv_spec(t, 1) for t in range(nb)]
                + [kv_spec(t, 2) for t in range(nb)]
                + [pl.BlockSpec((None, ctx_len, d), lambda b, i, band, typ: (b, 0, 1)),
                   pl.BlockSpec((None, ctx_len, d), lambda b, i, band, typ: (b, 0, 2)),
                   pl.BlockSpec((None, NA_HEADS, qw, nk), lambda b, i, band, typ: (typ[i], 0, 0, 0))])
    vmem = 2 * (2 * qw * d * 2 + 2 * nk * d * 2 + 2 * ctx_len * d * 2 + NA_HEADS * qw * nk * 4) + (12 << 20)
    grid_spec = pltpu.PrefetchScalarGridSpec(
        num_scalar_prefetch=2,
        grid=(bn, n_blk),
        in_specs=in_specs,
        out_specs=pl.BlockSpec((qw, d), lambda b, i, band, typ: (b * n_blk + i, 0)),
    )
    return pl.pallas_call(
        functools.partial(_na_kernel, nb=nb, heads=NA_HEADS, head_dim=head_dim, scale=head_dim ** -0.5),
        grid_spec=grid_spec,
        out_shape=jax.ShapeDtypeStruct((bn * rows * GRID_W, d), BF16),
        compiler_params=_compiler_params(("parallel", "arbitrary"), vmem),
        name="na_attention",
    )(band_tbl, type_tbl, qkv, *([kv_view] * (2 * nb)), qkv_ctx, qkv_ctx, tbl)


def _ctx_attn_kernel(q_ref, k_ref, v_ref, o_ref, *, heads, head_dim, scale):
    nt = (((1,), (1,)), ((), ()))
    for h in range(heads):
        sl = slice(h * head_dim, (h + 1) * head_dim)
        s = lax.dot_general(q_ref[:, sl], k_ref[:, sl], nt, preferred_element_type=F32) * scale
        m = jnp.max(s, axis=-1, keepdims=True)
        p = jnp.exp(s - m)
        l = jnp.sum(p, axis=-1, keepdims=True)
        o = jnp.dot(p.astype(BF16), v_ref[:, sl], preferred_element_type=F32)
        o_ref[:, sl] = (o * (1.0 / l)).astype(o_ref.dtype)


def _ctx_attention(qkv_ctx, d, head_dim):
    bn, ctx_len, _ = qkv_ctx.shape
    spec = lambda col: pl.BlockSpec((None, ctx_len, d), lambda b: (b, 0, col))
    return pl.pallas_call(
        functools.partial(_ctx_attn_kernel, heads=NA_HEADS, head_dim=head_dim, scale=head_dim ** -0.5),
        grid=(bn,),
        in_specs=[spec(0), spec(1), spec(2)],
        out_specs=pl.BlockSpec((None, ctx_len, d), lambda b: (b, 0, 0)),
        out_shape=jax.ShapeDtypeStruct((bn, ctx_len, d), BF16),
        compiler_params=_compiler_params(("arbitrary",), 8 * ctx_len * d * 2 + (12 << 20)),
        name="ctx_attention",
    )(qkv_ctx, qkv_ctx, qkv_ctx)


HALO = V7X_SUBLANES_F32


def _shift_rows(u, tm):
    n = u.shape[0]
    prev = pltpu.roll(u, 1, 0)[HALO:HALO + tm]
    nxt = pltpu.roll(u, n - 1, 0)[HALO:HALO + tm]
    return prev, nxt


def _conv3(u, cw, rows):
    prev, nxt = _shift_rows(u, rows)
    return prev * cw[0:1] + u[HALO:HALO + rows] * cw[1:2] + nxt * cw[2:3]


def _gated_conv_kernel(*refs, kind, tiles_per_seq, mod_rows, sub_rows):
    x_ref, xp_ref, xn_ref, g_ref, mod_ref = refs[:5]
    if kind == "ffn":
        w1_refs, cw_refs, cb_refs = refs[5:7], refs[7:9], refs[9:11]
        w2_ref, o_ref, hs_ref = refs[11:]
    else:
        w1_refs, cw_refs = refs[5:8], refs[8:9]
        w2_ref, o_ref, hs_ref = refs[9:]
    i = pl.program_id(0)
    j = pl.program_id(1)
    nj = pl.num_programs(1)
    tm = x_ref.shape[0]
    r_shift, r_scale, r_gate = mod_rows

    @pl.when(j == 0)
    def _():
        g = g_ref[...]
        shift = mod_ref[r_shift:r_shift + 1, :]
        scale = mod_ref[r_scale:r_scale + 1, :]
        pos = i % tiles_per_seq
        _store_norm_mod(hs_ref, [(xp_ref, pos == 0), (x_ref, None), (xn_ref, pos == tiles_per_seq - 1)],
                        g, shift, scale)
        o_ref[...] = jnp.zeros_like(o_ref)

    def first_matmuls(r):
        lhs = hs_ref[r * sub_rows:r * sub_rows + sub_rows + 2 * HALO, :]
        return [jnp.dot(lhs, w[...], preferred_element_type=F32) for w in w1_refs]

    def mid_and_second_matmul(r, u):
        if kind == "ffn":
            gate, up = [_conv3(u[p], cw_refs[p][...], sub_rows) + cb_refs[p][...] for p in range(2)]
            a = gate * jax.nn.sigmoid(gate) * up
        else:
            a = u[0][HALO:HALO + sub_rows] * _conv3(u[1] * u[2], cw_refs[0][...], sub_rows)
        o_ref[r * sub_rows:(r + 1) * sub_rows, :] += jnp.dot(a.astype(BF16), w2_ref[...],
                                                            preferred_element_type=F32)

    _skewed_subtiles(tm // sub_rows, first_matmuls, mid_and_second_matmul)

    @pl.when(j == nj - 1)
    def _():
        o_ref[...] = x_ref[...] + mod_ref[r_gate:r_gate + 1, :] * o_ref[...]


def _gated_conv(kind, x, seq_len, norm_g, mods, layer, row_fn, mod_rows, w1, cw, cb, w2, tc, tm_pref,
                sub_pref=SUB_ROWS):
    m, d = x.shape
    parts = 2 if kind == "ffn" else 3
    cparts = 2 if kind == "ffn" else 1
    n_chunks = w2.shape[0] // tc
    assert w1.shape[1] == parts * n_chunks * tc and cw.shape[1] == cparts * n_chunks * tc
    tm = _tile(seq_len, tm_pref)
    sub_rows = _tile(tm, sub_pref)
    assert sub_rows % V7X_SUBLANES_BF16 == 0
    tiles_per_seq = seq_len // tm
    hb = tm // HALO
    n_hblk = m // HALO
    vmem = (2 * (2 * tm * d * 4 + d * parts * tc * 2 + tc * d * 2) + (tm + 2 * HALO) * d * 2
            + 8 * (sub_rows + 2 * HALO) * parts * tc * 4 + 2 * sub_rows * d * 4 + (6 << 20))
    col_spec = lambda rows, p: pl.BlockSpec((rows, tc), lambda i, j: (0, p * n_chunks + j))
    in_specs = [
        pl.BlockSpec((tm, d), lambda i, j: (i, 0)),
        pl.BlockSpec((HALO, d), lambda i, j: (jnp.maximum(i * hb - 1, 0), 0)),
        pl.BlockSpec((HALO, d), lambda i, j: (jnp.minimum((i + 1) * hb, n_hblk - 1), 0)),
        pl.BlockSpec((1, d), lambda i, j: (0, 0)),
        _mod_spec(layer, lambda i: row_fn(i, tm), d),
    ]
    in_specs += [col_spec(d, p) for p in range(parts)]
    in_specs += [col_spec(3, p) for p in range(cparts)]
    operands = [x, x, x, norm_g, mods] + [w1] * parts + [cw] * cparts
    if kind == "ffn":
        in_specs += [col_spec(1, p) for p in range(cparts)]
        operands += [cb] * cparts
    in_specs.append(pl.BlockSpec((tc, d), lambda i, j: (j, 0)))
    operands.append(w2)
    return pl.pallas_call(
        functools.partial(_gated_conv_kernel, kind=kind, tiles_per_seq=tiles_per_seq, mod_rows=mod_rows,
                          sub_rows=sub_rows),
        grid=(m // tm, n_chunks),
        in_specs=in_specs,
        out_specs=pl.BlockSpec((tm, d), lambda i, j: (i, 0)),
        out_shape=jax.ShapeDtypeStruct((m, d), F32),
        scratch_shapes=[pltpu.VMEM((tm + 2 * HALO, d), BF16)],
        compiler_params=_compiler_params(("parallel", "arbitrary"), vmem),
        name=kind,
    )(*operands)


def _gmlp_kernel(x_ref, g_ref, mod_ref, win_ref, vg_ref, ws_ref, bs_ref, wout_ref, o_ref,
                 hs_ref, v_ref, ssq_ref, *, n_chunks, group_dim, width):
    j = pl.program_id(1)
    tm = x_ref.shape[0]
    tc = win_ref.shape[1]

    @pl.when(j == 0)
    def _():
        _store_norm_mod(hs_ref, [(x_ref, None)], g_ref[...], mod_ref[0:1, :], mod_ref[1:2, :])
        ssq_ref[...] = jnp.zeros_like(ssq_ref)

    t = jax.nn.gelu(jnp.dot(hs_ref[...], win_ref[...], preferred_element_type=F32))

    @pl.when(j < n_chunks)
    def _():
        v_ref[j] = t
        ssq_ref[...] += jnp.sum(t * t, axis=-1, keepdims=True)

    @pl.when(j >= n_chunks)
    def _():
        jj = j - n_chunks
        r = lax.rsqrt(ssq_ref[...] * (1.0 / width) + EPS)
        vn = (v_ref[jj] * r * vg_ref[...]).astype(BF16)
        bs = bs_ref[...]
        gpc = tc // group_dim
        cols = []
        for gi in range(gpc):
            w = ws_ref[jj * gpc + gi]
            csl = slice(gi * group_dim, (gi + 1) * group_dim)
            rows_out = []
            for c in range(tm // GM_CHUNK):
                rsl = slice(c * GM_CHUNK, (c + 1) * GM_CHUNK)
                rows_out.append(jnp.dot(w, vn[rsl, csl], preferred_element_type=F32) + bs[:, csl])
            cols.append(jnp.concatenate(rows_out, axis=0))
        sv = jnp.concatenate(cols, axis=1)
        contrib = jnp.dot((t * sv).astype(BF16), wout_ref[...], preferred_element_type=F32)

        @pl.when(jj == 0)
        def _():
            o_ref[...] = contrib

        @pl.when(jj > 0)
        def _():
            o_ref[...] += contrib

        @pl.when(jj == n_chunks - 1)
        def _():
            o_ref[...] = x_ref[...] + mod_ref[2:3, :] * o_ref[...]


def _gmlp(x, norm_g, mods, layer, row_fn, w_in, v_g, w_s, b_s_cols, w_out, tm_pref, tc=512):
    m, d = x.shape
    width = w_out.shape[0]
    group_dim = width // GM_GROUPS
    tc = max(group_dim, _tile(width, tc))
    n_chunks = width // tc
    tm = _tile(m, tm_pref)
    assert tm % GM_CHUNK == 0
    vmem = (2 * (2 * tm * d * 4 + d * tc * 2 + tc * d * 2 + GM_CHUNK * tc * 4) + tm * d * 2 + tm * width * 4
            + GM_GROUPS * GM_CHUNK * GM_CHUNK * 2 * 2 + 6 * tm * tc * 4 + (6 << 20))
    return pl.pallas_call(
        functools.partial(_gmlp_kernel, n_chunks=n_chunks, group_dim=group_dim, width=width),
        grid=(m // tm, 2 * n_chunks),
        in_specs=[
            pl.BlockSpec((tm, d), lambda i, j: (i, 0)),
            pl.BlockSpec((1, d), lambda i, j: (0, 0)),
            _mod_spec(layer, lambda i: row_fn(i, tm), d),
            pl.BlockSpec((d, tc), lambda i, j: (0, jnp.where(j < n_chunks, j + n_chunks, j - n_chunks))),
            pl.BlockSpec((1, tc), lambda i, j: (0, jnp.maximum(j - n_chunks, 0))),
            pl.BlockSpec((GM_GROUPS, GM_CHUNK, GM_CHUNK), lambda i, j: (0, 0, 0)),
            pl.BlockSpec((GM_CHUNK, tc), lambda i, j: (0, jnp.maximum(j - n_chunks, 0))),
            pl.BlockSpec((tc, d), lambda i, j: (jnp.maximum(j - n_chunks, 0), 0)),
        ],
        out_specs=pl.BlockSpec((tm, d), lambda i, j: (i, 0)),
        out_shape=jax.ShapeDtypeStruct((m, d), F32),
        scratch_shapes=[pltpu.VMEM((tm, d), BF16), pltpu.VMEM((n_chunks, tm, tc), F32), pltpu.VMEM((tm, 1), F32)],
        compiler_params=_compiler_params(("parallel", "arbitrary"), vmem),
        name="gmlp",
    )(x, norm_g, mods, w_in, v_g, w_s, b_s_cols, w_out)


def _pad_halves(a, f, fp, dtype):
    r = a.shape[0]
    return jnp.pad(a.reshape(r, 2, f), ((0, 0), (0, 0), (0, fp - f))).reshape(r, 2 * fp).astype(dtype)


def _prep_ffn(w_up, conv_w, conv_b, w_down, tc):
    f = w_down.shape[0]
    fp = -(-f // tc) * tc
    w1 = _pad_halves(w_up, f, fp, BF16)
    cw = _pad_halves(conv_w, f, fp, F32)
    cb = _pad_halves(conv_b[None, :], f, fp, F32)
    w2 = jnp.pad(w_down, ((0, fp - f), (0, 0))).astype(BF16)
    return w1, cw, cb, w2


def kernel(x, c, ctx, c_ctx, norm_mix_g, norm_ffn_g, w_ada, b_ada, na_w_qkv, na_q_g, na_k_g, na_rpb, na_w_o,
           gm_w_in, gm_v_g, gm_w_s, gm_b_s, gm_w_out, sc_w_in, sc_conv_w, sc_w_out,
           ffn_w_up, ffn_conv_w, ffn_conv_b, ffn_w_down):
    bn, seq, d = x.shape
    ctx_len = ctx.shape[1]
    depth = w_ada.shape[0]
    head_dim = d // NA_HEADS
    rows = seq // GRID_W
    assert bn + 1 <= COND_ROWS and seq % GRID_W == 0 and ctx_len % GM_CHUNK == 0

    cond = jnp.concatenate([c, c_ctx[None, :], jnp.zeros((COND_ROWS - bn - 1, d), F32)], axis=0)
    mods = _adaln(cond, w_ada, b_ada).reshape(depth, COND_ROWS, N_MOD, d)

    lat_row = lambda i, tm: i // (seq // tm)
    ctx_row = lambda i, tm: bn

    ffn_tc = 512 if ffn_w_down.shape[1] >= 512 else 128
    sc_tc = _tile(d, 512)
    lat_tm = 512

    xl = x.reshape(bn * seq, d)
    xc = ctx.reshape(bn * ctx_len, d)
    for i in range(depth):
        last = i == depth - 1
        mixer, jx = i % N_MIXERS, i // N_MIXERS
        g_mix = norm_mix_g[i][None, :]
        g_ffn = norm_ffn_g[i][None, :]
        do_ctx = (not last) or mixer == 0
        if mixer == 0:
            w_qkv = na_w_qkv[jx].astype(BF16)
            w_o = na_w_o[jx].astype(BF16)
            head_gain = jnp.concatenate([jnp.tile(na_q_g[jx], NA_HEADS), jnp.tile(na_k_g[jx], NA_HEADS),
                                         jnp.ones((d,), F32)])[None, :]
            qkv = _qkv(xl, g_mix, mods, i, lat_row, w_qkv, head_gain, head_dim)
            qkv_c = _qkv(xc, g_mix, mods, i, ctx_row, w_qkv, head_gain, head_dim).reshape(bn, ctx_len, 3 * d)
            att = _na_attention(qkv, qkv_c, na_rpb[jx], bn, rows, d, head_dim)
            xl_new = _proj_res(att, w_o, xl, mods, i, lat_row, 2)
            if not last:
                att_c = _ctx_attention(qkv_c, d, head_dim).reshape(bn * ctx_len, d)
                xc = _proj_res(att_c, w_o, xc, mods, i, ctx_row, 2)
            xl = xl_new
        elif mixer == 1:
            w_in = gm_w_in[jx].astype(BF16)
            w_out = gm_w_out[jx].astype(BF16)
            w_s = gm_w_s[jx].astype(BF16)
            width = w_out.shape[0]
            v_g = gm_v_g[jx][None, :]
            b_s_cols = jnp.repeat(gm_b_s[jx].T, width // GM_GROUPS, axis=1)
            xl = _gmlp(xl, g_mix, mods, i, lat_row, w_in, v_g, w_s, b_s_cols, w_out, lat_tm)
            if not last:
                xc = _gmlp(xc, g_mix, mods, i, ctx_row, w_in, v_g, w_s, b_s_cols, w_out, ctx_len)
        else:
            w1, cw, w2 = sc_w_in[jx].astype(BF16), sc_conv_w[jx], sc_w_out[jx].astype(BF16)
            xl = _gated_conv("sc", xl, seq, g_mix, mods, i, lat_row, (0, 1, 2), w1, cw, None, w2, sc_tc, lat_tm)
            if not last:
                xc = _gated_conv("sc", xc, ctx_len, g_mix, mods, i, ctx_row, (0, 1, 2), w1, cw, None, w2, sc_tc,
                                 ctx_len)
        w1, cw, cb, w2 = _prep_ffn(ffn_w_up[i], ffn_conv_w[i], ffn_conv_b[i], ffn_w_down[i], ffn_tc)
        xl = _gated_conv("ffn", xl, seq, g_ffn, mods, i, lat_row, (3, 4, 5), w1, cw, cb, w2, ffn_tc, lat_tm)
        if not last:
            xc = _gated_conv("ffn", xc, ctx_len, g_ffn, mods, i, ctx_row, (3, 4, 5), w1, cw, cb, w2, ffn_tc,
                             ctx_len)
    return xl.reshape(bn, seq, d)
```

```python
import functools
import math

import numpy as np
import jax
import jax.numpy as jnp
from jax import lax
from jax.experimental import pallas as pl
from jax.experimental.pallas import tpu as pltpu

GRID_W = 64
NA_HEADS = 16
WIN_H = 8
WIN_W = 16
GM_GROUPS = 16
GM_CHUNK = 128
N_MOD = 6
N_MIXERS = 3
EPS = 1e-6

V7X_VMEM_BYTES = 64 * 1024 * 1024
V7X_SUBLANES_F32 = 8
V7X_SUBLANES_BF16 = 16
V7X_LANES = 128
V7X_MXU_DIM = 256

F32 = jnp.float32
BF16 = jnp.bfloat16

NA_QROWS = 2
COND_ROWS = V7X_SUBLANES_F32


def _compiler_params(semantics, vmem_bytes):
    return pltpu.CompilerParams(dimension_semantics=semantics,
                                vmem_limit_bytes=int(min(vmem_bytes, V7X_VMEM_BYTES - (4 << 20))))


def _tile(n, pref):
    t = min(n, pref)
    while n % t:
        t -= 1
    return t


def _norm_mod(x, g, shift, scale):
    ms = jnp.mean(x * x, axis=-1, keepdims=True)
    y = x * lax.rsqrt(ms + EPS) * g
    return y * (1.0 + scale) + shift


NORM_CHUNK = 32


def _store_norm_mod(hs_ref, sources, g, shift, scale):
    bounds = np.cumsum([0] + [ref.shape[0] for ref, _ in sources])
    total = int(bounds[-1])
    assert hs_ref.shape[0] == total
    for c0 in range(0, total, NORM_CHUNK):
        c1 = min(c0 + NORM_CHUNK, total)
        pieces = []
        for (ref, zero_pred), b0, b1 in zip(sources, bounds[:-1], bounds[1:]):
            lo, hi = max(c0, int(b0)), min(c1, int(b1))
            if lo >= hi:
                continue
            h = _norm_mod(ref[lo - int(b0):hi - int(b0), :], g, shift, scale)
            pieces.append(h if zero_pred is None else jnp.where(zero_pred, 0.0, h))
        h = pieces[0] if len(pieces) == 1 else jnp.concatenate(pieces, axis=0)
        hs_ref[c0:c1, :] = h.astype(BF16)


def _adaln_kernel(c_ref, w_ref, b_ref, o_ref):
    c = c_ref[...]
    a = (c * jax.nn.sigmoid(c)).astype(BF16)
    o_ref[...] = jnp.dot(a, w_ref[...].astype(BF16), preferred_element_type=F32) + b_ref[...]


def _adaln(cond, w_ada, b_ada):
    depth, d, n = w_ada.shape
    tn = _tile(n, 1024)
    return pl.pallas_call(
        _adaln_kernel,
        grid=(depth, n // tn),
        in_specs=[
            pl.BlockSpec((COND_ROWS, d), lambda l, j: (0, 0)),
            pl.BlockSpec((None, d, tn), lambda l, j: (l, 0, j)),
            pl.BlockSpec((None, 1, tn), lambda l, j: (l, 0, j)),
        ],
        out_specs=pl.BlockSpec((None, COND_ROWS, tn), lambda l, j: (l, 0, j)),
        out_shape=jax.ShapeDtypeStruct((depth, COND_ROWS, n), F32),
        compiler_params=_compiler_params(("arbitrary", "arbitrary"), 2 * d * tn * 4 + (8 << 20)),
        name="adaln",
    )(cond, w_ada, b_ada.reshape(depth, 1, n))


def _mod_spec(layer, row_fn, width, col_fn=None):
    if col_fn is None:
        return pl.BlockSpec((None, None, N_MOD, width), lambda i, j: (layer, row_fn(i), 0, 0))
    return pl.BlockSpec((None, None, N_MOD, width), lambda i, j: (layer, row_fn(i), 0, col_fn(j)))


SUB_ROWS = 256


def _skewed_subtiles(n_sub, first, second):
    nxt = first(0)
    for r in range(n_sub):
        cur = nxt
        if r + 1 < n_sub:
            nxt = first(r + 1)
        second(r, cur)


def _proj_res_kernel(a_ref, w_ref, x_ref, mod_ref, o_ref, *, gate_row, sub_rows):
    gate = mod_ref[gate_row:gate_row + 1, :]

    def matmul(r):
        return jnp.dot(a_ref[r * sub_rows:(r + 1) * sub_rows, :], w_ref[...], preferred_element_type=F32)

    def residual(r, y):
        rows = slice(r * sub_rows, (r + 1) * sub_rows)
        o_ref[rows, :] = x_ref[rows, :] + gate * y

    _skewed_subtiles(o_ref.shape[0] // sub_rows, matmul, residual)


def _proj_res(a, w, x, mods, layer, row_fn, gate_row, tm_pref=1024, tn_pref=512):
    m, k = a.shape
    n = w.shape[1]
    tm, tn = _tile(m, tm_pref), _tile(n, tn_pref)
    vmem = 2 * (tm * k * 2 + k * tn * 2 + 2 * tm * tn * 4) + (8 << 20)
    return pl.pallas_call(
        functools.partial(_proj_res_kernel, gate_row=gate_row, sub_rows=_tile(tm, SUB_ROWS)),
        grid=(m // tm, n // tn),
        in_specs=[
            pl.BlockSpec((tm, k), lambda i, j: (i, 0)),
            pl.BlockSpec((k, tn), lambda i, j: (0, j)),
            pl.BlockSpec((tm, tn), lambda i, j: (i, j)),
            _mod_spec(layer, lambda i: row_fn(i, tm), tn, lambda j: j),
        ],
        out_specs=pl.BlockSpec((tm, tn), lambda i, j: (i, j)),
        out_shape=jax.ShapeDtypeStruct((m, n), F32),
        compiler_params=_compiler_params(("parallel", "arbitrary"), vmem),
        name="proj_res",
    )(a, w, x, mods)


def _qkv_kernel(x_ref, g_ref, mod_ref, w_ref, hg_ref, o_ref, hs_ref, *, n_norm_tiles, head_dim, sub_rows):
    j = pl.program_id(1)

    @pl.when(j == 0)
    def _():
        _store_norm_mod(hs_ref, [(x_ref, None)], g_ref[...], mod_ref[0:1, :], mod_ref[1:2, :])

    tm, tn = o_ref.shape
    is_norm_tile = j < n_norm_tiles
    hg = hg_ref[...]

    def matmul(r):
        return jnp.dot(hs_ref[r * sub_rows:(r + 1) * sub_rows, :], w_ref[...], preferred_element_type=F32)

    def head_norm(r, y):
        rows = slice(r * sub_rows, (r + 1) * sub_rows)
        for hh in range(tn // head_dim):
            sl = slice(hh * head_dim, (hh + 1) * head_dim)
            t = y[:, sl]
            inv = lax.rsqrt(jnp.mean(t * t, axis=-1, keepdims=True) + EPS)
            o_ref[rows, sl] = (t * jnp.where(is_norm_tile, inv, 1.0) * hg[:, sl]).astype(o_ref.dtype)

    _skewed_subtiles(tm // sub_rows, matmul, head_norm)


def _qkv(x, norm_g, mods, layer, row_fn, w_qkv, head_gain, head_dim, tm_pref=1024, tn_pref=512):
    m, d = x.shape
    n = w_qkv.shape[1]
    tm = _tile(m, tm_pref)
    tn = _tile(n // 3, tn_pref)
    assert tn % head_dim == 0
    vmem = 2 * (tm * d * 4 + d * tn * 2 + tm * tn * 2) + tm * d * 2 + 3 * tm * tn * 4 + (8 << 20)
    return pl.pallas_call(
        functools.partial(_qkv_kernel, n_norm_tiles=2 * (n // 3) // tn, head_dim=head_dim,
                          sub_rows=_tile(tm, SUB_ROWS)),
        grid=(m // tm, n // tn),
        in_specs=[
            pl.BlockSpec((tm, d), lambda i, j: (i, 0)),
            pl.BlockSpec((1, d), lambda i, j: (0, 0)),
            _mod_spec(layer, lambda i: row_fn(i, tm), d),
            pl.BlockSpec((d, tn), lambda i, j: (0, j)),
            pl.BlockSpec((1, tn), lambda i, j: (0, j)),
        ],
        out_specs=pl.BlockSpec((tm, tn), lambda i, j: (i, j)),
        out_shape=jax.ShapeDtypeStruct((m, n), BF16),
        scratch_shapes=[pltpu.VMEM((tm, d), BF16)],
        compiler_params=_compiler_params(("parallel", "arbitrary"), vmem),
        name="qkv",
    )(x, norm_g, mods, w_qkv, head_gain)


LOG2_E = math.log2(math.e)
_NT_DIMS = (((1,), (1,)), ((), ()))


def _na_kernel(*refs, nb, heads, head_dim):
    refs = refs[2:]
    q_ref = refs[0]
    k_refs = refs[1:1 + nb]
    v_refs = refs[1 + nb:1 + 2 * nb]
    kc_ref, vc_ref, tbl_ref, o_ref = refs[1 + 2 * nb:]

    def scores(h):
        sl = slice(h * head_dim, (h + 1) * head_dim)
        q = q_ref[:, sl]
        k = jnp.concatenate([r[:, sl] for r in k_refs], axis=0)
        s = lax.dot_general(q, k, _NT_DIMS, preferred_element_type=F32) + tbl_ref[h]
        sc = lax.dot_general(q, kc_ref[:, sl], _NT_DIMS, preferred_element_type=F32)
        return s, sc

    def attend(h, s_sc):
        s, sc = s_sc
        sl = slice(h * head_dim, (h + 1) * head_dim)
        v = jnp.concatenate([r[:, sl] for r in v_refs], axis=0)
        m = jnp.maximum(jnp.max(s, axis=-1, keepdims=True), jnp.max(sc, axis=-1, keepdims=True))
        p = jnp.exp2(s - m)
        pc = jnp.exp2(sc - m)
        l = jnp.sum(p, axis=-1, keepdims=True) + jnp.sum(pc, axis=-1, keepdims=True)
        o = (jnp.dot(p.astype(BF16), v, preferred_element_type=F32)
             + jnp.dot(pc.astype(BF16), vc_ref[:, sl], preferred_element_type=F32))
        o_ref[:, sl] = (o * (1.0 / l)).astype(o_ref.dtype)

    _skewed_subtiles(heads, scores, attend)


def _na_geometry(rows, qrows):
    kh = min(WIN_H, rows)
    nb = min(kh + qrows - 1, rows)
    n_blk = rows // qrows
    band0s, keys, types, row_rel = [], {}, [], []
    for blk in range(n_blk):
        r0 = blk * qrows
        q_row = r0 + np.arange(qrows)
        r_start = np.clip(q_row - kh // 2, 0, rows - kh)
        band0 = min(int(np.clip(r0 - kh // 2, 0, rows - kh)), rows - nb)
        k_row = band0 + np.arange(nb)
        row_ok = (k_row[None, :] >= r_start[:, None]) & (k_row[None, :] < r_start[:, None] + kh)
        rel = np.where(row_ok, k_row[None, :] - q_row[:, None] + WIN_H - 1, -1)
        key = rel.tobytes()
        if key not in keys:
            keys[key] = len(row_rel)
            row_rel.append(rel)
        types.append(keys[key])
        band0s.append(band0)
    return nb, n_blk, np.array(band0s, np.int32), np.array(types, np.int32), row_rel


def _na_bias_table(rpb, row_rel):
    heads, n_dr, n_dc = rpb.shape
    period = 2 * GRID_W
    assert n_dc <= period
    r = jnp.pad(rpb, ((0, 0), (0, 0), (0, period - n_dc)))
    skew = jnp.tile(r, (1, 1, GRID_W))[:, :, :GRID_W * (period - 1)].reshape(heads, n_dr, GRID_W, period - 1)
    toep = skew[:, :, :, WIN_W - 1:WIN_W - 1 + GRID_W]
    q_col = np.arange(GRID_W)
    c_start = np.clip(q_col - WIN_W // 2, 0, GRID_W - WIN_W)
    col_ok = (q_col[None, :] >= c_start[:, None]) & (q_col[None, :] < c_start[:, None] + WIN_W)
    toep = jnp.where(col_ok, toep, -jnp.inf)
    masked = jnp.full((heads, GRID_W, GRID_W), -jnp.inf, F32)
    tables = []
    for rel in row_rel:
        q_blocks = []
        for qr in range(rel.shape[0]):
            q_blocks.append(jnp.concatenate(
                [toep[:, rel[qr, kr]] if rel[qr, kr] >= 0 else masked for kr in range(rel.shape[1])], axis=-1))
        tables.append(jnp.concatenate(q_blocks, axis=-2))
    return jnp.stack(tables).astype(F32)


def _na_attention(qkv, qkv_ctx, rpb, bn, rows, d, head_dim):
    qrows = min(NA_QROWS, rows)
    nb, n_blk, band0s, types, row_rel = _na_geometry(rows, qrows)
    qw, nk = qrows * GRID_W, nb * GRID_W
    ctx_len = qkv_ctx.shape[1]
    tbl = _na_bias_table(rpb * LOG2_E, row_rel)
    band_tbl = jnp.asarray(band0s)
    type_tbl = jnp.asarray(types)
    kv_view = qkv.reshape(bn * rows, GRID_W, 3 * d)

    def kv_spec(t, col):
        return pl.BlockSpec((None, GRID_W, d), lambda b, i, band, typ: (b * rows + band[i] + t, 0, col))

    in_specs = ([pl.BlockSpec((qw, d), lambda b, i, band, typ: (b * n_blk + i, 0))]
                + [kv_spec(t, 1) for t in range(nb)]
                + [kv_spec(t, 2) for t in range(nb)]
                + [pl.BlockSpec((None, ctx_len, d), lambda b, i, band, typ: (b, 0, 1)),
                   pl.BlockSpec((None, ctx_len, d), lambda b, i, band, typ: (b, 0, 2)),
                   pl.BlockSpec((None, NA_HEADS, qw, nk), lambda b, i, band, typ: (typ[i], 0, 0, 0))])
    vmem = 2 * (2 * qw * d * 2 + 2 * nk * d * 2 + 2 * ctx_len * d * 2 + NA_HEADS * qw * nk * 4) + (12 << 20)
    grid_spec = pltpu.PrefetchScalarGridSpec(
        num_scalar_prefetch=2,
        grid=(bn, n_blk),
        in_specs=in_specs,
        out_specs=pl.BlockSpec((qw, d), lambda b, i, band, typ: (b * n_blk + i, 0)),
    )
    return pl.pallas_call(
        functools.partial(_na_kernel, nb=nb, heads=NA_HEADS, head_dim=head_dim),
        grid_spec=grid_spec,
        out_shape=jax.ShapeDtypeStruct((bn * rows * GRID_W, d), BF16),
        compiler_params=_compiler_params(("parallel", "arbitrary"), vmem),
        name="na_attention",
    )(band_tbl, type_tbl, qkv, *([kv_view] * (2 * nb)), qkv_ctx, qkv_ctx, tbl)


def _ctx_attn_kernel(q_ref, k_ref, v_ref, o_ref, *, heads, head_dim):
    for h in range(heads):
        sl = slice(h * head_dim, (h + 1) * head_dim)
        s = lax.dot_general(q_ref[:, sl], k_ref[:, sl], _NT_DIMS, preferred_element_type=F32)
        m = jnp.max(s, axis=-1, keepdims=True)
        p = jnp.exp2(s - m)
        l = jnp.sum(p, axis=-1, keepdims=True)
        o = jnp.dot(p.astype(BF16), v_ref[:, sl], preferred_element_type=F32)
        o_ref[:, sl] = (o * (1.0 / l)).astype(o_ref.dtype)


def _ctx_attention(qkv_ctx, d, head_dim):
    bn, ctx_len, _ = qkv_ctx.shape
    spec = lambda col: pl.BlockSpec((None, ctx_len, d), lambda b: (b, 0, col))
    return pl.pallas_call(
        functools.partial(_ctx_attn_kernel, heads=NA_HEADS, head_dim=head_dim),
        grid=(bn,),
        in_specs=[spec(0), spec(1), spec(2)],
        out_specs=pl.BlockSpec((None, ctx_len, d), lambda b: (b, 0, 0)),
        out_shape=jax.ShapeDtypeStruct((bn, ctx_len, d), BF16),
        compiler_params=_compiler_params(("arbitrary",), 8 * ctx_len * d * 2 + (12 << 20)),
        name="ctx_attention",
    )(qkv_ctx, qkv_ctx, qkv_ctx)


HALO = V7X_SUBLANES_F32


def _shift_rows(u, tm):
    n = u.shape[0]
    prev = pltpu.roll(u, 1, 0)[HALO:HALO + tm]
    nxt = pltpu.roll(u, n - 1, 0)[HALO:HALO + tm]
    return prev, nxt


def _conv3(u, cw, rows):
    prev, nxt = _shift_rows(u, rows)
    return prev * cw[0:1] + u[HALO:HALO + rows] * cw[1:2] + nxt * cw[2:3]


def _gated_conv_kernel(*refs, kind, tiles_per_seq, mod_rows, sub_rows):
    x_ref, xp_ref, xn_ref, g_ref, mod_ref = refs[:5]
    if kind == "ffn":
        w1_refs, cw_refs, cb_refs = refs[5:7], refs[7:9], refs[9:11]
        w2_ref, o_ref, hs_ref = refs[11:]
    else:
        w1_refs, cw_refs = refs[5:8], refs[8:9]
        w2_ref, o_ref, hs_ref = refs[9:]
    i = pl.program_id(0)
    j = pl.program_id(1)
    nj = pl.num_programs(1)
    tm = x_ref.shape[0]
    r_shift, r_scale, r_gate = mod_rows

    @pl.when(j == 0)
    def _():
        g = g_ref[...]
        shift = mod_ref[r_shift:r_shift + 1, :]
        scale = mod_ref[r_scale:r_scale + 1, :]
        pos = i % tiles_per_seq
        _store_norm_mod(hs_ref, [(xp_ref, pos == 0), (x_ref, None), (xn_ref, pos == tiles_per_seq - 1)],
                        g, shift, scale)
        o_ref[...] = jnp.zeros_like(o_ref)

    def first_matmuls(r):
        lhs = hs_ref[r * sub_rows:r * sub_rows + sub_rows + 2 * HALO, :]
        return [jnp.dot(lhs, w[...], preferred_element_type=F32) for w in w1_refs]

    def mid_and_second_matmul(r, u):
        if kind == "ffn":
            gate, up = [_conv3(u[p], cw_refs[p][...], sub_rows) + cb_refs[p][...] for p in range(2)]
            a = gate * jax.nn.sigmoid(gate) * up
        else:
            a = u[0][HALO:HALO + sub_rows] * _conv3(u[1] * u[2], cw_refs[0][...], sub_rows)
        o_ref[r * sub_rows:(r + 1) * sub_rows, :] += jnp.dot(a.astype(BF16), w2_ref[...],
                                                            preferred_element_type=F32)

    _skewed_subtiles(tm // sub_rows, first_matmuls, mid_and_second_matmul)

    @pl.when(j == nj - 1)
    def _():
        o_ref[...] = x_ref[...] + mod_ref[r_gate:r_gate + 1, :] * o_ref[...]


def _gated_conv(kind, x, seq_len, norm_g, mods, layer, row_fn, mod_rows, w1, cw, cb, w2, tc, tm_pref,
                sub_pref=SUB_ROWS):
    m, d = x.shape
    parts = 2 if kind == "ffn" else 3
    cparts = 2 if kind == "ffn" else 1
    n_chunks = w2.shape[0] // tc
    assert w1.shape[1] == parts * n_chunks * tc and cw.shape[1] == cparts * n_chunks * tc
    tm = _tile(seq_len, tm_pref)
    sub_rows = _tile(tm, sub_pref)
    assert sub_rows % V7X_SUBLANES_BF16 == 0
    tiles_per_seq = seq_len // tm
    hb = tm // HALO
    n_hblk = m // HALO
    vmem = (2 * (2 * tm * d * 4 + d * parts * tc * 2 + tc * d * 2) + (tm + 2 * HALO) * d * 2
            + 8 * (sub_rows + 2 * HALO) * parts * tc * 4 + 2 * sub_rows * d * 4 + (6 << 20))
    col_spec = lambda rows, p: pl.BlockSpec((rows, tc), lambda i, j: (0, p * n_chunks + j))
    in_specs = [
        pl.BlockSpec((tm, d), lambda i, j: (i, 0)),
        pl.BlockSpec((HALO, d), lambda i, j: (jnp.maximum(i * hb - 1, 0), 0)),
        pl.BlockSpec((HALO, d), lambda i, j: (jnp.minimum((i + 1) * hb, n_hblk - 1), 0)),
        pl.BlockSpec((1, d), lambda i, j: (0, 0)),
        _mod_spec(layer, lambda i: row_fn(i, tm), d),
    ]
    in_specs += [col_spec(d, p) for p in range(parts)]
    in_specs += [col_spec(3, p) for p in range(cparts)]
    operands = [x, x, x, norm_g, mods] + [w1] * parts + [cw] * cparts
    if kind == "ffn":
        in_specs += [col_spec(1, p) for p in range(cparts)]
        operands += [cb] * cparts
    in_specs.append(pl.BlockSpec((tc, d), lambda i, j: (j, 0)))
    operands.append(w2)
    return pl.pallas_call(
        functools.partial(_gated_conv_kernel, kind=kind, tiles_per_seq=tiles_per_seq, mod_rows=mod_rows,
                          sub_rows=sub_rows),
        grid=(m // tm, n_chunks),
        in_specs=in_specs,
        out_specs=pl.BlockSpec((tm, d), lambda i, j: (i, 0)),
        out_shape=jax.ShapeDtypeStruct((m, d), F32),
        scratch_shapes=[pltpu.VMEM((tm + 2 * HALO, d), BF16)],
        compiler_params=_compiler_params(("parallel", "arbitrary"), vmem),
        name=kind,
    )(*operands)


def _gmlp_kernel(x_ref, g_ref, mod_ref, win_ref, vg_ref, ws_ref, bs_ref, wout_ref, o_ref,
                 hs_ref, v_ref, ssq_ref, *, n_chunks, group_dim, width, sub_rows):
    j = pl.program_id(1)
    tm = x_ref.shape[0]
    tc = win_ref.shape[1]
    n_sub = tm // sub_rows

    @pl.when(j == 0)
    def _():
        _store_norm_mod(hs_ref, [(x_ref, None)], g_ref[...], mod_ref[0:1, :], mod_ref[1:2, :])
        ssq_ref[...] = jnp.zeros_like(ssq_ref)
        o_ref[...] = jnp.zeros_like(o_ref)

    def in_matmul(r):
        return jnp.dot(hs_ref[r * sub_rows:(r + 1) * sub_rows, :], win_ref[...], preferred_element_type=F32)

    @pl.when(j < n_chunks)
    def _():
        def keep_v(r, y):
            rows = slice(r * sub_rows, (r + 1) * sub_rows)
            t = jax.nn.gelu(y)
            v_ref[j, rows, :] = t
            ssq_ref[rows, :] += jnp.sum(t * t, axis=-1, keepdims=True)

        _skewed_subtiles(n_sub, in_matmul, keep_v)

    @pl.when(j >= n_chunks)
    def _():
        jj = j - n_chunks
        vg = vg_ref[...]
        bs = bs_ref[...]
        gpc = tc // group_dim
        ws = [ws_ref[jj * gpc + gi] for gi in range(gpc)]

        def gate_and_project(r, y):
            rows = slice(r * sub_rows, (r + 1) * sub_rows)
            inv = lax.rsqrt(ssq_ref[rows, :] * (1.0 / width) + EPS)
            vn = (v_ref[jj, rows, :] * inv * vg).astype(BF16)
            cols = []
            for gi in range(gpc):
                csl = slice(gi * group_dim, (gi + 1) * group_dim)
                cols.append(jnp.concatenate(
                    [jnp.dot(ws[gi], vn[c * GM_CHUNK:(c + 1) * GM_CHUNK, csl], preferred_element_type=F32)
                     + bs[:, csl] for c in range(sub_rows // GM_CHUNK)], axis=0))
            sv = jnp.concatenate(cols, axis=1)
            o_ref[rows, :] += jnp.dot((jax.nn.gelu(y) * sv).astype(BF16), wout_ref[...],
                                      preferred_element_type=F32)

        _skewed_subtiles(n_sub, in_matmul, gate_and_project)

        @pl.when(jj == n_chunks - 1)
        def _():
            o_ref[...] = x_ref[...] + mod_ref[2:3, :] * o_ref[...]


def _gmlp(x, norm_g, mods, layer, row_fn, w_in, v_g, w_s, b_s_cols, w_out, tm_pref, tc=512):
    m, d = x.shape
    width = w_out.shape[0]
    group_dim = width // GM_GROUPS
    tc = max(group_dim, _tile(width, tc))
    n_chunks = width // tc
    tm = _tile(m, tm_pref)
    sub_rows = _tile(tm, SUB_ROWS)
    assert sub_rows % GM_CHUNK == 0
    vmem = (2 * (2 * tm * d * 4 + d * tc * 2 + tc * d * 2 + GM_CHUNK * tc * 4) + tm * d * 2 + tm * width * 4
            + GM_GROUPS * GM_CHUNK * GM_CHUNK * 2 * 2 + 6 * tm * tc * 4 + (6 << 20))
    return pl.pallas_call(
        functools.partial(_gmlp_kernel, n_chunks=n_chunks, group_dim=group_dim, width=width,
                          sub_rows=sub_rows),
        grid=(m // tm, 2 * n_chunks),
        in_specs=[
            pl.BlockSpec((tm, d), lambda i, j: (i, 0)),
            pl.BlockSpec((1, d), lambda i, j: (0, 0)),
            _mod_spec(layer, lambda i: row_fn(i, tm), d),
            pl.BlockSpec((d, tc), lambda i, j: (0, jnp.where(j < n_chunks, j + n_chunks, j - n_chunks))),
            pl.BlockSpec((1, tc), lambda i, j: (0, jnp.maximum(j - n_chunks, 0))),
            pl.BlockSpec((GM_GROUPS, GM_CHUNK, GM_CHUNK), lambda i, j: (0, 0, 0)),
            pl.BlockSpec((GM_CHUNK, tc), lambda i, j: (0, jnp.maximum(j - n_chunks, 0))),
            pl.BlockSpec((tc, d), lambda i, j: (jnp.maximum(j - n_chunks, 0), 0)),
        ],
        out_specs=pl.BlockSpec((tm, d), lambda i, j: (i, 0)),
        out_shape=jax.ShapeDtypeStruct((m, d), F32),
        scratch_shapes=[pltpu.VMEM((tm, d), BF16), pltpu.VMEM((n_chunks, tm, tc), F32), pltpu.VMEM((tm, 1), F32)],
        compiler_params=_compiler_params(("parallel", "arbitrary"), vmem),
        name="gmlp",
    )(x, norm_g, mods, w_in, v_g, w_s, b_s_cols, w_out)


CAST_BLOCK_BYTES = 6 << 20


def _cast_kernel(x_ref, o_ref, *, valid_row_blocks):
    rows, f = x_ref.shape
    rows_o, fp = o_ref.shape
    assert rows == rows_o
    if valid_row_blocks is None:
        o_ref[:, :f] = x_ref[...].astype(o_ref.dtype)
    else:
        o_ref[:, :f] = jnp.where(pl.program_id(0) < valid_row_blocks, x_ref[...], 0.0).astype(o_ref.dtype)
    if fp > f:
        o_ref[:, f:] = jnp.zeros((rows, fp - f), o_ref.dtype)


def _cast_weight(w, idx, parts=1, fp=None, rows_out=None, row_block=None):
    _, r, c = w.shape
    f = c // parts
    fp = f if fp is None else fp
    rows_out = r if rows_out is None else rows_out
    assert parts == 1 or f % V7X_LANES == 0
    if row_block is None:
        row_block = _tile(r, max(V7X_SUBLANES_BF16, CAST_BLOCK_BYTES // (4 * f) // V7X_SUBLANES_BF16
                                 * V7X_SUBLANES_BF16))
    assert r % row_block == 0 and rows_out % row_block == 0 and row_block % V7X_SUBLANES_BF16 == 0
    n_valid = r // row_block
    padded_rows = rows_out > r
    return pl.pallas_call(
        functools.partial(_cast_kernel, valid_row_blocks=n_valid if padded_rows else None),
        grid=(rows_out // row_block, parts),
        in_specs=[pl.BlockSpec((None, row_block, f), lambda i, p: (idx, jnp.minimum(i, n_valid - 1), p))],
        out_specs=pl.BlockSpec((row_block, fp), lambda i, p: (i, p)),
        out_shape=jax.ShapeDtypeStruct((rows_out, parts * fp), BF16),
        compiler_params=_compiler_params(("parallel", "arbitrary"),
                                         2 * row_block * (4 * f + 2 * fp) + (8 << 20)),
        name="cast_weight",
    )(w)


def _pad_halves(a, f, fp):
    pad = lambda h: jnp.pad(h, ((0, 0), (0, fp - f)))
    return jnp.concatenate([pad(a[:, :f]), pad(a[:, f:])], axis=1)


def _prep_ffn(w_up, conv_w, conv_b, w_down, layer, tc):
    f = w_down.shape[1]
    fp = -(-f // tc) * tc
    w1 = _cast_weight(w_up, layer, parts=2, fp=fp)
    cw = _pad_halves(conv_w[layer], f, fp)
    cb = _pad_halves(conv_b[layer][None, :], f, fp)
    w2 = _cast_weight(w_down, layer, rows_out=fp, row_block=math.gcd(f, fp))
    return w1, cw, cb, w2


def kernel(x, c, ctx, c_ctx, norm_mix_g, norm_ffn_g, w_ada, b_ada, na_w_qkv, na_q_g, na_k_g, na_rpb, na_w_o,
           gm_w_in, gm_v_g, gm_w_s, gm_b_s, gm_w_out, sc_w_in, sc_conv_w, sc_w_out,
           ffn_w_up, ffn_conv_w, ffn_conv_b, ffn_w_down):
    bn, seq, d = x.shape
    ctx_len = ctx.shape[1]
    depth = w_ada.shape[0]
    head_dim = d // NA_HEADS
    rows = seq // GRID_W
    assert bn + 1 <= COND_ROWS and seq % GRID_W == 0 and ctx_len % GM_CHUNK == 0

    cond = jnp.concatenate([c, c_ctx[None, :], jnp.zeros((COND_ROWS - bn - 1, d), F32)], axis=0)
    mods = _adaln(cond, w_ada, b_ada).reshape(depth, COND_ROWS, N_MOD, d)

    lat_row = lambda i, tm: i // (seq // tm)
    ctx_row = lambda i, tm: bn

    ffn_tc = 512 if ffn_w_down.shape[1] >= 512 else 128
    sc_tc = _tile(d, 512)
    lat_tm = 512

    xl = x.reshape(bn * seq, d)
    xc = ctx.reshape(bn * ctx_len, d)
    for i in range(depth):
        last = i == depth - 1
        mixer, jx = i % N_MIXERS, i // N_MIXERS
        g_mix = norm_mix_g[i][None, :]
        g_ffn = norm_ffn_g[i][None, :]
        do_ctx = (not last) or mixer == 0
        if mixer == 0:
            w_qkv = _cast_weight(na_w_qkv, jx)
            w_o = _cast_weight(na_w_o, jx)
            q_gain = na_q_g[jx] * (head_dim ** -0.5 * LOG2_E)
            head_gain = jnp.concatenate([jnp.tile(q_gain, NA_HEADS), jnp.tile(na_k_g[jx], NA_HEADS),
                                         jnp.ones((d,), F32)])[None, :]
            qkv = _qkv(xl, g_mix, mods, i, lat_row, w_qkv, head_gain, head_dim)
            qkv_c = _qkv(xc, g_mix, mods, i, ctx_row, w_qkv, head_gain, head_dim).reshape(bn, ctx_len, 3 * d)
            att = _na_attention(qkv, qkv_c, na_rpb[jx], bn, rows, d, head_dim)
            xl_new = _proj_res(att, w_o, xl, mods, i, lat_row, 2)
            if not last:
                att_c = _ctx_attention(qkv_c, d, head_dim).reshape(bn * ctx_len, d)
                xc = _proj_res(att_c, w_o, xc, mods, i, ctx_row, 2)
            xl = xl_new
        elif mixer == 1:
            w_in = _cast_weight(gm_w_in, jx)
            w_out = _cast_weight(gm_w_out, jx)
            w_s = gm_w_s[jx].astype(BF16)
            width = w_out.shape[0]
            v_g = gm_v_g[jx][None, :]
            b_s_cols = jnp.repeat(gm_b_s[jx].T, width // GM_GROUPS, axis=1)
            xl = _gmlp(xl, g_mix, mods, i, lat_row, w_in, v_g, w_s, b_s_cols, w_out, lat_tm)
            if not last:
                xc = _gmlp(xc, g_mix, mods, i, ctx_row, w_in, v_g, w_s, b_s_cols, w_out, ctx_len)
        else:
            w1, cw, w2 = _cast_weight(sc_w_in, jx), sc_conv_w[jx], _cast_weight(sc_w_out, jx)
            xl = _gated_conv("sc", xl, seq, g_mix, mods, i, lat_row, (0, 1, 2), w1, cw, None, w2, sc_tc, lat_tm)
            if not last:
                xc = _gated_conv("sc", xc, ctx_len, g_mix, mods, i, ctx_row, (0, 1, 2), w1, cw, None, w2, sc_tc,
                                 ctx_len)
        w1, cw, cb, w2 = _prep_ffn(ffn_w_up, ffn_conv_w, ffn_conv_b, ffn_w_down, i, ffn_tc)
        xl = _gated_conv("ffn", xl, seq, g_ffn, mods, i, lat_row, (3, 4, 5), w1, cw, cb, w2, ffn_tc, lat_tm)
        if not last:
            xc = _gated_conv("ffn", xc, ctx_len, g_ffn, mods, i, ctx_row, (3, 4, 5), w1, cw, cb, w2, ffn_tc,
                             ctx_len)
    return xl.reshape(bn, seq, d)
```

```python
import functools
import math

import numpy as np
import jax
import jax.numpy as jnp
from jax import lax
from jax.experimental import pallas as pl
from jax.experimental.pallas import tpu as pltpu

GRID_W = 64
NA_HEADS = 16
WIN_H = 8
WIN_W = 16
GM_GROUPS = 16
GM_CHUNK = 128
N_MOD = 6
N_MIXERS = 3
EPS = 1e-6

V7X_VMEM_BYTES = 64 * 1024 * 1024
V7X_SUBLANES_F32 = 8
V7X_SUBLANES_BF16 = 16
V7X_LANES = 128
V7X_MXU_DIM = 256

F32 = jnp.float32
BF16 = jnp.bfloat16

NA_QROWS = 2
COND_ROWS = V7X_SUBLANES_F32


def _compiler_params(semantics, vmem_bytes):
    return pltpu.CompilerParams(dimension_semantics=semantics,
                                vmem_limit_bytes=int(min(vmem_bytes, V7X_VMEM_BYTES - (4 << 20))))


def _tile(n, pref):
    t = min(n, pref)
    while n % t:
        t -= 1
    return t


IO_SINGLE_BUFFER_BYTES = V7X_VMEM_BYTES // 8


def _row_tile_buffers(tile_bytes):
    return 1 if tile_bytes >= IO_SINGLE_BUFFER_BYTES else 2


def _norm_mod(x, g, shift, scale):
    ms = jnp.mean(x * x, axis=-1, keepdims=True)
    y = x * lax.rsqrt(ms + EPS) * g
    return y * (1.0 + scale) + shift


NORM_CHUNK = 32


def _store_norm_mod(hs_ref, sources, g, shift, scale):
    bounds = np.cumsum([0] + [ref.shape[0] for ref, _ in sources])
    total = int(bounds[-1])
    assert hs_ref.shape[0] == total
    for c0 in range(0, total, NORM_CHUNK):
        c1 = min(c0 + NORM_CHUNK, total)
        pieces = []
        for (ref, zero_pred), b0, b1 in zip(sources, bounds[:-1], bounds[1:]):
            lo, hi = max(c0, int(b0)), min(c1, int(b1))
            if lo >= hi:
                continue
            h = _norm_mod(ref[lo - int(b0):hi - int(b0), :], g, shift, scale)
            pieces.append(h if zero_pred is None else jnp.where(zero_pred, 0.0, h))
        h = pieces[0] if len(pieces) == 1 else jnp.concatenate(pieces, axis=0)
        hs_ref[c0:c1, :] = h.astype(BF16)


def _adaln_kernel(c_ref, w_ref, b_ref, o_ref):
    c = c_ref[...]
    a = (c * jax.nn.sigmoid(c)).astype(BF16)
    o_ref[...] = jnp.dot(a, w_ref[...].astype(BF16), preferred_element_type=F32) + b_ref[...]


def _adaln(cond, w_ada, b_ada):
    depth, d, n = w_ada.shape
    tn = _tile(n, 1024)
    return pl.pallas_call(
        _adaln_kernel,
        grid=(depth, n // tn),
        in_specs=[
            pl.BlockSpec((COND_ROWS, d), lambda l, j: (0, 0)),
            pl.BlockSpec((None, d, tn), lambda l, j: (l, 0, j)),
            pl.BlockSpec((None, 1, tn), lambda l, j: (l, 0, j)),
        ],
        out_specs=pl.BlockSpec((None, COND_ROWS, tn), lambda l, j: (l, 0, j)),
        out_shape=jax.ShapeDtypeStruct((depth, COND_ROWS, n), F32),
        compiler_params=_compiler_params(("arbitrary", "arbitrary"), 2 * d * tn * 4 + (8 << 20)),
        name="adaln",
    )(cond, w_ada, b_ada.reshape(depth, 1, n))


def _mod_spec(layer, row_fn, width, col_fn=None):
    if col_fn is None:
        return pl.BlockSpec((None, None, N_MOD, width), lambda i, j: (layer, row_fn(i), 0, 0))
    return pl.BlockSpec((None, None, N_MOD, width), lambda i, j: (layer, row_fn(i), 0, col_fn(j)))


SUB_ROWS = 256


def _skewed_subtiles(n_sub, first, second):
    nxt = first(0)
    for r in range(n_sub):
        cur = nxt
        if r + 1 < n_sub:
            nxt = first(r + 1)
        second(r, cur)


def _proj_res_kernel(a_ref, w_ref, x_ref, mod_ref, o_ref, *, gate_row, sub_rows):
    gate = mod_ref[gate_row:gate_row + 1, :]

    def matmul(r):
        return jnp.dot(a_ref[r * sub_rows:(r + 1) * sub_rows, :], w_ref[...], preferred_element_type=F32)

    def residual(r, y):
        rows = slice(r * sub_rows, (r + 1) * sub_rows)
        o_ref[rows, :] = x_ref[rows, :] + gate * y

    _skewed_subtiles(o_ref.shape[0] // sub_rows, matmul, residual)


def _proj_res(a, w, x, mods, layer, row_fn, gate_row, tm_pref=1024, tn_pref=1024):
    m, k = a.shape
    n = w.shape[1]
    tm, tn = _tile(m, tm_pref), _tile(n, tn_pref)
    vmem = 2 * (tm * k * 2 + k * tn * 2 + 2 * tm * tn * 4) + (8 << 20)
    return pl.pallas_call(
        functools.partial(_proj_res_kernel, gate_row=gate_row, sub_rows=_tile(tm, SUB_ROWS)),
        grid=(m // tm, n // tn),
        in_specs=[
            pl.BlockSpec((tm, k), lambda i, j: (i, 0)),
            pl.BlockSpec((k, tn), lambda i, j: (0, j)),
            pl.BlockSpec((tm, tn), lambda i, j: (i, j)),
            _mod_spec(layer, lambda i: row_fn(i, tm), tn, lambda j: j),
        ],
        out_specs=pl.BlockSpec((tm, tn), lambda i, j: (i, j)),
        out_shape=jax.ShapeDtypeStruct((m, n), F32),
        compiler_params=_compiler_params(("parallel", "arbitrary"), vmem),
        name="proj_res",
    )(a, w, x, mods)


def _qkv_kernel(x_ref, g_ref, mod_ref, w_ref, hg_ref, o_ref, hs_ref, *, n_norm_tiles, head_dim, sub_rows):
    j = pl.program_id(1)

    @pl.when(j == 0)
    def _():
        _store_norm_mod(hs_ref, [(x_ref, None)], g_ref[...], mod_ref[0:1, :], mod_ref[1:2, :])

    tm, tn = o_ref.shape
    is_norm_tile = j < n_norm_tiles
    hg = hg_ref[...]

    def matmul(r):
        return jnp.dot(hs_ref[r * sub_rows:(r + 1) * sub_rows, :], w_ref[...], preferred_element_type=F32)

    def head_norm(r, y):
        rows = slice(r * sub_rows, (r + 1) * sub_rows)
        for hh in range(tn // head_dim):
            sl = slice(hh * head_dim, (hh + 1) * head_dim)
            t = y[:, sl]
            inv = lax.rsqrt(jnp.mean(t * t, axis=-1, keepdims=True) + EPS)
            o_ref[rows, sl] = (t * jnp.where(is_norm_tile, inv, 1.0) * hg[:, sl]).astype(o_ref.dtype)

    _skewed_subtiles(tm // sub_rows, matmul, head_norm)


def _qkv(x, norm_g, mods, layer, row_fn, w_qkv, head_gain, head_dim, tm_pref=1024, tn_pref=1024):
    m, d = x.shape
    n = w_qkv.shape[1]
    tm = _tile(m, tm_pref)
    tn = _tile(n // 3, tn_pref)
    assert tn % head_dim == 0
    vmem = 2 * (tm * d * 4 + d * tn * 2 + tm * tn * 2) + tm * d * 2 + 3 * tm * tn * 4 + (8 << 20)
    return pl.pallas_call(
        functools.partial(_qkv_kernel, n_norm_tiles=2 * (n // 3) // tn, head_dim=head_dim,
                          sub_rows=_tile(tm, SUB_ROWS)),
        grid=(m // tm, n // tn),
        in_specs=[
            pl.BlockSpec((tm, d), lambda i, j: (i, 0)),
            pl.BlockSpec((1, d), lambda i, j: (0, 0)),
            _mod_spec(layer, lambda i: row_fn(i, tm), d),
            pl.BlockSpec((d, tn), lambda i, j: (0, j)),
            pl.BlockSpec((1, tn), lambda i, j: (0, j)),
        ],
        out_specs=pl.BlockSpec((tm, tn), lambda i, j: (i, j)),
        out_shape=jax.ShapeDtypeStruct((m, n), BF16),
        scratch_shapes=[pltpu.VMEM((tm, d), BF16)],
        compiler_params=_compiler_params(("parallel", "arbitrary"), vmem),
        name="qkv",
    )(x, norm_g, mods, w_qkv, head_gain)


LOG2_E = math.log2(math.e)
_NT_DIMS = (((1,), (1,)), ((), ()))


def _na_kernel(*refs, nb, heads, head_dim):
    refs = refs[2:]
    q_ref = refs[0]
    k_refs = refs[1:1 + nb]
    v_refs = refs[1 + nb:1 + 2 * nb]
    kc_ref, vc_ref, tbl_ref, o_ref = refs[1 + 2 * nb:]

    def scores(h):
        sl = slice(h * head_dim, (h + 1) * head_dim)
        q = q_ref[:, sl]
        k = jnp.concatenate([r[:, sl] for r in k_refs], axis=0)
        s = lax.dot_general(q, k, _NT_DIMS, preferred_element_type=F32) + tbl_ref[h]
        sc = lax.dot_general(q, kc_ref[:, sl], _NT_DIMS, preferred_element_type=F32)
        return s, sc

    def attend(h, s_sc):
        s, sc = s_sc
        sl = slice(h * head_dim, (h + 1) * head_dim)
        v = jnp.concatenate([r[:, sl] for r in v_refs], axis=0)
        m = jnp.maximum(jnp.max(s, axis=-1, keepdims=True), jnp.max(sc, axis=-1, keepdims=True))
        p = jnp.exp2(s - m)
        pc = jnp.exp2(sc - m)
        l = jnp.sum(p, axis=-1, keepdims=True) + jnp.sum(pc, axis=-1, keepdims=True)
        o = (jnp.dot(p.astype(BF16), v, preferred_element_type=F32)
             + jnp.dot(pc.astype(BF16), vc_ref[:, sl], preferred_element_type=F32))
        o_ref[:, sl] = (o * (1.0 / l)).astype(o_ref.dtype)

    _skewed_subtiles(heads, scores, attend)


def _na_geometry(rows, qrows):
    kh = min(WIN_H, rows)
    nb = min(kh + qrows - 1, rows)
    n_blk = rows // qrows
    band0s, keys, types, row_rel = [], {}, [], []
    for blk in range(n_blk):
        r0 = blk * qrows
        q_row = r0 + np.arange(qrows)
        r_start = np.clip(q_row - kh // 2, 0, rows - kh)
        band0 = min(int(np.clip(r0 - kh // 2, 0, rows - kh)), rows - nb)
        k_row = band0 + np.arange(nb)
        row_ok = (k_row[None, :] >= r_start[:, None]) & (k_row[None, :] < r_start[:, None] + kh)
        rel = np.where(row_ok, k_row[None, :] - q_row[:, None] + WIN_H - 1, -1)
        key = rel.tobytes()
        if key not in keys:
            keys[key] = len(row_rel)
            row_rel.append(rel)
        types.append(keys[key])
        band0s.append(band0)
    return nb, n_blk, np.array(band0s, np.int32), np.array(types, np.int32), row_rel


def _na_bias_table(rpb, row_rel):
    heads, n_dr, n_dc = rpb.shape
    period = 2 * GRID_W
    assert n_dc <= period
    r = jnp.pad(rpb, ((0, 0), (0, 0), (0, period - n_dc)))
    skew = jnp.tile(r, (1, 1, GRID_W))[:, :, :GRID_W * (period - 1)].reshape(heads, n_dr, GRID_W, period - 1)
    toep = skew[:, :, :, WIN_W - 1:WIN_W - 1 + GRID_W]
    q_col = np.arange(GRID_W)
    c_start = np.clip(q_col - WIN_W // 2, 0, GRID_W - WIN_W)
    col_ok = (q_col[None, :] >= c_start[:, None]) & (q_col[None, :] < c_start[:, None] + WIN_W)
    toep = jnp.where(col_ok, toep, -jnp.inf)
    masked = jnp.full((heads, GRID_W, GRID_W), -jnp.inf, F32)
    tables = []
    for rel in row_rel:
        q_blocks = []
        for qr in range(rel.shape[0]):
            q_blocks.append(jnp.concatenate(
                [toep[:, rel[qr, kr]] if rel[qr, kr] >= 0 else masked for kr in range(rel.shape[1])], axis=-1))
        tables.append(jnp.concatenate(q_blocks, axis=-2))
    return jnp.stack(tables).astype(F32)


def _na_attention(qkv, qkv_ctx, rpb, bn, rows, d, head_dim):
    qrows = min(NA_QROWS, rows)
    nb, n_blk, band0s, types, row_rel = _na_geometry(rows, qrows)
    qw, nk = qrows * GRID_W, nb * GRID_W
    ctx_len = qkv_ctx.shape[1]
    tbl = _na_bias_table(rpb * LOG2_E, row_rel)
    band_tbl = jnp.asarray(band0s)
    type_tbl = jnp.asarray(types)
    kv_view = qkv.reshape(bn * rows, GRID_W, 3 * d)

    def kv_spec(t, col):
        return pl.BlockSpec((None, GRID_W, d), lambda b, i, band, typ: (b * rows + band[i] + t, 0, col))

    in_specs = ([pl.BlockSpec((qw, d), lambda b, i, band, typ: (b * n_blk + i, 0))]
                + [kv_spec(t, 1) for t in range(nb)]
                + [kv_spec(t, 2) for t in range(nb)]
                + [pl.BlockSpec((None, ctx_len, d), lambda b, i, band, typ: (b, 0, 1)),
                   pl.BlockSpec((None, ctx_len, d), lambda b, i, band, typ: (b, 0, 2)),
                   pl.BlockSpec((None, NA_HEADS, qw, nk), lambda b, i, band, typ: (typ[i], 0, 0, 0))])
    vmem = 2 * (2 * qw * d * 2 + 2 * nk * d * 2 + 2 * ctx_len * d * 2 + NA_HEADS * qw * nk * 4) + (12 << 20)
    grid_spec = pltpu.PrefetchScalarGridSpec(
        num_scalar_prefetch=2,
        grid=(bn, n_blk),
        in_specs=in_specs,
        out_specs=pl.BlockSpec((qw, d), lambda b, i, band, typ: (b * n_blk + i, 0)),
    )
    return pl.pallas_call(
        functools.partial(_na_kernel, nb=nb, heads=NA_HEADS, head_dim=head_dim),
        grid_spec=grid_spec,
        out_shape=jax.ShapeDtypeStruct((bn * rows * GRID_W, d), BF16),
        compiler_params=_compiler_params(("parallel", "arbitrary"), vmem),
        name="na_attention",
    )(band_tbl, type_tbl, qkv, *([kv_view] * (2 * nb)), qkv_ctx, qkv_ctx, tbl)


def _ctx_attn_kernel(q_ref, k_ref, v_ref, o_ref, *, heads, head_dim):
    for h in range(heads):
        sl = slice(h * head_dim, (h + 1) * head_dim)
        s = lax.dot_general(q_ref[:, sl], k_ref[:, sl], _NT_DIMS, preferred_element_type=F32)
        m = jnp.max(s, axis=-1, keepdims=True)
        p = jnp.exp2(s - m)
        l = jnp.sum(p, axis=-1, keepdims=True)
        o = jnp.dot(p.astype(BF16), v_ref[:, sl], preferred_element_type=F32)
        o_ref[:, sl] = (o * (1.0 / l)).astype(o_ref.dtype)


def _ctx_attention(qkv_ctx, d, head_dim):
    bn, ctx_len, _ = qkv_ctx.shape
    spec = lambda col: pl.BlockSpec((None, ctx_len, d), lambda b: (b, 0, col))
    return pl.pallas_call(
        functools.partial(_ctx_attn_kernel, heads=NA_HEADS, head_dim=head_dim),
        grid=(bn,),
        in_specs=[spec(0), spec(1), spec(2)],
        out_specs=pl.BlockSpec((None, ctx_len, d), lambda b: (b, 0, 0)),
        out_shape=jax.ShapeDtypeStruct((bn, ctx_len, d), BF16),
        compiler_params=_compiler_params(("arbitrary",), 8 * ctx_len * d * 2 + (12 << 20)),
        name="ctx_attention",
    )(qkv_ctx, qkv_ctx, qkv_ctx)


HALO = V7X_SUBLANES_F32


def _shift_rows(u, tm):
    n = u.shape[0]
    prev = pltpu.roll(u, 1, 0)[HALO:HALO + tm]
    nxt = pltpu.roll(u, n - 1, 0)[HALO:HALO + tm]
    return prev, nxt


def _conv3(u, cw, rows):
    prev, nxt = _shift_rows(u, rows)
    return prev * cw[0:1] + u[HALO:HALO + rows] * cw[1:2] + nxt * cw[2:3]


def _gated_conv_kernel(*refs, kind, tiles_per_seq, mod_rows, sub_rows):
    x_ref, xp_ref, xn_ref, g_ref, mod_ref = refs[:5]
    if kind == "ffn":
        w1_refs, cw_refs, cb_refs = refs[5:7], refs[7:9], refs[9:11]
        w2_ref, o_ref, hs_ref = refs[11:]
    else:
        w1_refs, cw_refs = refs[5:8], refs[8:9]
        w2_ref, o_ref, hs_ref = refs[9:]
    i = pl.program_id(0)
    j = pl.program_id(1)
    nj = pl.num_programs(1)
    tm = x_ref.shape[0]
    r_shift, r_scale, r_gate = mod_rows

    @pl.when(j == 0)
    def _():
        g = g_ref[...]
        shift = mod_ref[r_shift:r_shift + 1, :]
        scale = mod_ref[r_scale:r_scale + 1, :]
        pos = i % tiles_per_seq
        _store_norm_mod(hs_ref, [(xp_ref, pos == 0), (x_ref, None), (xn_ref, pos == tiles_per_seq - 1)],
                        g, shift, scale)
        o_ref[...] = jnp.zeros_like(o_ref)

    def first_matmuls(r):
        lhs = hs_ref[r * sub_rows:r * sub_rows + sub_rows + 2 * HALO, :]
        return [jnp.dot(lhs, w[...], preferred_element_type=F32) for w in w1_refs]

    def mid_and_second_matmul(r, u):
        if kind == "ffn":
            gate, up = [_conv3(u[p], cw_refs[p][...], sub_rows) + cb_refs[p][...] for p in range(2)]
            a = gate * jax.nn.sigmoid(gate) * up
        else:
            a = u[0][HALO:HALO + sub_rows] * _conv3(u[1] * u[2], cw_refs[0][...], sub_rows)
        o_ref[r * sub_rows:(r + 1) * sub_rows, :] += jnp.dot(a.astype(BF16), w2_ref[...],
                                                            preferred_element_type=F32)

    _skewed_subtiles(tm // sub_rows, first_matmuls, mid_and_second_matmul)

    @pl.when(j == nj - 1)
    def _():
        o_ref[...] = x_ref[...] + mod_ref[r_gate:r_gate + 1, :] * o_ref[...]


def _gated_conv(kind, x, seq_len, norm_g, mods, layer, row_fn, mod_rows, w1, cw, cb, w2, tc, tm_pref,
                sub_pref=SUB_ROWS):
    m, d = x.shape
    parts = 2 if kind == "ffn" else 3
    cparts = 2 if kind == "ffn" else 1
    n_chunks = w2.shape[0] // tc
    assert w1.shape[1] == parts * n_chunks * tc and cw.shape[1] == cparts * n_chunks * tc
    tm = _tile(seq_len, tm_pref)
    sub_rows = _tile(tm, sub_pref)
    assert sub_rows % V7X_SUBLANES_BF16 == 0
    tiles_per_seq = seq_len // tm
    hb = tm // HALO
    n_hblk = m // HALO
    io_buffers = _row_tile_buffers(tm * d * 4)
    io_mode = pl.Buffered(io_buffers)
    vmem = (2 * io_buffers * tm * d * 4 + 2 * (d * parts * tc * 2 + tc * d * 2) + (tm + 2 * HALO) * d * 2
            + 8 * (sub_rows + 2 * HALO) * parts * tc * 4 + 2 * sub_rows * d * 4 + (6 << 20))
    col_spec = lambda rows, p: pl.BlockSpec((rows, tc), lambda i, j: (0, p * n_chunks + j))
    in_specs = [
        pl.BlockSpec((tm, d), lambda i, j: (i, 0), pipeline_mode=io_mode),
        pl.BlockSpec((HALO, d), lambda i, j: (jnp.maximum(i * hb - 1, 0), 0)),
        pl.BlockSpec((HALO, d), lambda i, j: (jnp.minimum((i + 1) * hb, n_hblk - 1), 0)),
        pl.BlockSpec((1, d), lambda i, j: (0, 0)),
        _mod_spec(layer, lambda i: row_fn(i, tm), d),
    ]
    in_specs += [col_spec(d, p) for p in range(parts)]
    in_specs += [col_spec(3, p) for p in range(cparts)]
    operands = [x, x, x, norm_g, mods] + [w1] * parts + [cw] * cparts
    if kind == "ffn":
        in_specs += [col_spec(1, p) for p in range(cparts)]
        operands += [cb] * cparts
    in_specs.append(pl.BlockSpec((tc, d), lambda i, j: (j, 0)))
    operands.append(w2)
    return pl.pallas_call(
        functools.partial(_gated_conv_kernel, kind=kind, tiles_per_seq=tiles_per_seq, mod_rows=mod_rows,
                          sub_rows=sub_rows),
        grid=(m // tm, n_chunks),
        in_specs=in_specs,
        out_specs=pl.BlockSpec((tm, d), lambda i, j: (i, 0), pipeline_mode=io_mode),
        out_shape=jax.ShapeDtypeStruct((m, d), F32),
        scratch_shapes=[pltpu.VMEM((tm + 2 * HALO, d), BF16)],
        compiler_params=_compiler_params(("parallel", "arbitrary"), vmem),
        name=kind,
    )(*operands)


def _gmlp_kernel(x_ref, g_ref, mod_ref, win_ref, vg_ref, ws_ref, bs_ref, wout_ref, o_ref,
                 hs_ref, v_ref, ssq_ref, *, n_chunks, group_dim, width, sub_rows):
    j = pl.program_id(1)
    tm = x_ref.shape[0]
    tc = win_ref.shape[1]
    n_sub = tm // sub_rows

    @pl.when(j == 0)
    def _():
        _store_norm_mod(hs_ref, [(x_ref, None)], g_ref[...], mod_ref[0:1, :], mod_ref[1:2, :])
        ssq_ref[...] = jnp.zeros_like(ssq_ref)
        o_ref[...] = jnp.zeros_like(o_ref)

    def in_matmul(r):
        return jnp.dot(hs_ref[r * sub_rows:(r + 1) * sub_rows, :], win_ref[...], preferred_element_type=F32)

    @pl.when(j < n_chunks)
    def _():
        def keep_v(r, y):
            rows = slice(r * sub_rows, (r + 1) * sub_rows)
            t = jax.nn.gelu(y)
            v_ref[j, rows, :] = t
            ssq_ref[rows, :] += jnp.sum(t * t, axis=-1, keepdims=True)

        _skewed_subtiles(n_sub, in_matmul, keep_v)

    @pl.when(j >= n_chunks)
    def _():
        jj = j - n_chunks
        vg = vg_ref[...]
        bs = bs_ref[...]
        gpc = tc // group_dim
        ws = [ws_ref[jj * gpc + gi] for gi in range(gpc)]

        def gate_and_project(r, y):
            rows = slice(r * sub_rows, (r + 1) * sub_rows)
            inv = lax.rsqrt(ssq_ref[rows, :] * (1.0 / width) + EPS)
            vn = (v_ref[jj, rows, :] * inv * vg).astype(BF16)
            cols = []
            for gi in range(gpc):
                csl = slice(gi * group_dim, (gi + 1) * group_dim)
                cols.append(jnp.concatenate(
                    [jnp.dot(ws[gi], vn[c * GM_CHUNK:(c + 1) * GM_CHUNK, csl], preferred_element_type=F32)
                     + bs[:, csl] for c in range(sub_rows // GM_CHUNK)], axis=0))
            sv = jnp.concatenate(cols, axis=1)
            o_ref[rows, :] += jnp.dot((jax.nn.gelu(y) * sv).astype(BF16), wout_ref[...],
                                      preferred_element_type=F32)

        _skewed_subtiles(n_sub, in_matmul, gate_and_project)

        @pl.when(jj == n_chunks - 1)
        def _():
            o_ref[...] = x_ref[...] + mod_ref[2:3, :] * o_ref[...]


def _gmlp(x, norm_g, mods, layer, row_fn, w_in, v_g, w_s, b_s_cols, w_out, tm_pref, tc=512):
    m, d = x.shape
    width = w_out.shape[0]
    group_dim = width // GM_GROUPS
    tc = max(group_dim, _tile(width, tc))
    n_chunks = width // tc
    tm = _tile(m, tm_pref)
    sub_rows = _tile(tm, SUB_ROWS)
    assert sub_rows % GM_CHUNK == 0
    io_buffers = _row_tile_buffers(tm * d * 4)
    io_mode = pl.Buffered(io_buffers)
    vmem = (2 * io_buffers * tm * d * 4 + 2 * (d * tc * 2 + tc * d * 2 + GM_CHUNK * tc * 4) + tm * d * 2
            + tm * width * 4 + GM_GROUPS * GM_CHUNK * GM_CHUNK * 2 * 2 + 12 * sub_rows * tc * 4 + (6 << 20))
    return pl.pallas_call(
        functools.partial(_gmlp_kernel, n_chunks=n_chunks, group_dim=group_dim, width=width,
                          sub_rows=sub_rows),
        grid=(m // tm, 2 * n_chunks),
        in_specs=[
            pl.BlockSpec((tm, d), lambda i, j: (i, 0), pipeline_mode=io_mode),
            pl.BlockSpec((1, d), lambda i, j: (0, 0)),
            _mod_spec(layer, lambda i: row_fn(i, tm), d),
            pl.BlockSpec((d, tc), lambda i, j: (0, jnp.where(j < n_chunks, j + n_chunks, j - n_chunks))),
            pl.BlockSpec((1, tc), lambda i, j: (0, jnp.maximum(j - n_chunks, 0))),
            pl.BlockSpec((GM_GROUPS, GM_CHUNK, GM_CHUNK), lambda i, j: (0, 0, 0)),
            pl.BlockSpec((GM_CHUNK, tc), lambda i, j: (0, jnp.maximum(j - n_chunks, 0))),
            pl.BlockSpec((tc, d), lambda i, j: (jnp.maximum(j - n_chunks, 0), 0)),
        ],
        out_specs=pl.BlockSpec((tm, d), lambda i, j: (i, 0), pipeline_mode=io_mode),
        out_shape=jax.ShapeDtypeStruct((m, d), F32),
        scratch_shapes=[pltpu.VMEM((tm, d), BF16), pltpu.VMEM((n_chunks, tm, tc), F32), pltpu.VMEM((tm, 1), F32)],
        compiler_params=_compiler_params(("parallel", "arbitrary"), vmem),
        name="gmlp",
    )(x, norm_g, mods, w_in, v_g, w_s, b_s_cols, w_out)


CAST_BLOCK_BYTES = 6 << 20


def _cast_kernel(x_ref, o_ref, *, valid_row_blocks):
    rows, f = x_ref.shape
    rows_o, fp = o_ref.shape
    assert rows == rows_o
    if valid_row_blocks is None:
        o_ref[:, :f] = x_ref[...].astype(o_ref.dtype)
    else:
        o_ref[:, :f] = jnp.where(pl.program_id(0) < valid_row_blocks, x_ref[...], 0.0).astype(o_ref.dtype)
    if fp > f:
        o_ref[:, f:] = jnp.zeros((rows, fp - f), o_ref.dtype)


def _cast_weight(w, idx, parts=1, fp=None, rows_out=None, row_block=None):
    _, r, c = w.shape
    f = c // parts
    fp = f if fp is None else fp
    rows_out = r if rows_out is None else rows_out
    assert parts == 1 or f % V7X_LANES == 0
    if row_block is None:
        row_block = _tile(r, max(V7X_SUBLANES_BF16, CAST_BLOCK_BYTES // (4 * f) // V7X_SUBLANES_BF16
                                 * V7X_SUBLANES_BF16))
    assert r % row_block == 0 and rows_out % row_block == 0 and row_block % V7X_SUBLANES_BF16 == 0
    n_valid = r // row_block
    padded_rows = rows_out > r
    return pl.pallas_call(
        functools.partial(_cast_kernel, valid_row_blocks=n_valid if padded_rows else None),
        grid=(rows_out // row_block, parts),
        in_specs=[pl.BlockSpec((None, row_block, f), lambda i, p: (idx, jnp.minimum(i, n_valid - 1), p))],
        out_specs=pl.BlockSpec((row_block, fp), lambda i, p: (i, p)),
        out_shape=jax.ShapeDtypeStruct((rows_out, parts * fp), BF16),
        compiler_params=_compiler_params(("parallel", "arbitrary"),
                                         2 * row_block * (4 * f + 2 * fp) + (8 << 20)),
        name="cast_weight",
    )(w)


def _pad_halves(a, f, fp):
    pad = lambda h: jnp.pad(h, ((0, 0), (0, fp - f)))
    return jnp.concatenate([pad(a[:, :f]), pad(a[:, f:])], axis=1)


def _prep_ffn(w_up, conv_w, conv_b, w_down, layer, tc):
    f = w_down.shape[1]
    fp = -(-f // tc) * tc
    w1 = _cast_weight(w_up, layer, parts=2, fp=fp)
    cw = _pad_halves(conv_w[layer], f, fp)
    cb = _pad_halves(conv_b[layer][None, :], f, fp)
    w2 = _cast_weight(w_down, layer, rows_out=fp, row_block=math.gcd(f, fp))
    return w1, cw, cb, w2


def kernel(x, c, ctx, c_ctx, norm_mix_g, norm_ffn_g, w_ada, b_ada, na_w_qkv, na_q_g, na_k_g, na_rpb, na_w_o,
           gm_w_in, gm_v_g, gm_w_s, gm_b_s, gm_w_out, sc_w_in, sc_conv_w, sc_w_out,
           ffn_w_up, ffn_conv_w, ffn_conv_b, ffn_w_down):
    bn, seq, d = x.shape
    ctx_len = ctx.shape[1]
    depth = w_ada.shape[0]
    head_dim = d // NA_HEADS
    rows = seq // GRID_W
    assert bn + 1 <= COND_ROWS and seq % GRID_W == 0 and ctx_len % GM_CHUNK == 0

    cond = jnp.concatenate([c, c_ctx[None, :], jnp.zeros((COND_ROWS - bn - 1, d), F32)], axis=0)
    mods = _adaln(cond, w_ada, b_ada).reshape(depth, COND_ROWS, N_MOD, d)

    lat_row = lambda i, tm: i // (seq // tm)
    ctx_row = lambda i, tm: bn

    ffn_tc = 512 if ffn_w_down.shape[1] >= 512 else 128
    sc_tc = _tile(d, 512)
    lat_tm = 1024
    lat_sub = 512

    xl = x.reshape(bn * seq, d)
    xc = ctx.reshape(bn * ctx_len, d)
    for i in range(depth):
        last = i == depth - 1
        mixer, jx = i % N_MIXERS, i // N_MIXERS
        g_mix = norm_mix_g[i][None, :]
        g_ffn = norm_ffn_g[i][None, :]
        do_ctx = (not last) or mixer == 0
        if mixer == 0:
            w_qkv = _cast_weight(na_w_qkv, jx)
            w_o = _cast_weight(na_w_o, jx)
            q_gain = na_q_g[jx] * (head_dim ** -0.5 * LOG2_E)
            head_gain = jnp.concatenate([jnp.tile(q_gain, NA_HEADS), jnp.tile(na_k_g[jx], NA_HEADS),
                                         jnp.ones((d,), F32)])[None, :]
            qkv = _qkv(xl, g_mix, mods, i, lat_row, w_qkv, head_gain, head_dim)
            qkv_c = _qkv(xc, g_mix, mods, i, ctx_row, w_qkv, head_gain, head_dim).reshape(bn, ctx_len, 3 * d)
            att = _na_attention(qkv, qkv_c, na_rpb[jx], bn, rows, d, head_dim)
            xl_new = _proj_res(att, w_o, xl, mods, i, lat_row, 2)
            if not last:
                att_c = _ctx_attention(qkv_c, d, head_dim).reshape(bn * ctx_len, d)
                xc = _proj_res(att_c, w_o, xc, mods, i, ctx_row, 2)
            xl = xl_new
        elif mixer == 1:
            w_in = _cast_weight(gm_w_in, jx)
            w_out = _cast_weight(gm_w_out, jx)
            w_s = gm_w_s[jx].astype(BF16)
            width = w_out.shape[0]
            v_g = gm_v_g[jx][None, :]
            b_s_cols = jnp.repeat(gm_b_s[jx].T, width // GM_GROUPS, axis=1)
            xl = _gmlp(xl, g_mix, mods, i, lat_row, w_in, v_g, w_s, b_s_cols, w_out, lat_tm)
            if not last:
                xc = _gmlp(xc, g_mix, mods, i, ctx_row, w_in, v_g, w_s, b_s_cols, w_out, ctx_len)
        else:
            w1, cw, w2 = _cast_weight(sc_w_in, jx), sc_conv_w[jx], _cast_weight(sc_w_out, jx)
            xl = _gated_conv("sc", xl, seq, g_mix, mods, i, lat_row, (0, 1, 2), w1, cw, None, w2, sc_tc, lat_tm,
                             sub_pref=lat_sub)
            if not last:
                xc = _gated_conv("sc", xc, ctx_len, g_mix, mods, i, ctx_row, (0, 1, 2), w1, cw, None, w2, sc_tc,
                                 ctx_len)
        w1, cw, cb, w2 = _prep_ffn(ffn_w_up, ffn_conv_w, ffn_conv_b, ffn_w_down, i, ffn_tc)
        xl = _gated_conv("ffn", xl, seq, g_ffn, mods, i, lat_row, (3, 4, 5), w1, cw, cb, w2, ffn_tc, lat_tm,
                         sub_pref=lat_sub)
        if not last:
            xc = _gated_conv("ffn", xc, ctx_len, g_ffn, mods, i, ctx_row, (3, 4, 5), w1, cw, cb, w2, ffn_tc,
                             ctx_len)
    return xl.reshape(bn, seq, d)
```

```python
import functools
import math

import numpy as np
import jax
import jax.numpy as jnp
from jax import lax
from jax.experimental import pallas as pl
from jax.experimental.pallas import tpu as pltpu

GRID_W = 64
NA_HEADS = 16
WIN_H = 8
WIN_W = 16
GM_GROUPS = 16
GM_CHUNK = 128
N_MOD = 6
N_MIXERS = 3
EPS = 1e-6

V7X_VMEM_BYTES = 64 * 1024 * 1024
V7X_SUBLANES_F32 = 8
V7X_SUBLANES_BF16 = 16
V7X_LANES = 128
V7X_MXU_DIM = 256

F32 = jnp.float32
BF16 = jnp.bfloat16

NA_QROWS = 2
COND_ROWS = V7X_SUBLANES_F32


def _compiler_params(semantics, vmem_bytes):
    return pltpu.CompilerParams(dimension_semantics=semantics,
                                vmem_limit_bytes=int(min(vmem_bytes, VMEM_LIMIT_BYTES)))


def _tile(n, pref):
    t = min(n, pref)
    while n % t:
        t -= 1
    return t


VMEM_LIMIT_BYTES = V7X_VMEM_BYTES - (4 << 20)
VMEM_TEMP_BYTES = 6 << 20


def _row_tile_buffers(tile_bytes, other_bytes):
    for x_bufs, out_bufs in ((2, 2), (1, 2), (1, 1)):
        if (x_bufs + out_bufs) * tile_bytes + other_bytes + VMEM_TEMP_BYTES <= VMEM_LIMIT_BYTES:
            break
    return x_bufs, out_bufs


def _norm_mod(x, g, shift, scale):
    ms = jnp.mean(x * x, axis=-1, keepdims=True)
    y = x * lax.rsqrt(ms + EPS) * g
    return y * (1.0 + scale) + shift


NORM_CHUNK = 32


def _store_norm_mod(hs_ref, sources, g, shift, scale):
    bounds = np.cumsum([0] + [ref.shape[0] for ref, _ in sources])
    total = int(bounds[-1])
    assert hs_ref.shape[0] == total
    for c0 in range(0, total, NORM_CHUNK):
        c1 = min(c0 + NORM_CHUNK, total)
        pieces = []
        for (ref, zero_pred), b0, b1 in zip(sources, bounds[:-1], bounds[1:]):
            lo, hi = max(c0, int(b0)), min(c1, int(b1))
            if lo >= hi:
                continue
            h = _norm_mod(ref[lo - int(b0):hi - int(b0), :], g, shift, scale)
            pieces.append(h if zero_pred is None else jnp.where(zero_pred, 0.0, h))
        h = pieces[0] if len(pieces) == 1 else jnp.concatenate(pieces, axis=0)
        hs_ref[c0:c1, :] = h.astype(BF16)


def _adaln_kernel(c_ref, w_ref, b_ref, o_ref):
    c = c_ref[...]
    a = (c * jax.nn.sigmoid(c)).astype(BF16)
    o_ref[...] = jnp.dot(a, w_ref[...].astype(BF16), preferred_element_type=F32) + b_ref[...]


def _adaln(cond, w_ada, b_ada):
    depth, d, n = w_ada.shape
    tn = _tile(n, 1024)
    return pl.pallas_call(
        _adaln_kernel,
        grid=(depth, n // tn),
        in_specs=[
            pl.BlockSpec((COND_ROWS, d), lambda l, j: (0, 0)),
            pl.BlockSpec((None, d, tn), lambda l, j: (l, 0, j)),
            pl.BlockSpec((None, 1, tn), lambda l, j: (l, 0, j)),
        ],
        out_specs=pl.BlockSpec((None, COND_ROWS, tn), lambda l, j: (l, 0, j)),
        out_shape=jax.ShapeDtypeStruct((depth, COND_ROWS, n), F32),
        compiler_params=_compiler_params(("arbitrary", "arbitrary"), 2 * d * tn * 4 + (8 << 20)),
        name="adaln",
    )(cond, w_ada, b_ada.reshape(depth, 1, n))


def _mod_spec(layer, row_fn, width, col_fn=None):
    if col_fn is None:
        return pl.BlockSpec((None, None, N_MOD, width), lambda i, j: (layer, row_fn(i), 0, 0))
    return pl.BlockSpec((None, None, N_MOD, width), lambda i, j: (layer, row_fn(i), 0, col_fn(j)))


SUB_ROWS = 256


def _skewed_subtiles(n_sub, first, second):
    nxt = first(0)
    for r in range(n_sub):
        cur = nxt
        if r + 1 < n_sub:
            nxt = first(r + 1)
        second(r, cur)


def _proj_res_kernel(a_ref, w_ref, x_ref, mod_ref, o_ref, *, gate_row, sub_rows):
    gate = mod_ref[gate_row:gate_row + 1, :]

    def matmul(r):
        return jnp.dot(a_ref[r * sub_rows:(r + 1) * sub_rows, :], w_ref[...], preferred_element_type=F32)

    def residual(r, y):
        rows = slice(r * sub_rows, (r + 1) * sub_rows)
        o_ref[rows, :] = x_ref[rows, :] + gate * y

    _skewed_subtiles(o_ref.shape[0] // sub_rows, matmul, residual)


def _proj_res(a, w, x, mods, layer, row_fn, gate_row, tm_pref=1024, tn_pref=1024):
    m, k = a.shape
    n = w.shape[1]
    tm, tn = _tile(m, tm_pref), _tile(n, tn_pref)
    vmem = 2 * (tm * k * 2 + k * tn * 2 + 2 * tm * tn * 4) + (8 << 20)
    return pl.pallas_call(
        functools.partial(_proj_res_kernel, gate_row=gate_row, sub_rows=_tile(tm, SUB_ROWS)),
        grid=(m // tm, n // tn),
        in_specs=[
            pl.BlockSpec((tm, k), lambda i, j: (i, 0)),
            pl.BlockSpec((k, tn), lambda i, j: (0, j)),
            pl.BlockSpec((tm, tn), lambda i, j: (i, j)),
            _mod_spec(layer, lambda i: row_fn(i, tm), tn, lambda j: j),
        ],
        out_specs=pl.BlockSpec((tm, tn), lambda i, j: (i, j)),
        out_shape=jax.ShapeDtypeStruct((m, n), F32),
        compiler_params=_compiler_params(("parallel", "arbitrary"), vmem),
        name="proj_res",
    )(a, w, x, mods)


def _qkv_kernel(x_ref, g_ref, mod_ref, w_ref, hg_ref, o_ref, hs_ref, *, n_norm_tiles, head_dim, sub_rows):
    j = pl.program_id(1)

    @pl.when(j == 0)
    def _():
        _store_norm_mod(hs_ref, [(x_ref, None)], g_ref[...], mod_ref[0:1, :], mod_ref[1:2, :])

    tm, tn = o_ref.shape
    is_norm_tile = j < n_norm_tiles
    hg = hg_ref[...]

    def matmul(r):
        return jnp.dot(hs_ref[r * sub_rows:(r + 1) * sub_rows, :], w_ref[...], preferred_element_type=F32)

    def head_norm(r, y):
        rows = slice(r * sub_rows, (r + 1) * sub_rows)
        for hh in range(tn // head_dim):
            sl = slice(hh * head_dim, (hh + 1) * head_dim)
            t = y[:, sl]
            inv = lax.rsqrt(jnp.mean(t * t, axis=-1, keepdims=True) + EPS)
            o_ref[rows, sl] = (t * jnp.where(is_norm_tile, inv, 1.0) * hg[:, sl]).astype(o_ref.dtype)

    _skewed_subtiles(tm // sub_rows, matmul, head_norm)


def _qkv(x, norm_g, mods, layer, row_fn, w_qkv, head_gain, head_dim, tm_pref=1024, tn_pref=1024):
    m, d = x.shape
    n = w_qkv.shape[1]
    tm = _tile(m, tm_pref)
    tn = _tile(n // 3, tn_pref)
    assert tn % head_dim == 0
    vmem = 2 * (tm * d * 4 + d * tn * 2 + tm * tn * 2) + tm * d * 2 + 3 * tm * tn * 4 + (8 << 20)
    return pl.pallas_call(
        functools.partial(_qkv_kernel, n_norm_tiles=2 * (n // 3) // tn, head_dim=head_dim,
                          sub_rows=_tile(tm, SUB_ROWS)),
        grid=(m // tm, n // tn),
        in_specs=[
            pl.BlockSpec((tm, d), lambda i, j: (i, 0)),
            pl.BlockSpec((1, d), lambda i, j: (0, 0)),
            _mod_spec(layer, lambda i: row_fn(i, tm), d),
            pl.BlockSpec((d, tn), lambda i, j: (0, j)),
            pl.BlockSpec((1, tn), lambda i, j: (0, j)),
        ],
        out_specs=pl.BlockSpec((tm, tn), lambda i, j: (i, j)),
        out_shape=jax.ShapeDtypeStruct((m, n), BF16),
        scratch_shapes=[pltpu.VMEM((tm, d), BF16)],
        compiler_params=_compiler_params(("parallel", "arbitrary"), vmem),
        name="qkv",
    )(x, norm_g, mods, w_qkv, head_gain)


LOG2_E = math.log2(math.e)
_NT_DIMS = (((1,), (1,)), ((), ()))


def _na_kernel(*refs, nb, heads, head_dim):
    refs = refs[2:]
    q_ref = refs[0]
    k_refs = refs[1:1 + nb]
    v_refs = refs[1 + nb:1 + 2 * nb]
    kc_ref, vc_ref, tbl_ref, o_ref = refs[1 + 2 * nb:]

    def scores(h):
        sl = slice(h * head_dim, (h + 1) * head_dim)
        q = q_ref[:, sl]
        k = jnp.concatenate([r[:, sl] for r in k_refs], axis=0)
        s = lax.dot_general(q, k, _NT_DIMS, preferred_element_type=F32) + tbl_ref[h]
        sc = lax.dot_general(q, kc_ref[:, sl], _NT_DIMS, preferred_element_type=F32)
        return s, sc

    def attend(h, s_sc):
        s, sc = s_sc
        sl = slice(h * head_dim, (h + 1) * head_dim)
        v = jnp.concatenate([r[:, sl] for r in v_refs], axis=0)
        m = jnp.maximum(jnp.max(s, axis=-1, keepdims=True), jnp.max(sc, axis=-1, keepdims=True))
        p = jnp.exp2(s - m)
        pc = jnp.exp2(sc - m)
        l = jnp.sum(p, axis=-1, keepdims=True) + jnp.sum(pc, axis=-1, keepdims=True)
        o = (jnp.dot(p.astype(BF16), v, preferred_element_type=F32)
             + jnp.dot(pc.astype(BF16), vc_ref[:, sl], preferred_element_type=F32))
        o_ref[:, sl] = (o * (1.0 / l)).astype(o_ref.dtype)

    _skewed_subtiles(heads, scores, attend)


def _na_geometry(rows, qrows):
    kh = min(WIN_H, rows)
    nb = min(kh + qrows - 1, rows)
    n_blk = rows // qrows
    band0s, keys, types, row_rel = [], {}, [], []
    for blk in range(n_blk):
        r0 = blk * qrows
        q_row = r0 + np.arange(qrows)
        r_start = np.clip(q_row - kh // 2, 0, rows - kh)
        band0 = min(int(np.clip(r0 - kh // 2, 0, rows - kh)), rows - nb)
        k_row = band0 + np.arange(nb)
        row_ok = (k_row[None, :] >= r_start[:, None]) & (k_row[None, :] < r_start[:, None] + kh)
        rel = np.where(row_ok, k_row[None, :] - q_row[:, None] + WIN_H - 1, -1)
        key = rel.tobytes()
        if key not in keys:
            keys[key] = len(row_rel)
            row_rel.append(rel)
        types.append(keys[key])
        band0s.append(band0)
    return nb, n_blk, np.array(band0s, np.int32), np.array(types, np.int32), row_rel


def _na_bias_table(rpb, row_rel):
    heads, n_dr, n_dc = rpb.shape
    period = 2 * GRID_W
    assert n_dc <= period
    r = jnp.pad(rpb, ((0, 0), (0, 0), (0, period - n_dc)))
    skew = jnp.tile(r, (1, 1, GRID_W))[:, :, :GRID_W * (period - 1)].reshape(heads, n_dr, GRID_W, period - 1)
    toep = skew[:, :, :, WIN_W - 1:WIN_W - 1 + GRID_W]
    q_col = np.arange(GRID_W)
    c_start = np.clip(q_col - WIN_W // 2, 0, GRID_W - WIN_W)
    col_ok = (q_col[None, :] >= c_start[:, None]) & (q_col[None, :] < c_start[:, None] + WIN_W)
    toep = jnp.where(col_ok, toep, -jnp.inf)
    masked = jnp.full((heads, GRID_W, GRID_W), -jnp.inf, F32)
    tables = []
    for rel in row_rel:
        q_blocks = []
        for qr in range(rel.shape[0]):
            q_blocks.append(jnp.concatenate(
                [toep[:, rel[qr, kr]] if rel[qr, kr] >= 0 else masked for kr in range(rel.shape[1])], axis=-1))
        tables.append(jnp.concatenate(q_blocks, axis=-2))
    return jnp.stack(tables).astype(F32)


def _na_attention(qkv, qkv_ctx, rpb, bn, rows, d, head_dim):
    qrows = min(NA_QROWS, rows)
    nb, n_blk, band0s, types, row_rel = _na_geometry(rows, qrows)
    qw, nk = qrows * GRID_W, nb * GRID_W
    ctx_len = qkv_ctx.shape[1]
    tbl = _na_bias_table(rpb * LOG2_E, row_rel)
    band_tbl = jnp.asarray(band0s)
    type_tbl = jnp.asarray(types)
    kv_view = qkv.reshape(bn * rows, GRID_W, 3 * d)

    def kv_spec(t, col):
        return pl.BlockSpec((None, GRID_W, d), lambda b, i, band, typ: (b * rows + band[i] + t, 0, col))

    in_specs = ([pl.BlockSpec((qw, d), lambda b, i, band, typ: (b * n_blk + i, 0))]
                + [kv_spec(t, 1) for t in range(nb)]
                + [kv_spec(t, 2) for t in range(nb)]
                + [pl.BlockSpec((None, ctx_len, d), lambda b, i, band, typ: (b, 0, 1)),
                   pl.BlockSpec((None, ctx_len, d), lambda b, i, band, typ: (b, 0, 2)),
                   pl.BlockSpec((None, NA_HEADS, qw, nk), lambda b, i, band, typ: (typ[i], 0, 0, 0))])
    vmem = 2 * (2 * qw * d * 2 + 2 * nk * d * 2 + 2 * ctx_len * d * 2 + NA_HEADS * qw * nk * 4) + (12 << 20)
    grid_spec = pltpu.PrefetchScalarGridSpec(
        num_scalar_prefetch=2,
        grid=(bn, n_blk),
        in_specs=in_specs,
        out_specs=pl.BlockSpec((qw, d), lambda b, i, band, typ: (b * n_blk + i, 0)),
    )
    return pl.pallas_call(
        functools.partial(_na_kernel, nb=nb, heads=NA_HEADS, head_dim=head_dim),
        grid_spec=grid_spec,
        out_shape=jax.ShapeDtypeStruct((bn * rows * GRID_W, d), BF16),
        compiler_params=_compiler_params(("parallel", "arbitrary"), vmem),
        name="na_attention",
    )(band_tbl, type_tbl, qkv, *([kv_view] * (2 * nb)), qkv_ctx, qkv_ctx, tbl)


def _ctx_attn_kernel(q_ref, k_ref, v_ref, o_ref, *, heads, head_dim):
    for h in range(heads):
        sl = slice(h * head_dim, (h + 1) * head_dim)
        s = lax.dot_general(q_ref[:, sl], k_ref[:, sl], _NT_DIMS, preferred_element_type=F32)
        m = jnp.max(s, axis=-1, keepdims=True)
        p = jnp.exp2(s - m)
        l = jnp.sum(p, axis=-1, keepdims=True)
        o = jnp.dot(p.astype(BF16), v_ref[:, sl], preferred_element_type=F32)
        o_ref[:, sl] = (o * (1.0 / l)).astype(o_ref.dtype)


def _ctx_attention(qkv_ctx, d, head_dim):
    bn, ctx_len, _ = qkv_ctx.shape
    spec = lambda col: pl.BlockSpec((None, ctx_len, d), lambda b: (b, 0, col))
    return pl.pallas_call(
        functools.partial(_ctx_attn_kernel, heads=NA_HEADS, head_dim=head_dim),
        grid=(bn,),
        in_specs=[spec(0), spec(1), spec(2)],
        out_specs=pl.BlockSpec((None, ctx_len, d), lambda b: (b, 0, 0)),
        out_shape=jax.ShapeDtypeStruct((bn, ctx_len, d), BF16),
        compiler_params=_compiler_params(("arbitrary",), 8 * ctx_len * d * 2 + (12 << 20)),
        name="ctx_attention",
    )(qkv_ctx, qkv_ctx, qkv_ctx)


HALO = V7X_SUBLANES_F32


def _shift_rows(u, tm):
    n = u.shape[0]
    prev = pltpu.roll(u, 1, 0)[HALO:HALO + tm]
    nxt = pltpu.roll(u, n - 1, 0)[HALO:HALO + tm]
    return prev, nxt


def _conv3(u, cw, rows):
    prev, nxt = _shift_rows(u, rows)
    return prev * cw[0:1] + u[HALO:HALO + rows] * cw[1:2] + nxt * cw[2:3]


def _gated_conv_kernel(*refs, kind, tiles_per_seq, mod_rows, sub_rows):
    x_ref, xp_ref, xn_ref, g_ref, mod_ref = refs[:5]
    if kind == "ffn":
        w1_refs, cw_refs, cb_refs = refs[5:7], refs[7:9], refs[9:11]
        w2_ref, o_ref, hs_ref = refs[11:]
    else:
        w1_refs, cw_refs = refs[5:8], refs[8:9]
        w2_ref, o_ref, hs_ref = refs[9:]
    i = pl.program_id(0)
    j = pl.program_id(1)
    tm = x_ref.shape[0]
    r_shift, r_scale, r_gate = mod_rows

    @pl.when(j == 0)
    def _():
        g = g_ref[...]
        shift = mod_ref[r_shift:r_shift + 1, :]
        scale = mod_ref[r_scale:r_scale + 1, :]
        pos = i % tiles_per_seq
        _store_norm_mod(hs_ref, [(xp_ref, pos == 0), (x_ref, None), (xn_ref, pos == tiles_per_seq - 1)],
                        g, shift, scale)
        o_ref[...] = x_ref[...]

    res_gate = mod_ref[r_gate:r_gate + 1, :]

    def first_matmuls(r):
        lhs = hs_ref[r * sub_rows:r * sub_rows + sub_rows + 2 * HALO, :]
        return [jnp.dot(lhs, w[...], preferred_element_type=F32) for w in w1_refs]

    def mid_and_second_matmul(r, u):
        if kind == "ffn":
            gate, up = [_conv3(u[p], cw_refs[p][...], sub_rows) + cb_refs[p][...] for p in range(2)]
            a = gate * jax.nn.sigmoid(gate) * up
        else:
            a = u[0][HALO:HALO + sub_rows] * _conv3(u[1] * u[2], cw_refs[0][...], sub_rows)
        o_ref[r * sub_rows:(r + 1) * sub_rows, :] += res_gate * jnp.dot(a.astype(BF16), w2_ref[...],
                                                                       preferred_element_type=F32)

    _skewed_subtiles(tm // sub_rows, first_matmuls, mid_and_second_matmul)


def _gated_conv(kind, x, seq_len, norm_g, mods, layer, row_fn, mod_rows, w1, cw, cb, w2, tc, tm_pref,
                sub_pref=SUB_ROWS):
    m, d = x.shape
    parts = 2 if kind == "ffn" else 3
    cparts = 2 if kind == "ffn" else 1
    n_chunks = w2.shape[0] // tc
    assert w1.shape[1] == parts * n_chunks * tc and cw.shape[1] == cparts * n_chunks * tc
    tm = _tile(seq_len, tm_pref)
    sub_rows = _tile(tm, sub_pref)
    assert sub_rows % V7X_SUBLANES_BF16 == 0
    tiles_per_seq = seq_len // tm
    hb = tm // HALO
    n_hblk = m // HALO
    other = 2 * (d * parts * tc * 2 + tc * d * 2) + (tm + 2 * HALO) * d * 2
    x_bufs, out_bufs = _row_tile_buffers(tm * d * 4, other)
    vmem = (x_bufs + out_bufs) * tm * d * 4 + other + VMEM_TEMP_BYTES
    col_spec = lambda rows, p: pl.BlockSpec((rows, tc), lambda i, j: (0, p * n_chunks + j))
    in_specs = [
        pl.BlockSpec((tm, d), lambda i, j: (i, 0), pipeline_mode=pl.Buffered(x_bufs)),
        pl.BlockSpec((HALO, d), lambda i, j: (jnp.maximum(i * hb - 1, 0), 0)),
        pl.BlockSpec((HALO, d), lambda i, j: (jnp.minimum((i + 1) * hb, n_hblk - 1), 0)),
        pl.BlockSpec((1, d), lambda i, j: (0, 0)),
        _mod_spec(layer, lambda i: row_fn(i, tm), d),
    ]
    in_specs += [col_spec(d, p) for p in range(parts)]
    in_specs += [col_spec(3, p) for p in range(cparts)]
    operands = [x, x, x, norm_g, mods] + [w1] * parts + [cw] * cparts
    if kind == "ffn":
        in_specs += [col_spec(1, p) for p in range(cparts)]
        operands += [cb] * cparts
    in_specs.append(pl.BlockSpec((tc, d), lambda i, j: (j, 0)))
    operands.append(w2)
    return pl.pallas_call(
        functools.partial(_gated_conv_kernel, kind=kind, tiles_per_seq=tiles_per_seq, mod_rows=mod_rows,
                          sub_rows=sub_rows),
        grid=(m // tm, n_chunks),
        in_specs=in_specs,
        out_specs=pl.BlockSpec((tm, d), lambda i, j: (i, 0), pipeline_mode=pl.Buffered(out_bufs)),
        out_shape=jax.ShapeDtypeStruct((m, d), F32),
        scratch_shapes=[pltpu.VMEM((tm + 2 * HALO, d), BF16)],
        compiler_params=_compiler_params(("parallel", "arbitrary"), vmem),
        name=kind,
    )(*operands)


def _gmlp_kernel(x_ref, g_ref, mod_ref, win_ref, vg_ref, ws_ref, bs_ref, wout_ref, o_ref,
                 hs_ref, v_ref, ssq_ref, *, n_chunks, group_dim, width, sub_rows):
    j = pl.program_id(1)
    tm = x_ref.shape[0]
    tc = win_ref.shape[1]
    n_sub = tm // sub_rows

    @pl.when(j == 0)
    def _():
        _store_norm_mod(hs_ref, [(x_ref, None)], g_ref[...], mod_ref[0:1, :], mod_ref[1:2, :])
        ssq_ref[...] = jnp.zeros_like(ssq_ref)
        o_ref[...] = x_ref[...]

    def in_matmul(r):
        return jnp.dot(hs_ref[r * sub_rows:(r + 1) * sub_rows, :], win_ref[...], preferred_element_type=F32)

    @pl.when(j < n_chunks)
    def _():
        def keep_v(r, y):
            rows = slice(r * sub_rows, (r + 1) * sub_rows)
            t = jax.nn.gelu(y)
            v_ref[j, rows, :] = t
            ssq_ref[rows, :] += jnp.sum(t * t, axis=-1, keepdims=True)

        _skewed_subtiles(n_sub, in_matmul, keep_v)

    @pl.when(j >= n_chunks)
    def _():
        jj = j - n_chunks
        vg = vg_ref[...]
        res_gate = mod_ref[2:3, :]
        bs = bs_ref[...]
        gpc = tc // group_dim
        ws = [ws_ref[jj * gpc + gi] for gi in range(gpc)]

        def gate_and_project(r, y):
            rows = slice(r * sub_rows, (r + 1) * sub_rows)
            inv = lax.rsqrt(ssq_ref[rows, :] * (1.0 / width) + EPS)
            vn = (v_ref[jj, rows, :] * inv * vg).astype(BF16)
            cols = []
            for gi in range(gpc):
                csl = slice(gi * group_dim, (gi + 1) * group_dim)
                cols.append(jnp.concatenate(
                    [jnp.dot(ws[gi], vn[c * GM_CHUNK:(c + 1) * GM_CHUNK, csl], preferred_element_type=F32)
                     + bs[:, csl] for c in range(sub_rows // GM_CHUNK)], axis=0))
            sv = jnp.concatenate(cols, axis=1)
            o_ref[rows, :] += res_gate * jnp.dot((jax.nn.gelu(y) * sv).astype(BF16), wout_ref[...],
                                                 preferred_element_type=F32)

        _skewed_subtiles(n_sub, in_matmul, gate_and_project)


def _gmlp(x, norm_g, mods, layer, row_fn, w_in, v_g, w_s, b_s_cols, w_out, tm_pref, tc=512):
    m, d = x.shape
    width = w_out.shape[0]
    group_dim = width // GM_GROUPS
    tc = max(group_dim, _tile(width, tc))
    n_chunks = width // tc
    tm = _tile(m, tm_pref)
    sub_rows = _tile(tm, SUB_ROWS)
    assert sub_rows % GM_CHUNK == 0
    other = (2 * (d * tc * 2 + tc * d * 2 + GM_CHUNK * tc * 4) + tm * d * 2 + tm * width * 4
             + GM_GROUPS * GM_CHUNK * GM_CHUNK * 2 + tm * V7X_LANES * 4)
    x_bufs, out_bufs = _row_tile_buffers(tm * d * 4, other)
    vmem = (x_bufs + out_bufs) * tm * d * 4 + other + VMEM_TEMP_BYTES
    return pl.pallas_call(
        functools.partial(_gmlp_kernel, n_chunks=n_chunks, group_dim=group_dim, width=width,
                          sub_rows=sub_rows),
        grid=(m // tm, 2 * n_chunks),
        in_specs=[
            pl.BlockSpec((tm, d), lambda i, j: (i, 0), pipeline_mode=pl.Buffered(x_bufs)),
            pl.BlockSpec((1, d), lambda i, j: (0, 0)),
            _mod_spec(layer, lambda i: row_fn(i, tm), d),
            pl.BlockSpec((d, tc), lambda i, j: (0, jnp.where(j < n_chunks, j + n_chunks, j - n_chunks))),
            pl.BlockSpec((1, tc), lambda i, j: (0, jnp.maximum(j - n_chunks, 0))),
            pl.BlockSpec((GM_GROUPS, GM_CHUNK, GM_CHUNK), lambda i, j: (0, 0, 0)),
            pl.BlockSpec((GM_CHUNK, tc), lambda i, j: (0, jnp.maximum(j - n_chunks, 0))),
            pl.BlockSpec((tc, d), lambda i, j: (jnp.maximum(j - n_chunks, 0), 0)),
        ],
        out_specs=pl.BlockSpec((tm, d), lambda i, j: (i, 0), pipeline_mode=pl.Buffered(out_bufs)),
        out_shape=jax.ShapeDtypeStruct((m, d), F32),
        scratch_shapes=[pltpu.VMEM((tm, d), BF16), pltpu.VMEM((n_chunks, tm, tc), F32), pltpu.VMEM((tm, 1), F32)],
        compiler_params=_compiler_params(("parallel", "arbitrary"), vmem),
        name="gmlp",
    )(x, norm_g, mods, w_in, v_g, w_s, b_s_cols, w_out)


CAST_BLOCK_BYTES = 6 << 20


def _cast_kernel(x_ref, o_ref, *, valid_row_blocks):
    rows, f = x_ref.shape
    rows_o, fp = o_ref.shape
    assert rows == rows_o
    if valid_row_blocks is None:
        o_ref[:, :f] = x_ref[...].astype(o_ref.dtype)
    else:
        o_ref[:, :f] = jnp.where(pl.program_id(0) < valid_row_blocks, x_ref[...], 0.0).astype(o_ref.dtype)
    if fp > f:
        o_ref[:, f:] = jnp.zeros((rows, fp - f), o_ref.dtype)


def _cast_weight(w, idx, parts=1, fp=None, rows_out=None, row_block=None):
    _, r, c = w.shape
    f = c // parts
    fp = f if fp is None else fp
    rows_out = r if rows_out is None else rows_out
    assert parts == 1 or f % V7X_LANES == 0
    if row_block is None:
        row_block = _tile(r, max(V7X_SUBLANES_BF16, CAST_BLOCK_BYTES // (4 * f) // V7X_SUBLANES_BF16
                                 * V7X_SUBLANES_BF16))
    assert r % row_block == 0 and rows_out % row_block == 0 and row_block % V7X_SUBLANES_BF16 == 0
    n_valid = r // row_block
    padded_rows = rows_out > r
    return pl.pallas_call(
        functools.partial(_cast_kernel, valid_row_blocks=n_valid if padded_rows else None),
        grid=(rows_out // row_block, parts),
        in_specs=[pl.BlockSpec((None, row_block, f), lambda i, p: (idx, jnp.minimum(i, n_valid - 1), p))],
        out_specs=pl.BlockSpec((row_block, fp), lambda i, p: (i, p)),
        out_shape=jax.ShapeDtypeStruct((rows_out, parts * fp), BF16),
        compiler_params=_compiler_params(("parallel", "arbitrary"),
                                         2 * row_block * (4 * f + 2 * fp) + (8 << 20)),
        name="cast_weight",
    )(w)


def _pad_halves(a, f, fp):
    pad = lambda h: jnp.pad(h, ((0, 0), (0, fp - f)))
    return jnp.concatenate([pad(a[:, :f]), pad(a[:, f:])], axis=1)


def _prep_ffn(w_up, conv_w, conv_b, w_down, layer, tc):
    f = w_down.shape[1]
    fp = -(-f // tc) * tc
    w1 = _cast_weight(w_up, layer, parts=2, fp=fp)
    cw = _pad_halves(conv_w[layer], f, fp)
    cb = _pad_halves(conv_b[layer][None, :], f, fp)
    w2 = _cast_weight(w_down, layer, rows_out=fp, row_block=math.gcd(f, fp))
    return w1, cw, cb, w2


def kernel(x, c, ctx, c_ctx, norm_mix_g, norm_ffn_g, w_ada, b_ada, na_w_qkv, na_q_g, na_k_g, na_rpb, na_w_o,
           gm_w_in, gm_v_g, gm_w_s, gm_b_s, gm_w_out, sc_w_in, sc_conv_w, sc_w_out,
           ffn_w_up, ffn_conv_w, ffn_conv_b, ffn_w_down):
    bn, seq, d = x.shape
    ctx_len = ctx.shape[1]
    depth = w_ada.shape[0]
    head_dim = d // NA_HEADS
    rows = seq // GRID_W
    assert bn + 1 <= COND_ROWS and seq % GRID_W == 0 and ctx_len % GM_CHUNK == 0

    cond = jnp.concatenate([c, c_ctx[None, :], jnp.zeros((COND_ROWS - bn - 1, d), F32)], axis=0)
    mods = _adaln(cond, w_ada, b_ada).reshape(depth, COND_ROWS, N_MOD, d)

    lat_row = lambda i, tm: i // (seq // tm)
    ctx_row = lambda i, tm: bn

    ffn_tc = 512 if ffn_w_down.shape[1] >= 512 else 128
    sc_tc = _tile(d, 512)
    lat_tm = 1024
    lat_sub = 512

    xl = x.reshape(bn * seq, d)
    xc = ctx.reshape(bn * ctx_len, d)
    for i in range(depth):
        last = i == depth - 1
        mixer, jx = i % N_MIXERS, i // N_MIXERS
        g_mix = norm_mix_g[i][None, :]
        g_ffn = norm_ffn_g[i][None, :]
        do_ctx = (not last) or mixer == 0
        if mixer == 0:
            w_qkv = _cast_weight(na_w_qkv, jx)
            w_o = _cast_weight(na_w_o, jx)
            q_gain = na_q_g[jx] * (head_dim ** -0.5 * LOG2_E)
            head_gain = jnp.concatenate([jnp.tile(q_gain, NA_HEADS), jnp.tile(na_k_g[jx], NA_HEADS),
                                         jnp.ones((d,), F32)])[None, :]
            qkv = _qkv(xl, g_mix, mods, i, lat_row, w_qkv, head_gain, head_dim)
            qkv_c = _qkv(xc, g_mix, mods, i, ctx_row, w_qkv, head_gain, head_dim).reshape(bn, ctx_len, 3 * d)
            att = _na_attention(qkv, qkv_c, na_rpb[jx], bn, rows, d, head_dim)
            xl_new = _proj_res(att, w_o, xl, mods, i, lat_row, 2)
            if not last:
                att_c = _ctx_attention(qkv_c, d, head_dim).reshape(bn * ctx_len, d)
                xc = _proj_res(att_c, w_o, xc, mods, i, ctx_row, 2)
            xl = xl_new
        elif mixer == 1:
            w_in = _cast_weight(gm_w_in, jx)
            w_out = _cast_weight(gm_w_out, jx)
            w_s = gm_w_s[jx].astype(BF16)
            width = w_out.shape[0]
            v_g = gm_v_g[jx][None, :]
            b_s_cols = jnp.repeat(gm_b_s[jx].T, width // GM_GROUPS, axis=1)
            xl = _gmlp(xl, g_mix, mods, i, lat_row, w_in, v_g, w_s, b_s_cols, w_out, lat_tm)
            if not last:
                xc = _gmlp(xc, g_mix, mods, i, ctx_row, w_in, v_g, w_s, b_s_cols, w_out, ctx_len)
        else:
            w1, cw, w2 = _cast_weight(sc_w_in, jx), sc_conv_w[jx], _cast_weight(sc_w_out, jx)
            xl = _gated_conv("sc", xl, seq, g_mix, mods, i, lat_row, (0, 1, 2), w1, cw, None, w2, sc_tc, lat_tm,
                             sub_pref=lat_sub)
            if not last:
                xc = _gated_conv("sc", xc, ctx_len, g_mix, mods, i, ctx_row, (0, 1, 2), w1, cw, None, w2, sc_tc,
                                 ctx_len)
        w1, cw, cb, w2 = _prep_ffn(ffn_w_up, ffn_conv_w, ffn_conv_b, ffn_w_down, i, ffn_tc)
        xl = _gated_conv("ffn", xl, seq, g_ffn, mods, i, lat_row, (3, 4, 5), w1, cw, cb, w2, ffn_tc, lat_tm,
                         sub_pref=lat_sub)
        if not last:
            xc = _gated_conv("ffn", xc, ctx_len, g_ffn, mods, i, ctx_row, (3, 4, 5), w1, cw, cb, w2, ffn_tc,
                             ctx_len)
    return xl.reshape(bn, seq, d)
```

```python
import functools
import math

import numpy as np
import jax
import jax.numpy as jnp
from jax import lax
from jax.experimental import pallas as pl
from jax.experimental.pallas import tpu as pltpu

GRID_W = 64
NA_HEADS = 16
WIN_H = 8
WIN_W = 16
GM_GROUPS = 16
GM_CHUNK = 128
N_MOD = 6
N_MIXERS = 3
EPS = 1e-6

V7X_VMEM_BYTES = 64 * 1024 * 1024
V7X_SUBLANES_F32 = 8
V7X_SUBLANES_BF16 = 16
V7X_LANES = 128
V7X_MXU_DIM = 256

F32 = jnp.float32
BF16 = jnp.bfloat16

NA_QROWS = 2
COND_ROWS = V7X_SUBLANES_F32


def _compiler_params(semantics, vmem_bytes):
    return pltpu.CompilerParams(dimension_semantics=semantics,
                                vmem_limit_bytes=int(min(vmem_bytes, VMEM_LIMIT_BYTES)))


def _tile(n, pref):
    t = min(n, pref)
    while n % t:
        t -= 1
    return t


VMEM_LIMIT_BYTES = V7X_VMEM_BYTES - (4 << 20)
VMEM_TEMP_BYTES = 6 << 20


def _row_tile_buffers(tile_bytes, other_bytes):
    for x_bufs, out_bufs in ((2, 2), (1, 2), (1, 1)):
        if (x_bufs + out_bufs) * tile_bytes + other_bytes + VMEM_TEMP_BYTES <= VMEM_LIMIT_BYTES:
            break
    return x_bufs, out_bufs


def _norm_mod(x, g, shift, scale):
    ms = jnp.mean(x * x, axis=-1, keepdims=True)
    y = x * lax.rsqrt(ms + EPS) * g
    return y * (1.0 + scale) + shift


NORM_CHUNK = 32


def _store_norm_mod(hs_ref, sources, g, shift, scale):
    bounds = np.cumsum([0] + [ref.shape[0] for ref, _ in sources])
    total = int(bounds[-1])
    assert hs_ref.shape[0] == total
    for c0 in range(0, total, NORM_CHUNK):
        c1 = min(c0 + NORM_CHUNK, total)
        pieces = []
        for (ref, zero_pred), b0, b1 in zip(sources, bounds[:-1], bounds[1:]):
            lo, hi = max(c0, int(b0)), min(c1, int(b1))
            if lo >= hi:
                continue
            h = _norm_mod(ref[lo - int(b0):hi - int(b0), :], g, shift, scale)
            pieces.append(h if zero_pred is None else jnp.where(zero_pred, 0.0, h))
        h = pieces[0] if len(pieces) == 1 else jnp.concatenate(pieces, axis=0)
        hs_ref[c0:c1, :] = h.astype(BF16)


def _adaln_kernel(c_ref, w_ref, b_ref, o_ref):
    c = c_ref[...]
    a = (c * jax.nn.sigmoid(c)).astype(BF16)
    o_ref[...] = jnp.dot(a, w_ref[...].astype(BF16), preferred_element_type=F32) + b_ref[...]


def _adaln(cond, w_ada, b_ada):
    depth, d, n = w_ada.shape
    tn = _tile(n, 1024)
    return pl.pallas_call(
        _adaln_kernel,
        grid=(depth, n // tn),
        in_specs=[
            pl.BlockSpec((COND_ROWS, d), lambda l, j: (0, 0)),
            pl.BlockSpec((None, d, tn), lambda l, j: (l, 0, j)),
            pl.BlockSpec((None, 1, tn), lambda l, j: (l, 0, j)),
        ],
        out_specs=pl.BlockSpec((None, COND_ROWS, tn), lambda l, j: (l, 0, j)),
        out_shape=jax.ShapeDtypeStruct((depth, COND_ROWS, n), F32),
        compiler_params=_compiler_params(("arbitrary", "arbitrary"), 2 * d * tn * 4 + (8 << 20)),
        name="adaln",
    )(cond, w_ada, b_ada.reshape(depth, 1, n))


def _mod_spec(layer, row_fn, width, col_fn=None):
    if col_fn is None:
        return pl.BlockSpec((None, None, N_MOD, width), lambda i, j: (layer, row_fn(i), 0, 0))
    return pl.BlockSpec((None, None, N_MOD, width), lambda i, j: (layer, row_fn(i), 0, col_fn(j)))


SUB_ROWS = 256


def _skewed_subtiles(n_sub, first, second, ahead=1):
    pending = [first(r) for r in range(min(ahead, n_sub))]
    for r in range(n_sub):
        if r + ahead < n_sub:
            pending.append(first(r + ahead))
        second(r, pending.pop(0))


def _proj_res_kernel(a_ref, w_ref, x_ref, mod_ref, o_ref, *, gate_row, sub_rows):
    gate = mod_ref[gate_row:gate_row + 1, :]

    def matmul(r):
        return jnp.dot(a_ref[r * sub_rows:(r + 1) * sub_rows, :], w_ref[...], preferred_element_type=F32)

    def residual(r, y):
        rows = slice(r * sub_rows, (r + 1) * sub_rows)
        o_ref[rows, :] = x_ref[rows, :] + gate * y

    _skewed_subtiles(o_ref.shape[0] // sub_rows, matmul, residual)


def _proj_res(a, w, x, mods, layer, row_fn, gate_row, tm_pref=1024):
    m, k = a.shape
    n_tiles, _, tn = w.shape
    n = n_tiles * tn
    tm = _tile(m, tm_pref)
    vmem = 2 * (tm * k * 2 + k * tn * 2 + 2 * tm * tn * 4) + (8 << 20)
    return pl.pallas_call(
        functools.partial(_proj_res_kernel, gate_row=gate_row, sub_rows=_tile(tm, SUB_ROWS)),
        grid=(m // tm, n // tn),
        in_specs=[
            pl.BlockSpec((tm, k), lambda i, j: (i, 0)),
            pl.BlockSpec((None, k, tn), lambda i, j: (j, 0, 0)),
            pl.BlockSpec((tm, tn), lambda i, j: (i, j)),
            _mod_spec(layer, lambda i: row_fn(i, tm), tn, lambda j: j),
        ],
        out_specs=pl.BlockSpec((tm, tn), lambda i, j: (i, j)),
        out_shape=jax.ShapeDtypeStruct((m, n), F32),
        compiler_params=_compiler_params(("parallel", "arbitrary"), vmem),
        name="proj_res",
    )(a, w, x, mods)


def _qkv_kernel(x_ref, g_ref, mod_ref, w_ref, hg_ref, o_ref, hs_ref, *, n_norm_tiles, head_dim, sub_rows):
    j = pl.program_id(1)

    @pl.when(j == 0)
    def _():
        _store_norm_mod(hs_ref, [(x_ref, None)], g_ref[...], mod_ref[0:1, :], mod_ref[1:2, :])

    tm, tn = o_ref.shape
    is_norm_tile = j < n_norm_tiles
    hg = hg_ref[...]

    def matmul(r):
        return jnp.dot(hs_ref[r * sub_rows:(r + 1) * sub_rows, :], w_ref[...], preferred_element_type=F32)

    def head_norm(r, y):
        rows = slice(r * sub_rows, (r + 1) * sub_rows)
        for hh in range(tn // head_dim):
            sl = slice(hh * head_dim, (hh + 1) * head_dim)
            t = y[:, sl]
            inv = lax.rsqrt(jnp.mean(t * t, axis=-1, keepdims=True) + EPS)
            o_ref[rows, sl] = (t * jnp.where(is_norm_tile, inv, 1.0) * hg[:, sl]).astype(o_ref.dtype)

    _skewed_subtiles(tm // sub_rows, matmul, head_norm)


def _qkv(x, norm_g, mods, layer, row_fn, w_qkv, head_gain, head_dim, tm_pref=1024):
    m, d = x.shape
    n_tiles, _, tn = w_qkv.shape
    n = n_tiles * tn
    tm = _tile(m, tm_pref)
    assert (n // 3) % tn == 0 and tn % head_dim == 0
    vmem = 2 * (tm * d * 4 + d * tn * 2 + tm * tn * 2) + tm * d * 2 + 3 * tm * tn * 4 + (8 << 20)
    return pl.pallas_call(
        functools.partial(_qkv_kernel, n_norm_tiles=2 * (n // 3) // tn, head_dim=head_dim,
                          sub_rows=_tile(tm, SUB_ROWS)),
        grid=(m // tm, n // tn),
        in_specs=[
            pl.BlockSpec((tm, d), lambda i, j: (i, 0)),
            pl.BlockSpec((1, d), lambda i, j: (0, 0)),
            _mod_spec(layer, lambda i: row_fn(i, tm), d),
            pl.BlockSpec((None, d, tn), lambda i, j: (j, 0, 0)),
            pl.BlockSpec((1, tn), lambda i, j: (0, j)),
        ],
        out_specs=pl.BlockSpec((tm, tn), lambda i, j: (i, j)),
        out_shape=jax.ShapeDtypeStruct((m, n), BF16),
        scratch_shapes=[pltpu.VMEM((tm, d), BF16)],
        compiler_params=_compiler_params(("parallel", "arbitrary"), vmem),
        name="qkv",
    )(x, norm_g, mods, w_qkv, head_gain)


LOG2_E = math.log2(math.e)
_NT_DIMS = (((1,), (1,)), ((), ()))


def _na_kernel(*refs, nb, heads, head_dim):
    refs = refs[2:]
    q_ref = refs[0]
    k_refs = refs[1:1 + nb]
    v_refs = refs[1 + nb:1 + 2 * nb]
    kc_ref, vc_ref, tbl_ref, o_ref = refs[1 + 2 * nb:]

    def scores(h):
        sl = slice(h * head_dim, (h + 1) * head_dim)
        q = q_ref[:, sl]
        k = jnp.concatenate([r[:, sl] for r in k_refs], axis=0)
        s = lax.dot_general(q, k, _NT_DIMS, preferred_element_type=F32) + tbl_ref[h]
        sc = lax.dot_general(q, kc_ref[:, sl], _NT_DIMS, preferred_element_type=F32)
        return s, sc

    def attend(h, s_sc):
        s, sc = s_sc
        sl = slice(h * head_dim, (h + 1) * head_dim)
        v = jnp.concatenate([r[:, sl] for r in v_refs], axis=0)
        m = jnp.maximum(jnp.max(s, axis=-1, keepdims=True), jnp.max(sc, axis=-1, keepdims=True))
        p = jnp.exp2(s - m)
        pc = jnp.exp2(sc - m)
        l = jnp.sum(p, axis=-1, keepdims=True) + jnp.sum(pc, axis=-1, keepdims=True)
        o = (jnp.dot(p.astype(BF16), v, preferred_element_type=F32)
             + jnp.dot(pc.astype(BF16), vc_ref[:, sl], preferred_element_type=F32))
        o_ref[:, sl] = (o * (1.0 / l)).astype(o_ref.dtype)

    _skewed_subtiles(heads, scores, attend, ahead=2)


def _na_geometry(rows, qrows):
    kh = min(WIN_H, rows)
    nb = min(kh + qrows - 1, rows)
    n_blk = rows // qrows
    band0s, keys, types, row_rel = [], {}, [], []
    for blk in range(n_blk):
        r0 = blk * qrows
        q_row = r0 + np.arange(qrows)
        r_start = np.clip(q_row - kh // 2, 0, rows - kh)
        band0 = min(int(np.clip(r0 - kh // 2, 0, rows - kh)), rows - nb)
        k_row = band0 + np.arange(nb)
        row_ok = (k_row[None, :] >= r_start[:, None]) & (k_row[None, :] < r_start[:, None] + kh)
        rel = np.where(row_ok, k_row[None, :] - q_row[:, None] + WIN_H - 1, -1)
        key = rel.tobytes()
        if key not in keys:
            keys[key] = len(row_rel)
            row_rel.append(rel)
        types.append(keys[key])
        band0s.append(band0)
    return nb, n_blk, np.array(band0s, np.int32), np.array(types, np.int32), row_rel


def _na_bias_table(rpb, row_rel):
    heads, n_dr, n_dc = rpb.shape
    period = 2 * GRID_W
    assert n_dc <= period
    r = jnp.pad(rpb, ((0, 0), (0, 0), (0, period - n_dc)))
    skew = jnp.tile(r, (1, 1, GRID_W))[:, :, :GRID_W * (period - 1)].reshape(heads, n_dr, GRID_W, period - 1)
    toep = skew[:, :, :, WIN_W - 1:WIN_W - 1 + GRID_W]
    q_col = np.arange(GRID_W)
    c_start = np.clip(q_col - WIN_W // 2, 0, GRID_W - WIN_W)
    col_ok = (q_col[None, :] >= c_start[:, None]) & (q_col[None, :] < c_start[:, None] + WIN_W)
    toep = jnp.where(col_ok, toep, -jnp.inf)
    masked = jnp.full((heads, GRID_W, GRID_W), -jnp.inf, F32)
    tables = []
    for rel in row_rel:
        q_blocks = []
        for qr in range(rel.shape[0]):
            q_blocks.append(jnp.concatenate(
                [toep[:, rel[qr, kr]] if rel[qr, kr] >= 0 else masked for kr in range(rel.shape[1])], axis=-1))
        tables.append(jnp.concatenate(q_blocks, axis=-2))
    return jnp.stack(tables).astype(F32)


def _na_attention(qkv, qkv_ctx, rpb, bn, rows, d, head_dim):
    qrows = min(NA_QROWS, rows)
    nb, n_blk, band0s, types, row_rel = _na_geometry(rows, qrows)
    qw, nk = qrows * GRID_W, nb * GRID_W
    ctx_len = qkv_ctx.shape[1]
    tbl = _na_bias_table(rpb * LOG2_E, row_rel)
    band_tbl = jnp.asarray(band0s)
    type_tbl = jnp.asarray(types)
    kv_view = qkv.reshape(bn * rows, GRID_W, 3 * d)

    def kv_spec(t, col):
        return pl.BlockSpec((None, GRID_W, d), lambda b, i, band, typ: (b * rows + band[i] + t, 0, col))

    in_specs = ([pl.BlockSpec((qw, d), lambda b, i, band, typ: (b * n_blk + i, 0))]
                + [kv_spec(t, 1) for t in range(nb)]
                + [kv_spec(t, 2) for t in range(nb)]
                + [pl.BlockSpec((None, ctx_len, d), lambda b, i, band, typ: (b, 0, 1)),
                   pl.BlockSpec((None, ctx_len, d), lambda b, i, band, typ: (b, 0, 2)),
                   pl.BlockSpec((None, NA_HEADS, qw, nk), lambda b, i, band, typ: (typ[i], 0, 0, 0))])
    vmem = 2 * (2 * qw * d * 2 + 2 * nk * d * 2 + 2 * ctx_len * d * 2 + NA_HEADS * qw * nk * 4) + (12 << 20)
    grid_spec = pltpu.PrefetchScalarGridSpec(
        num_scalar_prefetch=2,
        grid=(bn, n_blk),
        in_specs=in_specs,
        out_specs=pl.BlockSpec((qw, d), lambda b, i, band, typ: (b * n_blk + i, 0)),
    )
    return pl.pallas_call(
        functools.partial(_na_kernel, nb=nb, heads=NA_HEADS, head_dim=head_dim),
        grid_spec=grid_spec,
        out_shape=jax.ShapeDtypeStruct((bn * rows * GRID_W, d), BF16),
        compiler_params=_compiler_params(("parallel", "arbitrary"), vmem),
        name="na_attention",
    )(band_tbl, type_tbl, qkv, *([kv_view] * (2 * nb)), qkv_ctx, qkv_ctx, tbl)


def _ctx_attn_kernel(q_ref, k_ref, v_ref, o_ref, *, heads, head_dim):
    for h in range(heads):
        sl = slice(h * head_dim, (h + 1) * head_dim)
        s = lax.dot_general(q_ref[:, sl], k_ref[:, sl], _NT_DIMS, preferred_element_type=F32)
        m = jnp.max(s, axis=-1, keepdims=True)
        p = jnp.exp2(s - m)
        l = jnp.sum(p, axis=-1, keepdims=True)
        o = jnp.dot(p.astype(BF16), v_ref[:, sl], preferred_element_type=F32)
        o_ref[:, sl] = (o * (1.0 / l)).astype(o_ref.dtype)


def _ctx_attention(qkv_ctx, d, head_dim):
    bn, ctx_len, _ = qkv_ctx.shape
    spec = lambda col: pl.BlockSpec((None, ctx_len, d), lambda b: (b, 0, col))
    return pl.pallas_call(
        functools.partial(_ctx_attn_kernel, heads=NA_HEADS, head_dim=head_dim),
        grid=(bn,),
        in_specs=[spec(0), spec(1), spec(2)],
        out_specs=pl.BlockSpec((None, ctx_len, d), lambda b: (b, 0, 0)),
        out_shape=jax.ShapeDtypeStruct((bn, ctx_len, d), BF16),
        compiler_params=_compiler_params(("arbitrary",), 8 * ctx_len * d * 2 + (12 << 20)),
        name="ctx_attention",
    )(qkv_ctx, qkv_ctx, qkv_ctx)


HALO = V7X_SUBLANES_F32


def _shift_rows(u, tm):
    n = u.shape[0]
    prev = pltpu.roll(u, 1, 0)[HALO:HALO + tm]
    nxt = pltpu.roll(u, n - 1, 0)[HALO:HALO + tm]
    return prev, nxt


def _conv3(u, cw, rows):
    prev, nxt = _shift_rows(u, rows)
    return prev * cw[0:1] + u[HALO:HALO + rows] * cw[1:2] + nxt * cw[2:3]


def _gated_conv_kernel(*refs, kind, tiles_per_seq, mod_rows, sub_rows):
    x_ref, xp_ref, xn_ref, g_ref, mod_ref = refs[:5]
    if kind == "ffn":
        w1_refs, cw_refs, cb_refs = refs[5:7], refs[7:9], refs[9:11]
        w2_ref, o_ref, hs_ref = refs[11:]
    else:
        w1_refs, cw_refs = refs[5:8], refs[8:9]
        w2_ref, o_ref, hs_ref = refs[9:]
    i = pl.program_id(0)
    j = pl.program_id(1)
    tm = x_ref.shape[0]
    r_shift, r_scale, r_gate = mod_rows

    @pl.when(j == 0)
    def _():
        g = g_ref[...]
        shift = mod_ref[r_shift:r_shift + 1, :]
        scale = mod_ref[r_scale:r_scale + 1, :]
        pos = i % tiles_per_seq
        _store_norm_mod(hs_ref, [(xp_ref, pos == 0), (x_ref, None), (xn_ref, pos == tiles_per_seq - 1)],
                        g, shift, scale)
        o_ref[...] = x_ref[...]

    res_gate = mod_ref[r_gate:r_gate + 1, :]

    def first_matmuls(r):
        lhs = hs_ref[r * sub_rows:r * sub_rows + sub_rows + 2 * HALO, :]
        return [jnp.dot(lhs, w[...], preferred_element_type=F32) for w in w1_refs]

    def mid_and_second_matmul(r, u):
        if kind == "ffn":
            gate, up = [_conv3(u[p], cw_refs[p][...], sub_rows) + cb_refs[p][...] for p in range(2)]
            a = gate * jax.nn.sigmoid(gate) * up
        else:
            a = u[0][HALO:HALO + sub_rows] * _conv3(u[1] * u[2], cw_refs[0][...], sub_rows)
        o_ref[r * sub_rows:(r + 1) * sub_rows, :] += res_gate * jnp.dot(a.astype(BF16), w2_ref[...],
                                                                       preferred_element_type=F32)

    _skewed_subtiles(tm // sub_rows, first_matmuls, mid_and_second_matmul)


def _gated_conv(kind, x, seq_len, norm_g, mods, layer, row_fn, mod_rows, w1, cw, cb, w2, tc, tm_pref,
                sub_pref=SUB_ROWS):
    m, d = x.shape
    parts = 2 if kind == "ffn" else 3
    cparts = 2 if kind == "ffn" else 1
    n_chunks = w2.shape[0] // tc
    assert w1.shape == (parts * n_chunks, d, tc) and cw.shape[1] == cparts * n_chunks * tc
    tm = _tile(seq_len, tm_pref)
    sub_rows = _tile(tm, sub_pref)
    assert sub_rows % V7X_SUBLANES_BF16 == 0
    tiles_per_seq = seq_len // tm
    hb = tm // HALO
    n_hblk = m // HALO
    other = 2 * (d * parts * tc * 2 + tc * d * 2) + (tm + 2 * HALO) * d * 2
    x_bufs, out_bufs = _row_tile_buffers(tm * d * 4, other)
    vmem = (x_bufs + out_bufs) * tm * d * 4 + other + VMEM_TEMP_BYTES
    col_spec = lambda rows, p: pl.BlockSpec((rows, tc), lambda i, j: (0, p * n_chunks + j))
    in_specs = [
        pl.BlockSpec((tm, d), lambda i, j: (i, 0), pipeline_mode=pl.Buffered(x_bufs)),
        pl.BlockSpec((HALO, d), lambda i, j: (jnp.maximum(i * hb - 1, 0), 0)),
        pl.BlockSpec((HALO, d), lambda i, j: (jnp.minimum((i + 1) * hb, n_hblk - 1), 0)),
        pl.BlockSpec((1, d), lambda i, j: (0, 0)),
        _mod_spec(layer, lambda i: row_fn(i, tm), d),
    ]
    in_specs += [pl.BlockSpec((None, d, tc), functools.partial(lambda p, i, j: (p * n_chunks + j, 0, 0), p))
                 for p in range(parts)]
    in_specs += [col_spec(3, p) for p in range(cparts)]
    operands = [x, x, x, norm_g, mods] + [w1] * parts + [cw] * cparts
    if kind == "ffn":
        in_specs += [col_spec(1, p) for p in range(cparts)]
        operands += [cb] * cparts
    in_specs.append(pl.BlockSpec((tc, d), lambda i, j: (j, 0)))
    operands.append(w2)
    return pl.pallas_call(
        functools.partial(_gated_conv_kernel, kind=kind, tiles_per_seq=tiles_per_seq, mod_rows=mod_rows,
                          sub_rows=sub_rows),
        grid=(m // tm, n_chunks),
        in_specs=in_specs,
        out_specs=pl.BlockSpec((tm, d), lambda i, j: (i, 0), pipeline_mode=pl.Buffered(out_bufs)),
        out_shape=jax.ShapeDtypeStruct((m, d), F32),
        scratch_shapes=[pltpu.VMEM((tm + 2 * HALO, d), BF16)],
        compiler_params=_compiler_params(("parallel", "arbitrary"), vmem),
        name=kind,
    )(*operands)


def _gmlp_kernel(x_ref, g_ref, mod_ref, win_ref, vg_ref, ws_ref, bs_ref, wout_ref, o_ref,
                 hs_ref, v_ref, ssq_ref, *, n_chunks, group_dim, width, sub_rows):
    j = pl.program_id(1)
    tm = x_ref.shape[0]
    tc = win_ref.shape[1]
    n_sub = tm // sub_rows

    @pl.when(j == 0)
    def _():
        _store_norm_mod(hs_ref, [(x_ref, None)], g_ref[...], mod_ref[0:1, :], mod_ref[1:2, :])
        ssq_ref[...] = jnp.zeros_like(ssq_ref)
        o_ref[...] = x_ref[...]

    def in_matmul(r):
        return jnp.dot(hs_ref[r * sub_rows:(r + 1) * sub_rows, :], win_ref[...], preferred_element_type=F32)

    @pl.when(j < n_chunks)
    def _():
        def keep_v(r, y):
            rows = slice(r * sub_rows, (r + 1) * sub_rows)
            t = jax.nn.gelu(y)
            v_ref[j, rows, :] = t
            ssq_ref[rows, :] += jnp.sum(t * t, axis=-1, keepdims=True)

        _skewed_subtiles(n_sub, in_matmul, keep_v)

    @pl.when(j >= n_chunks)
    def _():
        jj = j - n_chunks
        vg = vg_ref[...]
        res_gate = mod_ref[2:3, :]
        bs = bs_ref[...]
        gpc = tc // group_dim
        ws = [ws_ref[jj * gpc + gi] for gi in range(gpc)]

        def gate_and_project(r, y):
            rows = slice(r * sub_rows, (r + 1) * sub_rows)
            inv = lax.rsqrt(ssq_ref[rows, :] * (1.0 / width) + EPS)
            vn = (v_ref[jj, rows, :] * inv * vg).astype(BF16)
            cols = []
            for gi in range(gpc):
                csl = slice(gi * group_dim, (gi + 1) * group_dim)
                cols.append(jnp.concatenate(
                    [jnp.dot(ws[gi], vn[c * GM_CHUNK:(c + 1) * GM_CHUNK, csl], preferred_element_type=F32)
                     + bs[:, csl] for c in range(sub_rows // GM_CHUNK)], axis=0))
            sv = jnp.concatenate(cols, axis=1)
            o_ref[rows, :] += res_gate * jnp.dot((jax.nn.gelu(y) * sv).astype(BF16), wout_ref[...],
                                                 preferred_element_type=F32)

        _skewed_subtiles(n_sub, in_matmul, gate_and_project)


def _gmlp(x, norm_g, mods, layer, row_fn, w_in, v_g, w_s, b_s_cols, w_out, tm_pref):
    m, d = x.shape
    width = w_out.shape[0]
    group_dim = width // GM_GROUPS
    tc = w_in.shape[2]
    n_chunks = width // tc
    assert w_in.shape[0] == 2 * n_chunks and tc % group_dim == 0
    tm = _tile(m, tm_pref)
    sub_rows = _tile(tm, SUB_ROWS)
    assert sub_rows % GM_CHUNK == 0
    other = (2 * (d * tc * 2 + tc * d * 2 + GM_CHUNK * tc * 4) + tm * d * 2 + tm * width * 4
             + GM_GROUPS * GM_CHUNK * GM_CHUNK * 2 + tm * V7X_LANES * 4)
    x_bufs, out_bufs = _row_tile_buffers(tm * d * 4, other)
    vmem = (x_bufs + out_bufs) * tm * d * 4 + other + VMEM_TEMP_BYTES
    return pl.pallas_call(
        functools.partial(_gmlp_kernel, n_chunks=n_chunks, group_dim=group_dim, width=width,
                          sub_rows=sub_rows),
        grid=(m // tm, 2 * n_chunks),
        in_specs=[
            pl.BlockSpec((tm, d), lambda i, j: (i, 0), pipeline_mode=pl.Buffered(x_bufs)),
            pl.BlockSpec((1, d), lambda i, j: (0, 0)),
            _mod_spec(layer, lambda i: row_fn(i, tm), d),
            pl.BlockSpec((None, d, tc),
                         lambda i, j: (jnp.where(j < n_chunks, j + n_chunks, j - n_chunks), 0, 0)),
            pl.BlockSpec((1, tc), lambda i, j: (0, jnp.maximum(j - n_chunks, 0))),
            pl.BlockSpec((GM_GROUPS, GM_CHUNK, GM_CHUNK), lambda i, j: (0, 0, 0)),
            pl.BlockSpec((GM_CHUNK, tc), lambda i, j: (0, jnp.maximum(j - n_chunks, 0))),
            pl.BlockSpec((tc, d), lambda i, j: (jnp.maximum(j - n_chunks, 0), 0)),
        ],
        out_specs=pl.BlockSpec((tm, d), lambda i, j: (i, 0), pipeline_mode=pl.Buffered(out_bufs)),
        out_shape=jax.ShapeDtypeStruct((m, d), F32),
        scratch_shapes=[pltpu.VMEM((tm, d), BF16), pltpu.VMEM((n_chunks, tm, tc), F32), pltpu.VMEM((tm, 1), F32)],
        compiler_params=_compiler_params(("parallel", "arbitrary"), vmem),
        name="gmlp",
    )(x, norm_g, mods, w_in, v_g, w_s, b_s_cols, w_out)


CAST_BLOCK_BYTES = 6 << 20


def _cast_kernel(x_ref, o_ref, *, valid_row_blocks):
    rows, f = x_ref.shape
    x = x_ref[...]
    if valid_row_blocks is not None:
        x = jnp.where(pl.program_id(0) < valid_row_blocks, x, 0.0)
    x = x.astype(o_ref.dtype)
    if len(o_ref.shape) == 2:
        o_ref[...] = x
        return
    n_chunks, _, tc = o_ref.shape
    for c in range(n_chunks):
        width = max(0, min(tc, f - c * tc))
        if width:
            o_ref[c, :, :width] = x[:, c * tc:c * tc + width]
        if width < tc:
            o_ref[c, :, width:] = jnp.zeros((rows, tc - width), o_ref.dtype)


def _cast_weight(w, idx, parts=1, fp=None, col_chunk=None, rows_out=None, row_block=None):
    _, r, c = w.shape
    f = c // parts
    fp = f if fp is None else fp
    rows_out = r if rows_out is None else rows_out
    assert parts == 1 or f % V7X_LANES == 0
    assert col_chunk is not None or (parts == 1 and fp == f)
    if row_block is None:
        row_block = _tile(r, max(V7X_SUBLANES_BF16, CAST_BLOCK_BYTES // (4 * f) // V7X_SUBLANES_BF16
                                 * V7X_SUBLANES_BF16))
    assert r % row_block == 0 and rows_out % row_block == 0 and row_block % V7X_SUBLANES_BF16 == 0
    n_valid = r // row_block
    padded_rows = rows_out > r
    if col_chunk is None:
        out_spec = pl.BlockSpec((row_block, f), lambda i, p: (i, 0))
        out_shape = (rows_out, f)
    else:
        assert fp % col_chunk == 0 and col_chunk % V7X_LANES == 0 and not padded_rows
        n_chunks = fp // col_chunk
        out_spec = pl.BlockSpec((n_chunks, row_block, col_chunk), lambda i, p: (p, i, 0))
        out_shape = (parts * n_chunks, r, col_chunk)
    return pl.pallas_call(
        functools.partial(_cast_kernel, valid_row_blocks=n_valid if padded_rows else None),
        grid=(rows_out // row_block, parts),
        in_specs=[pl.BlockSpec((None, row_block, f), lambda i, p: (idx, jnp.minimum(i, n_valid - 1), p))],
        out_specs=out_spec,
        out_shape=jax.ShapeDtypeStruct(out_shape, BF16),
        compiler_params=_compiler_params(("parallel", "arbitrary"),
                                         2 * row_block * (4 * f + 2 * fp) + (8 << 20)),
        name="cast_weight",
    )(w)


def _pad_halves(a, f, fp):
    pad = lambda h: jnp.pad(h, ((0, 0), (0, fp - f)))
    return jnp.concatenate([pad(a[:, :f]), pad(a[:, f:])], axis=1)


def _prep_ffn(w_up, conv_w, conv_b, w_down, layer, tc):
    f = w_down.shape[1]
    fp = -(-f // tc) * tc
    w1 = _cast_weight(w_up, layer, parts=2, fp=fp, col_chunk=tc)
    cw = _pad_halves(conv_w[layer], f, fp)
    cb = _pad_halves(conv_b[layer][None, :], f, fp)
    w2 = _cast_weight(w_down, layer, rows_out=fp, row_block=math.gcd(f, fp))
    return w1, cw, cb, w2


def kernel(x, c, ctx, c_ctx, norm_mix_g, norm_ffn_g, w_ada, b_ada, na_w_qkv, na_q_g, na_k_g, na_rpb, na_w_o,
           gm_w_in, gm_v_g, gm_w_s, gm_b_s, gm_w_out, sc_w_in, sc_conv_w, sc_w_out,
           ffn_w_up, ffn_conv_w, ffn_conv_b, ffn_w_down):
    bn, seq, d = x.shape
    ctx_len = ctx.shape[1]
    depth = w_ada.shape[0]
    head_dim = d // NA_HEADS
    rows = seq // GRID_W
    assert bn + 1 <= COND_ROWS and seq % GRID_W == 0 and ctx_len % GM_CHUNK == 0

    cond = jnp.concatenate([c, c_ctx[None, :], jnp.zeros((COND_ROWS - bn - 1, d), F32)], axis=0)
    mods = _adaln(cond, w_ada, b_ada).reshape(depth, COND_ROWS, N_MOD, d)

    lat_row = lambda i, tm: i // (seq // tm)
    ctx_row = lambda i, tm: bn

    ffn_tc = 512 if ffn_w_down.shape[1] >= 512 else 128
    sc_tc = _tile(d, 512)
    gm_width = gm_w_out.shape[1]
    gm_tc = max(gm_width // GM_GROUPS, _tile(gm_width, 512))
    proj_tn = _tile(d, 1024)
    lat_tm = 1024
    lat_sub = 512

    xl = x.reshape(bn * seq, d)
    xc = ctx.reshape(bn * ctx_len, d)
    for i in range(depth):
        last = i == depth - 1
        mixer, jx = i % N_MIXERS, i // N_MIXERS
        g_mix = norm_mix_g[i][None, :]
        g_ffn = norm_ffn_g[i][None, :]
        do_ctx = (not last) or mixer == 0
        if mixer == 0:
            w_qkv = _cast_weight(na_w_qkv, jx, col_chunk=proj_tn)
            w_o = _cast_weight(na_w_o, jx, col_chunk=proj_tn)
            q_gain = na_q_g[jx] * (head_dim ** -0.5 * LOG2_E)
            head_gain = jnp.concatenate([jnp.tile(q_gain, NA_HEADS), jnp.tile(na_k_g[jx], NA_HEADS),
                                         jnp.ones((d,), F32)])[None, :]
            qkv = _qkv(xl, g_mix, mods, i, lat_row, w_qkv, head_gain, head_dim)
            qkv_c = _qkv(xc, g_mix, mods, i, ctx_row, w_qkv, head_gain, head_dim).reshape(bn, ctx_len, 3 * d)
            att = _na_attention(qkv, qkv_c, na_rpb[jx], bn, rows, d, head_dim)
            xl_new = _proj_res(att, w_o, xl, mods, i, lat_row, 2)
            if not last:
                att_c = _ctx_attention(qkv_c, d, head_dim).reshape(bn * ctx_len, d)
                xc = _proj_res(att_c, w_o, xc, mods, i, ctx_row, 2)
            xl = xl_new
        elif mixer == 1:
            w_in = _cast_weight(gm_w_in, jx, col_chunk=gm_tc)
            w_out = _cast_weight(gm_w_out, jx)
            w_s = gm_w_s[jx].astype(BF16)
            width = w_out.shape[0]
            v_g = gm_v_g[jx][None, :]
            b_s_cols = jnp.repeat(gm_b_s[jx].T, width // GM_GROUPS, axis=1)
            xl = _gmlp(xl, g_mix, mods, i, lat_row, w_in, v_g, w_s, b_s_cols, w_out, lat_tm)
            if not last:
                xc = _gmlp(xc, g_mix, mods, i, ctx_row, w_in, v_g, w_s, b_s_cols, w_out, ctx_len)
        else:
            w1 = _cast_weight(sc_w_in, jx, parts=3, col_chunk=sc_tc)
            cw, w2 = sc_conv_w[jx], _cast_weight(sc_w_out, jx)
            xl = _gated_conv("sc", xl, seq, g_mix, mods, i, lat_row, (0, 1, 2), w1, cw, None, w2, sc_tc, lat_tm,
                             sub_pref=lat_sub)
            if not last:
                xc = _gated_conv("sc", xc, ctx_len, g_mix, mods, i, ctx_row, (0, 1, 2), w1, cw, None, w2, sc_tc,
                                 ctx_len)
        w1, cw, cb, w2 = _prep_ffn(ffn_w_up, ffn_conv_w, ffn_conv_b, ffn_w_down, i, ffn_tc)
        xl = _gated_conv("ffn", xl, seq, g_ffn, mods, i, lat_row, (3, 4, 5), w1, cw, cb, w2, ffn_tc, lat_tm,
                         sub_pref=lat_sub)
        if not last:
            xc = _gated_conv("ffn", xc, ctx_len, g_ffn, mods, i, ctx_row, (3, 4, 5), w1, cw, cb, w2, ffn_tc,
                             ctx_len)
    return xl.reshape(bn, seq, d)
```

```python
import functools
import math

import numpy as np
import jax
import jax.numpy as jnp
from jax import lax
from jax.experimental import pallas as pl
from jax.experimental.pallas import tpu as pltpu

GRID_W = 64
NA_HEADS = 16
WIN_H = 8
WIN_W = 16
GM_GROUPS = 16
GM_CHUNK = 128
N_MOD = 6
N_MIXERS = 3
EPS = 1e-6

V7X_VMEM_BYTES = 64 * 1024 * 1024
V7X_SUBLANES_F32 = 8
V7X_SUBLANES_BF16 = 16
V7X_LANES = 128
V7X_MXU_DIM = 256

F32 = jnp.float32
BF16 = jnp.bfloat16

NA_QROWS = 2
COND_ROWS = V7X_SUBLANES_F32


def _compiler_params(semantics, vmem_bytes):
    return pltpu.CompilerParams(dimension_semantics=semantics,
                                vmem_limit_bytes=int(min(vmem_bytes, VMEM_LIMIT_BYTES)))


def _tile(n, pref):
    t = min(n, pref)
    while n % t:
        t -= 1
    return t


VMEM_LIMIT_BYTES = V7X_VMEM_BYTES - (4 << 20)
VMEM_TEMP_BYTES = 6 << 20


def _row_tile_buffers(tile_bytes, other_bytes):
    for x_bufs, out_bufs in ((2, 2), (1, 2), (1, 1)):
        if (x_bufs + out_bufs) * tile_bytes + other_bytes + VMEM_TEMP_BYTES <= VMEM_LIMIT_BYTES:
            break
    return x_bufs, out_bufs


def _norm_mod(x, gain, shift):
    ms = jnp.mean(x * x, axis=-1, keepdims=True)
    return x * lax.rsqrt(ms + EPS) * gain + shift


NORM_CHUNK = 32


def _store_norm_mod(hs_ref, sources, g, shift, scale):
    bounds = np.cumsum([0] + [ref.shape[0] for ref, _ in sources])
    total = int(bounds[-1])
    assert hs_ref.shape[0] == total
    gain = g * (1.0 + scale)
    for c0 in range(0, total, NORM_CHUNK):
        c1 = min(c0 + NORM_CHUNK, total)
        pieces = []
        for (ref, zero_pred), b0, b1 in zip(sources, bounds[:-1], bounds[1:]):
            lo, hi = max(c0, int(b0)), min(c1, int(b1))
            if lo >= hi:
                continue
            h = _norm_mod(ref[lo - int(b0):hi - int(b0), :], gain, shift)
            pieces.append(h if zero_pred is None else jnp.where(zero_pred, 0.0, h))
        h = pieces[0] if len(pieces) == 1 else jnp.concatenate(pieces, axis=0)
        hs_ref[c0:c1, :] = h.astype(BF16)


def _adaln_kernel(c_ref, w_ref, b_ref, o_ref):
    c = c_ref[...]
    a = (c * jax.nn.sigmoid(c)).astype(BF16)
    o_ref[...] = jnp.dot(a, w_ref[...].astype(BF16), preferred_element_type=F32) + b_ref[...]


def _adaln(cond, w_ada, b_ada):
    depth, d, n = w_ada.shape
    tn = _tile(n, 1024)
    return pl.pallas_call(
        _adaln_kernel,
        grid=(depth, n // tn),
        in_specs=[
            pl.BlockSpec((COND_ROWS, d), lambda l, j: (0, 0)),
            pl.BlockSpec((None, d, tn), lambda l, j: (l, 0, j)),
            pl.BlockSpec((None, 1, tn), lambda l, j: (l, 0, j)),
        ],
        out_specs=pl.BlockSpec((None, COND_ROWS, tn), lambda l, j: (l, 0, j)),
        out_shape=jax.ShapeDtypeStruct((depth, COND_ROWS, n), F32),
        compiler_params=_compiler_params(("arbitrary", "arbitrary"), 2 * d * tn * 4 + (8 << 20)),
        name="adaln",
    )(cond, w_ada, b_ada.reshape(depth, 1, n))


def _mod_spec(layer, row_fn, width, col_fn=None):
    if col_fn is None:
        return pl.BlockSpec((None, None, N_MOD, width), lambda i, j: (layer, row_fn(i), 0, 0))
    return pl.BlockSpec((None, None, N_MOD, width), lambda i, j: (layer, row_fn(i), 0, col_fn(j)))


SUB_ROWS = 256


def _skewed_subtiles(n_sub, first, second, ahead=1):
    pending = [first(r) for r in range(min(ahead, n_sub))]
    for r in range(n_sub):
        if r + ahead < n_sub:
            pending.append(first(r + ahead))
        second(r, pending.pop(0))


def _proj_res_kernel(a_ref, w_ref, x_ref, mod_ref, o_ref, *, gate_row, sub_rows):
    gate = mod_ref[gate_row:gate_row + 1, :]

    def matmul(r):
        return jnp.dot(a_ref[r * sub_rows:(r + 1) * sub_rows, :], w_ref[...], preferred_element_type=F32)

    def residual(r, y):
        rows = slice(r * sub_rows, (r + 1) * sub_rows)
        o_ref[rows, :] = x_ref[rows, :] + gate * y

    _skewed_subtiles(o_ref.shape[0] // sub_rows, matmul, residual)


def _proj_res(a, w, x, mods, layer, row_fn, gate_row, tm_pref=1024):
    m, k = a.shape
    n_tiles, _, tn = w.shape
    n = n_tiles * tn
    tm = _tile(m, tm_pref)
    vmem = 2 * (tm * k * 2 + k * tn * 2 + 2 * tm * tn * 4) + (8 << 20)
    return pl.pallas_call(
        functools.partial(_proj_res_kernel, gate_row=gate_row, sub_rows=_tile(tm, SUB_ROWS)),
        grid=(m // tm, n // tn),
        in_specs=[
            pl.BlockSpec((tm, k), lambda i, j: (i, 0)),
            pl.BlockSpec((None, k, tn), lambda i, j: (j, 0, 0)),
            pl.BlockSpec((tm, tn), lambda i, j: (i, j)),
            _mod_spec(layer, lambda i: row_fn(i, tm), tn, lambda j: j),
        ],
        out_specs=pl.BlockSpec((tm, tn), lambda i, j: (i, j)),
        out_shape=jax.ShapeDtypeStruct((m, n), F32),
        compiler_params=_compiler_params(("parallel", "arbitrary"), vmem),
        name="proj_res",
    )(a, w, x, mods)


def _qkv_kernel(x_ref, g_ref, mod_ref, w_ref, hg_ref, o_ref, hs_ref, *, n_norm_tiles, head_dim, sub_rows):
    j = pl.program_id(1)

    @pl.when(j == 0)
    def _():
        _store_norm_mod(hs_ref, [(x_ref, None)], g_ref[...], mod_ref[0:1, :], mod_ref[1:2, :])

    tm, tn = o_ref.shape
    is_norm_tile = j < n_norm_tiles
    hg = hg_ref[...]

    def matmul(r):
        return jnp.dot(hs_ref[r * sub_rows:(r + 1) * sub_rows, :], w_ref[...], preferred_element_type=F32)

    def head_norm(r, y):
        rows = slice(r * sub_rows, (r + 1) * sub_rows)
        for hh in range(tn // head_dim):
            sl = slice(hh * head_dim, (hh + 1) * head_dim)
            t = y[:, sl]
            inv = lax.rsqrt(jnp.mean(t * t, axis=-1, keepdims=True) + EPS)
            o_ref[rows, sl] = (t * jnp.where(is_norm_tile, inv, 1.0) * hg[:, sl]).astype(o_ref.dtype)

    _skewed_subtiles(tm // sub_rows, matmul, head_norm)


def _qkv(x, norm_g, mods, layer, row_fn, w_qkv, head_gain, head_dim, tm_pref=1024):
    m, d = x.shape
    n_tiles, _, tn = w_qkv.shape
    n = n_tiles * tn
    tm = _tile(m, tm_pref)
    assert (n // 3) % tn == 0 and tn % head_dim == 0
    vmem = 2 * (tm * d * 4 + d * tn * 2 + tm * tn * 2) + tm * d * 2 + 3 * tm * tn * 4 + (8 << 20)
    return pl.pallas_call(
        functools.partial(_qkv_kernel, n_norm_tiles=2 * (n // 3) // tn, head_dim=head_dim,
                          sub_rows=_tile(tm, SUB_ROWS)),
        grid=(m // tm, n // tn),
        in_specs=[
            pl.BlockSpec((tm, d), lambda i, j: (i, 0)),
            pl.BlockSpec((1, d), lambda i, j: (0, 0)),
            _mod_spec(layer, lambda i: row_fn(i, tm), d),
            pl.BlockSpec((None, d, tn), lambda i, j: (j, 0, 0)),
            pl.BlockSpec((1, tn), lambda i, j: (0, j)),
        ],
        out_specs=pl.BlockSpec((tm, tn), lambda i, j: (i, j)),
        out_shape=jax.ShapeDtypeStruct((m, n), BF16),
        scratch_shapes=[pltpu.VMEM((tm, d), BF16)],
        compiler_params=_compiler_params(("parallel", "arbitrary"), vmem),
        name="qkv",
    )(x, norm_g, mods, w_qkv, head_gain)


LOG2_E = math.log2(math.e)
_NT_DIMS = (((1,), (1,)), ((), ()))


def _na_kernel(*refs, nb, heads, head_dim):
    refs = refs[2:]
    q_ref = refs[0]
    k_refs = refs[1:1 + nb]
    v_refs = refs[1 + nb:1 + 2 * nb]
    kc_ref, vc_ref, tbl_ref, o_ref = refs[1 + 2 * nb:]

    def scores(h):
        sl = slice(h * head_dim, (h + 1) * head_dim)
        q = q_ref[:, sl]
        k = jnp.concatenate([r[:, sl] for r in k_refs], axis=0)
        s = lax.dot_general(q, k, _NT_DIMS, preferred_element_type=F32) + tbl_ref[h]
        sc = lax.dot_general(q, kc_ref[:, sl], _NT_DIMS, preferred_element_type=F32)
        return s, sc

    ones_lat = jnp.ones((nb * k_refs[0].shape[0], head_dim), BF16)
    ones_ctx = jnp.ones((vc_ref.shape[0], head_dim), BF16)

    def attend(h, s_sc):
        s, sc = s_sc
        sl = slice(h * head_dim, (h + 1) * head_dim)
        v = jnp.concatenate([jnp.concatenate([r[:, sl] for r in v_refs], axis=0), ones_lat], axis=1)
        vc = jnp.concatenate([vc_ref[:, sl], ones_ctx], axis=1)
        m = jnp.maximum(jnp.max(s, axis=-1, keepdims=True), jnp.max(sc, axis=-1, keepdims=True))
        p = jnp.exp2(s - m).astype(BF16)
        pc = jnp.exp2(sc - m).astype(BF16)
        o = (jnp.dot(p, v, preferred_element_type=F32) + jnp.dot(pc, vc, preferred_element_type=F32))
        o_ref[:, sl] = (o[:, :head_dim] * (1.0 / o[:, head_dim:head_dim + 1])).astype(o_ref.dtype)

    _skewed_subtiles(heads, scores, attend, ahead=2)


def _na_geometry(rows, qrows):
    kh = min(WIN_H, rows)
    nb = min(kh + qrows - 1, rows)
    n_blk = rows // qrows
    band0s, keys, types, row_rel = [], {}, [], []
    for blk in range(n_blk):
        r0 = blk * qrows
        q_row = r0 + np.arange(qrows)
        r_start = np.clip(q_row - kh // 2, 0, rows - kh)
        band0 = min(int(np.clip(r0 - kh // 2, 0, rows - kh)), rows - nb)
        k_row = band0 + np.arange(nb)
        row_ok = (k_row[None, :] >= r_start[:, None]) & (k_row[None, :] < r_start[:, None] + kh)
        rel = np.where(row_ok, k_row[None, :] - q_row[:, None] + WIN_H - 1, -1)
        key = rel.tobytes()
        if key not in keys:
            keys[key] = len(row_rel)
            row_rel.append(rel)
        types.append(keys[key])
        band0s.append(band0)
    return nb, n_blk, np.array(band0s, np.int32), np.array(types, np.int32), row_rel


def _na_bias_table(rpb, row_rel):
    heads, n_dr, n_dc = rpb.shape
    period = 2 * GRID_W
    assert n_dc <= period
    r = jnp.pad(rpb, ((0, 0), (0, 0), (0, period - n_dc)))
    skew = jnp.tile(r, (1, 1, GRID_W))[:, :, :GRID_W * (period - 1)].reshape(heads, n_dr, GRID_W, period - 1)
    toep = skew[:, :, :, WIN_W - 1:WIN_W - 1 + GRID_W]
    q_col = np.arange(GRID_W)
    c_start = np.clip(q_col - WIN_W // 2, 0, GRID_W - WIN_W)
    col_ok = (q_col[None, :] >= c_start[:, None]) & (q_col[None, :] < c_start[:, None] + WIN_W)
    toep = jnp.where(col_ok, toep, -jnp.inf)
    masked = jnp.full((heads, GRID_W, GRID_W), -jnp.inf, F32)
    tables = []
    for rel in row_rel:
        q_blocks = []
        for qr in range(rel.shape[0]):
            q_blocks.append(jnp.concatenate(
                [toep[:, rel[qr, kr]] if rel[qr, kr] >= 0 else masked for kr in range(rel.shape[1])], axis=-1))
        tables.append(jnp.concatenate(q_blocks, axis=-2))
    return jnp.stack(tables).astype(F32)


def _na_attention(qkv, qkv_ctx, rpb, bn, rows, d, head_dim):
    qrows = min(NA_QROWS, rows)
    nb, n_blk, band0s, types, row_rel = _na_geometry(rows, qrows)
    qw, nk = qrows * GRID_W, nb * GRID_W
    ctx_len = qkv_ctx.shape[1]
    tbl = _na_bias_table(rpb * LOG2_E, row_rel)
    band_tbl = jnp.asarray(band0s)
    type_tbl = jnp.asarray(types)
    kv_view = qkv.reshape(bn * rows, GRID_W, 3 * d)

    def kv_spec(t, col):
        return pl.BlockSpec((None, GRID_W, d), lambda b, i, band, typ: (b * rows + band[i] + t, 0, col))

    in_specs = ([pl.BlockSpec((qw, d), lambda b, i, band, typ: (b * n_blk + i, 0))]
                + [kv_spec(t, 1) for t in range(nb)]
                + [kv_spec(t, 2) for t in range(nb)]
                + [pl.BlockSpec((None, ctx_len, d), lambda b, i, band, typ: (b, 0, 1)),
                   pl.BlockSpec((None, ctx_len, d), lambda b, i, band, typ: (b, 0, 2)),
                   pl.BlockSpec((None, NA_HEADS, qw, nk), lambda b, i, band, typ: (typ[i], 0, 0, 0))])
    vmem = 2 * (2 * qw * d * 2 + 2 * nk * d * 2 + 2 * ctx_len * d * 2 + NA_HEADS * qw * nk * 4) + (12 << 20)
    grid_spec = pltpu.PrefetchScalarGridSpec(
        num_scalar_prefetch=2,
        grid=(bn, n_blk),
        in_specs=in_specs,
        out_specs=pl.BlockSpec((qw, d), lambda b, i, band, typ: (b * n_blk + i, 0)),
    )
    return pl.pallas_call(
        functools.partial(_na_kernel, nb=nb, heads=NA_HEADS, head_dim=head_dim),
        grid_spec=grid_spec,
        out_shape=jax.ShapeDtypeStruct((bn * rows * GRID_W, d), BF16),
        compiler_params=_compiler_params(("parallel", "arbitrary"), vmem),
        name="na_attention",
    )(band_tbl, type_tbl, qkv, *([kv_view] * (2 * nb)), qkv_ctx, qkv_ctx, tbl)


def _ctx_attn_kernel(q_ref, k_ref, v_ref, o_ref, *, heads, head_dim):
    for h in range(heads):
        sl = slice(h * head_dim, (h + 1) * head_dim)
        s = lax.dot_general(q_ref[:, sl], k_ref[:, sl], _NT_DIMS, preferred_element_type=F32)
        m = jnp.max(s, axis=-1, keepdims=True)
        p = jnp.exp2(s - m)
        l = jnp.sum(p, axis=-1, keepdims=True)
        o = jnp.dot(p.astype(BF16), v_ref[:, sl], preferred_element_type=F32)
        o_ref[:, sl] = (o * (1.0 / l)).astype(o_ref.dtype)


def _ctx_attention(qkv_ctx, d, head_dim):
    bn, ctx_len, _ = qkv_ctx.shape
    spec = lambda col: pl.BlockSpec((None, ctx_len, d), lambda b: (b, 0, col))
    return pl.pallas_call(
        functools.partial(_ctx_attn_kernel, heads=NA_HEADS, head_dim=head_dim),
        grid=(bn,),
        in_specs=[spec(0), spec(1), spec(2)],
        out_specs=pl.BlockSpec((None, ctx_len, d), lambda b: (b, 0, 0)),
        out_shape=jax.ShapeDtypeStruct((bn, ctx_len, d), BF16),
        compiler_params=_compiler_params(("arbitrary",), 8 * ctx_len * d * 2 + (12 << 20)),
        name="ctx_attention",
    )(qkv_ctx, qkv_ctx, qkv_ctx)


HALO = V7X_SUBLANES_F32


def _shift_rows(u, tm):
    n = u.shape[0]
    prev = pltpu.roll(u, 1, 0)[HALO:HALO + tm]
    nxt = pltpu.roll(u, n - 1, 0)[HALO:HALO + tm]
    return prev, nxt


def _conv3(u, cw, rows):
    prev, nxt = _shift_rows(u, rows)
    return prev * cw[0:1] + u[HALO:HALO + rows] * cw[1:2] + nxt * cw[2:3]


def _gated_conv_kernel(*refs, kind, tiles_per_seq, mod_rows, sub_rows):
    x_ref, xp_ref, xn_ref, g_ref, mod_ref = refs[:5]
    if kind == "ffn":
        w1_refs, cw_refs, cb_refs = refs[5:7], refs[7:9], refs[9:11]
        w2_ref, o_ref, hs_ref = refs[11:]
    else:
        w1_refs, cw_refs = refs[5:8], refs[8:9]
        w2_ref, o_ref, hs_ref = refs[9:]
    i = pl.program_id(0)
    j = pl.program_id(1)
    tm = x_ref.shape[0]
    r_shift, r_scale, r_gate = mod_rows

    @pl.when(j == 0)
    def _():
        g = g_ref[...]
        shift = mod_ref[r_shift:r_shift + 1, :]
        scale = mod_ref[r_scale:r_scale + 1, :]
        pos = i % tiles_per_seq
        _store_norm_mod(hs_ref, [(xp_ref, pos == 0), (x_ref, None), (xn_ref, pos == tiles_per_seq - 1)],
                        g, shift, scale)
        o_ref[...] = x_ref[...]

    res_gate = mod_ref[r_gate:r_gate + 1, :]

    def first_matmuls(r):
        lhs = hs_ref[r * sub_rows:r * sub_rows + sub_rows + 2 * HALO, :]
        return [jnp.dot(lhs, w[...], preferred_element_type=F32) for w in w1_refs]

    def mid_and_second_matmul(r, u):
        if kind == "ffn":
            gate, up = [_conv3(u[p], cw_refs[p][...], sub_rows) + cb_refs[p][...] for p in range(2)]
            a = gate * jax.nn.sigmoid(gate) * up
        else:
            a = u[0][HALO:HALO + sub_rows] * _conv3(u[1] * u[2], cw_refs[0][...], sub_rows)
        o_ref[r * sub_rows:(r + 1) * sub_rows, :] += res_gate * jnp.dot(a.astype(BF16), w2_ref[...],
                                                                       preferred_element_type=F32)

    _skewed_subtiles(tm // sub_rows, first_matmuls, mid_and_second_matmul)


def _gated_conv(kind, x, seq_len, norm_g, mods, layer, row_fn, mod_rows, w1, cw, cb, w2, tc, tm_pref,
                sub_pref=SUB_ROWS):
    m, d = x.shape
    parts = 2 if kind == "ffn" else 3
    cparts = 2 if kind == "ffn" else 1
    n_chunks = w2.shape[0] // tc
    assert w1.shape == (parts * n_chunks, d, tc) and cw.shape[1] == cparts * n_chunks * tc
    tm = _tile(seq_len, tm_pref)
    sub_rows = _tile(tm, sub_pref)
    assert sub_rows % V7X_SUBLANES_BF16 == 0
    tiles_per_seq = seq_len // tm
    hb = tm // HALO
    n_hblk = m // HALO
    other = 2 * (d * parts * tc * 2 + tc * d * 2) + (tm + 2 * HALO) * d * 2
    x_bufs, out_bufs = _row_tile_buffers(tm * d * 4, other)
    vmem = (x_bufs + out_bufs) * tm * d * 4 + other + VMEM_TEMP_BYTES
    col_spec = lambda rows, p: pl.BlockSpec((rows, tc), lambda i, j: (0, p * n_chunks + j))
    in_specs = [
        pl.BlockSpec((tm, d), lambda i, j: (i, 0), pipeline_mode=pl.Buffered(x_bufs)),
        pl.BlockSpec((HALO, d), lambda i, j: (jnp.maximum(i * hb - 1, 0), 0)),
        pl.BlockSpec((HALO, d), lambda i, j: (jnp.minimum((i + 1) * hb, n_hblk - 1), 0)),
        pl.BlockSpec((1, d), lambda i, j: (0, 0)),
        _mod_spec(layer, lambda i: row_fn(i, tm), d),
    ]
    in_specs += [pl.BlockSpec((None, d, tc), functools.partial(lambda p, i, j: (p * n_chunks + j, 0, 0), p))
                 for p in range(parts)]
    in_specs += [col_spec(3, p) for p in range(cparts)]
    operands = [x, x, x, norm_g, mods] + [w1] * parts + [cw] * cparts
    if kind == "ffn":
        in_specs += [col_spec(1, p) for p in range(cparts)]
        operands += [cb] * cparts
    in_specs.append(pl.BlockSpec((tc, d), lambda i, j: (j, 0)))
    operands.append(w2)
    return pl.pallas_call(
        functools.partial(_gated_conv_kernel, kind=kind, tiles_per_seq=tiles_per_seq, mod_rows=mod_rows,
                          sub_rows=sub_rows),
        grid=(m // tm, n_chunks),
        in_specs=in_specs,
        out_specs=pl.BlockSpec((tm, d), lambda i, j: (i, 0), pipeline_mode=pl.Buffered(out_bufs)),
        out_shape=jax.ShapeDtypeStruct((m, d), F32),
        scratch_shapes=[pltpu.VMEM((tm + 2 * HALO, d), BF16)],
        compiler_params=_compiler_params(("parallel", "arbitrary"), vmem),
        name=kind,
    )(*operands)


def _gmlp_kernel(x_ref, g_ref, mod_ref, win_ref, vg_ref, ws_ref, bs_ref, wout_ref, o_ref,
                 hs_ref, v_ref, ssq_ref, *, n_chunks, group_dim, width, sub_rows):
    j = pl.program_id(1)
    tm = x_ref.shape[0]
    tc = win_ref.shape[1]
    n_sub = tm // sub_rows

    @pl.when(j == 0)
    def _():
        _store_norm_mod(hs_ref, [(x_ref, None)], g_ref[...], mod_ref[0:1, :], mod_ref[1:2, :])
        ssq_ref[...] = jnp.zeros_like(ssq_ref)
        o_ref[...] = x_ref[...]

    def in_matmul(r):
        return jnp.dot(hs_ref[r * sub_rows:(r + 1) * sub_rows, :], win_ref[...], preferred_element_type=F32)

    @pl.when(j < n_chunks)
    def _():
        def keep_v(r, y):
            rows = slice(r * sub_rows, (r + 1) * sub_rows)
            t = jax.nn.gelu(y)
            v_ref[j, rows, :] = t
            ssq_ref[rows, :] += jnp.sum(t * t, axis=-1, keepdims=True)

        _skewed_subtiles(n_sub, in_matmul, keep_v)

    @pl.when(j >= n_chunks)
    def _():
        jj = j - n_chunks
        vg = vg_ref[...]
        res_gate = mod_ref[2:3, :]
        bs = bs_ref[...]
        gpc = tc // group_dim
        ws = [ws_ref[jj * gpc + gi] for gi in range(gpc)]

        def gate_and_project(r, y):
            rows = slice(r * sub_rows, (r + 1) * sub_rows)
            inv = lax.rsqrt(ssq_ref[rows, :] * (1.0 / width) + EPS)
            vn = (v_ref[jj, rows, :] * inv * vg).astype(BF16)
            cols = []
            for gi in range(gpc):
                csl = slice(gi * group_dim, (gi + 1) * group_dim)
                cols.append(jnp.concatenate(
                    [jnp.dot(ws[gi], vn[c * GM_CHUNK:(c + 1) * GM_CHUNK, csl], preferred_element_type=F32)
                     + bs[:, csl] for c in range(sub_rows // GM_CHUNK)], axis=0))
            sv = jnp.concatenate(cols, axis=1)
            o_ref[rows, :] += res_gate * jnp.dot((jax.nn.gelu(y) * sv).astype(BF16), wout_ref[...],
                                                 preferred_element_type=F32)

        _skewed_subtiles(n_sub, in_matmul, gate_and_project)


def _gmlp(x, norm_g, mods, layer, row_fn, w_in, v_g, w_s, b_s_cols, w_out, tm_pref):
    m, d = x.shape
    width = w_out.shape[0]
    group_dim = width // GM_GROUPS
    tc = w_in.shape[2]
    n_chunks = width // tc
    assert w_in.shape[0] == 2 * n_chunks and tc % group_dim == 0
    tm = _tile(m, tm_pref)
    sub_rows = _tile(tm, SUB_ROWS)
    assert sub_rows % GM_CHUNK == 0
    other = (2 * (d * tc * 2 + tc * d * 2 + GM_CHUNK * tc * 4) + tm * d * 2 + tm * width * 4
             + GM_GROUPS * GM_CHUNK * GM_CHUNK * 2 + tm * V7X_LANES * 4)
    x_bufs, out_bufs = _row_tile_buffers(tm * d * 4, other)
    vmem = (x_bufs + out_bufs) * tm * d * 4 + other + VMEM_TEMP_BYTES
    return pl.pallas_call(
        functools.partial(_gmlp_kernel, n_chunks=n_chunks, group_dim=group_dim, width=width,
                          sub_rows=sub_rows),
        grid=(m // tm, 2 * n_chunks),
        in_specs=[
            pl.BlockSpec((tm, d), lambda i, j: (i, 0), pipeline_mode=pl.Buffered(x_bufs)),
            pl.BlockSpec((1, d), lambda i, j: (0, 0)),
            _mod_spec(layer, lambda i: row_fn(i, tm), d),
            pl.BlockSpec((None, d, tc),
                         lambda i, j: (jnp.where(j < n_chunks, j + n_chunks, j - n_chunks), 0, 0)),
            pl.BlockSpec((1, tc), lambda i, j: (0, jnp.maximum(j - n_chunks, 0))),
            pl.BlockSpec((GM_GROUPS, GM_CHUNK, GM_CHUNK), lambda i, j: (0, 0, 0)),
            pl.BlockSpec((GM_CHUNK, tc), lambda i, j: (0, jnp.maximum(j - n_chunks, 0))),
            pl.BlockSpec((tc, d), lambda i, j: (jnp.maximum(j - n_chunks, 0), 0)),
        ],
        out_specs=pl.BlockSpec((tm, d), lambda i, j: (i, 0), pipeline_mode=pl.Buffered(out_bufs)),
        out_shape=jax.ShapeDtypeStruct((m, d), F32),
        scratch_shapes=[pltpu.VMEM((tm, d), BF16), pltpu.VMEM((n_chunks, tm, tc), F32), pltpu.VMEM((tm, 1), F32)],
        compiler_params=_compiler_params(("parallel", "arbitrary"), vmem),
        name="gmlp",
    )(x, norm_g, mods, w_in, v_g, w_s, b_s_cols, w_out)


CAST_BLOCK_BYTES = 6 << 20


def _cast_kernel(x_ref, o_ref, *, valid_row_blocks):
    rows, f = x_ref.shape
    x = x_ref[...]
    if valid_row_blocks is not None:
        x = jnp.where(pl.program_id(0) < valid_row_blocks, x, 0.0)
    x = x.astype(o_ref.dtype)
    if len(o_ref.shape) == 2:
        o_ref[...] = x
        return
    n_chunks, _, tc = o_ref.shape
    for c in range(n_chunks):
        width = max(0, min(tc, f - c * tc))
        if width:
            o_ref[c, :, :width] = x[:, c * tc:c * tc + width]
        if width < tc:
            o_ref[c, :, width:] = jnp.zeros((rows, tc - width), o_ref.dtype)


def _cast_weight(w, idx, parts=1, fp=None, col_chunk=None, rows_out=None, row_block=None):
    _, r, c = w.shape
    f = c // parts
    fp = f if fp is None else fp
    rows_out = r if rows_out is None else rows_out
    assert parts == 1 or f % V7X_LANES == 0
    assert col_chunk is not None or (parts == 1 and fp == f)
    if row_block is None:
        row_block = _tile(r, max(V7X_SUBLANES_BF16, CAST_BLOCK_BYTES // (4 * f) // V7X_SUBLANES_BF16
                                 * V7X_SUBLANES_BF16))
    assert r % row_block == 0 and rows_out % row_block == 0 and row_block % V7X_SUBLANES_BF16 == 0
    n_valid = r // row_block
    padded_rows = rows_out > r
    if col_chunk is None:
        out_spec = pl.BlockSpec((row_block, f), lambda i, p: (i, 0))
        out_shape = (rows_out, f)
    else:
        assert fp % col_chunk == 0 and col_chunk % V7X_LANES == 0 and not padded_rows
        n_chunks = fp // col_chunk
        out_spec = pl.BlockSpec((n_chunks, row_block, col_chunk), lambda i, p: (p, i, 0))
        out_shape = (parts * n_chunks, r, col_chunk)
    return pl.pallas_call(
        functools.partial(_cast_kernel, valid_row_blocks=n_valid if padded_rows else None),
        grid=(rows_out // row_block, parts),
        in_specs=[pl.BlockSpec((None, row_block, f), lambda i, p: (idx, jnp.minimum(i, n_valid - 1), p))],
        out_specs=out_spec,
        out_shape=jax.ShapeDtypeStruct(out_shape, BF16),
        compiler_params=_compiler_params(("parallel", "arbitrary"),
                                         2 * row_block * (4 * f + 2 * fp) + (8 << 20)),
        name="cast_weight",
    )(w)


def _pad_halves(a, f, fp):
    pad = lambda h: jnp.pad(h, ((0, 0), (0, fp - f)))
    return jnp.concatenate([pad(a[:, :f]), pad(a[:, f:])], axis=1)


def _prep_ffn(w_up, conv_w, conv_b, w_down, layer, tc):
    f = w_down.shape[1]
    fp = -(-f // tc) * tc
    w1 = _cast_weight(w_up, layer, parts=2, fp=fp, col_chunk=tc)
    cw = _pad_halves(conv_w[layer], f, fp)
    cb = _pad_halves(conv_b[layer][None, :], f, fp)
    w2 = _cast_weight(w_down, layer, rows_out=fp, row_block=math.gcd(f, fp))
    return w1, cw, cb, w2


def kernel(x, c, ctx, c_ctx, norm_mix_g, norm_ffn_g, w_ada, b_ada, na_w_qkv, na_q_g, na_k_g, na_rpb, na_w_o,
           gm_w_in, gm_v_g, gm_w_s, gm_b_s, gm_w_out, sc_w_in, sc_conv_w, sc_w_out,
           ffn_w_up, ffn_conv_w, ffn_conv_b, ffn_w_down):
    bn, seq, d = x.shape
    ctx_len = ctx.shape[1]
    depth = w_ada.shape[0]
    head_dim = d // NA_HEADS
    rows = seq // GRID_W
    assert bn + 1 <= COND_ROWS and seq % GRID_W == 0 and ctx_len % GM_CHUNK == 0

    cond = jnp.concatenate([c, c_ctx[None, :], jnp.zeros((COND_ROWS - bn - 1, d), F32)], axis=0)
    mods = _adaln(cond, w_ada, b_ada).reshape(depth, COND_ROWS, N_MOD, d)

    lat_row = lambda i, tm: i // (seq // tm)
    ctx_row = lambda i, tm: bn

    ffn_tc = 512 if ffn_w_down.shape[1] >= 512 else 128
    sc_tc = _tile(d, 512)
    gm_width = gm_w_out.shape[1]
    gm_tc = max(gm_width // GM_GROUPS, _tile(gm_width, 512))
    proj_tn = _tile(d, 1024)
    lat_tm = 1024
    lat_sub = 512

    xl = x.reshape(bn * seq, d)
    xc = ctx.reshape(bn * ctx_len, d)
    for i in range(depth):
        last = i == depth - 1
        mixer, jx = i % N_MIXERS, i // N_MIXERS
        g_mix = norm_mix_g[i][None, :]
        g_ffn = norm_ffn_g[i][None, :]
        do_ctx = (not last) or mixer == 0
        if mixer == 0:
            w_qkv = _cast_weight(na_w_qkv, jx, col_chunk=proj_tn)
            w_o = _cast_weight(na_w_o, jx, col_chunk=proj_tn)
            q_gain = na_q_g[jx] * (head_dim ** -0.5 * LOG2_E)
            head_gain = jnp.concatenate([jnp.tile(q_gain, NA_HEADS), jnp.tile(na_k_g[jx], NA_HEADS),
                                         jnp.ones((d,), F32)])[None, :]
            qkv = _qkv(xl, g_mix, mods, i, lat_row, w_qkv, head_gain, head_dim)
            qkv_c = _qkv(xc, g_mix, mods, i, ctx_row, w_qkv, head_gain, head_dim).reshape(bn, ctx_len, 3 * d)
            att = _na_attention(qkv, qkv_c, na_rpb[jx], bn, rows, d, head_dim)
            xl_new = _proj_res(att, w_o, xl, mods, i, lat_row, 2)
            if not last:
                att_c = _ctx_attention(qkv_c, d, head_dim).reshape(bn * ctx_len, d)
                xc = _proj_res(att_c, w_o, xc, mods, i, ctx_row, 2)
            xl = xl_new
        elif mixer == 1:
            w_in = _cast_weight(gm_w_in, jx, col_chunk=gm_tc)
            w_out = _cast_weight(gm_w_out, jx)
            w_s = gm_w_s[jx].astype(BF16)
            width = w_out.shape[0]
            v_g = gm_v_g[jx][None, :]
            b_s_cols = jnp.repeat(gm_b_s[jx].T, width // GM_GROUPS, axis=1)
            xl = _gmlp(xl, g_mix, mods, i, lat_row, w_in, v_g, w_s, b_s_cols, w_out, lat_tm)
            if not last:
                xc = _gmlp(xc, g_mix, mods, i, ctx_row, w_in, v_g, w_s, b_s_cols, w_out, ctx_len)
        else:
            w1 = _cast_weight(sc_w_in, jx, parts=3, col_chunk=sc_tc)
            cw, w2 = sc_conv_w[jx], _cast_weight(sc_w_out, jx)
            xl = _gated_conv("sc", xl, seq, g_mix, mods, i, lat_row, (0, 1, 2), w1, cw, None, w2, sc_tc, lat_tm,
                             sub_pref=lat_sub)
            if not last:
                xc = _gated_conv("sc", xc, ctx_len, g_mix, mods, i, ctx_row, (0, 1, 2), w1, cw, None, w2, sc_tc,
                                 ctx_len)
        w1, cw, cb, w2 = _prep_ffn(ffn_w_up, ffn_conv_w, ffn_conv_b, ffn_w_down, i, ffn_tc)
        xl = _gated_conv("ffn", xl, seq, g_ffn, mods, i, lat_row, (3, 4, 5), w1, cw, cb, w2, ffn_tc, lat_tm,
                         sub_pref=lat_sub)
        if not last:
            xc = _gated_conv("ffn", xc, ctx_len, g_ffn, mods, i, ctx_row, (3, 4, 5), w1, cw, cb, w2, ffn_tc,
                             ctx_len)
    return xl.reshape(bn, seq, d)
```

```python
import functools
import math

import numpy as np
import jax
import jax.numpy as jnp
from jax import lax
from jax.experimental import pallas as pl
from jax.experimental.pallas import tpu as pltpu

GRID_W = 64
NA_HEADS = 16
WIN_H = 8
WIN_W = 16
GM_GROUPS = 16
GM_CHUNK = 128
N_MOD = 6
N_MIXERS = 3
EPS = 1e-6

V7X_VMEM_BYTES = 64 * 1024 * 1024
V7X_SUBLANES_F32 = 8
V7X_SUBLANES_BF16 = 16
V7X_LANES = 128
V7X_MXU_DIM = 256

F32 = jnp.float32
BF16 = jnp.bfloat16

NA_QROWS = 2
COND_ROWS = V7X_SUBLANES_F32


def _compiler_params(semantics, vmem_bytes):
    return pltpu.CompilerParams(dimension_semantics=semantics,
                                vmem_limit_bytes=int(min(vmem_bytes, VMEM_LIMIT_BYTES)))


def _tile(n, pref):
    t = min(n, pref)
    while n % t:
        t -= 1
    return t


VMEM_LIMIT_BYTES = V7X_VMEM_BYTES - (4 << 20)
VMEM_TEMP_BYTES = 6 << 20


def _row_tile_buffers(tile_bytes, other_bytes):
    for x_bufs, out_bufs in ((2, 2), (1, 2), (1, 1)):
        if (x_bufs + out_bufs) * tile_bytes + other_bytes + VMEM_TEMP_BYTES <= VMEM_LIMIT_BYTES:
            break
    return x_bufs, out_bufs


def _norm_mod(x, gain, shift):
    ms = jnp.mean(x * x, axis=-1, keepdims=True)
    return x * lax.rsqrt(ms + EPS) * gain + shift


NORM_CHUNK = 32


def _store_norm_mod(hs_ref, sources, g, shift, scale):
    bounds = np.cumsum([0] + [ref.shape[0] for ref, _ in sources])
    total = int(bounds[-1])
    assert hs_ref.shape[0] == total
    gain = g * (1.0 + scale)
    for c0 in range(0, total, NORM_CHUNK):
        c1 = min(c0 + NORM_CHUNK, total)
        pieces = []
        for (ref, zero_pred), b0, b1 in zip(sources, bounds[:-1], bounds[1:]):
            lo, hi = max(c0, int(b0)), min(c1, int(b1))
            if lo >= hi:
                continue
            h = _norm_mod(ref[lo - int(b0):hi - int(b0), :], gain, shift)
            pieces.append(h if zero_pred is None else jnp.where(zero_pred, 0.0, h))
        h = pieces[0] if len(pieces) == 1 else jnp.concatenate(pieces, axis=0)
        hs_ref[c0:c1, :] = h.astype(BF16)


def _adaln_kernel(c_ref, w_ref, b_ref, o_ref):
    c = c_ref[...]
    a = (c * jax.nn.sigmoid(c)).astype(BF16)
    o_ref[...] = jnp.dot(a, w_ref[...].astype(BF16), preferred_element_type=F32) + b_ref[...]


def _adaln(cond, w_ada, b_ada):
    depth, d, n = w_ada.shape
    tn = _tile(n, 1024)
    return pl.pallas_call(
        _adaln_kernel,
        grid=(depth, n // tn),
        in_specs=[
            pl.BlockSpec((COND_ROWS, d), lambda l, j: (0, 0)),
            pl.BlockSpec((None, d, tn), lambda l, j: (l, 0, j)),
            pl.BlockSpec((None, 1, tn), lambda l, j: (l, 0, j)),
        ],
        out_specs=pl.BlockSpec((None, COND_ROWS, tn), lambda l, j: (l, 0, j)),
        out_shape=jax.ShapeDtypeStruct((depth, COND_ROWS, n), F32),
        compiler_params=_compiler_params(("arbitrary", "arbitrary"), 2 * d * tn * 4 + (8 << 20)),
        name="adaln",
    )(cond, w_ada, b_ada.reshape(depth, 1, n))


def _mod_spec(layer, row_fn, width, col_fn=None):
    if col_fn is None:
        return pl.BlockSpec((None, None, N_MOD, width), lambda i, j: (layer, row_fn(i), 0, 0))
    return pl.BlockSpec((None, None, N_MOD, width), lambda i, j: (layer, row_fn(i), 0, col_fn(j)))


SUB_ROWS = 256


def _skewed_subtiles(n_sub, first, second, ahead=1):
    pending = [first(r) for r in range(min(ahead, n_sub))]
    for r in range(n_sub):
        if r + ahead < n_sub:
            pending.append(first(r + ahead))
        second(r, pending.pop(0))


def _proj_res_kernel(a_ref, w_ref, x_ref, mod_ref, o_ref, *, gate_row, sub_rows):
    gate = mod_ref[gate_row:gate_row + 1, :]

    def matmul(r):
        return jnp.dot(a_ref[r * sub_rows:(r + 1) * sub_rows, :], w_ref[...], preferred_element_type=F32)

    def residual(r, y):
        rows = slice(r * sub_rows, (r + 1) * sub_rows)
        o_ref[rows, :] = x_ref[rows, :] + gate * y

    _skewed_subtiles(o_ref.shape[0] // sub_rows, matmul, residual)


def _proj_res(a, w, x, mods, layer, row_fn, gate_row, tm_pref=1024):
    m, k = a.shape
    n_tiles, _, tn = w.shape
    n = n_tiles * tn
    tm = _tile(m, tm_pref)
    vmem = 2 * (tm * k * 2 + k * tn * 2 + 2 * tm * tn * 4) + (8 << 20)
    return pl.pallas_call(
        functools.partial(_proj_res_kernel, gate_row=gate_row, sub_rows=_tile(tm, SUB_ROWS)),
        grid=(m // tm, n // tn),
        in_specs=[
            pl.BlockSpec((tm, k), lambda i, j: (i, 0)),
            pl.BlockSpec((None, k, tn), lambda i, j: (j, 0, 0)),
            pl.BlockSpec((tm, tn), lambda i, j: (i, j)),
            _mod_spec(layer, lambda i: row_fn(i, tm), tn, lambda j: j),
        ],
        out_specs=pl.BlockSpec((tm, tn), lambda i, j: (i, j)),
        out_shape=jax.ShapeDtypeStruct((m, n), F32),
        compiler_params=_compiler_params(("parallel", "arbitrary"), vmem),
        name="proj_res",
    )(a, w, x, mods)


def _qkv_kernel(x_ref, g_ref, mod_ref, w_ref, hg_ref, o_ref, hs_ref, *, n_norm_tiles, head_dim, sub_rows):
    j = pl.program_id(1)

    @pl.when(j == 0)
    def _():
        _store_norm_mod(hs_ref, [(x_ref, None)], g_ref[...], mod_ref[0:1, :], mod_ref[1:2, :])

    tm, tn = o_ref.shape
    is_norm_tile = j < n_norm_tiles
    hg = hg_ref[...]

    def matmul(r):
        return jnp.dot(hs_ref[r * sub_rows:(r + 1) * sub_rows, :], w_ref[...], preferred_element_type=F32)

    def head_norm(r, y):
        rows = slice(r * sub_rows, (r + 1) * sub_rows)
        for hh in range(tn // head_dim):
            sl = slice(hh * head_dim, (hh + 1) * head_dim)
            t = y[:, sl]
            inv = lax.rsqrt(jnp.mean(t * t, axis=-1, keepdims=True) + EPS)
            o_ref[rows, sl] = (t * jnp.where(is_norm_tile, inv, 1.0) * hg[:, sl]).astype(o_ref.dtype)

    _skewed_subtiles(tm // sub_rows, matmul, head_norm)


def _qkv(x, norm_g, mods, layer, row_fn, w_qkv, head_gain, head_dim, tm_pref=1024):
    m, d = x.shape
    n_tiles, _, tn = w_qkv.shape
    n = n_tiles * tn
    tm = _tile(m, tm_pref)
    assert (n // 3) % tn == 0 and tn % head_dim == 0
    vmem = 2 * (tm * d * 4 + d * tn * 2 + tm * tn * 2) + tm * d * 2 + 3 * tm * tn * 4 + (8 << 20)
    return pl.pallas_call(
        functools.partial(_qkv_kernel, n_norm_tiles=2 * (n // 3) // tn, head_dim=head_dim,
                          sub_rows=_tile(tm, SUB_ROWS)),
        grid=(m // tm, n // tn),
        in_specs=[
            pl.BlockSpec((tm, d), lambda i, j: (i, 0)),
            pl.BlockSpec((1, d), lambda i, j: (0, 0)),
            _mod_spec(layer, lambda i: row_fn(i, tm), d),
            pl.BlockSpec((None, d, tn), lambda i, j: (j, 0, 0)),
            pl.BlockSpec((1, tn), lambda i, j: (0, j)),
        ],
        out_specs=pl.BlockSpec((tm, tn), lambda i, j: (i, j)),
        out_shape=jax.ShapeDtypeStruct((m, n), BF16),
        scratch_shapes=[pltpu.VMEM((tm, d), BF16)],
        compiler_params=_compiler_params(("parallel", "arbitrary"), vmem),
        name="qkv",
    )(x, norm_g, mods, w_qkv, head_gain)


LOG2_E = math.log2(math.e)
_NT_DIMS = (((1,), (1,)), ((), ()))


def _na_kernel(*refs, nb, heads, head_dim):
    refs = refs[2:]
    q_ref = refs[0]
    k_refs = refs[1:1 + nb]
    v_refs = refs[1 + nb:1 + 2 * nb]
    kc_ref, vc_ref, tbl_ref, o_ref = refs[1 + 2 * nb:]

    def scores(h):
        sl = slice(h * head_dim, (h + 1) * head_dim)
        q = q_ref[:, sl]
        k = jnp.concatenate([r[:, sl] for r in k_refs], axis=0)
        s = lax.dot_general(q, k, _NT_DIMS, preferred_element_type=F32) + tbl_ref[h]
        sc = lax.dot_general(q, kc_ref[:, sl], _NT_DIMS, preferred_element_type=F32)
        return s, sc

    ones_lat = jnp.ones((nb * k_refs[0].shape[0], head_dim), BF16)
    ones_ctx = jnp.ones((vc_ref.shape[0], head_dim), BF16)

    def attend(h, s_sc):
        s, sc = s_sc
        sl = slice(h * head_dim, (h + 1) * head_dim)
        v = jnp.concatenate([jnp.concatenate([r[:, sl] for r in v_refs], axis=0), ones_lat], axis=1)
        vc = jnp.concatenate([vc_ref[:, sl], ones_ctx], axis=1)
        m = jnp.maximum(jnp.max(s, axis=-1, keepdims=True), jnp.max(sc, axis=-1, keepdims=True))
        p = jnp.exp2(s - m).astype(BF16)
        pc = jnp.exp2(sc - m).astype(BF16)
        o = (jnp.dot(p, v, preferred_element_type=F32) + jnp.dot(pc, vc, preferred_element_type=F32))
        o_ref[:, sl] = (o[:, :head_dim] * (1.0 / o[:, head_dim:head_dim + 1])).astype(o_ref.dtype)

    _skewed_subtiles(heads, scores, attend, ahead=2)


def _na_geometry(rows, qrows):
    kh = min(WIN_H, rows)
    nb = min(kh + qrows - 1, rows)
    n_blk = rows // qrows
    band0s, keys, types, row_rel = [], {}, [], []
    for blk in range(n_blk):
        r0 = blk * qrows
        q_row = r0 + np.arange(qrows)
        r_start = np.clip(q_row - kh // 2, 0, rows - kh)
        band0 = min(int(np.clip(r0 - kh // 2, 0, rows - kh)), rows - nb)
        k_row = band0 + np.arange(nb)
        row_ok = (k_row[None, :] >= r_start[:, None]) & (k_row[None, :] < r_start[:, None] + kh)
        rel = np.where(row_ok, k_row[None, :] - q_row[:, None] + WIN_H - 1, -1)
        key = rel.tobytes()
        if key not in keys:
            keys[key] = len(row_rel)
            row_rel.append(rel)
        types.append(keys[key])
        band0s.append(band0)
    return nb, n_blk, np.array(band0s, np.int32), np.array(types, np.int32), row_rel


def _na_bias_table(rpb, row_rel):
    heads, n_dr, n_dc = rpb.shape
    period = 2 * GRID_W
    assert n_dc <= period
    r = jnp.pad(rpb, ((0, 0), (0, 0), (0, period - n_dc)))
    skew = jnp.tile(r, (1, 1, GRID_W))[:, :, :GRID_W * (period - 1)].reshape(heads, n_dr, GRID_W, period - 1)
    toep = skew[:, :, :, WIN_W - 1:WIN_W - 1 + GRID_W]
    q_col = np.arange(GRID_W)
    c_start = np.clip(q_col - WIN_W // 2, 0, GRID_W - WIN_W)
    col_ok = (q_col[None, :] >= c_start[:, None]) & (q_col[None, :] < c_start[:, None] + WIN_W)
    toep = jnp.where(col_ok, toep, -jnp.inf)
    masked = jnp.full((heads, GRID_W, GRID_W), -jnp.inf, F32)
    tables = []
    for rel in row_rel:
        q_blocks = []
        for qr in range(rel.shape[0]):
            q_blocks.append(jnp.concatenate(
                [toep[:, rel[qr, kr]] if rel[qr, kr] >= 0 else masked for kr in range(rel.shape[1])], axis=-1))
        tables.append(jnp.concatenate(q_blocks, axis=-2))
    return jnp.stack(tables).astype(F32)


def _na_attention(qkv, qkv_ctx, rpb, bn, rows, d, head_dim):
    qrows = min(NA_QROWS, rows)
    nb, n_blk, band0s, types, row_rel = _na_geometry(rows, qrows)
    qw, nk = qrows * GRID_W, nb * GRID_W
    ctx_len = qkv_ctx.shape[1]
    tbl = _na_bias_table(rpb * LOG2_E, row_rel)
    band_tbl = jnp.asarray(band0s)
    type_tbl = jnp.asarray(types)
    kv_view = qkv.reshape(bn * rows, GRID_W, 3 * d)

    def kv_spec(t, col):
        return pl.BlockSpec((None, GRID_W, d), lambda b, i, band, typ: (b * rows + band[i] + t, 0, col))

    in_specs = ([pl.BlockSpec((qw, d), lambda b, i, band, typ: (b * n_blk + i, 0))]
                + [kv_spec(t, 1) for t in range(nb)]
                + [kv_spec(t, 2) for t in range(nb)]
                + [pl.BlockSpec((None, ctx_len, d), lambda b, i, band, typ: (b, 0, 1)),
                   pl.BlockSpec((None, ctx_len, d), lambda b, i, band, typ: (b, 0, 2)),
                   pl.BlockSpec((None, NA_HEADS, qw, nk), lambda b, i, band, typ: (typ[i], 0, 0, 0))])
    vmem = 2 * (2 * qw * d * 2 + 2 * nk * d * 2 + 2 * ctx_len * d * 2 + NA_HEADS * qw * nk * 4) + (12 << 20)
    grid_spec = pltpu.PrefetchScalarGridSpec(
        num_scalar_prefetch=2,
        grid=(bn, n_blk),
        in_specs=in_specs,
        out_specs=pl.BlockSpec((qw, d), lambda b, i, band, typ: (b * n_blk + i, 0)),
    )
    return pl.pallas_call(
        functools.partial(_na_kernel, nb=nb, heads=NA_HEADS, head_dim=head_dim),
        grid_spec=grid_spec,
        out_shape=jax.ShapeDtypeStruct((bn * rows * GRID_W, d), BF16),
        compiler_params=_compiler_params(("parallel", "arbitrary"), vmem),
        name="na_attention",
    )(band_tbl, type_tbl, qkv, *([kv_view] * (2 * nb)), qkv_ctx, qkv_ctx, tbl)


def _ctx_attn_kernel(q_ref, k_ref, v_ref, o_ref, *, heads, head_dim):
    for h in range(heads):
        sl = slice(h * head_dim, (h + 1) * head_dim)
        s = lax.dot_general(q_ref[:, sl], k_ref[:, sl], _NT_DIMS, preferred_element_type=F32)
        m = jnp.max(s, axis=-1, keepdims=True)
        p = jnp.exp2(s - m)
        l = jnp.sum(p, axis=-1, keepdims=True)
        o = jnp.dot(p.astype(BF16), v_ref[:, sl], preferred_element_type=F32)
        o_ref[:, sl] = (o * (1.0 / l)).astype(o_ref.dtype)


def _ctx_attention(qkv_ctx, d, head_dim):
    bn, ctx_len, _ = qkv_ctx.shape
    spec = lambda col: pl.BlockSpec((None, ctx_len, d), lambda b: (b, 0, col))
    return pl.pallas_call(
        functools.partial(_ctx_attn_kernel, heads=NA_HEADS, head_dim=head_dim),
        grid=(bn,),
        in_specs=[spec(0), spec(1), spec(2)],
        out_specs=pl.BlockSpec((None, ctx_len, d), lambda b: (b, 0, 0)),
        out_shape=jax.ShapeDtypeStruct((bn, ctx_len, d), BF16),
        compiler_params=_compiler_params(("arbitrary",), 8 * ctx_len * d * 2 + (12 << 20)),
        name="ctx_attention",
    )(qkv_ctx, qkv_ctx, qkv_ctx)


HALO = V7X_SUBLANES_F32


def _shift_rows(u, tm):
    n = u.shape[0]
    prev = pltpu.roll(u, 1, 0)[HALO:HALO + tm]
    nxt = pltpu.roll(u, n - 1, 0)[HALO:HALO + tm]
    return prev, nxt


def _conv3(u, cw, rows):
    prev, nxt = _shift_rows(u, rows)
    return prev * cw[0:1] + u[HALO:HALO + rows] * cw[1:2] + nxt * cw[2:3]


def _gated_conv_kernel(*refs, kind, tiles_per_seq, mod_rows, sub_rows):
    x_ref, xp_ref, xn_ref, g_ref, mod_ref = refs[:5]
    if kind == "ffn":
        w1_refs, cw_refs, cb_refs = refs[5:7], refs[7:9], refs[9:11]
        w2_ref, o_ref, hs_ref = refs[11:]
    else:
        w1_refs, cw_refs = refs[5:8], refs[8:9]
        w2_ref, o_ref, hs_ref = refs[9:]
    i = pl.program_id(0)
    j = pl.program_id(1)
    tm = x_ref.shape[0]
    r_shift, r_scale, r_gate = mod_rows

    @pl.when(j == 0)
    def _():
        g = g_ref[...]
        shift = mod_ref[r_shift:r_shift + 1, :]
        scale = mod_ref[r_scale:r_scale + 1, :]
        pos = i % tiles_per_seq
        _store_norm_mod(hs_ref, [(xp_ref, pos == 0), (x_ref, None), (xn_ref, pos == tiles_per_seq - 1)],
                        g, shift, scale)
        o_ref[...] = x_ref[...]

    res_gate = mod_ref[r_gate:r_gate + 1, :]

    def first_matmuls(r):
        lhs = hs_ref[r * sub_rows:r * sub_rows + sub_rows + 2 * HALO, :]
        return [jnp.dot(lhs, w[...], preferred_element_type=F32) for w in w1_refs]

    def mid_and_second_matmul(r, u):
        if kind == "ffn":
            gate, up = [_conv3(u[p], cw_refs[p][...], sub_rows) + cb_refs[p][...] for p in range(2)]
            a = gate * jax.nn.sigmoid(gate) * up
        else:
            a = u[0][HALO:HALO + sub_rows] * _conv3(u[1] * u[2], cw_refs[0][...], sub_rows)
        o_ref[r * sub_rows:(r + 1) * sub_rows, :] += res_gate * jnp.dot(a.astype(BF16), w2_ref[...],
                                                                       preferred_element_type=F32)

    _skewed_subtiles(tm // sub_rows, first_matmuls, mid_and_second_matmul)


def _gated_conv(kind, x, seq_len, norm_g, mods, layer, row_fn, mod_rows, w1, cw, cb, w2, tc, tm_pref,
                sub_pref=SUB_ROWS):
    m, d = x.shape
    parts = 2 if kind == "ffn" else 3
    cparts = 2 if kind == "ffn" else 1
    n_chunks = w2.shape[0] // tc
    assert w1.shape == (parts * n_chunks, d, tc) and cw.shape[1] == cparts * n_chunks * tc
    tm = _tile(seq_len, tm_pref)
    sub_rows = _tile(tm, sub_pref)
    assert sub_rows % V7X_SUBLANES_BF16 == 0
    tiles_per_seq = seq_len // tm
    hb = tm // HALO
    n_hblk = m // HALO
    other = 2 * (d * parts * tc * 2 + tc * d * 2) + (tm + 2 * HALO) * d * 2
    x_bufs, out_bufs = _row_tile_buffers(tm * d * 4, other)
    vmem = (x_bufs + out_bufs) * tm * d * 4 + other + VMEM_TEMP_BYTES
    col_spec = lambda rows, p: pl.BlockSpec((rows, tc), lambda i, j: (0, p * n_chunks + j))
    in_specs = [
        pl.BlockSpec((tm, d), lambda i, j: (i, 0), pipeline_mode=pl.Buffered(x_bufs)),
        pl.BlockSpec((HALO, d), lambda i, j: (jnp.maximum(i * hb - 1, 0), 0)),
        pl.BlockSpec((HALO, d), lambda i, j: (jnp.minimum((i + 1) * hb, n_hblk - 1), 0)),
        pl.BlockSpec((1, d), lambda i, j: (0, 0)),
        _mod_spec(layer, lambda i: row_fn(i, tm), d),
    ]
    in_specs += [pl.BlockSpec((None, d, tc), functools.partial(lambda p, i, j: (p * n_chunks + j, 0, 0), p))
                 for p in range(parts)]
    in_specs += [col_spec(3, p) for p in range(cparts)]
    operands = [x, x, x, norm_g, mods] + [w1] * parts + [cw] * cparts
    if kind == "ffn":
        in_specs += [col_spec(1, p) for p in range(cparts)]
        operands += [cb] * cparts
    in_specs.append(pl.BlockSpec((tc, d), lambda i, j: (j, 0)))
    operands.append(w2)
    return pl.pallas_call(
        functools.partial(_gated_conv_kernel, kind=kind, tiles_per_seq=tiles_per_seq, mod_rows=mod_rows,
                          sub_rows=sub_rows),
        grid=(m // tm, n_chunks),
        in_specs=in_specs,
        out_specs=pl.BlockSpec((tm, d), lambda i, j: (i, 0), pipeline_mode=pl.Buffered(out_bufs)),
        out_shape=jax.ShapeDtypeStruct((m, d), F32),
        scratch_shapes=[pltpu.VMEM((tm + 2 * HALO, d), BF16)],
        compiler_params=_compiler_params(("parallel", "arbitrary"), vmem),
        name=kind,
    )(*operands)


_GELU_C1 = 2.0 * math.sqrt(2.0 / math.pi)
_GELU_C2 = _GELU_C1 * 0.044715


def _gelu_tanh(x):
    return x * jax.nn.sigmoid(x * (_GELU_C1 + _GELU_C2 * (x * x)))


def _gmlp_kernel(x_ref, g_ref, mod_ref, win_ref, vg_ref, ws_ref, bs_ref, wout_ref, o_ref,
                 hs_ref, v_ref, ssq_ref, *, n_chunks, group_dim, width, sub_rows):
    j = pl.program_id(1)
    tm = x_ref.shape[0]
    tc = win_ref.shape[1]
    n_sub = tm // sub_rows

    @pl.when(j == 0)
    def _():
        _store_norm_mod(hs_ref, [(x_ref, None)], g_ref[...], mod_ref[0:1, :], mod_ref[1:2, :])
        ssq_ref[...] = jnp.zeros_like(ssq_ref)
        o_ref[...] = x_ref[...]

    def in_matmul(r):
        return jnp.dot(hs_ref[r * sub_rows:(r + 1) * sub_rows, :], win_ref[...], preferred_element_type=F32)

    @pl.when(j < n_chunks)
    def _():
        def keep_v(r, y):
            rows = slice(r * sub_rows, (r + 1) * sub_rows)
            t = _gelu_tanh(y)
            v_ref[j, rows, :] = t
            ssq_ref[rows, :] += jnp.sum(t * t, axis=-1, keepdims=True)

        _skewed_subtiles(n_sub, in_matmul, keep_v)

    @pl.when(j >= n_chunks)
    def _():
        jj = j - n_chunks
        vg = vg_ref[...]
        res_gate = mod_ref[2:3, :]
        bs = bs_ref[...]
        gpc = tc // group_dim
        ws = [ws_ref[jj * gpc + gi] for gi in range(gpc)]

        def gate_and_project(r, y):
            rows = slice(r * sub_rows, (r + 1) * sub_rows)
            inv = lax.rsqrt(ssq_ref[rows, :] * (1.0 / width) + EPS)
            vn = (v_ref[jj, rows, :] * inv * vg).astype(BF16)
            cols = []
            for gi in range(gpc):
                csl = slice(gi * group_dim, (gi + 1) * group_dim)
                cols.append(jnp.concatenate(
                    [jnp.dot(ws[gi], vn[c * GM_CHUNK:(c + 1) * GM_CHUNK, csl], preferred_element_type=F32)
                     + bs[:, csl] for c in range(sub_rows // GM_CHUNK)], axis=0))
            sv = jnp.concatenate(cols, axis=1)
            o_ref[rows, :] += res_gate * jnp.dot((_gelu_tanh(y) * sv).astype(BF16), wout_ref[...],
                                                 preferred_element_type=F32)

        _skewed_subtiles(n_sub, in_matmul, gate_and_project)


def _gmlp(x, norm_g, mods, layer, row_fn, w_in, v_g, w_s, b_s_cols, w_out, tm_pref):
    m, d = x.shape
    width = w_out.shape[0]
    group_dim = width // GM_GROUPS
    tc = w_in.shape[2]
    n_chunks = width // tc
    assert w_in.shape[0] == 2 * n_chunks and tc % group_dim == 0
    tm = _tile(m, tm_pref)
    sub_rows = _tile(tm, SUB_ROWS)
    assert sub_rows % GM_CHUNK == 0
    other = (2 * (d * tc * 2 + tc * d * 2 + GM_CHUNK * tc * 4) + tm * d * 2 + tm * width * 4
             + GM_GROUPS * GM_CHUNK * GM_CHUNK * 2 + tm * V7X_LANES * 4)
    x_bufs, out_bufs = _row_tile_buffers(tm * d * 4, other)
    vmem = (x_bufs + out_bufs) * tm * d * 4 + other + VMEM_TEMP_BYTES
    return pl.pallas_call(
        functools.partial(_gmlp_kernel, n_chunks=n_chunks, group_dim=group_dim, width=width,
                          sub_rows=sub_rows),
        grid=(m // tm, 2 * n_chunks),
        in_specs=[
            pl.BlockSpec((tm, d), lambda i, j: (i, 0), pipeline_mode=pl.Buffered(x_bufs)),
            pl.BlockSpec((1, d), lambda i, j: (0, 0)),
            _mod_spec(layer, lambda i: row_fn(i, tm), d),
            pl.BlockSpec((None, d, tc),
                         lambda i, j: (jnp.where(j < n_chunks, j + n_chunks, j - n_chunks), 0, 0)),
            pl.BlockSpec((1, tc), lambda i, j: (0, jnp.maximum(j - n_chunks, 0))),
            pl.BlockSpec((GM_GROUPS, GM_CHUNK, GM_CHUNK), lambda i, j: (0, 0, 0)),
            pl.BlockSpec((GM_CHUNK, tc), lambda i, j: (0, jnp.maximum(j - n_chunks, 0))),
            pl.BlockSpec((tc, d), lambda i, j: (jnp.maximum(j - n_chunks, 0), 0)),
        ],
        out_specs=pl.BlockSpec((tm, d), lambda i, j: (i, 0), pipeline_mode=pl.Buffered(out_bufs)),
        out_shape=jax.ShapeDtypeStruct((m, d), F32),
        scratch_shapes=[pltpu.VMEM((tm, d), BF16), pltpu.VMEM((n_chunks, tm, tc), F32), pltpu.VMEM((tm, 1), F32)],
        compiler_params=_compiler_params(("parallel", "arbitrary"), vmem),
        name="gmlp",
    )(x, norm_g, mods, w_in, v_g, w_s, b_s_cols, w_out)


CAST_BLOCK_BYTES = 6 << 20


def _cast_kernel(x_ref, o_ref, *, valid_row_blocks):
    rows, f = x_ref.shape
    x = x_ref[...]
    if valid_row_blocks is not None:
        x = jnp.where(pl.program_id(0) < valid_row_blocks, x, 0.0)
    x = x.astype(o_ref.dtype)
    if len(o_ref.shape) == 2:
        o_ref[...] = x
        return
    n_chunks, _, tc = o_ref.shape
    for c in range(n_chunks):
        width = max(0, min(tc, f - c * tc))
        if width:
            o_ref[c, :, :width] = x[:, c * tc:c * tc + width]
        if width < tc:
            o_ref[c, :, width:] = jnp.zeros((rows, tc - width), o_ref.dtype)


def _cast_weight(w, idx, parts=1, fp=None, col_chunk=None, rows_out=None, row_block=None):
    _, r, c = w.shape
    f = c // parts
    fp = f if fp is None else fp
    rows_out = r if rows_out is None else rows_out
    assert parts == 1 or f % V7X_LANES == 0
    assert col_chunk is not None or (parts == 1 and fp == f)
    if row_block is None:
        row_block = _tile(r, max(V7X_SUBLANES_BF16, CAST_BLOCK_BYTES // (4 * f) // V7X_SUBLANES_BF16
                                 * V7X_SUBLANES_BF16))
    assert r % row_block == 0 and rows_out % row_block == 0 and row_block % V7X_SUBLANES_BF16 == 0
    n_valid = r // row_block
    padded_rows = rows_out > r
    if col_chunk is None:
        out_spec = pl.BlockSpec((row_block, f), lambda i, p: (i, 0))
        out_shape = (rows_out, f)
    else:
        assert fp % col_chunk == 0 and col_chunk % V7X_LANES == 0 and not padded_rows
        n_chunks = fp // col_chunk
        out_spec = pl.BlockSpec((n_chunks, row_block, col_chunk), lambda i, p: (p, i, 0))
        out_shape = (parts * n_chunks, r, col_chunk)
    return pl.pallas_call(
        functools.partial(_cast_kernel, valid_row_blocks=n_valid if padded_rows else None),
        grid=(rows_out // row_block, parts),
        in_specs=[pl.BlockSpec((None, row_block, f), lambda i, p: (idx, jnp.minimum(i, n_valid - 1), p))],
        out_specs=out_spec,
        out_shape=jax.ShapeDtypeStruct(out_shape, BF16),
        compiler_params=_compiler_params(("parallel", "arbitrary"),
                                         2 * row_block * (4 * f + 2 * fp) + (8 << 20)),
        name="cast_weight",
    )(w)


def _pad_halves(a, f, fp):
    pad = lambda h: jnp.pad(h, ((0, 0), (0, fp - f)))
    return jnp.concatenate([pad(a[:, :f]), pad(a[:, f:])], axis=1)


def _prep_ffn(w_up, conv_w, conv_b, w_down, layer, tc):
    f = w_down.shape[1]
    fp = -(-f // tc) * tc
    w1 = _cast_weight(w_up, layer, parts=2, fp=fp, col_chunk=tc)
    cw = _pad_halves(conv_w[layer], f, fp)
    cb = _pad_halves(conv_b[layer][None, :], f, fp)
    w2 = _cast_weight(w_down, layer, rows_out=fp, row_block=math.gcd(f, fp))
    return w1, cw, cb, w2


def kernel(x, c, ctx, c_ctx, norm_mix_g, norm_ffn_g, w_ada, b_ada, na_w_qkv, na_q_g, na_k_g, na_rpb, na_w_o,
           gm_w_in, gm_v_g, gm_w_s, gm_b_s, gm_w_out, sc_w_in, sc_conv_w, sc_w_out,
           ffn_w_up, ffn_conv_w, ffn_conv_b, ffn_w_down):
    bn, seq, d = x.shape
    ctx_len = ctx.shape[1]
    depth = w_ada.shape[0]
    head_dim = d // NA_HEADS
    rows = seq // GRID_W
    assert bn + 1 <= COND_ROWS and seq % GRID_W == 0 and ctx_len % GM_CHUNK == 0

    cond = jnp.concatenate([c, c_ctx[None, :], jnp.zeros((COND_ROWS - bn - 1, d), F32)], axis=0)
    mods = _adaln(cond, w_ada, b_ada).reshape(depth, COND_ROWS, N_MOD, d)

    lat_row = lambda i, tm: i // (seq // tm)
    ctx_row = lambda i, tm: bn

    ffn_tc = 512 if ffn_w_down.shape[1] >= 512 else 128
    sc_tc = _tile(d, 512)
    gm_width = gm_w_out.shape[1]
    gm_tc = max(gm_width // GM_GROUPS, _tile(gm_width, 512))
    proj_tn = _tile(d, 1024)
    qkv_tn = _tile(d, 2048)
    lat_tm = 1024
    lat_sub = 512

    xl = x.reshape(bn * seq, d)
    xc = ctx.reshape(bn * ctx_len, d)
    for i in range(depth):
        last = i == depth - 1
        mixer, jx = i % N_MIXERS, i // N_MIXERS
        g_mix = norm_mix_g[i][None, :]
        g_ffn = norm_ffn_g[i][None, :]
        do_ctx = (not last) or mixer == 0
        if mixer == 0:
            w_qkv = _cast_weight(na_w_qkv, jx, col_chunk=qkv_tn)
            w_o = _cast_weight(na_w_o, jx, col_chunk=proj_tn)
            q_gain = na_q_g[jx] * (head_dim ** -0.5 * LOG2_E)
            head_gain = jnp.concatenate([jnp.tile(q_gain, NA_HEADS), jnp.tile(na_k_g[jx], NA_HEADS),
                                         jnp.ones((d,), F32)])[None, :]
            qkv = _qkv(xl, g_mix, mods, i, lat_row, w_qkv, head_gain, head_dim)
            qkv_c = _qkv(xc, g_mix, mods, i, ctx_row, w_qkv, head_gain, head_dim).reshape(bn, ctx_len, 3 * d)
            att = _na_attention(qkv, qkv_c, na_rpb[jx], bn, rows, d, head_dim)
            xl_new = _proj_res(att, w_o, xl, mods, i, lat_row, 2)
            if not last:
                att_c = _ctx_attention(qkv_c, d, head_dim).reshape(bn * ctx_len, d)
                xc = _proj_res(att_c, w_o, xc, mods, i, ctx_row, 2)
            xl = xl_new
        elif mixer == 1:
            w_in = _cast_weight(gm_w_in, jx, col_chunk=gm_tc)
            w_out = _cast_weight(gm_w_out, jx)
            w_s = gm_w_s[jx].astype(BF16)
            width = w_out.shape[0]
            v_g = gm_v_g[jx][None, :]
            b_s_cols = jnp.repeat(gm_b_s[jx].T, width // GM_GROUPS, axis=1)
            xl = _gmlp(xl, g_mix, mods, i, lat_row, w_in, v_g, w_s, b_s_cols, w_out, lat_tm)
            if not last:
                xc = _gmlp(xc, g_mix, mods, i, ctx_row, w_in, v_g, w_s, b_s_cols, w_out, ctx_len)
        else:
            w1 = _cast_weight(sc_w_in, jx, parts=3, col_chunk=sc_tc)
            cw, w2 = sc_conv_w[jx], _cast_weight(sc_w_out, jx)
            xl = _gated_conv("sc", xl, seq, g_mix, mods, i, lat_row, (0, 1, 2), w1, cw, None, w2, sc_tc, lat_tm,
                             sub_pref=lat_sub)
            if not last:
                xc = _gated_conv("sc", xc, ctx_len, g_mix, mods, i, ctx_row, (0, 1, 2), w1, cw, None, w2, sc_tc,
                                 ctx_len)
        w1, cw, cb, w2 = _prep_ffn(ffn_w_up, ffn_conv_w, ffn_conv_b, ffn_w_down, i, ffn_tc)
        xl = _gated_conv("ffn", xl, seq, g_ffn, mods, i, lat_row, (3, 4, 5), w1, cw, cb, w2, ffn_tc, lat_tm,
                         sub_pref=lat_sub)
        if not last:
            xc = _gated_conv("ffn", xc, ctx_len, g_ffn, mods, i, ctx_row, (3, 4, 5), w1, cw, cb, w2, ffn_tc,
                             ctx_len)
    return xl.reshape(bn, seq, d)
```

```python
import functools
import math

import numpy as np
import jax
import jax.numpy as jnp
from jax import lax
from jax.experimental import pallas as pl
from jax.experimental.pallas import tpu as pltpu

GRID_W = 64
NA_HEADS = 16
WIN_H = 8
WIN_W = 16
GM_GROUPS = 16
GM_CHUNK = 128
N_MOD = 6
N_MIXERS = 3
EPS = 1e-6

V7X_VMEM_BYTES = 64 * 1024 * 1024
V7X_SUBLANES_F32 = 8
V7X_SUBLANES_BF16 = 16
V7X_LANES = 128
V7X_MXU_DIM = 256

F32 = jnp.float32
BF16 = jnp.bfloat16

NA_QROWS = 2
COND_ROWS = V7X_SUBLANES_F32


def _compiler_params(semantics, vmem_bytes):
    return pltpu.CompilerParams(dimension_semantics=semantics,
                                vmem_limit_bytes=int(min(vmem_bytes, VMEM_LIMIT_BYTES)))


def _tile(n, pref):
    t = min(n, pref)
    while n % t:
        t -= 1
    return t


VMEM_LIMIT_BYTES = V7X_VMEM_BYTES - (4 << 20)
VMEM_TEMP_BYTES = 6 << 20


def _row_tile_buffers(tile_bytes, other_bytes):
    for x_bufs, out_bufs in ((2, 2), (1, 2), (1, 1)):
        if (x_bufs + out_bufs) * tile_bytes + other_bytes + VMEM_TEMP_BYTES <= VMEM_LIMIT_BYTES:
            break
    return x_bufs, out_bufs


def _norm_mod(x, gain, shift):
    ms = jnp.mean(x * x, axis=-1, keepdims=True)
    return x * lax.rsqrt(ms + EPS) * gain + shift


NORM_CHUNK = 32


def _store_norm_mod(hs_ref, sources, g, shift, scale):
    bounds = np.cumsum([0] + [ref.shape[0] for ref, _ in sources])
    total = int(bounds[-1])
    assert hs_ref.shape[0] == total
    gain = g * (1.0 + scale)
    for c0 in range(0, total, NORM_CHUNK):
        c1 = min(c0 + NORM_CHUNK, total)
        pieces = []
        for (ref, zero_pred), b0, b1 in zip(sources, bounds[:-1], bounds[1:]):
            lo, hi = max(c0, int(b0)), min(c1, int(b1))
            if lo >= hi:
                continue
            h = _norm_mod(ref[lo - int(b0):hi - int(b0), :], gain, shift)
            pieces.append(h if zero_pred is None else jnp.where(zero_pred, 0.0, h))
        h = pieces[0] if len(pieces) == 1 else jnp.concatenate(pieces, axis=0)
        hs_ref[c0:c1, :] = h.astype(BF16)


def _adaln_kernel(c_ref, w_ref, b_ref, o_ref):
    c = c_ref[...]
    a = (c * jax.nn.sigmoid(c)).astype(BF16)
    o_ref[...] = jnp.dot(a, w_ref[...].astype(BF16), preferred_element_type=F32) + b_ref[...]


def _adaln(cond, w_ada, b_ada):
    depth, d, n = w_ada.shape
    tn = _tile(n, 1024)
    return pl.pallas_call(
        _adaln_kernel,
        grid=(depth, n // tn),
        in_specs=[
            pl.BlockSpec((COND_ROWS, d), lambda l, j: (0, 0)),
            pl.BlockSpec((None, d, tn), lambda l, j: (l, 0, j)),
            pl.BlockSpec((None, 1, tn), lambda l, j: (l, 0, j)),
        ],
        out_specs=pl.BlockSpec((None, COND_ROWS, tn), lambda l, j: (l, 0, j)),
        out_shape=jax.ShapeDtypeStruct((depth, COND_ROWS, n), F32),
        compiler_params=_compiler_params(("arbitrary", "arbitrary"), 2 * d * tn * 4 + (8 << 20)),
        name="adaln",
    )(cond, w_ada, b_ada.reshape(depth, 1, n))


def _mod_spec(layer, row_fn, width, col_fn=None):
    if col_fn is None:
        return pl.BlockSpec((None, None, N_MOD, width), lambda i, j: (layer, row_fn(i), 0, 0))
    return pl.BlockSpec((None, None, N_MOD, width), lambda i, j: (layer, row_fn(i), 0, col_fn(j)))


SUB_ROWS = 256


def _skewed_subtiles(n_sub, first, second, ahead=1):
    pending = [first(r) for r in range(min(ahead, n_sub))]
    for r in range(n_sub):
        if r + ahead < n_sub:
            pending.append(first(r + ahead))
        second(r, pending.pop(0))


def _proj_res_kernel(a_ref, w_ref, x_ref, mod_ref, o_ref, *, gate_row, sub_rows):
    gate = mod_ref[gate_row:gate_row + 1, :]

    def matmul(r):
        return jnp.dot(a_ref[r * sub_rows:(r + 1) * sub_rows, :], w_ref[...], preferred_element_type=F32)

    def residual(r, y):
        rows = slice(r * sub_rows, (r + 1) * sub_rows)
        o_ref[rows, :] = x_ref[rows, :] + gate * y

    _skewed_subtiles(o_ref.shape[0] // sub_rows, matmul, residual)


def _proj_res(a, w, x, mods, layer, row_fn, gate_row, tm_pref=1024):
    m, k = a.shape
    n_tiles, _, tn = w.shape
    n = n_tiles * tn
    tm = _tile(m, tm_pref)
    vmem = 2 * (tm * k * 2 + k * tn * 2 + 2 * tm * tn * 4) + (8 << 20)
    return pl.pallas_call(
        functools.partial(_proj_res_kernel, gate_row=gate_row, sub_rows=_tile(tm, SUB_ROWS)),
        grid=(m // tm, n // tn),
        in_specs=[
            pl.BlockSpec((tm, k), lambda i, j: (i, 0)),
            pl.BlockSpec((None, k, tn), lambda i, j: (j, 0, 0)),
            pl.BlockSpec((tm, tn), lambda i, j: (i, j)),
            _mod_spec(layer, lambda i: row_fn(i, tm), tn, lambda j: j),
        ],
        out_specs=pl.BlockSpec((tm, tn), lambda i, j: (i, j)),
        out_shape=jax.ShapeDtypeStruct((m, n), F32),
        compiler_params=_compiler_params(("parallel", "arbitrary"), vmem),
        name="proj_res",
    )(a, w, x, mods)


def _qkv_kernel(x_ref, g_ref, mod_ref, w_ref, hg_ref, o_ref, hs_ref, *, n_norm_tiles, head_dim, sub_rows):
    j = pl.program_id(1)

    @pl.when(j == 0)
    def _():
        _store_norm_mod(hs_ref, [(x_ref, None)], g_ref[...], mod_ref[0:1, :], mod_ref[1:2, :])

    tm, tn = o_ref.shape
    is_norm_tile = j < n_norm_tiles
    hg = hg_ref[...]

    def matmul(r):
        return jnp.dot(hs_ref[r * sub_rows:(r + 1) * sub_rows, :], w_ref[...], preferred_element_type=F32)

    def head_norm(r, y):
        rows = slice(r * sub_rows, (r + 1) * sub_rows)
        for hh in range(tn // head_dim):
            sl = slice(hh * head_dim, (hh + 1) * head_dim)
            t = y[:, sl]
            inv = lax.rsqrt(jnp.mean(t * t, axis=-1, keepdims=True) + EPS)
            o_ref[rows, sl] = (t * jnp.where(is_norm_tile, inv, 1.0) * hg[:, sl]).astype(o_ref.dtype)

    _skewed_subtiles(tm // sub_rows, matmul, head_norm)


def _qkv(x, norm_g, mods, layer, row_fn, w_qkv, head_gain, head_dim, tm_pref=1024):
    m, d = x.shape
    n_tiles, _, tn = w_qkv.shape
    n = n_tiles * tn
    tm = _tile(m, tm_pref)
    assert (n // 3) % tn == 0 and tn % head_dim == 0
    vmem = 2 * (tm * d * 4 + d * tn * 2 + tm * tn * 2) + tm * d * 2 + 3 * tm * tn * 4 + (8 << 20)
    return pl.pallas_call(
        functools.partial(_qkv_kernel, n_norm_tiles=2 * (n // 3) // tn, head_dim=head_dim,
                          sub_rows=_tile(tm, SUB_ROWS)),
        grid=(m // tm, n // tn),
        in_specs=[
            pl.BlockSpec((tm, d), lambda i, j: (i, 0)),
            pl.BlockSpec((1, d), lambda i, j: (0, 0)),
            _mod_spec(layer, lambda i: row_fn(i, tm), d),
            pl.BlockSpec((None, d, tn), lambda i, j: (j, 0, 0)),
            pl.BlockSpec((1, tn), lambda i, j: (0, j)),
        ],
        out_specs=pl.BlockSpec((tm, tn), lambda i, j: (i, j)),
        out_shape=jax.ShapeDtypeStruct((m, n), BF16),
        scratch_shapes=[pltpu.VMEM((tm, d), BF16)],
        compiler_params=_compiler_params(("parallel", "arbitrary"), vmem),
        name="qkv",
    )(x, norm_g, mods, w_qkv, head_gain)


LOG2_E = math.log2(math.e)
_NT_DIMS = (((1,), (1,)), ((), ()))


def _na_kernel(*refs, nb, heads, head_dim):
    refs = refs[2:]
    q_ref = refs[0]
    k_refs = refs[1:1 + nb]
    v_refs = refs[1 + nb:1 + 2 * nb]
    kc_ref, vc_ref, tbl_ref, o_ref = refs[1 + 2 * nb:]

    def scores(h):
        sl = slice(h * head_dim, (h + 1) * head_dim)
        q = q_ref[:, sl]
        k = jnp.concatenate([r[:, sl] for r in k_refs], axis=0)
        s = lax.dot_general(q, k, _NT_DIMS, preferred_element_type=F32) + tbl_ref[h]
        sc = lax.dot_general(q, kc_ref[:, sl], _NT_DIMS, preferred_element_type=F32)
        return s, sc

    ones_lat = jnp.ones((nb * k_refs[0].shape[0], head_dim), BF16)
    ones_ctx = jnp.ones((vc_ref.shape[0], head_dim), BF16)

    def attend(h, s_sc):
        s, sc = s_sc
        sl = slice(h * head_dim, (h + 1) * head_dim)
        v = jnp.concatenate([jnp.concatenate([r[:, sl] for r in v_refs], axis=0), ones_lat], axis=1)
        vc = jnp.concatenate([vc_ref[:, sl], ones_ctx], axis=1)
        m = jnp.maximum(jnp.max(s, axis=-1, keepdims=True), jnp.max(sc, axis=-1, keepdims=True))
        p = jnp.exp2(s - m).astype(BF16)
        pc = jnp.exp2(sc - m).astype(BF16)
        o = (jnp.dot(p, v, preferred_element_type=F32) + jnp.dot(pc, vc, preferred_element_type=F32))
        o_ref[:, sl] = (o[:, :head_dim] * (1.0 / o[:, head_dim:head_dim + 1])).astype(o_ref.dtype)

    _skewed_subtiles(heads, scores, attend, ahead=2)


def _na_geometry(rows, qrows):
    kh = min(WIN_H, rows)
    nb = min(kh + qrows - 1, rows)
    n_blk = rows // qrows
    band0s, keys, types, row_rel = [], {}, [], []
    for blk in range(n_blk):
        r0 = blk * qrows
        q_row = r0 + np.arange(qrows)
        r_start = np.clip(q_row - kh // 2, 0, rows - kh)
        band0 = min(int(np.clip(r0 - kh // 2, 0, rows - kh)), rows - nb)
        k_row = band0 + np.arange(nb)
        row_ok = (k_row[None, :] >= r_start[:, None]) & (k_row[None, :] < r_start[:, None] + kh)
        rel = np.where(row_ok, k_row[None, :] - q_row[:, None] + WIN_H - 1, -1)
        key = rel.tobytes()
        if key not in keys:
            keys[key] = len(row_rel)
            row_rel.append(rel)
        types.append(keys[key])
        band0s.append(band0)
    return nb, n_blk, np.array(band0s, np.int32), np.array(types, np.int32), row_rel


def _na_bias_table(rpb, row_rel):
    heads, n_dr, n_dc = rpb.shape
    period = 2 * GRID_W
    assert n_dc <= period
    r = jnp.pad(rpb, ((0, 0), (0, 0), (0, period - n_dc)))
    skew = jnp.tile(r, (1, 1, GRID_W))[:, :, :GRID_W * (period - 1)].reshape(heads, n_dr, GRID_W, period - 1)
    toep = skew[:, :, :, WIN_W - 1:WIN_W - 1 + GRID_W]
    q_col = np.arange(GRID_W)
    c_start = np.clip(q_col - WIN_W // 2, 0, GRID_W - WIN_W)
    col_ok = (q_col[None, :] >= c_start[:, None]) & (q_col[None, :] < c_start[:, None] + WIN_W)
    toep = jnp.where(col_ok, toep, -jnp.inf)
    masked = jnp.full((heads, GRID_W, GRID_W), -jnp.inf, F32)
    tables = []
    for rel in row_rel:
        q_blocks = []
        for qr in range(rel.shape[0]):
            q_blocks.append(jnp.concatenate(
                [toep[:, rel[qr, kr]] if rel[qr, kr] >= 0 else masked for kr in range(rel.shape[1])], axis=-1))
        tables.append(jnp.concatenate(q_blocks, axis=-2))
    return jnp.stack(tables).astype(F32)


def _na_attention(qkv, qkv_ctx, rpb, bn, rows, d, head_dim):
    qrows = min(NA_QROWS, rows)
    nb, n_blk, band0s, types, row_rel = _na_geometry(rows, qrows)
    qw, nk = qrows * GRID_W, nb * GRID_W
    ctx_len = qkv_ctx.shape[1]
    tbl = _na_bias_table(rpb * LOG2_E, row_rel)
    band_tbl = jnp.asarray(band0s)
    type_tbl = jnp.asarray(types)
    kv_view = qkv.reshape(bn * rows, GRID_W, 3 * d)

    def kv_spec(t, col):
        return pl.BlockSpec((None, GRID_W, d), lambda b, i, band, typ: (b * rows + band[i] + t, 0, col))

    in_specs = ([pl.BlockSpec((qw, d), lambda b, i, band, typ: (b * n_blk + i, 0))]
                + [kv_spec(t, 1) for t in range(nb)]
                + [kv_spec(t, 2) for t in range(nb)]
                + [pl.BlockSpec((None, ctx_len, d), lambda b, i, band, typ: (b, 0, 1)),
                   pl.BlockSpec((None, ctx_len, d), lambda b, i, band, typ: (b, 0, 2)),
                   pl.BlockSpec((None, NA_HEADS, qw, nk), lambda b, i, band, typ: (typ[i], 0, 0, 0))])
    vmem = 2 * (2 * qw * d * 2 + 2 * nk * d * 2 + 2 * ctx_len * d * 2 + NA_HEADS * qw * nk * 4) + (12 << 20)
    grid_spec = pltpu.PrefetchScalarGridSpec(
        num_scalar_prefetch=2,
        grid=(bn, n_blk),
        in_specs=in_specs,
        out_specs=pl.BlockSpec((qw, d), lambda b, i, band, typ: (b * n_blk + i, 0)),
    )
    return pl.pallas_call(
        functools.partial(_na_kernel, nb=nb, heads=NA_HEADS, head_dim=head_dim),
        grid_spec=grid_spec,
        out_shape=jax.ShapeDtypeStruct((bn * rows * GRID_W, d), BF16),
        compiler_params=_compiler_params(("parallel", "arbitrary"), vmem),
        name="na_attention",
    )(band_tbl, type_tbl, qkv, *([kv_view] * (2 * nb)), qkv_ctx, qkv_ctx, tbl)


def _ctx_attn_kernel(q_ref, k_ref, v_ref, o_ref, *, heads, head_dim):
    for h in range(heads):
        sl = slice(h * head_dim, (h + 1) * head_dim)
        s = lax.dot_general(q_ref[:, sl], k_ref[:, sl], _NT_DIMS, preferred_element_type=F32)
        m = jnp.max(s, axis=-1, keepdims=True)
        p = jnp.exp2(s - m)
        l = jnp.sum(p, axis=-1, keepdims=True)
        o = jnp.dot(p.astype(BF16), v_ref[:, sl], preferred_element_type=F32)
        o_ref[:, sl] = (o * (1.0 / l)).astype(o_ref.dtype)


def _ctx_attention(qkv_ctx, d, head_dim):
    bn, ctx_len, _ = qkv_ctx.shape
    spec = lambda col: pl.BlockSpec((None, ctx_len, d), lambda b: (b, 0, col))
    return pl.pallas_call(
        functools.partial(_ctx_attn_kernel, heads=NA_HEADS, head_dim=head_dim),
        grid=(bn,),
        in_specs=[spec(0), spec(1), spec(2)],
        out_specs=pl.BlockSpec((None, ctx_len, d), lambda b: (b, 0, 0)),
        out_shape=jax.ShapeDtypeStruct((bn, ctx_len, d), BF16),
        compiler_params=_compiler_params(("arbitrary",), 8 * ctx_len * d * 2 + (12 << 20)),
        name="ctx_attention",
    )(qkv_ctx, qkv_ctx, qkv_ctx)


HALO = V7X_SUBLANES_F32


def _shift_rows(u, tm):
    n = u.shape[0]
    prev = pltpu.roll(u, 1, 0)[HALO:HALO + tm]
    nxt = pltpu.roll(u, n - 1, 0)[HALO:HALO + tm]
    return prev, nxt


def _silu(x):
    h = 0.5 * x
    return h + h * jnp.tanh(h)


def _conv3(u, cw, rows):
    prev, nxt = _shift_rows(u, rows)
    return prev * cw[0:1] + u[HALO:HALO + rows] * cw[1:2] + nxt * cw[2:3]


def _gated_conv_kernel(*refs, kind, tiles_per_seq, mod_rows, sub_rows):
    x_ref, xp_ref, xn_ref, g_ref, mod_ref = refs[:5]
    if kind == "ffn":
        w1_refs, cw_refs, cb_refs = refs[5:7], refs[7:9], refs[9:11]
        w2_ref, o_ref, hs_ref = refs[11:]
    else:
        w1_refs, cw_refs = refs[5:8], refs[8:9]
        w2_ref, o_ref, hs_ref = refs[9:]
    i = pl.program_id(0)
    j = pl.program_id(1)
    tm = x_ref.shape[0]
    r_shift, r_scale, r_gate = mod_rows

    @pl.when(j == 0)
    def _():
        g = g_ref[...]
        shift = mod_ref[r_shift:r_shift + 1, :]
        scale = mod_ref[r_scale:r_scale + 1, :]
        pos = i % tiles_per_seq
        _store_norm_mod(hs_ref, [(xp_ref, pos == 0), (x_ref, None), (xn_ref, pos == tiles_per_seq - 1)],
                        g, shift, scale)
        o_ref[...] = x_ref[...]

    res_gate = mod_ref[r_gate:r_gate + 1, :]

    def first_matmuls(r):
        lhs = hs_ref[r * sub_rows:r * sub_rows + sub_rows + 2 * HALO, :]
        return [jnp.dot(lhs, w[...], preferred_element_type=F32) for w in w1_refs]

    def mid_and_second_matmul(r, u):
        if kind == "ffn":
            gate, up = [_conv3(u[p], cw_refs[p][...], sub_rows) + cb_refs[p][...] for p in range(2)]
            a = _silu(gate) * up
        else:
            a = u[0][HALO:HALO + sub_rows] * _conv3(u[1] * u[2], cw_refs[0][...], sub_rows)
        o_ref[r * sub_rows:(r + 1) * sub_rows, :] += res_gate * jnp.dot(a.astype(BF16), w2_ref[...],
                                                                       preferred_element_type=F32)

    _skewed_subtiles(tm // sub_rows, first_matmuls, mid_and_second_matmul)


def _gated_conv(kind, x, seq_len, norm_g, mods, layer, row_fn, mod_rows, w1, cw, cb, w2, tc, tm_pref,
                sub_pref=SUB_ROWS):
    m, d = x.shape
    parts = 2 if kind == "ffn" else 3
    cparts = 2 if kind == "ffn" else 1
    n_chunks = w2.shape[0] // tc
    assert w1.shape == (parts * n_chunks, d, tc) and cw.shape[1] == cparts * n_chunks * tc
    tm = _tile(seq_len, tm_pref)
    sub_rows = _tile(tm, sub_pref)
    assert sub_rows % V7X_SUBLANES_BF16 == 0
    tiles_per_seq = seq_len // tm
    hb = tm // HALO
    n_hblk = m // HALO
    other = 2 * (d * parts * tc * 2 + tc * d * 2) + (tm + 2 * HALO) * d * 2
    x_bufs, out_bufs = _row_tile_buffers(tm * d * 4, other)
    vmem = (x_bufs + out_bufs) * tm * d * 4 + other + VMEM_TEMP_BYTES
    col_spec = lambda rows, p: pl.BlockSpec((rows, tc), lambda i, j: (0, p * n_chunks + j))
    in_specs = [
        pl.BlockSpec((tm, d), lambda i, j: (i, 0), pipeline_mode=pl.Buffered(x_bufs)),
        pl.BlockSpec((HALO, d), lambda i, j: (jnp.maximum(i * hb - 1, 0), 0)),
        pl.BlockSpec((HALO, d), lambda i, j: (jnp.minimum((i + 1) * hb, n_hblk - 1), 0)),
        pl.BlockSpec((1, d), lambda i, j: (0, 0)),
        _mod_spec(layer, lambda i: row_fn(i, tm), d),
    ]
    in_specs += [pl.BlockSpec((None, d, tc), functools.partial(lambda p, i, j: (p * n_chunks + j, 0, 0), p))
                 for p in range(parts)]
    in_specs += [col_spec(3, p) for p in range(cparts)]
    operands = [x, x, x, norm_g, mods] + [w1] * parts + [cw] * cparts
    if kind == "ffn":
        in_specs += [col_spec(1, p) for p in range(cparts)]
        operands += [cb] * cparts
    in_specs.append(pl.BlockSpec((tc, d), lambda i, j: (j, 0)))
    operands.append(w2)
    return pl.pallas_call(
        functools.partial(_gated_conv_kernel, kind=kind, tiles_per_seq=tiles_per_seq, mod_rows=mod_rows,
                          sub_rows=sub_rows),
        grid=(m // tm, n_chunks),
        in_specs=in_specs,
        out_specs=pl.BlockSpec((tm, d), lambda i, j: (i, 0), pipeline_mode=pl.Buffered(out_bufs)),
        out_shape=jax.ShapeDtypeStruct((m, d), F32),
        scratch_shapes=[pltpu.VMEM((tm + 2 * HALO, d), BF16)],
        compiler_params=_compiler_params(("parallel", "arbitrary"), vmem),
        name=kind,
    )(*operands)


def _gelu_tanh(x):
    return jax.nn.gelu(x, approximate=True)


def _gmlp_kernel(x_ref, g_ref, mod_ref, win_ref, vg_ref, ws_ref, bs_ref, wout_ref, o_ref,
                 hs_ref, v_ref, ssq_ref, *, n_chunks, group_dim, width, sub_rows):
    j = pl.program_id(1)
    tm = x_ref.shape[0]
    tc = win_ref.shape[1]
    n_sub = tm // sub_rows

    @pl.when(j == 0)
    def _():
        _store_norm_mod(hs_ref, [(x_ref, None)], g_ref[...], mod_ref[0:1, :], mod_ref[1:2, :])
        ssq_ref[...] = jnp.zeros_like(ssq_ref)
        o_ref[...] = x_ref[...]

    def in_matmul(r):
        return jnp.dot(hs_ref[r * sub_rows:(r + 1) * sub_rows, :], win_ref[...], preferred_element_type=F32)

    @pl.when(j < n_chunks)
    def _():
        def keep_v(r, y):
            rows = slice(r * sub_rows, (r + 1) * sub_rows)
            t = _gelu_tanh(y)
            v_ref[j, rows, :] = t
            ssq_ref[rows, :] += jnp.sum(t * t, axis=-1, keepdims=True)

        _skewed_subtiles(n_sub, in_matmul, keep_v)

    @pl.when(j >= n_chunks)
    def _():
        jj = j - n_chunks
        vg = vg_ref[...]
        res_gate = mod_ref[2:3, :]
        bs = bs_ref[...]
        gpc = tc // group_dim
        ws = [ws_ref[jj * gpc + gi] for gi in range(gpc)]

        def gate_and_project(r, y):
            rows = slice(r * sub_rows, (r + 1) * sub_rows)
            inv = lax.rsqrt(ssq_ref[rows, :] * (1.0 / width) + EPS)
            vn = (v_ref[jj, rows, :] * inv * vg).astype(BF16)
            cols = []
            for gi in range(gpc):
                csl = slice(gi * group_dim, (gi + 1) * group_dim)
                cols.append(jnp.concatenate(
                    [jnp.dot(ws[gi], vn[c * GM_CHUNK:(c + 1) * GM_CHUNK, csl], preferred_element_type=F32)
                     + bs[:, csl] for c in range(sub_rows // GM_CHUNK)], axis=0))
            sv = jnp.concatenate(cols, axis=1)
            o_ref[rows, :] += res_gate * jnp.dot((_gelu_tanh(y) * sv).astype(BF16), wout_ref[...],
                                                 preferred_element_type=F32)

        _skewed_subtiles(n_sub, in_matmul, gate_and_project)


def _gmlp(x, norm_g, mods, layer, row_fn, w_in, v_g, w_s, b_s_cols, w_out, tm_pref):
    m, d = x.shape
    width = w_out.shape[0]
    group_dim = width // GM_GROUPS
    tc = w_in.shape[2]
    n_chunks = width // tc
    assert w_in.shape[0] == 2 * n_chunks and tc % group_dim == 0
    tm = _tile(m, tm_pref)
    sub_rows = _tile(tm, SUB_ROWS)
    assert sub_rows % GM_CHUNK == 0
    other = (2 * (d * tc * 2 + tc * d * 2 + GM_CHUNK * tc * 4) + tm * d * 2 + tm * width * 4
             + GM_GROUPS * GM_CHUNK * GM_CHUNK * 2 + tm * V7X_LANES * 4)
    x_bufs, out_bufs = _row_tile_buffers(tm * d * 4, other)
    vmem = (x_bufs + out_bufs) * tm * d * 4 + other + VMEM_TEMP_BYTES
    return pl.pallas_call(
        functools.partial(_gmlp_kernel, n_chunks=n_chunks, group_dim=group_dim, width=width,
                          sub_rows=sub_rows),
        grid=(m // tm, 2 * n_chunks),
        in_specs=[
            pl.BlockSpec((tm, d), lambda i, j: (i, 0), pipeline_mode=pl.Buffered(x_bufs)),
            pl.BlockSpec((1, d), lambda i, j: (0, 0)),
            _mod_spec(layer, lambda i: row_fn(i, tm), d),
            pl.BlockSpec((None, d, tc),
                         lambda i, j: (jnp.where(j < n_chunks, j + n_chunks, j - n_chunks), 0, 0)),
            pl.BlockSpec((1, tc), lambda i, j: (0, jnp.maximum(j - n_chunks, 0))),
            pl.BlockSpec((GM_GROUPS, GM_CHUNK, GM_CHUNK), lambda i, j: (0, 0, 0)),
            pl.BlockSpec((GM_CHUNK, tc), lambda i, j: (0, jnp.maximum(j - n_chunks, 0))),
            pl.BlockSpec((tc, d), lambda i, j: (jnp.maximum(j - n_chunks, 0), 0)),
        ],
        out_specs=pl.BlockSpec((tm, d), lambda i, j: (i, 0), pipeline_mode=pl.Buffered(out_bufs)),
        out_shape=jax.ShapeDtypeStruct((m, d), F32),
        scratch_shapes=[pltpu.VMEM((tm, d), BF16), pltpu.VMEM((n_chunks, tm, tc), F32), pltpu.VMEM((tm, 1), F32)],
        compiler_params=_compiler_params(("parallel", "arbitrary"), vmem),
        name="gmlp",
    )(x, norm_g, mods, w_in, v_g, w_s, b_s_cols, w_out)


CAST_BLOCK_BYTES = 6 << 20


def _cast_kernel(x_ref, o_ref, *, valid_rows):
    rows, f = x_ref.shape
    x = x_ref[...]
    if valid_rows is not None:
        row = pl.program_id(0) * rows + lax.broadcasted_iota(jnp.int32, (rows, 1), 0)
        x = jnp.where(row < valid_rows, x, 0.0)
    x = x.astype(o_ref.dtype)
    if len(o_ref.shape) == 2:
        o_ref[...] = x
        return
    n_chunks, _, tc = o_ref.shape
    for c in range(n_chunks):
        width = max(0, min(tc, f - c * tc))
        if width:
            o_ref[c, :, :width] = x[:, c * tc:c * tc + width]
        if width < tc:
            o_ref[c, :, width:] = jnp.zeros((rows, tc - width), o_ref.dtype)


def _cast_weight(w, idx, parts=1, fp=None, col_chunk=None, rows_out=None, row_block=None):
    _, r, c = w.shape
    f = c // parts
    fp = f if fp is None else fp
    rows_out = r if rows_out is None else rows_out
    assert parts == 1 or f % V7X_LANES == 0
    assert col_chunk is not None or (parts == 1 and fp == f)
    padded_rows = rows_out > r
    if row_block is None:
        row_pref = max(V7X_SUBLANES_BF16, CAST_BLOCK_BYTES // (4 * f))
        row_block = max(t for t in range(V7X_SUBLANES_BF16, row_pref + 1, V7X_SUBLANES_BF16)
                        if rows_out % t == 0)
    assert rows_out % row_block == 0 and row_block % V7X_SUBLANES_BF16 == 0
    assert r % row_block == 0 or (padded_rows and rows_out - row_block < r)
    if col_chunk is None:
        out_spec = pl.BlockSpec((row_block, f), lambda i, p: (i, 0))
        out_shape = (rows_out, f)
    else:
        assert fp % col_chunk == 0 and col_chunk % V7X_LANES == 0 and not padded_rows
        n_chunks = fp // col_chunk
        out_spec = pl.BlockSpec((n_chunks, row_block, col_chunk), lambda i, p: (p, i, 0))
        out_shape = (parts * n_chunks, r, col_chunk)
    return pl.pallas_call(
        functools.partial(_cast_kernel, valid_rows=r if padded_rows else None),
        grid=(rows_out // row_block, parts),
        in_specs=[pl.BlockSpec((None, row_block, f), lambda i, p: (idx, i, p))],
        out_specs=out_spec,
        out_shape=jax.ShapeDtypeStruct(out_shape, BF16),
        compiler_params=_compiler_params(("parallel", "arbitrary"),
                                         2 * row_block * (4 * f + 2 * fp) + (8 << 20)),
        name="cast_weight",
    )(w)


def _pad_halves(a, f, fp):
    pad = lambda h: jnp.pad(h, ((0, 0), (0, fp - f)))
    return jnp.concatenate([pad(a[:, :f]), pad(a[:, f:])], axis=1)


def _prep_ffn(w_up, conv_w, conv_b, w_down, layer, tc):
    f = w_down.shape[1]
    fp = -(-f // tc) * tc
    w1 = _cast_weight(w_up, layer, parts=2, fp=fp, col_chunk=tc)
    cw = _pad_halves(conv_w[layer], f, fp)
    cb = _pad_halves(conv_b[layer][None, :], f, fp)
    w2 = _cast_weight(w_down, layer, rows_out=fp)
    return w1, cw, cb, w2


def kernel(x, c, ctx, c_ctx, norm_mix_g, norm_ffn_g, w_ada, b_ada, na_w_qkv, na_q_g, na_k_g, na_rpb, na_w_o,
           gm_w_in, gm_v_g, gm_w_s, gm_b_s, gm_w_out, sc_w_in, sc_conv_w, sc_w_out,
           ffn_w_up, ffn_conv_w, ffn_conv_b, ffn_w_down):
    bn, seq, d = x.shape
    ctx_len = ctx.shape[1]
    depth = w_ada.shape[0]
    head_dim = d // NA_HEADS
    rows = seq // GRID_W
    assert bn + 1 <= COND_ROWS and seq % GRID_W == 0 and ctx_len % GM_CHUNK == 0

    cond = jnp.concatenate([c, c_ctx[None, :], jnp.zeros((COND_ROWS - bn - 1, d), F32)], axis=0)
    mods = _adaln(cond, w_ada, b_ada).reshape(depth, COND_ROWS, N_MOD, d)

    lat_row = lambda i, tm: i // (seq // tm)
    ctx_row = lambda i, tm: bn

    ffn_tc = 512 if ffn_w_down.shape[1] >= 512 else 128
    sc_tc = _tile(d, 512)
    gm_width = gm_w_out.shape[1]
    gm_tc = max(gm_width // GM_GROUPS, _tile(gm_width, 512))
    proj_tn = _tile(d, 1024)
    qkv_tn = _tile(d, 2048)
    lat_tm = 1024
    lat_sub = 512

    xl = x.reshape(bn * seq, d)
    xc = ctx.reshape(bn * ctx_len, d)
    for i in range(depth):
        last = i == depth - 1
        mixer, jx = i % N_MIXERS, i // N_MIXERS
        g_mix = norm_mix_g[i][None, :]
        g_ffn = norm_ffn_g[i][None, :]
        do_ctx = (not last) or mixer == 0
        if mixer == 0:
            w_qkv = _cast_weight(na_w_qkv, jx, col_chunk=qkv_tn)
            w_o = _cast_weight(na_w_o, jx, col_chunk=proj_tn)
            q_gain = na_q_g[jx] * (head_dim ** -0.5 * LOG2_E)
            head_gain = jnp.concatenate([jnp.tile(q_gain, NA_HEADS), jnp.tile(na_k_g[jx], NA_HEADS),
                                         jnp.ones((d,), F32)])[None, :]
            qkv = _qkv(xl, g_mix, mods, i, lat_row, w_qkv, head_gain, head_dim)
            qkv_c = _qkv(xc, g_mix, mods, i, ctx_row, w_qkv, head_gain, head_dim).reshape(bn, ctx_len, 3 * d)
            att = _na_attention(qkv, qkv_c, na_rpb[jx], bn, rows, d, head_dim)
            xl_new = _proj_res(att, w_o, xl, mods, i, lat_row, 2)
            if not last:
                att_c = _ctx_attention(qkv_c, d, head_dim).reshape(bn * ctx_len, d)
                xc = _proj_res(att_c, w_o, xc, mods, i, ctx_row, 2)
            xl = xl_new
        elif mixer == 1:
            w_in = _cast_weight(gm_w_in, jx, col_chunk=gm_tc)
            w_out = _cast_weight(gm_w_out, jx)
            w_s = gm_w_s[jx].astype(BF16)
            width = w_out.shape[0]
            v_g = gm_v_g[jx][None, :]
            b_s_cols = jnp.repeat(gm_b_s[jx].T, width // GM_GROUPS, axis=1)
            xl = _gmlp(xl, g_mix, mods, i, lat_row, w_in, v_g, w_s, b_s_cols, w_out, lat_tm)
            if not last:
                xc = _gmlp(xc, g_mix, mods, i, ctx_row, w_in, v_g, w_s, b_s_cols, w_out, ctx_len)
        else:
            w1 = _cast_weight(sc_w_in, jx, parts=3, col_chunk=sc_tc)
            cw, w2 = sc_conv_w[jx], _cast_weight(sc_w_out, jx)
            xl = _gated_conv("sc", xl, seq, g_mix, mods, i, lat_row, (0, 1, 2), w1, cw, None, w2, sc_tc, lat_tm,
                             sub_pref=lat_sub)
            if not last:
                xc = _gated_conv("sc", xc, ctx_len, g_mix, mods, i, ctx_row, (0, 1, 2), w1, cw, None, w2, sc_tc,
                                 ctx_len)
        w1, cw, cb, w2 = _prep_ffn(ffn_w_up, ffn_conv_w, ffn_conv_b, ffn_w_down, i, ffn_tc)
        xl = _gated_conv("ffn", xl, seq, g_ffn, mods, i, lat_row, (3, 4, 5), w1, cw, cb, w2, ffn_tc, lat_tm,
                         sub_pref=lat_sub)
        if not last:
            xc = _gated_conv("ffn", xc, ctx_len, g_ffn, mods, i, ctx_row, (3, 4, 5), w1, cw, cb, w2, ffn_tc,
                             ctx_len)
    return xl.reshape(bn, seq, d)
```

```python
import functools
import math

import numpy as np
import jax
import jax.numpy as jnp
from jax import lax
from jax.experimental import pallas as pl
from jax.experimental.pallas import tpu as pltpu

GRID_W = 64
NA_HEADS = 16
WIN_H = 8
WIN_W = 16
GM_GROUPS = 16
GM_CHUNK = 128
N_MOD = 6
N_MIXERS = 3
EPS = 1e-6

V7X_VMEM_BYTES = 64 * 1024 * 1024
V7X_SUBLANES_F32 = 8
V7X_SUBLANES_BF16 = 16
V7X_LANES = 128
V7X_MXU_DIM = 256

F32 = jnp.float32
BF16 = jnp.bfloat16

NA_QROWS = 2
COND_ROWS = V7X_SUBLANES_F32


def _compiler_params(semantics, vmem_bytes):
    return pltpu.CompilerParams(dimension_semantics=semantics,
                                vmem_limit_bytes=int(min(vmem_bytes, VMEM_LIMIT_BYTES)))


def _tile(n, pref):
    t = min(n, pref)
    while n % t:
        t -= 1
    return t


VMEM_LIMIT_BYTES = V7X_VMEM_BYTES - (4 << 20)
VMEM_TEMP_BYTES = 6 << 20


def _row_tile_buffers(tile_bytes, other_bytes):
    for x_bufs, out_bufs in ((2, 2), (1, 2), (1, 1)):
        if (x_bufs + out_bufs) * tile_bytes + other_bytes + VMEM_TEMP_BYTES <= VMEM_LIMIT_BYTES:
            break
    return x_bufs, out_bufs


def _norm_mod(x, gain, shift):
    ms = jnp.mean(x * x, axis=-1, keepdims=True)
    return x * lax.rsqrt(ms + EPS) * gain + shift


NORM_CHUNK = 32


def _store_norm_mod(hs_ref, sources, g, shift, scale):
    bounds = np.cumsum([0] + [ref.shape[0] for ref, _ in sources])
    total = int(bounds[-1])
    assert hs_ref.shape[0] == total
    gain = g * (1.0 + scale)
    for c0 in range(0, total, NORM_CHUNK):
        c1 = min(c0 + NORM_CHUNK, total)
        pieces = []
        for (ref, zero_pred), b0, b1 in zip(sources, bounds[:-1], bounds[1:]):
            lo, hi = max(c0, int(b0)), min(c1, int(b1))
            if lo >= hi:
                continue
            h = _norm_mod(ref[lo - int(b0):hi - int(b0), :], gain, shift)
            pieces.append(h if zero_pred is None else jnp.where(zero_pred, 0.0, h))
        h = pieces[0] if len(pieces) == 1 else jnp.concatenate(pieces, axis=0)
        hs_ref[c0:c1, :] = h.astype(BF16)


def _adaln_kernel(c_ref, w_ref, b_ref, o_ref):
    c = c_ref[...]
    a = (c * jax.nn.sigmoid(c)).astype(BF16)
    o_ref[...] = jnp.dot(a, w_ref[...].astype(BF16), preferred_element_type=F32) + b_ref[...]


def _adaln(cond, w_ada, b_ada):
    depth, d, n = w_ada.shape
    tn = _tile(n, 1024)
    return pl.pallas_call(
        _adaln_kernel,
        grid=(depth, n // tn),
        in_specs=[
            pl.BlockSpec((COND_ROWS, d), lambda l, j: (0, 0)),
            pl.BlockSpec((None, d, tn), lambda l, j: (l, 0, j)),
            pl.BlockSpec((None, 1, tn), lambda l, j: (l, 0, j)),
        ],
        out_specs=pl.BlockSpec((None, COND_ROWS, tn), lambda l, j: (l, 0, j)),
        out_shape=jax.ShapeDtypeStruct((depth, COND_ROWS, n), F32),
        compiler_params=_compiler_params(("arbitrary", "arbitrary"), 2 * d * tn * 4 + (8 << 20)),
        name="adaln",
    )(cond, w_ada, b_ada.reshape(depth, 1, n))


def _mod_spec(layer, row_fn, width, col_fn=None):
    if col_fn is None:
        return pl.BlockSpec((None, None, N_MOD, width), lambda i, j: (layer, row_fn(i), 0, 0))
    return pl.BlockSpec((None, None, N_MOD, width), lambda i, j: (layer, row_fn(i), 0, col_fn(j)))


SUB_ROWS = 256


def _skewed_subtiles(n_sub, first, second, ahead=1):
    pending = [first(r) for r in range(min(ahead, n_sub))]
    for r in range(n_sub):
        if r + ahead < n_sub:
            pending.append(first(r + ahead))
        second(r, pending.pop(0))


def _proj_res_kernel(a_ref, w_ref, x_ref, mod_ref, o_ref, *, gate_row, sub_rows):
    gate = mod_ref[gate_row:gate_row + 1, :]

    def matmul(r):
        return jnp.dot(a_ref[r * sub_rows:(r + 1) * sub_rows, :], w_ref[...], preferred_element_type=F32)

    def residual(r, y):
        rows = slice(r * sub_rows, (r + 1) * sub_rows)
        o_ref[rows, :] = x_ref[rows, :] + gate * y

    _skewed_subtiles(o_ref.shape[0] // sub_rows, matmul, residual)


def _proj_res(a, w, x, mods, layer, row_fn, gate_row, tm_pref=1024):
    m, k = a.shape
    n_tiles, _, tn = w.shape
    n = n_tiles * tn
    tm = _tile(m, tm_pref)
    vmem = 2 * (tm * k * 2 + k * tn * 2 + 2 * tm * tn * 4) + (8 << 20)
    return pl.pallas_call(
        functools.partial(_proj_res_kernel, gate_row=gate_row, sub_rows=_tile(tm, SUB_ROWS)),
        grid=(m // tm, n // tn),
        in_specs=[
            pl.BlockSpec((tm, k), lambda i, j: (i, 0)),
            pl.BlockSpec((None, k, tn), lambda i, j: (j, 0, 0)),
            pl.BlockSpec((tm, tn), lambda i, j: (i, j)),
            _mod_spec(layer, lambda i: row_fn(i, tm), tn, lambda j: j),
        ],
        out_specs=pl.BlockSpec((tm, tn), lambda i, j: (i, j)),
        out_shape=jax.ShapeDtypeStruct((m, n), F32),
        compiler_params=_compiler_params(("parallel", "arbitrary"), vmem),
        name="proj_res",
    )(a, w, x, mods)


def _qkv_kernel(x_ref, g_ref, mod_ref, w_ref, hg_ref, o_ref, hs_ref, *, n_norm_tiles, head_dim, sub_rows):
    j = pl.program_id(1)

    @pl.when(j == 0)
    def _():
        _store_norm_mod(hs_ref, [(x_ref, None)], g_ref[...], mod_ref[0:1, :], mod_ref[1:2, :])

    tm, tn = o_ref.shape
    is_norm_tile = j < n_norm_tiles
    hg = hg_ref[...]

    def matmul(r):
        return jnp.dot(hs_ref[r * sub_rows:(r + 1) * sub_rows, :], w_ref[...], preferred_element_type=F32)

    def head_norm(r, y):
        rows = slice(r * sub_rows, (r + 1) * sub_rows)
        for hh in range(tn // head_dim):
            sl = slice(hh * head_dim, (hh + 1) * head_dim)
            t = y[:, sl]
            inv = lax.rsqrt(jnp.mean(t * t, axis=-1, keepdims=True) + EPS)
            o_ref[rows, sl] = (t * jnp.where(is_norm_tile, inv, 1.0) * hg[:, sl]).astype(o_ref.dtype)

    _skewed_subtiles(tm // sub_rows, matmul, head_norm)


def _qkv(x, norm_g, mods, layer, row_fn, w_qkv, head_gain, head_dim, tm_pref=1024):
    m, d = x.shape
    n_tiles, _, tn = w_qkv.shape
    n = n_tiles * tn
    tm = _tile(m, tm_pref)
    assert (n // 3) % tn == 0 and tn % head_dim == 0
    vmem = 2 * (tm * d * 4 + d * tn * 2 + tm * tn * 2) + tm * d * 2 + 3 * tm * tn * 4 + (8 << 20)
    return pl.pallas_call(
        functools.partial(_qkv_kernel, n_norm_tiles=2 * (n // 3) // tn, head_dim=head_dim,
                          sub_rows=_tile(tm, SUB_ROWS)),
        grid=(m // tm, n // tn),
        in_specs=[
            pl.BlockSpec((tm, d), lambda i, j: (i, 0)),
            pl.BlockSpec((1, d), lambda i, j: (0, 0)),
            _mod_spec(layer, lambda i: row_fn(i, tm), d),
            pl.BlockSpec((None, d, tn), lambda i, j: (j, 0, 0)),
            pl.BlockSpec((1, tn), lambda i, j: (0, j)),
        ],
        out_specs=pl.BlockSpec((tm, tn), lambda i, j: (i, j)),
        out_shape=jax.ShapeDtypeStruct((m, n), BF16),
        scratch_shapes=[pltpu.VMEM((tm, d), BF16)],
        compiler_params=_compiler_params(("parallel", "arbitrary"), vmem),
        name="qkv",
    )(x, norm_g, mods, w_qkv, head_gain)


LOG2_E = math.log2(math.e)
_NT_DIMS = (((1,), (1,)), ((), ()))


def _na_kernel(*refs, nb, heads, head_dim):
    refs = refs[2:]
    q_ref = refs[0]
    k_refs = refs[1:1 + nb]
    v_refs = refs[1 + nb:1 + 2 * nb]
    kc_ref, vc_ref, tbl_ref, o_ref = refs[1 + 2 * nb:]

    def scores(h):
        sl = slice(h * head_dim, (h + 1) * head_dim)
        q = q_ref[:, sl]
        k = jnp.concatenate([r[:, sl] for r in k_refs], axis=0)
        s = lax.dot_general(q, k, _NT_DIMS, preferred_element_type=F32) + tbl_ref[h]
        sc = lax.dot_general(q, kc_ref[:, sl], _NT_DIMS, preferred_element_type=F32)
        return s, sc

    ones_lat = jnp.ones((nb * k_refs[0].shape[0], head_dim), BF16)
    ones_ctx = jnp.ones((vc_ref.shape[0], head_dim), BF16)

    def attend(h, s_sc):
        s, sc = s_sc
        sl = slice(h * head_dim, (h + 1) * head_dim)
        v = jnp.concatenate([jnp.concatenate([r[:, sl] for r in v_refs], axis=0), ones_lat], axis=1)
        vc = jnp.concatenate([vc_ref[:, sl], ones_ctx], axis=1)
        m = jnp.maximum(jnp.max(s, axis=-1, keepdims=True), jnp.max(sc, axis=-1, keepdims=True))
        p = jnp.exp2(s - m).astype(BF16)
        pc = jnp.exp2(sc - m).astype(BF16)
        o = (jnp.dot(p, v, preferred_element_type=F32) + jnp.dot(pc, vc, preferred_element_type=F32))
        o_ref[:, sl] = (o[:, :head_dim] * (1.0 / o[:, head_dim:head_dim + 1])).astype(o_ref.dtype)

    _skewed_subtiles(heads, scores, attend, ahead=2)


def _na_geometry(rows, qrows):
    kh = min(WIN_H, rows)
    nb = min(kh + qrows - 1, rows)
    n_blk = rows // qrows
    band0s, keys, types, row_rel = [], {}, [], []
    for blk in range(n_blk):
        r0 = blk * qrows
        q_row = r0 + np.arange(qrows)
        r_start = np.clip(q_row - kh // 2, 0, rows - kh)
        band0 = min(int(np.clip(r0 - kh // 2, 0, rows - kh)), rows - nb)
        k_row = band0 + np.arange(nb)
        row_ok = (k_row[None, :] >= r_start[:, None]) & (k_row[None, :] < r_start[:, None] + kh)
        rel = np.where(row_ok, k_row[None, :] - q_row[:, None] + WIN_H - 1, -1)
        key = rel.tobytes()
        if key not in keys:
            keys[key] = len(row_rel)
            row_rel.append(rel)
        types.append(keys[key])
        band0s.append(band0)
    return nb, n_blk, np.array(band0s, np.int32), np.array(types, np.int32), row_rel


def _na_bias_table(rpb, row_rel):
    heads, n_dr, n_dc = rpb.shape
    period = 2 * GRID_W
    assert n_dc <= period
    r = jnp.pad(rpb, ((0, 0), (0, 0), (0, period - n_dc)))
    skew = jnp.tile(r, (1, 1, GRID_W))[:, :, :GRID_W * (period - 1)].reshape(heads, n_dr, GRID_W, period - 1)
    toep = skew[:, :, :, WIN_W - 1:WIN_W - 1 + GRID_W]
    q_col = np.arange(GRID_W)
    c_start = np.clip(q_col - WIN_W // 2, 0, GRID_W - WIN_W)
    col_ok = (q_col[None, :] >= c_start[:, None]) & (q_col[None, :] < c_start[:, None] + WIN_W)
    toep = jnp.where(col_ok, toep, -jnp.inf)
    masked = jnp.full((heads, GRID_W, GRID_W), -jnp.inf, F32)
    tables = []
    for rel in row_rel:
        q_blocks = []
        for qr in range(rel.shape[0]):
            q_blocks.append(jnp.concatenate(
                [toep[:, rel[qr, kr]] if rel[qr, kr] >= 0 else masked for kr in range(rel.shape[1])], axis=-1))
        tables.append(jnp.concatenate(q_blocks, axis=-2))
    return jnp.stack(tables).astype(F32)


def _na_attention(qkv, qkv_ctx, rpb, bn, rows, d, head_dim):
    qrows = min(NA_QROWS, rows)
    nb, n_blk, band0s, types, row_rel = _na_geometry(rows, qrows)
    qw, nk = qrows * GRID_W, nb * GRID_W
    ctx_len = qkv_ctx.shape[1]
    tbl = _na_bias_table(rpb * LOG2_E, row_rel)
    band_tbl = jnp.asarray(band0s)
    type_tbl = jnp.asarray(types)
    kv_view = qkv.reshape(bn * rows, GRID_W, 3 * d)

    def kv_spec(t, col):
        return pl.BlockSpec((None, GRID_W, d), lambda b, i, band, typ: (b * rows + band[i] + t, 0, col))

    in_specs = ([pl.BlockSpec((qw, d), lambda b, i, band, typ: (b * n_blk + i, 0))]
                + [kv_spec(t, 1) for t in range(nb)]
                + [kv_spec(t, 2) for t in range(nb)]
                + [pl.BlockSpec((None, ctx_len, d), lambda b, i, band, typ: (b, 0, 1)),
                   pl.BlockSpec((None, ctx_len, d), lambda b, i, band, typ: (b, 0, 2)),
                   pl.BlockSpec((None, NA_HEADS, qw, nk), lambda b, i, band, typ: (typ[i], 0, 0, 0))])
    vmem = 2 * (2 * qw * d * 2 + 2 * nk * d * 2 + 2 * ctx_len * d * 2 + NA_HEADS * qw * nk * 4) + (12 << 20)
    grid_spec = pltpu.PrefetchScalarGridSpec(
        num_scalar_prefetch=2,
        grid=(bn, n_blk),
        in_specs=in_specs,
        out_specs=pl.BlockSpec((qw, d), lambda b, i, band, typ: (b * n_blk + i, 0)),
    )
    return pl.pallas_call(
        functools.partial(_na_kernel, nb=nb, heads=NA_HEADS, head_dim=head_dim),
        grid_spec=grid_spec,
        out_shape=jax.ShapeDtypeStruct((bn * rows * GRID_W, d), BF16),
        compiler_params=_compiler_params(("parallel", "arbitrary"), vmem),
        name="na_attention",
    )(band_tbl, type_tbl, qkv, *([kv_view] * (2 * nb)), qkv_ctx, qkv_ctx, tbl)


def _ctx_attn_kernel(q_ref, k_ref, v_ref, o_ref, *, heads, head_dim):
    for h in range(heads):
        sl = slice(h * head_dim, (h + 1) * head_dim)
        s = lax.dot_general(q_ref[:, sl], k_ref[:, sl], _NT_DIMS, preferred_element_type=F32)
        m = jnp.max(s, axis=-1, keepdims=True)
        p = jnp.exp2(s - m)
        l = jnp.sum(p, axis=-1, keepdims=True)
        o = jnp.dot(p.astype(BF16), v_ref[:, sl], preferred_element_type=F32)
        o_ref[:, sl] = (o * (1.0 / l)).astype(o_ref.dtype)


def _ctx_attention(qkv_ctx, d, head_dim):
    bn, ctx_len, _ = qkv_ctx.shape
    spec = lambda col: pl.BlockSpec((None, ctx_len, d), lambda b: (b, 0, col))
    return pl.pallas_call(
        functools.partial(_ctx_attn_kernel, heads=NA_HEADS, head_dim=head_dim),
        grid=(bn,),
        in_specs=[spec(0), spec(1), spec(2)],
        out_specs=pl.BlockSpec((None, ctx_len, d), lambda b: (b, 0, 0)),
        out_shape=jax.ShapeDtypeStruct((bn, ctx_len, d), BF16),
        compiler_params=_compiler_params(("arbitrary",), 8 * ctx_len * d * 2 + (12 << 20)),
        name="ctx_attention",
    )(qkv_ctx, qkv_ctx, qkv_ctx)


HALO = V7X_SUBLANES_F32


def _shift_rows(u, tm):
    n = u.shape[0]
    prev = pltpu.roll(u, 1, 0)[HALO:HALO + tm]
    nxt = pltpu.roll(u, n - 1, 0)[HALO:HALO + tm]
    return prev, nxt


def _conv3(u, cw, rows, seq_edges):
    prev, nxt = _shift_rows(u, rows)
    if seq_edges:
        row = lax.broadcasted_iota(jnp.int32, (rows, 1), 0)
        prev = jnp.where(row == 0, 0.0, prev)
        nxt = jnp.where(row == rows - 1, 0.0, nxt)
    return prev * cw[0:1] + u[HALO:HALO + rows] * cw[1:2] + nxt * cw[2:3]


def _gated_conv_kernel(*refs, kind, tiles_per_seq, mod_rows, sub_rows, seq_edges):
    x_ref, xp_ref, xn_ref, g_ref, mod_ref = refs[:5]
    if kind == "ffn":
        w1_refs, cw_refs, cb_refs = refs[5:7], refs[7:9], refs[9:11]
        w2_ref, o_ref, hs_ref = refs[11:]
    else:
        w1_refs, cw_refs = refs[5:8], refs[8:9]
        w2_ref, o_ref, hs_ref = refs[9:]
    i = pl.program_id(0)
    j = pl.program_id(1)
    tm = x_ref.shape[0]
    r_shift, r_scale, r_gate = mod_rows

    @pl.when(j == 0)
    def _():
        g = g_ref[...]
        shift = mod_ref[r_shift:r_shift + 1, :]
        scale = mod_ref[r_scale:r_scale + 1, :]
        pos = i % tiles_per_seq
        _store_norm_mod(hs_ref, [(xp_ref, pos == 0), (x_ref, None), (xn_ref, pos == tiles_per_seq - 1)],
                        g, shift, scale)
        o_ref[...] = x_ref[...]

    res_gate = mod_ref[r_gate:r_gate + 1, :]

    def first_matmuls(r):
        lhs = hs_ref[r * sub_rows:r * sub_rows + sub_rows + 2 * HALO, :]
        return [jnp.dot(lhs, w[...], preferred_element_type=F32) for w in w1_refs]

    def mid_and_second_matmul(r, u):
        if kind == "ffn":
            gate, up = [_conv3(u[p], cw_refs[p][...], sub_rows, seq_edges) + cb_refs[p][...]
                        for p in range(2)]
            a = gate * jax.nn.sigmoid(gate) * up
        else:
            a = u[0][HALO:HALO + sub_rows] * _conv3(u[1] * u[2], cw_refs[0][...], sub_rows, seq_edges)
        o_ref[r * sub_rows:(r + 1) * sub_rows, :] += res_gate * jnp.dot(a.astype(BF16), w2_ref[...],
                                                                       preferred_element_type=F32)

    _skewed_subtiles(tm // sub_rows, first_matmuls, mid_and_second_matmul)


def _gated_conv(kind, x, seq_len, norm_g, mods, layer, row_fn, mod_rows, w1, cw, cb, w2, tc, tm_pref,
                sub_pref=SUB_ROWS):
    m, d = x.shape
    parts = 2 if kind == "ffn" else 3
    cparts = 2 if kind == "ffn" else 1
    n_chunks = w2.shape[0] // tc
    assert w1.shape == (parts * n_chunks, d, tc) and cw.shape[1] == cparts * n_chunks * tc
    if seq_len >= tm_pref:
        tm = _tile(seq_len, tm_pref)
        sub_rows = _tile(tm, sub_pref)
        tiles_per_seq, seq_edges = seq_len // tm, False
    else:
        tm = seq_len * _tile(m // seq_len, tm_pref // seq_len)
        sub_rows = seq_len
        tiles_per_seq, seq_edges = 1, True
    assert sub_rows % V7X_SUBLANES_BF16 == 0
    hb = tm // HALO
    n_hblk = m // HALO
    other = 2 * (d * parts * tc * 2 + tc * d * 2) + (tm + 2 * HALO) * d * 2
    x_bufs, out_bufs = _row_tile_buffers(tm * d * 4, other)
    vmem = (x_bufs + out_bufs) * tm * d * 4 + other + VMEM_TEMP_BYTES
    col_spec = lambda rows, p: pl.BlockSpec((rows, tc), lambda i, j: (0, p * n_chunks + j))
    in_specs = [
        pl.BlockSpec((tm, d), lambda i, j: (i, 0), pipeline_mode=pl.Buffered(x_bufs)),
        pl.BlockSpec((HALO, d), lambda i, j: (jnp.maximum(i * hb - 1, 0), 0)),
        pl.BlockSpec((HALO, d), lambda i, j: (jnp.minimum((i + 1) * hb, n_hblk - 1), 0)),
        pl.BlockSpec((1, d), lambda i, j: (0, 0)),
        _mod_spec(layer, lambda i: row_fn(i, tm), d),
    ]
    in_specs += [pl.BlockSpec((None, d, tc), functools.partial(lambda p, i, j: (p * n_chunks + j, 0, 0), p))
                 for p in range(parts)]
    in_specs += [col_spec(3, p) for p in range(cparts)]
    operands = [x, x, x, norm_g, mods] + [w1] * parts + [cw] * cparts
    if kind == "ffn":
        in_specs += [col_spec(1, p) for p in range(cparts)]
        operands += [cb] * cparts
    in_specs.append(pl.BlockSpec((tc, d), lambda i, j: (j, 0)))
    operands.append(w2)
    return pl.pallas_call(
        functools.partial(_gated_conv_kernel, kind=kind, tiles_per_seq=tiles_per_seq, mod_rows=mod_rows,
                          sub_rows=sub_rows, seq_edges=seq_edges),
        grid=(m // tm, n_chunks),
        in_specs=in_specs,
        out_specs=pl.BlockSpec((tm, d), lambda i, j: (i, 0), pipeline_mode=pl.Buffered(out_bufs)),
        out_shape=jax.ShapeDtypeStruct((m, d), F32),
        scratch_shapes=[pltpu.VMEM((tm + 2 * HALO, d), BF16)],
        compiler_params=_compiler_params(("parallel", "arbitrary"), vmem),
        name=kind,
    )(*operands)


def _gelu_tanh(x):
    return jax.nn.gelu(x, approximate=True)


def _gmlp_kernel(x_ref, g_ref, mod_ref, win_ref, vg_ref, ws_ref, bs_ref, wout_ref, o_ref,
                 hs_ref, v_ref, ssq_ref, *, n_chunks, group_dim, width, sub_rows):
    j = pl.program_id(1)
    tm = x_ref.shape[0]
    tc = win_ref.shape[1]
    n_sub = tm // sub_rows

    @pl.when(j == 0)
    def _():
        _store_norm_mod(hs_ref, [(x_ref, None)], g_ref[...], mod_ref[0:1, :], mod_ref[1:2, :])
        ssq_ref[...] = jnp.zeros_like(ssq_ref)
        o_ref[...] = x_ref[...]

    def in_matmul(r):
        return jnp.dot(hs_ref[r * sub_rows:(r + 1) * sub_rows, :], win_ref[...], preferred_element_type=F32)

    @pl.when(j < n_chunks)
    def _():
        def keep_v(r, y):
            rows = slice(r * sub_rows, (r + 1) * sub_rows)
            t = _gelu_tanh(y)
            v_ref[j, rows, :] = t.astype(v_ref.dtype)
            ssq_ref[rows, :] += jnp.sum(t * t, axis=-1, keepdims=True)

        _skewed_subtiles(n_sub, in_matmul, keep_v)

    @pl.when(j >= n_chunks)
    def _():
        jj = j - n_chunks
        vg = vg_ref[...]
        res_gate = mod_ref[2:3, :]
        bs = bs_ref[...]
        gpc = tc // group_dim
        ws = [ws_ref[jj * gpc + gi] for gi in range(gpc)]

        def gate_and_project(r, y):
            rows = slice(r * sub_rows, (r + 1) * sub_rows)
            inv = lax.rsqrt(ssq_ref[rows, :] * (1.0 / width) + EPS)
            vn = (v_ref[jj, rows, :].astype(F32) * inv * vg).astype(BF16)
            cols = []
            for gi in range(gpc):
                csl = slice(gi * group_dim, (gi + 1) * group_dim)
                cols.append(jnp.concatenate(
                    [jnp.dot(ws[gi], vn[c * GM_CHUNK:(c + 1) * GM_CHUNK, csl], preferred_element_type=F32)
                     + bs[:, csl] for c in range(sub_rows // GM_CHUNK)], axis=0))
            sv = jnp.concatenate(cols, axis=1)
            o_ref[rows, :] += res_gate * jnp.dot((_gelu_tanh(y) * sv).astype(BF16), wout_ref[...],
                                                 preferred_element_type=F32)

        _skewed_subtiles(n_sub, in_matmul, gate_and_project)


def _gmlp(x, norm_g, mods, layer, row_fn, w_in, v_g, w_s, b_s_cols, w_out, tm_pref):
    m, d = x.shape
    width = w_out.shape[0]
    group_dim = width // GM_GROUPS
    tc = w_in.shape[2]
    n_chunks = width // tc
    assert w_in.shape[0] == 2 * n_chunks and tc % group_dim == 0
    tm = _tile(m, tm_pref)
    sub_rows = _tile(tm, SUB_ROWS)
    assert sub_rows % GM_CHUNK == 0
    other = (2 * (d * tc * 2 + tc * d * 2 + GM_CHUNK * tc * 4) + tm * d * 2 + tm * width * 2
             + GM_GROUPS * GM_CHUNK * GM_CHUNK * 2 + tm * V7X_LANES * 4)
    x_bufs, out_bufs = _row_tile_buffers(tm * d * 4, other)
    vmem = (x_bufs + out_bufs) * tm * d * 4 + other + VMEM_TEMP_BYTES
    return pl.pallas_call(
        functools.partial(_gmlp_kernel, n_chunks=n_chunks, group_dim=group_dim, width=width,
                          sub_rows=sub_rows),
        grid=(m // tm, 2 * n_chunks),
        in_specs=[
            pl.BlockSpec((tm, d), lambda i, j: (i, 0), pipeline_mode=pl.Buffered(x_bufs)),
            pl.BlockSpec((1, d), lambda i, j: (0, 0)),
            _mod_spec(layer, lambda i: row_fn(i, tm), d),
            pl.BlockSpec((None, d, tc),
                         lambda i, j: (jnp.where(j < n_chunks, j + n_chunks, j - n_chunks), 0, 0)),
            pl.BlockSpec((1, tc), lambda i, j: (0, jnp.maximum(j - n_chunks, 0))),
            pl.BlockSpec((GM_GROUPS, GM_CHUNK, GM_CHUNK), lambda i, j: (0, 0, 0)),
            pl.BlockSpec((GM_CHUNK, tc), lambda i, j: (0, jnp.maximum(j - n_chunks, 0))),
            pl.BlockSpec((tc, d), lambda i, j: (jnp.maximum(j - n_chunks, 0), 0)),
        ],
        out_specs=pl.BlockSpec((tm, d), lambda i, j: (i, 0), pipeline_mode=pl.Buffered(out_bufs)),
        out_shape=jax.ShapeDtypeStruct((m, d), F32),
        scratch_shapes=[pltpu.VMEM((tm, d), BF16), pltpu.VMEM((n_chunks, tm, tc), BF16), pltpu.VMEM((tm, 1), F32)],
        compiler_params=_compiler_params(("parallel", "arbitrary"), vmem),
        name="gmlp",
    )(x, norm_g, mods, w_in, v_g, w_s, b_s_cols, w_out)


CAST_BLOCK_BYTES = 6 << 20


def _cast_kernel(x_ref, o_ref, *, valid_rows):
    rows, f = x_ref.shape
    x = x_ref[...]
    if valid_rows is not None:
        row = pl.program_id(0) * rows + lax.broadcasted_iota(jnp.int32, (rows, 1), 0)
        x = jnp.where(row < valid_rows, x, 0.0)
    x = x.astype(o_ref.dtype)
    if len(o_ref.shape) == 2:
        o_ref[...] = x
        return
    n_chunks, _, tc = o_ref.shape
    for c in range(n_chunks):
        width = max(0, min(tc, f - c * tc))
        if width:
            o_ref[c, :, :width] = x[:, c * tc:c * tc + width]
        if width < tc:
            o_ref[c, :, width:] = jnp.zeros((rows, tc - width), o_ref.dtype)


def _cast_weight(w, idx, parts=1, fp=None, col_chunk=None, rows_out=None, row_block=None):
    _, r, c = w.shape
    f = c // parts
    fp = f if fp is None else fp
    rows_out = r if rows_out is None else rows_out
    assert parts == 1 or f % V7X_LANES == 0
    assert col_chunk is not None or (parts == 1 and fp == f)
    padded_rows = rows_out > r
    if row_block is None:
        row_pref = max(V7X_SUBLANES_BF16, CAST_BLOCK_BYTES // (4 * f))
        row_block = max(t for t in range(V7X_SUBLANES_BF16, row_pref + 1, V7X_SUBLANES_BF16)
                        if rows_out % t == 0)
    assert rows_out % row_block == 0 and row_block % V7X_SUBLANES_BF16 == 0
    assert r % row_block == 0 or (padded_rows and rows_out - row_block < r)
    if col_chunk is None:
        out_spec = pl.BlockSpec((row_block, f), lambda i, p: (i, 0))
        out_shape = (rows_out, f)
    else:
        assert fp % col_chunk == 0 and col_chunk % V7X_LANES == 0 and not padded_rows
        n_chunks = fp // col_chunk
        out_spec = pl.BlockSpec((n_chunks, row_block, col_chunk), lambda i, p: (p, i, 0))
        out_shape = (parts * n_chunks, r, col_chunk)
    return pl.pallas_call(
        functools.partial(_cast_kernel, valid_rows=r if padded_rows else None),
        grid=(rows_out // row_block, parts),
        in_specs=[pl.BlockSpec((None, row_block, f), lambda i, p: (idx, i, p))],
        out_specs=out_spec,
        out_shape=jax.ShapeDtypeStruct(out_shape, BF16),
        compiler_params=_compiler_params(("parallel", "arbitrary"),
                                         2 * row_block * (4 * f + 2 * fp) + (8 << 20)),
        name="cast_weight",
    )(w)


def _pad_halves(a, f, fp):
    pad = lambda h: jnp.pad(h, ((0, 0), (0, fp - f)))
    return jnp.concatenate([pad(a[:, :f]), pad(a[:, f:])], axis=1)


def _prep_ffn(w_up, conv_w, conv_b, w_down, layer, tc):
    f = w_down.shape[1]
    fp = -(-f // tc) * tc
    w1 = _cast_weight(w_up, layer, parts=2, fp=fp, col_chunk=tc)
    cw = _pad_halves(conv_w[layer], f, fp)
    cb = _pad_halves(conv_b[layer][None, :], f, fp)
    w2 = _cast_weight(w_down, layer, rows_out=fp)
    return w1, cw, cb, w2


def kernel(x, c, ctx, c_ctx, norm_mix_g, norm_ffn_g, w_ada, b_ada, na_w_qkv, na_q_g, na_k_g, na_rpb, na_w_o,
           gm_w_in, gm_v_g, gm_w_s, gm_b_s, gm_w_out, sc_w_in, sc_conv_w, sc_w_out,
           ffn_w_up, ffn_conv_w, ffn_conv_b, ffn_w_down):
    bn, seq, d = x.shape
    ctx_len = ctx.shape[1]
    depth = w_ada.shape[0]
    head_dim = d // NA_HEADS
    rows = seq // GRID_W
    assert bn + 1 <= COND_ROWS and seq % GRID_W == 0 and ctx_len % GM_CHUNK == 0

    cond = jnp.concatenate([c, c_ctx[None, :], jnp.zeros((COND_ROWS - bn - 1, d), F32)], axis=0)
    mods = _adaln(cond, w_ada, b_ada).reshape(depth, COND_ROWS, N_MOD, d)

    lat_row = lambda i, tm: i // (seq // tm)
    ctx_row = lambda i, tm: bn

    ffn_tc = 512 if ffn_w_down.shape[1] >= 512 else 128
    sc_tc = _tile(d, 512)
    gm_width = gm_w_out.shape[1]
    gm_tc = max(gm_width // GM_GROUPS, _tile(gm_width, 512))
    proj_tn = _tile(d, 1024)
    qkv_tn = _tile(d, 2048)
    lat_tm = 1024
    ctx_tm = min(lat_tm, bn * ctx_len)
    lat_sub = 512

    xl = x.reshape(bn * seq, d)
    xc = ctx.reshape(bn * ctx_len, d)
    for i in range(depth):
        last = i == depth - 1
        mixer, jx = i % N_MIXERS, i // N_MIXERS
        g_mix = norm_mix_g[i][None, :]
        g_ffn = norm_ffn_g[i][None, :]
        do_ctx = (not last) or mixer == 0
        if mixer == 0:
            w_qkv = _cast_weight(na_w_qkv, jx, col_chunk=qkv_tn)
            w_o = _cast_weight(na_w_o, jx, col_chunk=proj_tn)
            q_gain = na_q_g[jx] * (head_dim ** -0.5 * LOG2_E)
            head_gain = jnp.concatenate([jnp.tile(q_gain, NA_HEADS), jnp.tile(na_k_g[jx], NA_HEADS),
                                         jnp.ones((d,), F32)])[None, :]
            qkv = _qkv(xl, g_mix, mods, i, lat_row, w_qkv, head_gain, head_dim)
            qkv_c = _qkv(xc, g_mix, mods, i, ctx_row, w_qkv, head_gain, head_dim).reshape(bn, ctx_len, 3 * d)
            att = _na_attention(qkv, qkv_c, na_rpb[jx], bn, rows, d, head_dim)
            xl_new = _proj_res(att, w_o, xl, mods, i, lat_row, 2)
            if not last:
                att_c = _ctx_attention(qkv_c, d, head_dim).reshape(bn * ctx_len, d)
                xc = _proj_res(att_c, w_o, xc, mods, i, ctx_row, 2)
            xl = xl_new
        elif mixer == 1:
            w_in = _cast_weight(gm_w_in, jx, col_chunk=gm_tc)
            w_out = _cast_weight(gm_w_out, jx)
            w_s = gm_w_s[jx].astype(BF16)
            width = w_out.shape[0]
            v_g = gm_v_g[jx][None, :]
            b_s_cols = jnp.repeat(gm_b_s[jx].T, width // GM_GROUPS, axis=1)
            xl = _gmlp(xl, g_mix, mods, i, lat_row, w_in, v_g, w_s, b_s_cols, w_out, lat_tm)
            if not last:
                xc = _gmlp(xc, g_mix, mods, i, ctx_row, w_in, v_g, w_s, b_s_cols, w_out, ctx_tm)
        else:
            w1 = _cast_weight(sc_w_in, jx, parts=3, col_chunk=sc_tc)
            cw, w2 = sc_conv_w[jx], _cast_weight(sc_w_out, jx)
            xl = _gated_conv("sc", xl, seq, g_mix, mods, i, lat_row, (0, 1, 2), w1, cw, None, w2, sc_tc, lat_tm,
                             sub_pref=lat_sub)
            if not last:
                xc = _gated_conv("sc", xc, ctx_len, g_mix, mods, i, ctx_row, (0, 1, 2), w1, cw, None, w2, sc_tc,
                                 ctx_tm)
        w1, cw, cb, w2 = _prep_ffn(ffn_w_up, ffn_conv_w, ffn_conv_b, ffn_w_down, i, ffn_tc)
        xl = _gated_conv("ffn", xl, seq, g_ffn, mods, i, lat_row, (3, 4, 5), w1, cw, cb, w2, ffn_tc, lat_tm,
                         sub_pref=lat_sub)
        if not last:
            xc = _gated_conv("ffn", xc, ctx_len, g_ffn, mods, i, ctx_row, (3, 4, 5), w1, cw, cb, w2, ffn_tc,
                             ctx_tm)
    return xl.reshape(bn, seq, d)
```

```python
import functools
import math

import numpy as np
import jax
import jax.numpy as jnp
from jax import lax
from jax.experimental import pallas as pl
from jax.experimental.pallas import tpu as pltpu

GRID_W = 64
NA_HEADS = 16
WIN_H = 8
WIN_W = 16
GM_GROUPS = 16
GM_CHUNK = 128
N_MOD = 6
N_MIXERS = 3
EPS = 1e-6

V7X_VMEM_BYTES = 64 * 1024 * 1024
V7X_SUBLANES_F32 = 8
V7X_SUBLANES_BF16 = 16
V7X_LANES = 128
VMEM_LIMIT_BYTES = V7X_VMEM_BYTES - (4 << 20)
VMEM_TEMP_BYTES = 6 << 20

F32 = jnp.float32
BF16 = jnp.bfloat16

ROW_TILE = 1024
SUB_ROWS = 256
HALO_SUB_ROWS = 512
HIDDEN_CHUNK = 512
PROJ_COLS = 1024
QKV_COLS = 2048
ADALN_COLS = 1024
NA_QROWS = 2
COND_ROWS = V7X_SUBLANES_F32


def _compiler_params(semantics, vmem_bytes):
    return pltpu.CompilerParams(dimension_semantics=semantics,
                                vmem_limit_bytes=int(min(vmem_bytes, VMEM_LIMIT_BYTES)))


def _tile(n, pref):
    t = min(n, pref)
    while n % t:
        t -= 1
    return t


def _row_tile_buffers(tile_bytes, other_bytes):
    for x_bufs, out_bufs in ((2, 2), (1, 2), (1, 1)):
        if (x_bufs + out_bufs) * tile_bytes + other_bytes + VMEM_TEMP_BYTES <= VMEM_LIMIT_BYTES:
            break
    return x_bufs, out_bufs


def _norm_mod(x, gain, shift):
    ms = jnp.mean(x * x, axis=-1, keepdims=True)
    return x * lax.rsqrt(ms + EPS) * gain + shift


NORM_CHUNK = 32


def _store_norm_mod(hs_ref, sources, g, shift, scale):
    bounds = np.cumsum([0] + [ref.shape[0] for ref, _ in sources])
    total = int(bounds[-1])
    assert hs_ref.shape[0] == total
    gain = g * (1.0 + scale)
    for c0 in range(0, total, NORM_CHUNK):
        c1 = min(c0 + NORM_CHUNK, total)
        pieces = []
        for (ref, zero_pred), b0, b1 in zip(sources, bounds[:-1], bounds[1:]):
            lo, hi = max(c0, int(b0)), min(c1, int(b1))
            if lo >= hi:
                continue
            h = _norm_mod(ref[lo - int(b0):hi - int(b0), :], gain, shift)
            pieces.append(h if zero_pred is None else jnp.where(zero_pred, 0.0, h))
        h = pieces[0] if len(pieces) == 1 else jnp.concatenate(pieces, axis=0)
        hs_ref[c0:c1, :] = h.astype(BF16)


def _adaln_kernel(c_ref, w_ref, b_ref, o_ref):
    c = c_ref[...]
    a = (c * jax.nn.sigmoid(c)).astype(BF16)
    o_ref[...] = jnp.dot(a, w_ref[...].astype(BF16), preferred_element_type=F32) + b_ref[...]


def _adaln(cond, w_ada, b_ada):
    depth, d, n = w_ada.shape
    tn = _tile(n, ADALN_COLS)
    return pl.pallas_call(
        _adaln_kernel,
        grid=(depth, n // tn),
        in_specs=[
            pl.BlockSpec((COND_ROWS, d), lambda l, j: (0, 0)),
            pl.BlockSpec((None, d, tn), lambda l, j: (l, 0, j)),
            pl.BlockSpec((None, 1, tn), lambda l, j: (l, 0, j)),
        ],
        out_specs=pl.BlockSpec((None, COND_ROWS, tn), lambda l, j: (l, 0, j)),
        out_shape=jax.ShapeDtypeStruct((depth, COND_ROWS, n), F32),
        compiler_params=_compiler_params(("arbitrary", "arbitrary"), 3 * d * tn * 4 + VMEM_TEMP_BYTES),
        name="adaln",
    )(cond, w_ada, b_ada.reshape(depth, 1, n))


def _mod_spec(layer, row_fn, width, col_fn=None):
    if col_fn is None:
        return pl.BlockSpec((None, None, N_MOD, width), lambda i, j: (layer, row_fn(i), 0, 0))
    return pl.BlockSpec((None, None, N_MOD, width), lambda i, j: (layer, row_fn(i), 0, col_fn(j)))


def _skewed_subtiles(n_sub, first, second, ahead=1):
    pending = [first(r) for r in range(min(ahead, n_sub))]
    for r in range(n_sub):
        if r + ahead < n_sub:
            pending.append(first(r + ahead))
        second(r, pending.pop(0))


def _proj_res_kernel(a_ref, w_ref, x_ref, mod_ref, o_ref, *, gate_row, sub_rows):
    gate = mod_ref[gate_row:gate_row + 1, :]

    def matmul(r):
        return jnp.dot(a_ref[r * sub_rows:(r + 1) * sub_rows, :], w_ref[...], preferred_element_type=F32)

    def residual(r, y):
        rows = slice(r * sub_rows, (r + 1) * sub_rows)
        o_ref[rows, :] = x_ref[rows, :] + gate * y

    _skewed_subtiles(o_ref.shape[0] // sub_rows, matmul, residual)


def _proj_res(a, w, x, mods, layer, row_fn, gate_row):
    m, k = a.shape
    n_tiles, _, tn = w.shape
    n = n_tiles * tn
    tm = _tile(m, ROW_TILE)
    vmem = 2 * (tm * k * 2 + k * tn * 2 + 2 * tm * tn * 4) + VMEM_TEMP_BYTES
    return pl.pallas_call(
        functools.partial(_proj_res_kernel, gate_row=gate_row, sub_rows=_tile(tm, SUB_ROWS)),
        grid=(m // tm, n // tn),
        in_specs=[
            pl.BlockSpec((tm, k), lambda i, j: (i, 0)),
            pl.BlockSpec((None, k, tn), lambda i, j: (j, 0, 0)),
            pl.BlockSpec((tm, tn), lambda i, j: (i, j)),
            _mod_spec(layer, lambda i: row_fn(i, tm), tn, lambda j: j),
        ],
        out_specs=pl.BlockSpec((tm, tn), lambda i, j: (i, j)),
        out_shape=jax.ShapeDtypeStruct((m, n), F32),
        compiler_params=_compiler_params(("parallel", "arbitrary"), vmem),
        name="proj_res",
    )(a, w, x, mods)


def _qkv_kernel(x_ref, g_ref, mod_ref, w_ref, hg_ref, o_ref, hs_ref, *, n_norm_tiles, head_dim, sub_rows):
    j = pl.program_id(1)

    @pl.when(j == 0)
    def _():
        _store_norm_mod(hs_ref, [(x_ref, None)], g_ref[...], mod_ref[0:1, :], mod_ref[1:2, :])

    tm, tn = o_ref.shape
    is_norm_tile = j < n_norm_tiles
    hg = hg_ref[...]

    def matmul(r):
        return jnp.dot(hs_ref[r * sub_rows:(r + 1) * sub_rows, :], w_ref[...], preferred_element_type=F32)

    def head_norm(r, y):
        rows = slice(r * sub_rows, (r + 1) * sub_rows)
        for hh in range(tn // head_dim):
            sl = slice(hh * head_dim, (hh + 1) * head_dim)
            t = y[:, sl]
            inv = lax.rsqrt(jnp.mean(t * t, axis=-1, keepdims=True) + EPS)
            o_ref[rows, sl] = (t * jnp.where(is_norm_tile, inv, 1.0) * hg[:, sl]).astype(o_ref.dtype)

    _skewed_subtiles(tm // sub_rows, matmul, head_norm)


def _qkv(x, norm_g, mods, layer, row_fn, w_qkv, head_gain, head_dim):
    m, d = x.shape
    n_tiles, _, tn = w_qkv.shape
    n = n_tiles * tn
    tm = _tile(m, ROW_TILE)
    assert (n // 3) % tn == 0 and tn % head_dim == 0
    vmem = 2 * (tm * d * 4 + d * tn * 2 + tm * tn * 2) + tm * d * 2 + 2 * VMEM_TEMP_BYTES
    return pl.pallas_call(
        functools.partial(_qkv_kernel, n_norm_tiles=2 * (n // 3) // tn, head_dim=head_dim,
                          sub_rows=_tile(tm, SUB_ROWS)),
        grid=(m // tm, n // tn),
        in_specs=[
            pl.BlockSpec((tm, d), lambda i, j: (i, 0)),
            pl.BlockSpec((1, d), lambda i, j: (0, 0)),
            _mod_spec(layer, lambda i: row_fn(i, tm), d),
            pl.BlockSpec((None, d, tn), lambda i, j: (j, 0, 0)),
            pl.BlockSpec((1, tn), lambda i, j: (0, j)),
        ],
        out_specs=pl.BlockSpec((tm, tn), lambda i, j: (i, j)),
        out_shape=jax.ShapeDtypeStruct((m, n), BF16),
        scratch_shapes=[pltpu.VMEM((tm, d), BF16)],
        compiler_params=_compiler_params(("parallel", "arbitrary"), vmem),
        name="qkv",
    )(x, norm_g, mods, w_qkv, head_gain)


LOG2_E = math.log2(math.e)
_NT_DIMS = (((1,), (1,)), ((), ()))


def _na_kernel(*refs, nb, heads, head_dim):
    refs = refs[2:]
    q_ref = refs[0]
    k_refs = refs[1:1 + nb]
    v_refs = refs[1 + nb:1 + 2 * nb]
    kc_ref, vc_ref, tbl_ref, o_ref = refs[1 + 2 * nb:]

    def scores(h):
        sl = slice(h * head_dim, (h + 1) * head_dim)
        q = q_ref[:, sl]
        k = jnp.concatenate([r[:, sl] for r in k_refs], axis=0)
        s = lax.dot_general(q, k, _NT_DIMS, preferred_element_type=F32) + tbl_ref[h]
        sc = lax.dot_general(q, kc_ref[:, sl], _NT_DIMS, preferred_element_type=F32)
        return s, sc

    ones_lat = jnp.ones((nb * k_refs[0].shape[0], head_dim), BF16)
    ones_ctx = jnp.ones((vc_ref.shape[0], head_dim), BF16)

    def attend(h, s_sc):
        s, sc = s_sc
        sl = slice(h * head_dim, (h + 1) * head_dim)
        v = jnp.concatenate([jnp.concatenate([r[:, sl] for r in v_refs], axis=0), ones_lat], axis=1)
        vc = jnp.concatenate([vc_ref[:, sl], ones_ctx], axis=1)
        m = jnp.maximum(jnp.max(s, axis=-1, keepdims=True), jnp.max(sc, axis=-1, keepdims=True))
        p = jnp.exp2(s - m).astype(BF16)
        pc = jnp.exp2(sc - m).astype(BF16)
        o = (jnp.dot(p, v, preferred_element_type=F32) + jnp.dot(pc, vc, preferred_element_type=F32))
        o_ref[:, sl] = (o[:, :head_dim] * (1.0 / o[:, head_dim:head_dim + 1])).astype(o_ref.dtype)

    _skewed_subtiles(heads, scores, attend, ahead=2)


def _na_geometry(rows, qrows):
    kh = min(WIN_H, rows)
    nb = min(kh + qrows - 1, rows)
    n_blk = rows // qrows
    band0s, keys, types, row_rel = [], {}, [], []
    for blk in range(n_blk):
        r0 = blk * qrows
        q_row = r0 + np.arange(qrows)
        r_start = np.clip(q_row - kh // 2, 0, rows - kh)
        band0 = min(int(np.clip(r0 - kh // 2, 0, rows - kh)), rows - nb)
        k_row = band0 + np.arange(nb)
        row_ok = (k_row[None, :] >= r_start[:, None]) & (k_row[None, :] < r_start[:, None] + kh)
        rel = np.where(row_ok, k_row[None, :] - q_row[:, None] + WIN_H - 1, -1)
        key = rel.tobytes()
        if key not in keys:
            keys[key] = len(row_rel)
            row_rel.append(rel)
        types.append(keys[key])
        band0s.append(band0)
    return nb, n_blk, np.array(band0s, np.int32), np.array(types, np.int32), row_rel


def _na_bias_table(rpb, row_rel):
    heads, n_dr, n_dc = rpb.shape
    period = 2 * GRID_W
    assert n_dc <= period
    r = jnp.pad(rpb, ((0, 0), (0, 0), (0, period - n_dc)))
    skew = jnp.tile(r, (1, 1, GRID_W))[:, :, :GRID_W * (period - 1)].reshape(heads, n_dr, GRID_W, period - 1)
    toep = skew[:, :, :, WIN_W - 1:WIN_W - 1 + GRID_W]
    q_col = np.arange(GRID_W)
    c_start = np.clip(q_col - WIN_W // 2, 0, GRID_W - WIN_W)
    col_ok = (q_col[None, :] >= c_start[:, None]) & (q_col[None, :] < c_start[:, None] + WIN_W)
    toep = jnp.where(col_ok, toep, -jnp.inf)
    masked = jnp.full((heads, GRID_W, GRID_W), -jnp.inf, F32)
    tables = []
    for rel in row_rel:
        q_blocks = []
        for qr in range(rel.shape[0]):
            q_blocks.append(jnp.concatenate(
                [toep[:, rel[qr, kr]] if rel[qr, kr] >= 0 else masked for kr in range(rel.shape[1])], axis=-1))
        tables.append(jnp.concatenate(q_blocks, axis=-2))
    return jnp.stack(tables).astype(F32)


def _na_attention(qkv, qkv_ctx, rpb, bn, rows, d, head_dim):
    qrows = min(NA_QROWS, rows)
    nb, n_blk, band0s, types, row_rel = _na_geometry(rows, qrows)
    qw, nk = qrows * GRID_W, nb * GRID_W
    ctx_len = qkv_ctx.shape[1]
    tbl = _na_bias_table(rpb * LOG2_E, row_rel)
    band_tbl = jnp.asarray(band0s)
    type_tbl = jnp.asarray(types)
    kv_view = qkv.reshape(bn * rows, GRID_W, 3 * d)

    def kv_spec(t, col):
        return pl.BlockSpec((None, GRID_W, d), lambda b, i, band, typ: (b * rows + band[i] + t, 0, col))

    in_specs = ([pl.BlockSpec((qw, d), lambda b, i, band, typ: (b * n_blk + i, 0))]
                + [kv_spec(t, 1) for t in range(nb)]
                + [kv_spec(t, 2) for t in range(nb)]
                + [pl.BlockSpec((None, ctx_len, d), lambda b, i, band, typ: (b, 0, 1)),
                   pl.BlockSpec((None, ctx_len, d), lambda b, i, band, typ: (b, 0, 2)),
                   pl.BlockSpec((None, NA_HEADS, qw, nk), lambda b, i, band, typ: (typ[i], 0, 0, 0))])
    vmem = 2 * (2 * qw * d * 2 + 2 * nk * d * 2 + 2 * ctx_len * d * 2 + NA_HEADS * qw * nk * 4) + 2 * VMEM_TEMP_BYTES
    grid_spec = pltpu.PrefetchScalarGridSpec(
        num_scalar_prefetch=2,
        grid=(bn, n_blk),
        in_specs=in_specs,
        out_specs=pl.BlockSpec((qw, d), lambda b, i, band, typ: (b * n_blk + i, 0)),
    )
    return pl.pallas_call(
        functools.partial(_na_kernel, nb=nb, heads=NA_HEADS, head_dim=head_dim),
        grid_spec=grid_spec,
        out_shape=jax.ShapeDtypeStruct((bn * rows * GRID_W, d), BF16),
        compiler_params=_compiler_params(("parallel", "arbitrary"), vmem),
        name="na_attention",
    )(band_tbl, type_tbl, qkv, *([kv_view] * (2 * nb)), qkv_ctx, qkv_ctx, tbl)


def _ctx_attn_kernel(q_ref, k_ref, v_ref, o_ref, *, heads, head_dim):
    for h in range(heads):
        sl = slice(h * head_dim, (h + 1) * head_dim)
        s = lax.dot_general(q_ref[:, sl], k_ref[:, sl], _NT_DIMS, preferred_element_type=F32)
        m = jnp.max(s, axis=-1, keepdims=True)
        p = jnp.exp2(s - m)
        l = jnp.sum(p, axis=-1, keepdims=True)
        o = jnp.dot(p.astype(BF16), v_ref[:, sl], preferred_element_type=F32)
        o_ref[:, sl] = (o * (1.0 / l)).astype(o_ref.dtype)


def _ctx_attention(qkv_ctx, d, head_dim):
    bn, ctx_len, _ = qkv_ctx.shape
    spec = lambda col: pl.BlockSpec((None, ctx_len, d), lambda b: (b, 0, col))
    return pl.pallas_call(
        functools.partial(_ctx_attn_kernel, heads=NA_HEADS, head_dim=head_dim),
        grid=(bn,),
        in_specs=[spec(0), spec(1), spec(2)],
        out_specs=pl.BlockSpec((None, ctx_len, d), lambda b: (b, 0, 0)),
        out_shape=jax.ShapeDtypeStruct((bn, ctx_len, d), BF16),
        compiler_params=_compiler_params(("arbitrary",), 8 * ctx_len * d * 2 + 2 * VMEM_TEMP_BYTES),
        name="ctx_attention",
    )(qkv_ctx, qkv_ctx, qkv_ctx)


HALO = V7X_SUBLANES_F32


def _shift_rows(u, tm):
    n = u.shape[0]
    prev = pltpu.roll(u, 1, 0)[HALO:HALO + tm]
    nxt = pltpu.roll(u, n - 1, 0)[HALO:HALO + tm]
    return prev, nxt


def _conv3(u, cw, rows, seq_edges):
    prev, nxt = _shift_rows(u, rows)
    if seq_edges:
        row = lax.broadcasted_iota(jnp.int32, (rows, 1), 0)
        prev = jnp.where(row == 0, 0.0, prev)
        nxt = jnp.where(row == rows - 1, 0.0, nxt)
    return prev * cw[0:1] + u[HALO:HALO + rows] * cw[1:2] + nxt * cw[2:3]


def _gated_conv_kernel(*refs, kind, tiles_per_seq, mod_rows, sub_rows, seq_edges, side_cast):
    x_ref, xp_ref, xn_ref, g_ref, mod_ref = refs[:5]
    if kind == "ffn":
        w1_refs, cw_refs, cb_refs = refs[5:7], refs[7:9], refs[9:11]
        rest = refs[11:]
    else:
        w1_refs, cw_refs = refs[5:8], refs[8:9]
        rest = refs[9:]
    if side_cast is None:
        w2_ref, o_ref, hs_ref = rest
    else:
        w2_ref, up_f32_ref, down_f32_ref, o_ref, up_bf16_ref, down_bf16_ref, hs_ref = rest
    i = pl.program_id(0)
    j = pl.program_id(1)
    tm = x_ref.shape[0]
    r_shift, r_scale, r_gate = mod_rows

    if side_cast is not None:
        n_up, n_down, down_rows = side_cast
        step = i * pl.num_programs(1) + j

        @pl.when(step < n_up)
        def _():
            _cast_block(up_f32_ref, up_bf16_ref, 0, None)

        @pl.when((step >= n_up) & (step < n_up + n_down))
        def _():
            _cast_block(down_f32_ref, down_bf16_ref, (step - n_up) * down_f32_ref.shape[0], down_rows)

    @pl.when(j == 0)
    def _():
        g = g_ref[...]
        shift = mod_ref[r_shift:r_shift + 1, :]
        scale = mod_ref[r_scale:r_scale + 1, :]
        pos = i % tiles_per_seq
        _store_norm_mod(hs_ref, [(xp_ref, pos == 0), (x_ref, None), (xn_ref, pos == tiles_per_seq - 1)],
                        g, shift, scale)
        o_ref[...] = x_ref[...]

    res_gate = mod_ref[r_gate:r_gate + 1, :]

    def first_matmuls(r):
        lhs = hs_ref[r * sub_rows:r * sub_rows + sub_rows + 2 * HALO, :]
        return [jnp.dot(lhs, w[...], preferred_element_type=F32) for w in w1_refs]

    def mid_and_second_matmul(r, u):
        if kind == "ffn":
            gate, up = [_conv3(u[p], cw_refs[p][...], sub_rows, seq_edges) + cb_refs[p][...]
                        for p in range(2)]
            a = gate * jax.nn.sigmoid(gate) * up
        else:
            a = u[0][HALO:HALO + sub_rows] * _conv3(u[1] * u[2], cw_refs[0][...], sub_rows, seq_edges)
        o_ref[r * sub_rows:(r + 1) * sub_rows, :] += res_gate * jnp.dot(a.astype(BF16), w2_ref[...],
                                                                       preferred_element_type=F32)

    _skewed_subtiles(tm // sub_rows, first_matmuls, mid_and_second_matmul)


def _gated_conv(kind, x, seq_len, norm_g, mods, layer, row_fn, mod_rows, w1, cw, cb, w2, tc, tm_pref,
                sub_pref=SUB_ROWS, cast_next=None):
    m, d = x.shape
    parts = 2 if kind == "ffn" else 3
    cparts = 2 if kind == "ffn" else 1
    n_chunks = w2.shape[0] // tc
    assert w1.shape == (parts * n_chunks, d, tc) and cw.shape[1] == cparts * n_chunks * tc
    if seq_len >= tm_pref:
        tm = _tile(seq_len, tm_pref)
        sub_rows = _tile(tm, sub_pref)
        tiles_per_seq, seq_edges = seq_len // tm, False
    else:
        tm = seq_len * _tile(m // seq_len, tm_pref // seq_len)
        sub_rows = seq_len
        tiles_per_seq, seq_edges = 1, True
    assert sub_rows % V7X_SUBLANES_BF16 == 0
    hb = tm // HALO
    n_hblk = m // HALO
    other = 2 * (d * parts * tc * 2 + tc * d * 2) + (tm + 2 * HALO) * d * 2
    n_steps = (m // tm) * n_chunks
    side_cast, side_specs_in, side_specs_out, side_shapes, side_operands = None, [], [], [], []
    if cast_next is not None:
        w_up, w_down, nl = cast_next
        f, fp = w_down.shape[1], n_chunks * tc
        up_rows = min(t for t in range(V7X_SUBLANES_BF16, d + 1, V7X_SUBLANES_BF16)
                      if d % t == 0 and 2 * (d // t) <= 3 * n_steps // 4)
        n_up = 2 * (d // up_rows)
        down_rows = min(t for t in range(V7X_SUBLANES_BF16, fp + 1, V7X_SUBLANES_BF16)
                        if fp % t == 0 and fp // t <= n_steps - n_up)
        n_down = fp // down_rows
        last_down_block = (f - 1) // down_rows
        up_item = lambda i, j: jnp.minimum(i * n_chunks + j, n_up - 1)
        down_item = lambda i, j: jnp.clip(i * n_chunks + j - n_up, 0, n_down - 1)
        side_specs_in = [
            pl.BlockSpec((None, up_rows, f), lambda i, j: (nl, up_item(i, j) // 2, up_item(i, j) % 2)),
            pl.BlockSpec((None, down_rows, d),
                         lambda i, j: (nl, jnp.minimum(down_item(i, j), last_down_block), 0)),
        ]
        side_specs_out = [
            pl.BlockSpec((n_chunks, up_rows, tc), lambda i, j: (up_item(i, j) % 2, up_item(i, j) // 2, 0)),
            pl.BlockSpec((down_rows, d), lambda i, j: (down_item(i, j), 0)),
        ]
        side_shapes = [jax.ShapeDtypeStruct((2 * n_chunks, d, tc), BF16), jax.ShapeDtypeStruct((fp, d), BF16)]
        side_operands = [w_up, w_down]
        side_cast = (n_up, n_down, f)
        other += 2 * (up_rows * f * 4 + up_rows * fp * 2 + down_rows * d * 6)
    x_bufs, out_bufs = _row_tile_buffers(tm * d * 4, other)
    vmem = (x_bufs + out_bufs) * tm * d * 4 + other + VMEM_TEMP_BYTES
    col_spec = lambda rows, p: pl.BlockSpec((rows, tc), lambda i, j: (0, p * n_chunks + j))
    in_specs = [
        pl.BlockSpec((tm, d), lambda i, j: (i, 0), pipeline_mode=pl.Buffered(x_bufs)),
        pl.BlockSpec((HALO, d), lambda i, j: (jnp.maximum(i * hb - 1, 0), 0)),
        pl.BlockSpec((HALO, d), lambda i, j: (jnp.minimum((i + 1) * hb, n_hblk - 1), 0)),
        pl.BlockSpec((1, d), lambda i, j: (0, 0)),
        _mod_spec(layer, lambda i: row_fn(i, tm), d),
    ]
    in_specs += [pl.BlockSpec((None, d, tc), functools.partial(lambda p, i, j: (p * n_chunks + j, 0, 0), p))
                 for p in range(parts)]
    in_specs += [col_spec(3, p) for p in range(cparts)]
    operands = [x, x, x, norm_g, mods] + [w1] * parts + [cw] * cparts
    if kind == "ffn":
        in_specs += [col_spec(1, p) for p in range(cparts)]
        operands += [cb] * cparts
    in_specs.append(pl.BlockSpec((tc, d), lambda i, j: (j, 0)))
    operands.append(w2)
    out_spec = pl.BlockSpec((tm, d), lambda i, j: (i, 0), pipeline_mode=pl.Buffered(out_bufs))
    out_shape = jax.ShapeDtypeStruct((m, d), F32)
    result = pl.pallas_call(
        functools.partial(_gated_conv_kernel, kind=kind, tiles_per_seq=tiles_per_seq, mod_rows=mod_rows,
                          sub_rows=sub_rows, seq_edges=seq_edges, side_cast=side_cast),
        grid=(m // tm, n_chunks),
        in_specs=in_specs + side_specs_in,
        out_specs=[out_spec] + side_specs_out if side_cast else out_spec,
        out_shape=[out_shape] + side_shapes if side_cast else out_shape,
        scratch_shapes=[pltpu.VMEM((tm + 2 * HALO, d), BF16)],
        compiler_params=_compiler_params(("arbitrary" if side_cast else "parallel", "arbitrary"), vmem),
        name=kind,
    )(*operands, *side_operands)
    return tuple(result) if side_cast else result


def _gelu_tanh(x):
    return jax.nn.gelu(x, approximate=True)


def _gmlp_kernel(x_ref, g_ref, mod_ref, win_ref, vg_ref, ws_ref, bs_ref, wout_ref, o_ref,
                 hs_ref, v_ref, ssq_ref, *, n_chunks, group_dim, width, sub_rows):
    j = pl.program_id(1)
    tm = x_ref.shape[0]
    tc = win_ref.shape[1]
    n_sub = tm // sub_rows

    @pl.when(j == 0)
    def _():
        _store_norm_mod(hs_ref, [(x_ref, None)], g_ref[...], mod_ref[0:1, :], mod_ref[1:2, :])
        ssq_ref[...] = jnp.zeros_like(ssq_ref)
        o_ref[...] = x_ref[...]

    def in_matmul(r):
        return jnp.dot(hs_ref[r * sub_rows:(r + 1) * sub_rows, :], win_ref[...], preferred_element_type=F32)

    @pl.when(j < n_chunks)
    def _():
        def keep_v(r, y):
            rows = slice(r * sub_rows, (r + 1) * sub_rows)
            t = _gelu_tanh(y)
            v_ref[j, rows, :] = t.astype(v_ref.dtype)
            ssq_ref[rows, :] += jnp.sum(t * t, axis=-1, keepdims=True)

        _skewed_subtiles(n_sub, in_matmul, keep_v)

    @pl.when(j >= n_chunks)
    def _():
        jj = j - n_chunks
        vg = vg_ref[...]
        res_gate = mod_ref[2:3, :]
        bs = bs_ref[...]
        gpc = tc // group_dim
        ws = [ws_ref[jj * gpc + gi] for gi in range(gpc)]

        def gate_and_project(r, y):
            rows = slice(r * sub_rows, (r + 1) * sub_rows)
            inv = lax.rsqrt(ssq_ref[rows, :] * (1.0 / width) + EPS)
            vn = (v_ref[jj, rows, :].astype(F32) * inv * vg).astype(BF16)
            cols = []
            for gi in range(gpc):
                csl = slice(gi * group_dim, (gi + 1) * group_dim)
                cols.append(jnp.concatenate(
                    [jnp.dot(ws[gi], vn[c * GM_CHUNK:(c + 1) * GM_CHUNK, csl], preferred_element_type=F32)
                     + bs[:, csl] for c in range(sub_rows // GM_CHUNK)], axis=0))
            sv = jnp.concatenate(cols, axis=1)
            o_ref[rows, :] += res_gate * jnp.dot((_gelu_tanh(y) * sv).astype(BF16), wout_ref[...],
                                                 preferred_element_type=F32)

        _skewed_subtiles(n_sub, in_matmul, gate_and_project)


def _gmlp(x, norm_g, mods, layer, row_fn, w_in, v_g, w_s, b_s_cols, w_out, tm_pref):
    m, d = x.shape
    width = w_out.shape[0]
    group_dim = width // GM_GROUPS
    tc = w_in.shape[2]
    n_chunks = width // tc
    assert w_in.shape[0] == 2 * n_chunks and tc % group_dim == 0
    tm = _tile(m, tm_pref)
    sub_rows = _tile(tm, SUB_ROWS)
    assert sub_rows % GM_CHUNK == 0
    other = (2 * (d * tc * 2 + tc * d * 2 + GM_CHUNK * tc * 4) + tm * d * 2 + tm * width * 2
             + GM_GROUPS * GM_CHUNK * GM_CHUNK * 2 + tm * V7X_LANES * 4)
    x_bufs, out_bufs = _row_tile_buffers(tm * d * 4, other)
    vmem = (x_bufs + out_bufs) * tm * d * 4 + other + VMEM_TEMP_BYTES
    return pl.pallas_call(
        functools.partial(_gmlp_kernel, n_chunks=n_chunks, group_dim=group_dim, width=width,
                          sub_rows=sub_rows),
        grid=(m // tm, 2 * n_chunks),
        in_specs=[
            pl.BlockSpec((tm, d), lambda i, j: (i, 0), pipeline_mode=pl.Buffered(x_bufs)),
            pl.BlockSpec((1, d), lambda i, j: (0, 0)),
            _mod_spec(layer, lambda i: row_fn(i, tm), d),
            pl.BlockSpec((None, d, tc),
                         lambda i, j: (jnp.where(j < n_chunks, j + n_chunks, j - n_chunks), 0, 0)),
            pl.BlockSpec((1, tc), lambda i, j: (0, jnp.maximum(j - n_chunks, 0))),
            pl.BlockSpec((GM_GROUPS, GM_CHUNK, GM_CHUNK), lambda i, j: (0, 0, 0)),
            pl.BlockSpec((GM_CHUNK, tc), lambda i, j: (0, jnp.maximum(j - n_chunks, 0))),
            pl.BlockSpec((tc, d), lambda i, j: (jnp.maximum(j - n_chunks, 0), 0)),
        ],
        out_specs=pl.BlockSpec((tm, d), lambda i, j: (i, 0), pipeline_mode=pl.Buffered(out_bufs)),
        out_shape=jax.ShapeDtypeStruct((m, d), F32),
        scratch_shapes=[pltpu.VMEM((tm, d), BF16), pltpu.VMEM((n_chunks, tm, tc), BF16), pltpu.VMEM((tm, 1), F32)],
        compiler_params=_compiler_params(("parallel", "arbitrary"), vmem),
        name="gmlp",
    )(x, norm_g, mods, w_in, v_g, w_s, b_s_cols, w_out)


CAST_BLOCK_BYTES = 6 << 20


def _cast_kernel(x_ref, o_ref, *, valid_rows):
    _cast_block(x_ref, o_ref, pl.program_id(0) * x_ref.shape[0], valid_rows)


def _cast_block(x_ref, o_ref, row0, valid_rows):
    rows, f = x_ref.shape
    x = x_ref[...]
    if valid_rows is not None:
        row = row0 + lax.broadcasted_iota(jnp.int32, (rows, 1), 0)
        x = jnp.where(row < valid_rows, x, 0.0)
    x = x.astype(o_ref.dtype)
    if len(o_ref.shape) == 2:
        o_ref[...] = x
        return
    n_chunks, _, tc = o_ref.shape
    for c in range(n_chunks):
        width = max(0, min(tc, f - c * tc))
        if width:
            o_ref[c, :, :width] = x[:, c * tc:c * tc + width]
        if width < tc:
            o_ref[c, :, width:] = jnp.zeros((rows, tc - width), o_ref.dtype)


def _cast_weight(w, idx, parts=1, fp=None, col_chunk=None, rows_out=None, row_block=None):
    _, r, c = w.shape
    f = c // parts
    fp = f if fp is None else fp
    rows_out = r if rows_out is None else rows_out
    assert parts == 1 or f % V7X_LANES == 0
    assert col_chunk is not None or (parts == 1 and fp == f)
    padded_rows = rows_out > r
    if row_block is None:
        row_pref = max(V7X_SUBLANES_BF16, CAST_BLOCK_BYTES // (4 * f))
        row_block = max(t for t in range(V7X_SUBLANES_BF16, row_pref + 1, V7X_SUBLANES_BF16)
                        if rows_out % t == 0)
    assert rows_out % row_block == 0 and row_block % V7X_SUBLANES_BF16 == 0
    assert r % row_block == 0 or (padded_rows and rows_out - row_block < r)
    if col_chunk is None:
        out_spec = pl.BlockSpec((row_block, f), lambda i, p: (i, 0))
        out_shape = (rows_out, f)
    else:
        assert fp % col_chunk == 0 and col_chunk % V7X_LANES == 0 and not padded_rows
        n_chunks = fp // col_chunk
        out_spec = pl.BlockSpec((n_chunks, row_block, col_chunk), lambda i, p: (p, i, 0))
        out_shape = (parts * n_chunks, r, col_chunk)
    return pl.pallas_call(
        functools.partial(_cast_kernel, valid_rows=r if padded_rows else None),
        grid=(rows_out // row_block, parts),
        in_specs=[pl.BlockSpec((None, row_block, f), lambda i, p: (idx, i, p))],
        out_specs=out_spec,
        out_shape=jax.ShapeDtypeStruct(out_shape, BF16),
        compiler_params=_compiler_params(("parallel", "arbitrary"),
                                         2 * row_block * (4 * f + 2 * fp) + VMEM_TEMP_BYTES),
        name="cast_weight",
    )(w)


def _pad_halves(a, f, fp):
    pad = lambda h: jnp.pad(h, ((0, 0), (0, fp - f)))
    return jnp.concatenate([pad(a[:, :f]), pad(a[:, f:])], axis=1)


def _prep_ffn(w_up, conv_w, conv_b, w_down, layer, tc, cast_weights):
    f = w_down.shape[1]
    fp = -(-f // tc) * tc
    cw = _pad_halves(conv_w[layer], f, fp)
    cb = _pad_halves(conv_b[layer][None, :], f, fp)
    if cast_weights is None:
        cast_weights = (_cast_weight(w_up, layer, parts=2, fp=fp, col_chunk=tc),
                        _cast_weight(w_down, layer, rows_out=fp))
    return cast_weights[0], cw, cb, cast_weights[1]


def kernel(x, c, ctx, c_ctx, norm_mix_g, norm_ffn_g, w_ada, b_ada, na_w_qkv, na_q_g, na_k_g, na_rpb, na_w_o,
           gm_w_in, gm_v_g, gm_w_s, gm_b_s, gm_w_out, sc_w_in, sc_conv_w, sc_w_out,
           ffn_w_up, ffn_conv_w, ffn_conv_b, ffn_w_down):
    bn, seq, d = x.shape
    ctx_len = ctx.shape[1]
    depth = w_ada.shape[0]
    head_dim = d // NA_HEADS
    rows = seq // GRID_W
    assert bn + 1 <= COND_ROWS and seq % GRID_W == 0 and ctx_len % GM_CHUNK == 0

    cond = jnp.concatenate([c, c_ctx[None, :], jnp.zeros((COND_ROWS - bn - 1, d), F32)], axis=0)
    mods = _adaln(cond, w_ada, b_ada).reshape(depth, COND_ROWS, N_MOD, d)

    lat_row = lambda i, tm: i // (seq // tm)
    ctx_row = lambda i, tm: bn

    ffn_hidden = ffn_w_down.shape[1]
    ffn_tc = HIDDEN_CHUNK if ffn_hidden >= HIDDEN_CHUNK else V7X_LANES
    sc_tc = _tile(d, HIDDEN_CHUNK)
    gm_width = gm_w_out.shape[1]
    gm_tc = max(gm_width // GM_GROUPS, _tile(gm_width, HIDDEN_CHUNK))
    proj_tn = _tile(d, PROJ_COLS)
    qkv_tn = _tile(d, QKV_COLS)
    lat_tm = ROW_TILE
    ctx_tm = min(ROW_TILE, bn * ctx_len)
    lat_sub = HALO_SUB_ROWS

    xl = x.reshape(bn * seq, d)
    xc = ctx.reshape(bn * ctx_len, d)
    ffn_weights = None
    for i in range(depth):
        last = i == depth - 1
        mixer, jx = i % N_MIXERS, i // N_MIXERS
        g_mix = norm_mix_g[i][None, :]
        g_ffn = norm_ffn_g[i][None, :]
        if mixer == 0:
            w_qkv = _cast_weight(na_w_qkv, jx, col_chunk=qkv_tn)
            w_o = _cast_weight(na_w_o, jx, col_chunk=proj_tn)
            q_gain = na_q_g[jx] * (head_dim ** -0.5 * LOG2_E)
            head_gain = jnp.concatenate([jnp.tile(q_gain, NA_HEADS), jnp.tile(na_k_g[jx], NA_HEADS),
                                         jnp.ones((d,), F32)])[None, :]
            qkv = _qkv(xl, g_mix, mods, i, lat_row, w_qkv, head_gain, head_dim)
            qkv_c = _qkv(xc, g_mix, mods, i, ctx_row, w_qkv, head_gain, head_dim).reshape(bn, ctx_len, 3 * d)
            att = _na_attention(qkv, qkv_c, na_rpb[jx], bn, rows, d, head_dim)
            xl_new = _proj_res(att, w_o, xl, mods, i, lat_row, 2)
            if not last:
                att_c = _ctx_attention(qkv_c, d, head_dim).reshape(bn * ctx_len, d)
                xc = _proj_res(att_c, w_o, xc, mods, i, ctx_row, 2)
            xl = xl_new
        elif mixer == 1:
            w_in = _cast_weight(gm_w_in, jx, col_chunk=gm_tc)
            w_out = _cast_weight(gm_w_out, jx)
            w_s = gm_w_s[jx].astype(BF16)
            width = w_out.shape[0]
            v_g = gm_v_g[jx][None, :]
            b_s_cols = jnp.repeat(gm_b_s[jx].T, width // GM_GROUPS, axis=1)
            xl = _gmlp(xl, g_mix, mods, i, lat_row, w_in, v_g, w_s, b_s_cols, w_out, lat_tm)
            if not last:
                xc = _gmlp(xc, g_mix, mods, i, ctx_row, w_in, v_g, w_s, b_s_cols, w_out, ctx_tm)
        else:
            w1 = _cast_weight(sc_w_in, jx, parts=3, col_chunk=sc_tc)
            cw, w2 = sc_conv_w[jx], _cast_weight(sc_w_out, jx)
            xl = _gated_conv("sc", xl, seq, g_mix, mods, i, lat_row, (0, 1, 2), w1, cw, None, w2, sc_tc, lat_tm,
                             sub_pref=lat_sub)
            if not last:
                xc = _gated_conv("sc", xc, ctx_len, g_mix, mods, i, ctx_row, (0, 1, 2), w1, cw, None, w2, sc_tc,
                                 ctx_tm)
        w1, cw, cb, w2 = _prep_ffn(ffn_w_up, ffn_conv_w, ffn_conv_b, ffn_w_down, i, ffn_tc, ffn_weights)
        if last:
            xl = _gated_conv("ffn", xl, seq, g_ffn, mods, i, lat_row, (3, 4, 5), w1, cw, cb, w2, ffn_tc, lat_tm,
                             sub_pref=lat_sub)
        else:
            xl, *ffn_weights = _gated_conv("ffn", xl, seq, g_ffn, mods, i, lat_row, (3, 4, 5), w1, cw, cb, w2,
                                           ffn_tc, lat_tm, sub_pref=lat_sub,
                                           cast_next=(ffn_w_up, ffn_w_down, i + 1))
        if not last:
            xc = _gated_conv("ffn", xc, ctx_len, g_ffn, mods, i, ctx_row, (3, 4, 5), w1, cw, cb, w2, ffn_tc,
                             ctx_tm)
    return xl.reshape(bn, seq, d)
```

```python
import functools
import math

import numpy as np
import jax
import jax.numpy as jnp
from jax import lax
from jax.experimental import pallas as pl
from jax.experimental.pallas import tpu as pltpu

GRID_W = 64
NA_HEADS = 16
WIN_H = 8
WIN_W = 16
GM_GROUPS = 16
GM_CHUNK = 128
N_MOD = 6
N_MIXERS = 3
EPS = 1e-6

V7X_VMEM_BYTES = 64 * 1024 * 1024
V7X_SUBLANES_F32 = 8
V7X_SUBLANES_BF16 = 16
V7X_LANES = 128
VMEM_LIMIT_BYTES = V7X_VMEM_BYTES - (4 << 20)
VMEM_TEMP_BYTES = 6 << 20

F32 = jnp.float32
BF16 = jnp.bfloat16

ROW_TILE = 1024
SUB_ROWS = 256
HALO_SUB_ROWS = 512
HIDDEN_CHUNK = 512
PROJ_COLS = 1024
QKV_COLS = 2048
ADALN_COLS = 1024
NA_QROWS = 2
NA_STEP_BLOCKS = 2
COND_ROWS = V7X_SUBLANES_F32


def _compiler_params(semantics, vmem_bytes):
    return pltpu.CompilerParams(dimension_semantics=semantics,
                                vmem_limit_bytes=int(min(vmem_bytes, VMEM_LIMIT_BYTES)))


def _tile(n, pref):
    t = min(n, pref)
    while n % t:
        t -= 1
    return t


def _row_tile_buffers(tile_bytes, other_bytes):
    for x_bufs, out_bufs in ((2, 2), (1, 2), (1, 1)):
        if (x_bufs + out_bufs) * tile_bytes + other_bytes + VMEM_TEMP_BYTES <= VMEM_LIMIT_BYTES:
            break
    return x_bufs, out_bufs


def _norm_mod(x, gain, shift):
    ms = jnp.mean(x * x, axis=-1, keepdims=True)
    return x * lax.rsqrt(ms + EPS) * gain + shift


NORM_CHUNK = 16


def _store_norm_mod(hs_ref, sources, g, shift, scale, copy_ref=None):
    bounds = np.cumsum([0] + [ref.shape[0] for ref, _ in sources])
    total = int(bounds[-1])
    assert hs_ref.shape[0] == total
    gain = g * (1.0 + scale)
    for c0 in range(0, total, NORM_CHUNK):
        c1 = min(c0 + NORM_CHUNK, total)
        pieces = []
        for (ref, zero_pred), b0, b1 in zip(sources, bounds[:-1], bounds[1:]):
            lo, hi = max(c0, int(b0)), min(c1, int(b1))
            if lo >= hi:
                continue
            rows = ref[lo - int(b0):hi - int(b0), :]
            if copy_ref is not None and zero_pred is None:
                copy_ref[lo - int(b0):hi - int(b0), :] = rows
            h = _norm_mod(rows, gain, shift)
            pieces.append(h if zero_pred is None else jnp.where(zero_pred, 0.0, h))
        h = pieces[0] if len(pieces) == 1 else jnp.concatenate(pieces, axis=0)
        hs_ref[c0:c1, :] = h.astype(BF16)


def _adaln_kernel(c_ref, w_ref, b_ref, o_ref):
    c = c_ref[...]
    a = (c * jax.nn.sigmoid(c)).astype(BF16)
    o_ref[...] = jnp.dot(a, w_ref[...].astype(BF16), preferred_element_type=F32) + b_ref[...]


def _adaln(cond, w_ada, b_ada):
    depth, d, n = w_ada.shape
    tn = _tile(n, ADALN_COLS)
    return pl.pallas_call(
        _adaln_kernel,
        grid=(depth, n // tn),
        in_specs=[
            pl.BlockSpec((COND_ROWS, d), lambda l, j: (0, 0)),
            pl.BlockSpec((None, d, tn), lambda l, j: (l, 0, j)),
            pl.BlockSpec((None, 1, tn), lambda l, j: (l, 0, j)),
        ],
        out_specs=pl.BlockSpec((None, COND_ROWS, tn), lambda l, j: (l, 0, j)),
        out_shape=jax.ShapeDtypeStruct((depth, COND_ROWS, n), F32),
        compiler_params=_compiler_params(("arbitrary", "arbitrary"), 3 * d * tn * 4 + VMEM_TEMP_BYTES),
        name="adaln",
    )(cond, w_ada, b_ada.reshape(depth, 1, n))


def _mod_spec(layer, row_fn, width, col_fn=None):
    if col_fn is None:
        return pl.BlockSpec((None, None, N_MOD, width), lambda i, j: (layer, row_fn(i), 0, 0))
    return pl.BlockSpec((None, None, N_MOD, width), lambda i, j: (layer, row_fn(i), 0, col_fn(j)))


def _skewed_subtiles(n_sub, first, second, ahead=1):
    pending = [first(r) for r in range(min(ahead, n_sub))]
    for r in range(n_sub):
        if r + ahead < n_sub:
            pending.append(first(r + ahead))
        second(r, pending.pop(0))


def _proj_res_kernel(a_ref, w_ref, x_ref, mod_ref, o_ref, *, gate_row, sub_rows):
    gate = mod_ref[gate_row:gate_row + 1, :]

    def matmul(r):
        return jnp.dot(a_ref[r * sub_rows:(r + 1) * sub_rows, :], w_ref[...], preferred_element_type=F32)

    def residual(r, y):
        rows = slice(r * sub_rows, (r + 1) * sub_rows)
        o_ref[rows, :] = x_ref[rows, :] + gate * y

    _skewed_subtiles(o_ref.shape[0] // sub_rows, matmul, residual)


def _proj_res(a, w, x, mods, layer, row_fn, gate_row):
    m, k = a.shape
    n_tiles, _, tn = w.shape
    n = n_tiles * tn
    tm = _tile(m, ROW_TILE)
    vmem = 2 * (tm * k * 2 + k * tn * 2 + 2 * tm * tn * 4) + VMEM_TEMP_BYTES
    return pl.pallas_call(
        functools.partial(_proj_res_kernel, gate_row=gate_row, sub_rows=_tile(tm, SUB_ROWS)),
        grid=(m // tm, n // tn),
        in_specs=[
            pl.BlockSpec((tm, k), lambda i, j: (i, 0)),
            pl.BlockSpec((None, k, tn), lambda i, j: (j, 0, 0)),
            pl.BlockSpec((tm, tn), lambda i, j: (i, j)),
            _mod_spec(layer, lambda i: row_fn(i, tm), tn, lambda j: j),
        ],
        out_specs=pl.BlockSpec((tm, tn), lambda i, j: (i, j)),
        out_shape=jax.ShapeDtypeStruct((m, n), F32),
        compiler_params=_compiler_params(("parallel", "arbitrary"), vmem),
        name="proj_res",
    )(a, w, x, mods)


def _qkv_kernel(x_ref, g_ref, mod_ref, w_ref, hg_ref, o_ref, hs_ref, *, n_norm_tiles, head_dim, sub_rows):
    j = pl.program_id(1)

    @pl.when(j == 0)
    def _():
        _store_norm_mod(hs_ref, [(x_ref, None)], g_ref[...], mod_ref[0:1, :], mod_ref[1:2, :])

    tm, tn = o_ref.shape
    is_norm_tile = j < n_norm_tiles
    hg = hg_ref[...]

    def matmul(r):
        return jnp.dot(hs_ref[r * sub_rows:(r + 1) * sub_rows, :], w_ref[...], preferred_element_type=F32)

    def head_norm(r, y):
        rows = slice(r * sub_rows, (r + 1) * sub_rows)
        for hh in range(tn // head_dim):
            sl = slice(hh * head_dim, (hh + 1) * head_dim)
            t = y[:, sl]
            inv = lax.rsqrt(jnp.mean(t * t, axis=-1, keepdims=True) + EPS)
            o_ref[rows, sl] = (t * jnp.where(is_norm_tile, inv, 1.0) * hg[:, sl]).astype(o_ref.dtype)

    _skewed_subtiles(tm // sub_rows, matmul, head_norm)


def _qkv(x, norm_g, mods, layer, row_fn, w_qkv, head_gain, head_dim):
    m, d = x.shape
    n_tiles, _, tn = w_qkv.shape
    n = n_tiles * tn
    tm = _tile(m, ROW_TILE)
    assert (n // 3) % tn == 0 and tn % head_dim == 0
    vmem = 2 * (tm * d * 4 + d * tn * 2 + tm * tn * 2) + tm * d * 2 + 2 * VMEM_TEMP_BYTES
    return pl.pallas_call(
        functools.partial(_qkv_kernel, n_norm_tiles=2 * (n // 3) // tn, head_dim=head_dim,
                          sub_rows=_tile(tm, SUB_ROWS)),
        grid=(m // tm, n // tn),
        in_specs=[
            pl.BlockSpec((tm, d), lambda i, j: (i, 0)),
            pl.BlockSpec((1, d), lambda i, j: (0, 0)),
            _mod_spec(layer, lambda i: row_fn(i, tm), d),
            pl.BlockSpec((None, d, tn), lambda i, j: (j, 0, 0)),
            pl.BlockSpec((1, tn), lambda i, j: (0, j)),
        ],
        out_specs=pl.BlockSpec((tm, tn), lambda i, j: (i, j)),
        out_shape=jax.ShapeDtypeStruct((m, n), BF16),
        scratch_shapes=[pltpu.VMEM((tm, d), BF16)],
        compiler_params=_compiler_params(("parallel", "arbitrary"), vmem),
        name="qkv",
    )(x, norm_g, mods, w_qkv, head_gain)


LOG2_E = math.log2(math.e)
_NT_DIMS = (((1,), (1,)), ((), ()))


def _na_kernel(*refs, nb, heads, head_dim, step_blocks):
    refs = refs[2:]
    q_ref = refs[0]
    k_refs = [refs[1 + b * nb:1 + (b + 1) * nb] for b in range(step_blocks)]
    v_refs = [refs[1 + (step_blocks + b) * nb:1 + (step_blocks + b + 1) * nb] for b in range(step_blocks)]
    rest = refs[1 + 2 * step_blocks * nb:]
    kc_ref, vc_ref = rest[:2]
    tbl_refs = rest[2:2 + step_blocks]
    o_ref = rest[2 + step_blocks]
    qw = q_ref.shape[0] // step_blocks

    def scores(u):
        blk, h = divmod(u, heads)
        sl = slice(h * head_dim, (h + 1) * head_dim)
        q = q_ref[blk * qw:(blk + 1) * qw, sl]
        k = jnp.concatenate([r[:, sl] for r in k_refs[blk]], axis=0)
        s = lax.dot_general(q, k, _NT_DIMS, preferred_element_type=F32) + tbl_refs[blk][h]
        sc = lax.dot_general(q, kc_ref[:, sl], _NT_DIMS, preferred_element_type=F32)
        return s, sc

    ones_lat = jnp.ones((nb * k_refs[0][0].shape[0], head_dim), BF16)
    ones_ctx = jnp.ones((vc_ref.shape[0], head_dim), BF16)

    def attend(u, s_sc):
        blk, h = divmod(u, heads)
        s, sc = s_sc
        sl = slice(h * head_dim, (h + 1) * head_dim)
        v = jnp.concatenate([jnp.concatenate([r[:, sl] for r in v_refs[blk]], axis=0), ones_lat], axis=1)
        vc = jnp.concatenate([vc_ref[:, sl], ones_ctx], axis=1)
        m = jnp.maximum(jnp.max(s, axis=-1, keepdims=True), jnp.max(sc, axis=-1, keepdims=True))
        p = jnp.exp2(s - m).astype(BF16)
        pc = jnp.exp2(sc - m).astype(BF16)
        o = (jnp.dot(p, v, preferred_element_type=F32) + jnp.dot(pc, vc, preferred_element_type=F32))
        o_ref[blk * qw:(blk + 1) * qw, sl] = (
            o[:, :head_dim] * (1.0 / o[:, head_dim:head_dim + 1])).astype(o_ref.dtype)

    _skewed_subtiles(step_blocks * heads, scores, attend, ahead=2)


def _na_geometry(rows, qrows):
    kh = min(WIN_H, rows)
    nb = min(kh + qrows - 1, rows)
    n_blk = rows // qrows
    band0s, keys, types, row_rel = [], {}, [], []
    for blk in range(n_blk):
        r0 = blk * qrows
        q_row = r0 + np.arange(qrows)
        r_start = np.clip(q_row - kh // 2, 0, rows - kh)
        band0 = min(int(np.clip(r0 - kh // 2, 0, rows - kh)), rows - nb)
        k_row = band0 + np.arange(nb)
        row_ok = (k_row[None, :] >= r_start[:, None]) & (k_row[None, :] < r_start[:, None] + kh)
        rel = np.where(row_ok, k_row[None, :] - q_row[:, None] + WIN_H - 1, -1)
        key = rel.tobytes()
        if key not in keys:
            keys[key] = len(row_rel)
            row_rel.append(rel)
        types.append(keys[key])
        band0s.append(band0)
    return nb, n_blk, np.array(band0s, np.int32), np.array(types, np.int32), row_rel


def _na_bias_table(rpb, row_rel):
    heads, n_dr, n_dc = rpb.shape
    period = 2 * GRID_W
    assert n_dc <= period
    r = jnp.pad(rpb, ((0, 0), (0, 0), (0, period - n_dc)))
    skew = jnp.tile(r, (1, 1, GRID_W))[:, :, :GRID_W * (period - 1)].reshape(heads, n_dr, GRID_W, period - 1)
    toep = skew[:, :, :, WIN_W - 1:WIN_W - 1 + GRID_W]
    q_col = np.arange(GRID_W)
    c_start = np.clip(q_col - WIN_W // 2, 0, GRID_W - WIN_W)
    col_ok = (q_col[None, :] >= c_start[:, None]) & (q_col[None, :] < c_start[:, None] + WIN_W)
    toep = jnp.where(col_ok, toep, -jnp.inf)
    masked = jnp.full((heads, GRID_W, GRID_W), -jnp.inf, F32)
    tables = []
    for rel in row_rel:
        q_blocks = []
        for qr in range(rel.shape[0]):
            q_blocks.append(jnp.concatenate(
                [toep[:, rel[qr, kr]] if rel[qr, kr] >= 0 else masked for kr in range(rel.shape[1])], axis=-1))
        tables.append(jnp.concatenate(q_blocks, axis=-2))
    return jnp.stack(tables).astype(F32)


def _na_attention(qkv, qkv_ctx, rpb, bn, rows, d, head_dim):
    qrows = min(NA_QROWS, rows)
    nb, n_blk, band0s, types, row_rel = _na_geometry(rows, qrows)
    qw, nk = qrows * GRID_W, nb * GRID_W
    ctx_len = qkv_ctx.shape[1]
    tbl = _na_bias_table(rpb * LOG2_E, row_rel)
    band_tbl = jnp.asarray(band0s)
    type_tbl = jnp.asarray(types)
    kv_view = qkv.reshape(bn * rows, GRID_W, 3 * d)

    spb = _tile(n_blk, NA_STEP_BLOCKS)
    n_steps = n_blk // spb

    def kv_spec(blk, t, col):
        return pl.BlockSpec((None, GRID_W, d),
                            lambda b, i, band, typ: (b * rows + band[i * spb + blk] + t, 0, col))

    def tbl_spec(blk):
        return pl.BlockSpec((None, NA_HEADS, qw, nk), lambda b, i, band, typ: (typ[i * spb + blk], 0, 0, 0))

    in_specs = ([pl.BlockSpec((spb * qw, d), lambda b, i, band, typ: (b * n_steps + i, 0))]
                + [kv_spec(blk, t, 1) for blk in range(spb) for t in range(nb)]
                + [kv_spec(blk, t, 2) for blk in range(spb) for t in range(nb)]
                + [pl.BlockSpec((None, ctx_len, d), lambda b, i, band, typ: (b, 0, 1)),
                   pl.BlockSpec((None, ctx_len, d), lambda b, i, band, typ: (b, 0, 2))]
                + [tbl_spec(blk) for blk in range(spb)])
    vmem = (2 * (spb * (2 * qw * d * 2 + 2 * nk * d * 2 + NA_HEADS * qw * nk * 4) + 2 * ctx_len * d * 2)
            + 2 * VMEM_TEMP_BYTES)
    grid_spec = pltpu.PrefetchScalarGridSpec(
        num_scalar_prefetch=2,
        grid=(bn, n_steps),
        in_specs=in_specs,
        out_specs=pl.BlockSpec((spb * qw, d), lambda b, i, band, typ: (b * n_steps + i, 0)),
    )
    return pl.pallas_call(
        functools.partial(_na_kernel, nb=nb, heads=NA_HEADS, head_dim=head_dim, step_blocks=spb),
        grid_spec=grid_spec,
        out_shape=jax.ShapeDtypeStruct((bn * rows * GRID_W, d), BF16),
        compiler_params=_compiler_params(("parallel", "arbitrary"), vmem),
        name="na_attention",
    )(band_tbl, type_tbl, qkv, *([kv_view] * (2 * spb * nb)), qkv_ctx, qkv_ctx, *([tbl] * spb))


def _ctx_attn_kernel(q_ref, k_ref, v_ref, o_ref, *, heads, head_dim):
    for h in range(heads):
        sl = slice(h * head_dim, (h + 1) * head_dim)
        s = lax.dot_general(q_ref[:, sl], k_ref[:, sl], _NT_DIMS, preferred_element_type=F32)
        m = jnp.max(s, axis=-1, keepdims=True)
        p = jnp.exp2(s - m)
        l = jnp.sum(p, axis=-1, keepdims=True)
        o = jnp.dot(p.astype(BF16), v_ref[:, sl], preferred_element_type=F32)
        o_ref[:, sl] = (o * (1.0 / l)).astype(o_ref.dtype)


def _ctx_attention(qkv_ctx, d, head_dim):
    bn, ctx_len, _ = qkv_ctx.shape
    spec = lambda col: pl.BlockSpec((None, ctx_len, d), lambda b: (b, 0, col))
    return pl.pallas_call(
        functools.partial(_ctx_attn_kernel, heads=NA_HEADS, head_dim=head_dim),
        grid=(bn,),
        in_specs=[spec(0), spec(1), spec(2)],
        out_specs=pl.BlockSpec((None, ctx_len, d), lambda b: (b, 0, 0)),
        out_shape=jax.ShapeDtypeStruct((bn, ctx_len, d), BF16),
        compiler_params=_compiler_params(("arbitrary",), 8 * ctx_len * d * 2 + 2 * VMEM_TEMP_BYTES),
        name="ctx_attention",
    )(qkv_ctx, qkv_ctx, qkv_ctx)


HALO = V7X_SUBLANES_F32


def _shift_rows(u, tm):
    n = u.shape[0]
    prev = pltpu.roll(u, 1, 0)[HALO:HALO + tm]
    nxt = pltpu.roll(u, n - 1, 0)[HALO:HALO + tm]
    return prev, nxt


def _conv3(u, cw, rows, seq_edges):
    prev, nxt = _shift_rows(u, rows)
    if seq_edges:
        row = lax.broadcasted_iota(jnp.int32, (rows, 1), 0)
        prev = jnp.where(row == 0, 0.0, prev)
        nxt = jnp.where(row == rows - 1, 0.0, nxt)
    return prev * cw[0:1] + u[HALO:HALO + rows] * cw[1:2] + nxt * cw[2:3]


def _gated_conv_kernel(*refs, kind, tiles_per_seq, mod_rows, sub_rows, seq_edges, side_cast):
    x_ref, xp_ref, xn_ref, g_ref, mod_ref = refs[:5]
    if kind == "ffn":
        w1_refs, cw_refs, cb_refs = refs[5:7], refs[7:9], refs[9:11]
        rest = refs[11:]
    else:
        w1_refs, cw_refs = refs[5:8], refs[8:9]
        rest = refs[9:]
    if side_cast is None:
        w2_ref, o_ref, hs_ref = rest
    else:
        w2_ref, up_f32_ref, down_f32_ref, o_ref, up_bf16_ref, down_bf16_ref, hs_ref = rest
    i = pl.program_id(0)
    j = pl.program_id(1)
    tm = x_ref.shape[0]
    r_shift, r_scale, r_gate = mod_rows

    if side_cast is not None:
        n_up, n_down, down_rows = side_cast
        step = i * pl.num_programs(1) + j

        @pl.when(step < n_up)
        def _():
            _cast_block(up_f32_ref, up_bf16_ref, 0, None)

        @pl.when((step >= n_up) & (step < n_up + n_down))
        def _():
            _cast_block(down_f32_ref, down_bf16_ref, (step - n_up) * down_f32_ref.shape[0], down_rows)

    @pl.when(j == 0)
    def _():
        g = g_ref[...]
        shift = mod_ref[r_shift:r_shift + 1, :]
        scale = mod_ref[r_scale:r_scale + 1, :]
        pos = i % tiles_per_seq
        _store_norm_mod(hs_ref, [(xp_ref, pos == 0), (x_ref, None), (xn_ref, pos == tiles_per_seq - 1)],
                        g, shift, scale, copy_ref=o_ref)

    res_gate = mod_ref[r_gate:r_gate + 1, :]

    def first_matmuls(r):
        lhs = hs_ref[r * sub_rows:r * sub_rows + sub_rows + 2 * HALO, :]
        return [jnp.dot(lhs, w[...], preferred_element_type=F32) for w in w1_refs]

    def mid_and_second_matmul(r, u):
        if kind == "ffn":
            gate, up = [_conv3(u[p], cw_refs[p][...], sub_rows, seq_edges) + cb_refs[p][...]
                        for p in range(2)]
            a = gate * jax.nn.sigmoid(gate) * up
        else:
            a = u[0][HALO:HALO + sub_rows] * _conv3(u[1] * u[2], cw_refs[0][...], sub_rows, seq_edges)
        o_ref[r * sub_rows:(r + 1) * sub_rows, :] += res_gate * jnp.dot(a.astype(BF16), w2_ref[...],
                                                                       preferred_element_type=F32)

    _skewed_subtiles(tm // sub_rows, first_matmuls, mid_and_second_matmul)


def _gated_conv(kind, x, seq_len, norm_g, mods, layer, row_fn, mod_rows, w1, cw, cb, w2, tc, tm_pref,
                sub_pref=SUB_ROWS, cast_next=None):
    m, d = x.shape
    parts = 2 if kind == "ffn" else 3
    cparts = 2 if kind == "ffn" else 1
    n_chunks = w2.shape[0] // tc
    assert w1.shape == (parts * n_chunks, d, tc) and cw.shape[1] == cparts * n_chunks * tc
    if seq_len >= tm_pref:
        tm = _tile(seq_len, tm_pref)
        sub_rows = _tile(tm, sub_pref)
        tiles_per_seq, seq_edges = seq_len // tm, False
    else:
        tm = seq_len * _tile(m // seq_len, tm_pref // seq_len)
        sub_rows = seq_len
        tiles_per_seq, seq_edges = 1, True
    assert sub_rows % V7X_SUBLANES_BF16 == 0
    hb = tm // HALO
    n_hblk = m // HALO
    other = 2 * (d * parts * tc * 2 + tc * d * 2) + (tm + 2 * HALO) * d * 2
    n_steps = (m // tm) * n_chunks
    side_cast, side_specs_in, side_specs_out, side_shapes, side_operands = None, [], [], [], []
    if cast_next is not None:
        w_up, w_down, nl = cast_next
        f, fp = w_down.shape[1], n_chunks * tc
        up_rows = min(t for t in range(V7X_SUBLANES_BF16, d + 1, V7X_SUBLANES_BF16)
                      if d % t == 0 and 2 * (d // t) <= 3 * n_steps // 4)
        n_up = 2 * (d // up_rows)
        down_rows = min(t for t in range(V7X_SUBLANES_BF16, fp + 1, V7X_SUBLANES_BF16)
                        if fp % t == 0 and fp // t <= n_steps - n_up)
        n_down = fp // down_rows
        last_down_block = (f - 1) // down_rows
        up_item = lambda i, j: jnp.minimum(i * n_chunks + j, n_up - 1)
        down_item = lambda i, j: jnp.clip(i * n_chunks + j - n_up, 0, n_down - 1)
        side_specs_in = [
            pl.BlockSpec((None, up_rows, f), lambda i, j: (nl, up_item(i, j) // 2, up_item(i, j) % 2)),
            pl.BlockSpec((None, down_rows, d),
                         lambda i, j: (nl, jnp.minimum(down_item(i, j), last_down_block), 0)),
        ]
        side_specs_out = [
            pl.BlockSpec((n_chunks, up_rows, tc), lambda i, j: (up_item(i, j) % 2, up_item(i, j) // 2, 0)),
            pl.BlockSpec((down_rows, d), lambda i, j: (down_item(i, j), 0)),
        ]
        side_shapes = [jax.ShapeDtypeStruct((2 * n_chunks, d, tc), BF16), jax.ShapeDtypeStruct((fp, d), BF16)]
        side_operands = [w_up, w_down]
        side_cast = (n_up, n_down, f)
        other += 2 * (up_rows * f * 4 + up_rows * fp * 2 + down_rows * d * 6)
    x_bufs, out_bufs = _row_tile_buffers(tm * d * 4, other)
    vmem = (x_bufs + out_bufs) * tm * d * 4 + other + VMEM_TEMP_BYTES
    col_spec = lambda rows, p: pl.BlockSpec((rows, tc), lambda i, j: (0, p * n_chunks + j))
    in_specs = [
        pl.BlockSpec((tm, d), lambda i, j: (i, 0), pipeline_mode=pl.Buffered(x_bufs)),
        pl.BlockSpec((HALO, d), lambda i, j: (jnp.maximum(i * hb - 1, 0), 0)),
        pl.BlockSpec((HALO, d), lambda i, j: (jnp.minimum((i + 1) * hb, n_hblk - 1), 0)),
        pl.BlockSpec((1, d), lambda i, j: (0, 0)),
        _mod_spec(layer, lambda i: row_fn(i, tm), d),
    ]
    in_specs += [pl.BlockSpec((None, d, tc), functools.partial(lambda p, i, j: (p * n_chunks + j, 0, 0), p))
                 for p in range(parts)]
    in_specs += [col_spec(3, p) for p in range(cparts)]
    operands = [x, x, x, norm_g, mods] + [w1] * parts + [cw] * cparts
    if kind == "ffn":
        in_specs += [col_spec(1, p) for p in range(cparts)]
        operands += [cb] * cparts
    in_specs.append(pl.BlockSpec((tc, d), lambda i, j: (j, 0)))
    operands.append(w2)
    out_spec = pl.BlockSpec((tm, d), lambda i, j: (i, 0), pipeline_mode=pl.Buffered(out_bufs))
    out_shape = jax.ShapeDtypeStruct((m, d), F32)
    result = pl.pallas_call(
        functools.partial(_gated_conv_kernel, kind=kind, tiles_per_seq=tiles_per_seq, mod_rows=mod_rows,
                          sub_rows=sub_rows, seq_edges=seq_edges, side_cast=side_cast),
        grid=(m // tm, n_chunks),
        in_specs=in_specs + side_specs_in,
        out_specs=[out_spec] + side_specs_out if side_cast else out_spec,
        out_shape=[out_shape] + side_shapes if side_cast else out_shape,
        scratch_shapes=[pltpu.VMEM((tm + 2 * HALO, d), BF16)],
        compiler_params=_compiler_params(("arbitrary" if side_cast else "parallel", "arbitrary"), vmem),
        name=kind,
    )(*operands, *side_operands)
    return tuple(result) if side_cast else result


def _gelu_tanh(x):
    return jax.nn.gelu(x, approximate=True)


def _gmlp_kernel(x_ref, g_ref, mod_ref, win_ref, vg_ref, ws_ref, bs_ref, wout_ref, o_ref,
                 hs_ref, v_ref, ssq_ref, *, n_chunks, group_dim, width, sub_rows):
    j = pl.program_id(1)
    tm = x_ref.shape[0]
    tc = win_ref.shape[1]
    n_sub = tm // sub_rows

    @pl.when(j == 0)
    def _():
        _store_norm_mod(hs_ref, [(x_ref, None)], g_ref[...], mod_ref[0:1, :], mod_ref[1:2, :], copy_ref=o_ref)
        ssq_ref[...] = jnp.zeros_like(ssq_ref)

    def in_matmul(r):
        return jnp.dot(hs_ref[r * sub_rows:(r + 1) * sub_rows, :], win_ref[...], preferred_element_type=F32)

    @pl.when(j < n_chunks)
    def _():
        def keep_v(r, y):
            rows = slice(r * sub_rows, (r + 1) * sub_rows)
            t = _gelu_tanh(y)
            v_ref[j, rows, :] = t.astype(v_ref.dtype)
            ssq_ref[rows, :] += jnp.sum(t * t, axis=-1, keepdims=True)

        _skewed_subtiles(n_sub, in_matmul, keep_v)

    @pl.when(j >= n_chunks)
    def _():
        jj = j - n_chunks
        vg = vg_ref[...]
        res_gate = mod_ref[2:3, :]
        bs = bs_ref[...]
        gpc = tc // group_dim
        ws = [ws_ref[jj * gpc + gi] for gi in range(gpc)]

        def gate_and_project(r, y):
            rows = slice(r * sub_rows, (r + 1) * sub_rows)
            inv = lax.rsqrt(ssq_ref[rows, :] * (1.0 / width) + EPS)
            vn = (v_ref[jj, rows, :].astype(F32) * inv * vg).astype(BF16)
            cols = []
            for gi in range(gpc):
                csl = slice(gi * group_dim, (gi + 1) * group_dim)
                cols.append(jnp.concatenate(
                    [jnp.dot(ws[gi], vn[c * GM_CHUNK:(c + 1) * GM_CHUNK, csl], preferred_element_type=F32)
                     + bs[:, csl] for c in range(sub_rows // GM_CHUNK)], axis=0))
            sv = jnp.concatenate(cols, axis=1)
            o_ref[rows, :] += res_gate * jnp.dot((_gelu_tanh(y) * sv).astype(BF16), wout_ref[...],
                                                 preferred_element_type=F32)

        _skewed_subtiles(n_sub, in_matmul, gate_and_project)


def _gmlp(x, norm_g, mods, layer, row_fn, w_in, v_g, w_s, b_s_cols, w_out, tm_pref):
    m, d = x.shape
    width = w_out.shape[0]
    group_dim = width // GM_GROUPS
    tc = w_in.shape[2]
    n_chunks = width // tc
    assert w_in.shape[0] == 2 * n_chunks and tc % group_dim == 0
    tm = _tile(m, tm_pref)
    sub_rows = _tile(tm, SUB_ROWS)
    assert sub_rows % GM_CHUNK == 0
    other = (2 * (d * tc * 2 + tc * d * 2 + GM_CHUNK * tc * 4) + tm * d * 2 + tm * width * 2
             + GM_GROUPS * GM_CHUNK * GM_CHUNK * 2 + tm * V7X_LANES * 4)
    x_bufs, out_bufs = _row_tile_buffers(tm * d * 4, other)
    vmem = (x_bufs + out_bufs) * tm * d * 4 + other + VMEM_TEMP_BYTES
    return pl.pallas_call(
        functools.partial(_gmlp_kernel, n_chunks=n_chunks, group_dim=group_dim, width=width,
                          sub_rows=sub_rows),
        grid=(m // tm, 2 * n_chunks),
        in_specs=[
            pl.BlockSpec((tm, d), lambda i, j: (i, 0), pipeline_mode=pl.Buffered(x_bufs)),
            pl.BlockSpec((1, d), lambda i, j: (0, 0)),
            _mod_spec(layer, lambda i: row_fn(i, tm), d),
            pl.BlockSpec((None, d, tc),
                         lambda i, j: (jnp.where(j < n_chunks, j + n_chunks, j - n_chunks), 0, 0)),
            pl.BlockSpec((1, tc), lambda i, j: (0, jnp.maximum(j - n_chunks, 0))),
            pl.BlockSpec((GM_GROUPS, GM_CHUNK, GM_CHUNK), lambda i, j: (0, 0, 0)),
            pl.BlockSpec((GM_CHUNK, tc), lambda i, j: (0, jnp.maximum(j - n_chunks, 0))),
            pl.BlockSpec((tc, d), lambda i, j: (jnp.maximum(j - n_chunks, 0), 0)),
        ],
        out_specs=pl.BlockSpec((tm, d), lambda i, j: (i, 0), pipeline_mode=pl.Buffered(out_bufs)),
        out_shape=jax.ShapeDtypeStruct((m, d), F32),
        scratch_shapes=[pltpu.VMEM((tm, d), BF16), pltpu.VMEM((n_chunks, tm, tc), BF16), pltpu.VMEM((tm, 1), F32)],
        compiler_params=_compiler_params(("parallel", "arbitrary"), vmem),
        name="gmlp",
    )(x, norm_g, mods, w_in, v_g, w_s, b_s_cols, w_out)


CAST_BLOCK_BYTES = 6 << 20


def _cast_kernel(x_ref, o_ref, *, valid_rows):
    _cast_block(x_ref, o_ref, pl.program_id(0) * x_ref.shape[0], valid_rows)


def _cast_block(x_ref, o_ref, row0, valid_rows):
    rows, f = x_ref.shape
    x = x_ref[...]
    if valid_rows is not None:
        row = row0 + lax.broadcasted_iota(jnp.int32, (rows, 1), 0)
        x = jnp.where(row < valid_rows, x, 0.0)
    x = x.astype(o_ref.dtype)
    if len(o_ref.shape) == 2:
        o_ref[...] = x
        return
    n_chunks, _, tc = o_ref.shape
    for c in range(n_chunks):
        width = max(0, min(tc, f - c * tc))
        if width:
            o_ref[c, :, :width] = x[:, c * tc:c * tc + width]
        if width < tc:
            o_ref[c, :, width:] = jnp.zeros((rows, tc - width), o_ref.dtype)


def _cast_weight(w, idx, parts=1, fp=None, col_chunk=None, rows_out=None, row_block=None):
    _, r, c = w.shape
    f = c // parts
    fp = f if fp is None else fp
    rows_out = r if rows_out is None else rows_out
    assert parts == 1 or f % V7X_LANES == 0
    assert col_chunk is not None or (parts == 1 and fp == f)
    padded_rows = rows_out > r
    if row_block is None:
        row_pref = max(V7X_SUBLANES_BF16, CAST_BLOCK_BYTES // (4 * f))
        row_block = max(t for t in range(V7X_SUBLANES_BF16, row_pref + 1, V7X_SUBLANES_BF16)
                        if rows_out % t == 0)
    assert rows_out % row_block == 0 and row_block % V7X_SUBLANES_BF16 == 0
    assert r % row_block == 0 or (padded_rows and rows_out - row_block < r)
    if col_chunk is None:
        out_spec = pl.BlockSpec((row_block, f), lambda i, p: (i, 0))
        out_shape = (rows_out, f)
    else:
        assert fp % col_chunk == 0 and col_chunk % V7X_LANES == 0 and not padded_rows
        n_chunks = fp // col_chunk
        out_spec = pl.BlockSpec((n_chunks, row_block, col_chunk), lambda i, p: (p, i, 0))
        out_shape = (parts * n_chunks, r, col_chunk)
    return pl.pallas_call(
        functools.partial(_cast_kernel, valid_rows=r if padded_rows else None),
        grid=(rows_out // row_block, parts),
        in_specs=[pl.BlockSpec((None, row_block, f), lambda i, p: (idx, i, p))],
        out_specs=out_spec,
        out_shape=jax.ShapeDtypeStruct(out_shape, BF16),
        compiler_params=_compiler_params(("parallel", "arbitrary"),
                                         2 * row_block * (4 * f + 2 * fp) + VMEM_TEMP_BYTES),
        name="cast_weight",
    )(w)


def _pad_halves(a, f, fp):
    pad = lambda h: jnp.pad(h, ((0, 0), (0, fp - f)))
    return jnp.concatenate([pad(a[:, :f]), pad(a[:, f:])], axis=1)


def _prep_ffn(w_up, conv_w, conv_b, w_down, layer, tc, cast_weights):
    f = w_down.shape[1]
    fp = -(-f // tc) * tc
    cw = _pad_halves(conv_w[layer], f, fp)
    cb = _pad_halves(conv_b[layer][None, :], f, fp)
    if cast_weights is None:
        cast_weights = (_cast_weight(w_up, layer, parts=2, fp=fp, col_chunk=tc),
                        _cast_weight(w_down, layer, rows_out=fp))
    return cast_weights[0], cw, cb, cast_weights[1]


def kernel(x, c, ctx, c_ctx, norm_mix_g, norm_ffn_g, w_ada, b_ada, na_w_qkv, na_q_g, na_k_g, na_rpb, na_w_o,
           gm_w_in, gm_v_g, gm_w_s, gm_b_s, gm_w_out, sc_w_in, sc_conv_w, sc_w_out,
           ffn_w_up, ffn_conv_w, ffn_conv_b, ffn_w_down):
    bn, seq, d = x.shape
    ctx_len = ctx.shape[1]
    depth = w_ada.shape[0]
    head_dim = d // NA_HEADS
    rows = seq // GRID_W
    assert bn + 1 <= COND_ROWS and seq % GRID_W == 0 and ctx_len % GM_CHUNK == 0

    cond = jnp.concatenate([c, c_ctx[None, :], jnp.zeros((COND_ROWS - bn - 1, d), F32)], axis=0)
    mods = _adaln(cond, w_ada, b_ada).reshape(depth, COND_ROWS, N_MOD, d)

    lat_row = lambda i, tm: i // (seq // tm)
    ctx_row = lambda i, tm: bn

    ffn_hidden = ffn_w_down.shape[1]
    ffn_tc = HIDDEN_CHUNK if ffn_hidden >= HIDDEN_CHUNK else V7X_LANES
    sc_tc = _tile(d, HIDDEN_CHUNK)
    gm_width = gm_w_out.shape[1]
    gm_tc = max(gm_width // GM_GROUPS, _tile(gm_width, HIDDEN_CHUNK))
    proj_tn = _tile(d, PROJ_COLS)
    qkv_tn = _tile(d, QKV_COLS)
    lat_tm = ROW_TILE
    ctx_tm = min(ROW_TILE, bn * ctx_len)
    lat_sub = HALO_SUB_ROWS

    xl = x.reshape(bn * seq, d)
    xc = ctx.reshape(bn * ctx_len, d)
    ffn_weights = None
    for i in range(depth):
        last = i == depth - 1
        mixer, jx = i % N_MIXERS, i // N_MIXERS
        g_mix = norm_mix_g[i][None, :]
        g_ffn = norm_ffn_g[i][None, :]
        if mixer == 0:
            w_qkv = _cast_weight(na_w_qkv, jx, col_chunk=qkv_tn)
            w_o = _cast_weight(na_w_o, jx, col_chunk=proj_tn)
            q_gain = na_q_g[jx] * (head_dim ** -0.5 * LOG2_E)
            head_gain = jnp.concatenate([jnp.tile(q_gain, NA_HEADS), jnp.tile(na_k_g[jx], NA_HEADS),
                                         jnp.ones((d,), F32)])[None, :]
            qkv = _qkv(xl, g_mix, mods, i, lat_row, w_qkv, head_gain, head_dim)
            qkv_c = _qkv(xc, g_mix, mods, i, ctx_row, w_qkv, head_gain, head_dim).reshape(bn, ctx_len, 3 * d)
            att = _na_attention(qkv, qkv_c, na_rpb[jx], bn, rows, d, head_dim)
            xl_new = _proj_res(att, w_o, xl, mods, i, lat_row, 2)
            if not last:
                att_c = _ctx_attention(qkv_c, d, head_dim).reshape(bn * ctx_len, d)
                xc = _proj_res(att_c, w_o, xc, mods, i, ctx_row, 2)
            xl = xl_new
        elif mixer == 1:
            w_in = _cast_weight(gm_w_in, jx, col_chunk=gm_tc)
            w_out = _cast_weight(gm_w_out, jx)
            w_s = gm_w_s[jx].astype(BF16)
            width = w_out.shape[0]
            v_g = gm_v_g[jx][None, :]
            b_s_cols = jnp.repeat(gm_b_s[jx].T, width // GM_GROUPS, axis=1)
            xl = _gmlp(xl, g_mix, mods, i, lat_row, w_in, v_g, w_s, b_s_cols, w_out, lat_tm)
            if not last:
                xc = _gmlp(xc, g_mix, mods, i, ctx_row, w_in, v_g, w_s, b_s_cols, w_out, ctx_tm)
        else:
            w1 = _cast_weight(sc_w_in, jx, parts=3, col_chunk=sc_tc)
            cw, w2 = sc_conv_w[jx], _cast_weight(sc_w_out, jx)
            xl = _gated_conv("sc", xl, seq, g_mix, mods, i, lat_row, (0, 1, 2), w1, cw, None, w2, sc_tc, lat_tm,
                             sub_pref=lat_sub)
            if not last:
                xc = _gated_conv("sc", xc, ctx_len, g_mix, mods, i, ctx_row, (0, 1, 2), w1, cw, None, w2, sc_tc,
                                 ctx_tm)
        w1, cw, cb, w2 = _prep_ffn(ffn_w_up, ffn_conv_w, ffn_conv_b, ffn_w_down, i, ffn_tc, ffn_weights)
        if last:
            xl = _gated_conv("ffn", xl, seq, g_ffn, mods, i, lat_row, (3, 4, 5), w1, cw, cb, w2, ffn_tc, lat_tm,
                             sub_pref=lat_sub)
        else:
            xl, *ffn_weights = _gated_conv("ffn", xl, seq, g_ffn, mods, i, lat_row, (3, 4, 5), w1, cw, cb, w2,
                                           ffn_tc, lat_tm, sub_pref=lat_sub,
                                           cast_next=(ffn_w_up, ffn_w_down, i + 1))
        if not last:
            xc = _gated_conv("ffn", xc, ctx_len, g_ffn, mods, i, ctx_row, (3, 4, 5), w1, cw, cb, w2, ffn_tc,
                             ctx_tm)
    return xl.reshape(bn, seq, d)
```

```python
import functools
import math

import numpy as np
import jax
import jax.numpy as jnp
from jax import lax
from jax.experimental import pallas as pl
from jax.experimental.pallas import tpu as pltpu

GRID_W = 64
NA_HEADS = 16
WIN_H = 8
WIN_W = 16
GM_GROUPS = 16
GM_CHUNK = 128
N_MOD = 6
N_MIXERS = 3
EPS = 1e-6

V7X_VMEM_BYTES = 64 * 1024 * 1024
V7X_SUBLANES_F32 = 8
V7X_SUBLANES_BF16 = 16
V7X_LANES = 128
VMEM_LIMIT_BYTES = V7X_VMEM_BYTES - (4 << 20)
VMEM_TEMP_BYTES = 6 << 20

F32 = jnp.float32
BF16 = jnp.bfloat16

ROW_TILE = 1024
SUB_ROWS = 256
HALO_SUB_ROWS = 512
HIDDEN_CHUNK = 512
PROJ_COLS = 1024
QKV_COLS = 2048
ADALN_COLS = 1024
NA_QROWS = 2
NA_STEP_BLOCKS = 2
COND_ROWS = V7X_SUBLANES_F32


def _compiler_params(semantics, vmem_bytes):
    return pltpu.CompilerParams(dimension_semantics=semantics,
                                vmem_limit_bytes=int(min(vmem_bytes, VMEM_LIMIT_BYTES)))


def _tile(n, pref):
    t = min(n, pref)
    while n % t:
        t -= 1
    return t


def _row_tile_buffers(tile_bytes, other_bytes):
    for x_bufs, out_bufs in ((2, 2), (1, 2), (1, 1)):
        if (x_bufs + out_bufs) * tile_bytes + other_bytes + VMEM_TEMP_BYTES <= VMEM_LIMIT_BYTES:
            break
    return x_bufs, out_bufs


def _norm_mod(x, gain, shift):
    ms = jnp.mean(x * x, axis=-1, keepdims=True)
    return x * lax.rsqrt(ms + EPS) * gain + shift


NORM_CHUNK = 16


def _store_norm_mod(hs_ref, sources, gain, shift, copy_ref=None, row_range=None):
    bounds = np.cumsum([0] + [ref.shape[0] for ref, _ in sources])
    total = int(bounds[-1])
    assert hs_ref.shape[0] == total
    r_lo, r_hi = (0, total) if row_range is None else row_range
    assert r_lo % NORM_CHUNK == 0
    for c0 in range(r_lo, r_hi, NORM_CHUNK):
        c1 = min(c0 + NORM_CHUNK, r_hi)
        pieces = []
        for (ref, zero_pred), b0, b1 in zip(sources, bounds[:-1], bounds[1:]):
            lo, hi = max(c0, int(b0)), min(c1, int(b1))
            if lo >= hi:
                continue
            rows = ref[lo - int(b0):hi - int(b0), :]
            if copy_ref is not None and zero_pred is None:
                copy_ref[lo - int(b0):hi - int(b0), :] = rows
            h = _norm_mod(rows, gain, shift)
            pieces.append(h if zero_pred is None else jnp.where(zero_pred, 0.0, h))
        h = pieces[0] if len(pieces) == 1 else jnp.concatenate(pieces, axis=0)
        hs_ref[c0:c1, :] = h.astype(BF16)


def _adaln_kernel(c_ref, w_ref, b_ref, o_ref):
    c = c_ref[...]
    a = (c * jax.nn.sigmoid(c)).astype(BF16)
    o_ref[...] = jnp.dot(a, w_ref[...].astype(BF16), preferred_element_type=F32) + b_ref[...]


def _adaln(cond, w_ada, b_ada):
    depth, d, n = w_ada.shape
    tn = _tile(n, ADALN_COLS)
    return pl.pallas_call(
        _adaln_kernel,
        grid=(depth, n // tn),
        in_specs=[
            pl.BlockSpec((COND_ROWS, d), lambda l, j: (0, 0)),
            pl.BlockSpec((None, d, tn), lambda l, j: (l, 0, j)),
            pl.BlockSpec((None, 1, tn), lambda l, j: (l, 0, j)),
        ],
        out_specs=pl.BlockSpec((None, COND_ROWS, tn), lambda l, j: (l, 0, j)),
        out_shape=jax.ShapeDtypeStruct((depth, COND_ROWS, n), F32),
        compiler_params=_compiler_params(("arbitrary", "arbitrary"), 3 * d * tn * 4 + VMEM_TEMP_BYTES),
        name="adaln",
    )(cond, w_ada, b_ada.reshape(depth, 1, n))


def _mod_spec(layer, row_fn, width, col_fn=None):
    if col_fn is None:
        return pl.BlockSpec((None, None, N_MOD, width), lambda i, j: (layer, row_fn(i), 0, 0))
    return pl.BlockSpec((None, None, N_MOD, width), lambda i, j: (layer, row_fn(i), 0, col_fn(j)))


def _skewed_subtiles(n_sub, first, second, ahead=1):
    pending = [first(r) for r in range(min(ahead, n_sub))]
    for r in range(n_sub):
        if r + ahead < n_sub:
            pending.append(first(r + ahead))
        second(r, pending.pop(0))


def _proj_res_kernel(a_ref, w_ref, x_ref, mod_ref, o_ref, *, gate_row, sub_rows):
    gate = mod_ref[gate_row:gate_row + 1, :]

    def matmul(r):
        return jnp.dot(a_ref[r * sub_rows:(r + 1) * sub_rows, :], w_ref[...], preferred_element_type=F32)

    def residual(r, y):
        rows = slice(r * sub_rows, (r + 1) * sub_rows)
        o_ref[rows, :] = x_ref[rows, :] + gate * y

    _skewed_subtiles(o_ref.shape[0] // sub_rows, matmul, residual)


def _proj_res(a, w, x, mods, layer, row_fn, gate_row):
    m, k = a.shape
    n_tiles, _, tn = w.shape
    n = n_tiles * tn
    tm = _tile(m, ROW_TILE)
    vmem = 2 * (tm * k * 2 + k * tn * 2 + 2 * tm * tn * 4) + VMEM_TEMP_BYTES
    return pl.pallas_call(
        functools.partial(_proj_res_kernel, gate_row=gate_row, sub_rows=_tile(tm, SUB_ROWS)),
        grid=(m // tm, n // tn),
        in_specs=[
            pl.BlockSpec((tm, k), lambda i, j: (i, 0)),
            pl.BlockSpec((None, k, tn), lambda i, j: (j, 0, 0)),
            pl.BlockSpec((tm, tn), lambda i, j: (i, j)),
            _mod_spec(layer, lambda i: row_fn(i, tm), tn, lambda j: j),
        ],
        out_specs=pl.BlockSpec((tm, tn), lambda i, j: (i, j)),
        out_shape=jax.ShapeDtypeStruct((m, n), F32),
        compiler_params=_compiler_params(("parallel", "arbitrary"), vmem),
        name="proj_res",
    )(a, w, x, mods)


def _qkv_kernel(x_ref, g_ref, mod_ref, w_ref, hg_ref, o_ref, hs_ref, *, n_norm_tiles, head_dim, sub_rows):
    j = pl.program_id(1)
    tm, tn = o_ref.shape
    is_norm_tile = j < n_norm_tiles
    hg = hg_ref[...]

    def matmul(r):
        return jnp.dot(hs_ref[r * sub_rows:(r + 1) * sub_rows, :], w_ref[...], preferred_element_type=F32)

    def head_norm(r, y):
        rows = slice(r * sub_rows, (r + 1) * sub_rows)
        for hh in range(tn // head_dim):
            sl = slice(hh * head_dim, (hh + 1) * head_dim)
            t = y[:, sl]
            inv = lax.rsqrt(jnp.mean(t * t, axis=-1, keepdims=True) + EPS)
            o_ref[rows, sl] = (t * jnp.where(is_norm_tile, inv, 1.0) * hg[:, sl]).astype(o_ref.dtype)

    @pl.when(j == 0)
    def _():
        gain = g_ref[...] * (1.0 + mod_ref[1:2, :])

        def norm_and_matmul(r):
            _store_norm_mod(hs_ref, [(x_ref, None)], gain, mod_ref[0:1, :],
                            row_range=(r * sub_rows, (r + 1) * sub_rows))
            return matmul(r)

        _skewed_subtiles(tm // sub_rows, norm_and_matmul, head_norm)

    @pl.when(j > 0)
    def _():
        _skewed_subtiles(tm // sub_rows, matmul, head_norm)


def _qkv(x, norm_g, mods, layer, row_fn, w_qkv, head_gain, head_dim):
    m, d = x.shape
    n_tiles, _, tn = w_qkv.shape
    n = n_tiles * tn
    tm = _tile(m, ROW_TILE)
    assert (n // 3) % tn == 0 and tn % head_dim == 0
    vmem = 2 * (tm * d * 4 + d * tn * 2 + tm * tn * 2) + tm * d * 2 + 2 * VMEM_TEMP_BYTES
    return pl.pallas_call(
        functools.partial(_qkv_kernel, n_norm_tiles=2 * (n // 3) // tn, head_dim=head_dim,
                          sub_rows=_tile(tm, SUB_ROWS)),
        grid=(m // tm, n // tn),
        in_specs=[
            pl.BlockSpec((tm, d), lambda i, j: (i, 0)),
            pl.BlockSpec((1, d), lambda i, j: (0, 0)),
            _mod_spec(layer, lambda i: row_fn(i, tm), d),
            pl.BlockSpec((None, d, tn), lambda i, j: (j, 0, 0)),
            pl.BlockSpec((1, tn), lambda i, j: (0, j)),
        ],
        out_specs=pl.BlockSpec((tm, tn), lambda i, j: (i, (j + 2 * (n_tiles // 3)) % n_tiles)),
        out_shape=jax.ShapeDtypeStruct((m, n), BF16),
        scratch_shapes=[pltpu.VMEM((tm, d), BF16)],
        compiler_params=_compiler_params(("parallel", "arbitrary"), vmem),
        name="qkv",
    )(x, norm_g, mods, w_qkv, head_gain)


LOG2_E = math.log2(math.e)
QKV_KV_BLOCK = 0
QKV_Q_BLOCK = 2
_NT_DIMS = (((1,), (1,)), ((), ()))


def _na_kernel(*refs, nb, heads, head_dim, step_blocks):
    refs = refs[2:]
    q_ref = refs[0]
    kv_refs = [refs[1 + b * nb:1 + (b + 1) * nb] for b in range(step_blocks)]
    rest = refs[1 + step_blocks * nb:]
    kvc_ref = rest[0]
    tbl_refs = rest[1:1 + step_blocks]
    o_ref = rest[1 + step_blocks]
    qw = q_ref.shape[0] // step_blocks
    d = q_ref.shape[1]

    def scores(u):
        blk, h = divmod(u, heads)
        sl = slice(h * head_dim, (h + 1) * head_dim)
        q = q_ref[blk * qw:(blk + 1) * qw, sl]
        k = jnp.concatenate([r[:, sl] for r in kv_refs[blk]], axis=0)
        s = lax.dot_general(q, k, _NT_DIMS, preferred_element_type=F32) + tbl_refs[blk][h]
        sc = lax.dot_general(q, kvc_ref[:, sl], _NT_DIMS, preferred_element_type=F32)
        return s, sc

    ones_lat = jnp.ones((nb * kv_refs[0][0].shape[0], head_dim), BF16)
    ones_ctx = jnp.ones((kvc_ref.shape[0], head_dim), BF16)

    def attend(u, s_sc):
        blk, h = divmod(u, heads)
        s, sc = s_sc
        sl = slice(h * head_dim, (h + 1) * head_dim)
        vsl = slice(d + h * head_dim, d + (h + 1) * head_dim)
        v = jnp.concatenate([jnp.concatenate([r[:, vsl] for r in kv_refs[blk]], axis=0), ones_lat], axis=1)
        vc = jnp.concatenate([kvc_ref[:, vsl], ones_ctx], axis=1)
        m = jnp.maximum(jnp.max(s, axis=-1, keepdims=True), jnp.max(sc, axis=-1, keepdims=True))
        p = jnp.exp2(s - m).astype(BF16)
        pc = jnp.exp2(sc - m).astype(BF16)
        o = (jnp.dot(p, v, preferred_element_type=F32) + jnp.dot(pc, vc, preferred_element_type=F32))
        o_ref[blk * qw:(blk + 1) * qw, sl] = (
            o[:, :head_dim] * (1.0 / o[:, head_dim:head_dim + 1])).astype(o_ref.dtype)

    _skewed_subtiles(step_blocks * heads, scores, attend, ahead=2)


def _na_geometry(rows, qrows):
    kh = min(WIN_H, rows)
    nb = min(kh + qrows - 1, rows)
    n_blk = rows // qrows
    band0s, keys, types, row_rel = [], {}, [], []
    for blk in range(n_blk):
        r0 = blk * qrows
        q_row = r0 + np.arange(qrows)
        r_start = np.clip(q_row - kh // 2, 0, rows - kh)
        band0 = min(int(np.clip(r0 - kh // 2, 0, rows - kh)), rows - nb)
        k_row = band0 + np.arange(nb)
        row_ok = (k_row[None, :] >= r_start[:, None]) & (k_row[None, :] < r_start[:, None] + kh)
        rel = np.where(row_ok, k_row[None, :] - q_row[:, None] + WIN_H - 1, -1)
        key = rel.tobytes()
        if key not in keys:
            keys[key] = len(row_rel)
            row_rel.append(rel)
        types.append(keys[key])
        band0s.append(band0)
    return nb, n_blk, np.array(band0s, np.int32), np.array(types, np.int32), row_rel


def _na_bias_table(rpb, row_rel):
    layers, heads, n_dr, n_dc = rpb.shape
    period = 2 * GRID_W
    assert n_dc <= period
    r = jnp.pad(rpb, ((0, 0), (0, 0), (0, 0), (0, period - n_dc)))
    skew = jnp.tile(r, (1, 1, 1, GRID_W))[..., :GRID_W * (period - 1)]
    skew = skew.reshape(layers, heads, n_dr, GRID_W, period - 1)
    toep = skew[..., WIN_W - 1:WIN_W - 1 + GRID_W]
    q_col = np.arange(GRID_W)
    c_start = np.clip(q_col - WIN_W // 2, 0, GRID_W - WIN_W)
    col_ok = (q_col[None, :] >= c_start[:, None]) & (q_col[None, :] < c_start[:, None] + WIN_W)
    toep = jnp.where(col_ok, toep, -jnp.inf)
    masked = jnp.full((layers, heads, GRID_W, GRID_W), -jnp.inf, F32)
    tables = []
    for rel in row_rel:
        q_blocks = []
        for qr in range(rel.shape[0]):
            q_blocks.append(jnp.concatenate(
                [toep[:, :, rel[qr, kr]] if rel[qr, kr] >= 0 else masked for kr in range(rel.shape[1])], axis=-1))
        tables.append(jnp.concatenate(q_blocks, axis=-2))
    return jnp.stack(tables, axis=1).astype(F32)


def _na_attention(qkv, qkv_ctx, tables, layer, geometry, bn, rows, d, head_dim):
    nb, n_blk, band0s, types, _ = geometry
    qw, nk = tables.shape[-2:]
    ctx_len = qkv_ctx.shape[1]
    band_tbl = jnp.asarray(band0s)
    type_tbl = jnp.asarray(types)
    kv_view = qkv.reshape(bn * rows, GRID_W, 3 * d)

    spb = _tile(n_blk, NA_STEP_BLOCKS)
    n_steps = n_blk // spb

    def kv_spec(blk, t):
        return pl.BlockSpec((None, GRID_W, 2 * d),
                            lambda b, i, band, typ: (b * rows + band[i * spb + blk] + t, 0, QKV_KV_BLOCK))

    def tbl_spec(blk):
        return pl.BlockSpec((None, None, NA_HEADS, qw, nk),
                            lambda b, i, band, typ: (layer, typ[i * spb + blk], 0, 0, 0))

    in_specs = ([pl.BlockSpec((spb * qw, d), lambda b, i, band, typ: (b * n_steps + i, QKV_Q_BLOCK))]
                + [kv_spec(blk, t) for blk in range(spb) for t in range(nb)]
                + [pl.BlockSpec((None, ctx_len, 2 * d), lambda b, i, band, typ: (b, 0, QKV_KV_BLOCK))]
                + [tbl_spec(blk) for blk in range(spb)])
    vmem = (2 * (spb * (2 * qw * d * 2 + 2 * nk * d * 2 + NA_HEADS * qw * nk * 4) + 2 * ctx_len * d * 2)
            + 2 * VMEM_TEMP_BYTES)
    grid_spec = pltpu.PrefetchScalarGridSpec(
        num_scalar_prefetch=2,
        grid=(bn, n_steps),
        in_specs=in_specs,
        out_specs=pl.BlockSpec((spb * qw, d), lambda b, i, band, typ: (b * n_steps + i, 0)),
    )
    return pl.pallas_call(
        functools.partial(_na_kernel, nb=nb, heads=NA_HEADS, head_dim=head_dim, step_blocks=spb),
        grid_spec=grid_spec,
        out_shape=jax.ShapeDtypeStruct((bn * rows * GRID_W, d), BF16),
        compiler_params=_compiler_params(("parallel", "arbitrary"), vmem),
        name="na_attention",
    )(band_tbl, type_tbl, qkv, *([kv_view] * (spb * nb)), qkv_ctx, *([tables] * spb))


def _ctx_attn_kernel(q_ref, k_ref, v_ref, o_ref, *, heads, head_dim):
    for h in range(heads):
        sl = slice(h * head_dim, (h + 1) * head_dim)
        s = lax.dot_general(q_ref[:, sl], k_ref[:, sl], _NT_DIMS, preferred_element_type=F32)
        m = jnp.max(s, axis=-1, keepdims=True)
        p = jnp.exp2(s - m)
        l = jnp.sum(p, axis=-1, keepdims=True)
        o = jnp.dot(p.astype(BF16), v_ref[:, sl], preferred_element_type=F32)
        o_ref[:, sl] = (o * (1.0 / l)).astype(o_ref.dtype)


def _ctx_attention(qkv_ctx, d, head_dim):
    bn, ctx_len, _ = qkv_ctx.shape
    spec = lambda col: pl.BlockSpec((None, ctx_len, d), lambda b: (b, 0, col))
    return pl.pallas_call(
        functools.partial(_ctx_attn_kernel, heads=NA_HEADS, head_dim=head_dim),
        grid=(bn,),
        in_specs=[spec(QKV_Q_BLOCK), spec(2 * QKV_KV_BLOCK), spec(2 * QKV_KV_BLOCK + 1)],
        out_specs=pl.BlockSpec((None, ctx_len, d), lambda b: (b, 0, 0)),
        out_shape=jax.ShapeDtypeStruct((bn, ctx_len, d), BF16),
        compiler_params=_compiler_params(("arbitrary",), 8 * ctx_len * d * 2 + 2 * VMEM_TEMP_BYTES),
        name="ctx_attention",
    )(qkv_ctx, qkv_ctx, qkv_ctx)


HALO = V7X_SUBLANES_F32


def _shift_rows(u, tm):
    n = u.shape[0]
    prev = pltpu.roll(u, 1, 0)[HALO:HALO + tm]
    nxt = pltpu.roll(u, n - 1, 0)[HALO:HALO + tm]
    return prev, nxt


def _conv3(u, cw, rows, seq_edges):
    prev, nxt = _shift_rows(u, rows)
    if seq_edges:
        row = lax.broadcasted_iota(jnp.int32, (rows, 1), 0)
        prev = jnp.where(row == 0, 0.0, prev)
        nxt = jnp.where(row == rows - 1, 0.0, nxt)
    return prev * cw[0:1] + u[HALO:HALO + rows] * cw[1:2] + nxt * cw[2:3]


def _gated_conv_kernel(*refs, kind, tiles_per_seq, mod_rows, sub_rows, seq_edges, side_cast):
    x_ref, xp_ref, xn_ref, g_ref, mod_ref = refs[:5]
    if kind == "ffn":
        w1_refs, cw_refs, cb_refs = refs[5:7], refs[7:9], refs[9:11]
        rest = refs[11:]
    else:
        w1_refs, cw_refs = refs[5:8], refs[8:9]
        rest = refs[9:]
    if side_cast is None:
        w2_ref, o_ref, hs_ref = rest
    else:
        w2_ref, up_f32_ref, down_f32_ref, o_ref, up_bf16_ref, down_bf16_ref, hs_ref = rest
    i = pl.program_id(0)
    j = pl.program_id(1)
    tm = x_ref.shape[0]
    r_shift, r_scale, r_gate = mod_rows

    if side_cast is not None:
        n_up, n_down, down_rows = side_cast
        step = i * pl.num_programs(1) + j

        @pl.when(step < n_up)
        def _():
            _cast_block(up_f32_ref, up_bf16_ref, 0, None)

        @pl.when((step >= n_up) & (step < n_up + n_down))
        def _():
            _cast_block(down_f32_ref, down_bf16_ref, (step - n_up) * down_f32_ref.shape[0], down_rows)

    res_gate = mod_ref[r_gate:r_gate + 1, :]
    n_sub = tm // sub_rows

    def first_matmuls(r):
        lhs = hs_ref[r * sub_rows:r * sub_rows + sub_rows + 2 * HALO, :]
        return [jnp.dot(lhs, w[...], preferred_element_type=F32) for w in w1_refs]

    def mid_and_second_matmul(r, u):
        if kind == "ffn":
            gate, up = [_conv3(u[p], cw_refs[p][...], sub_rows, seq_edges) + cb_refs[p][...]
                        for p in range(2)]
            a = gate * jax.nn.sigmoid(gate) * up
        else:
            a = u[0][HALO:HALO + sub_rows] * _conv3(u[1] * u[2], cw_refs[0][...], sub_rows, seq_edges)
        o_ref[r * sub_rows:(r + 1) * sub_rows, :] += res_gate * jnp.dot(a.astype(BF16), w2_ref[...],
                                                                       preferred_element_type=F32)

    @pl.when(j == 0)
    def _():
        gain = g_ref[...] * (1.0 + mod_ref[r_scale:r_scale + 1, :])
        shift = mod_ref[r_shift:r_shift + 1, :]
        pos = i % tiles_per_seq
        sources = [(xp_ref, pos == 0), (x_ref, None), (xn_ref, pos == tiles_per_seq - 1)]
        total = tm + 2 * HALO

        def norm_and_first_matmuls(r):
            lo = 0 if r == 0 else r * sub_rows + 2 * HALO
            hi = total if r == n_sub - 1 else (r + 1) * sub_rows + 2 * HALO
            _store_norm_mod(hs_ref, sources, gain, shift, copy_ref=o_ref, row_range=(lo, hi))
            return first_matmuls(r)

        _skewed_subtiles(n_sub, norm_and_first_matmuls, mid_and_second_matmul)

    @pl.when(j > 0)
    def _():
        _skewed_subtiles(n_sub, first_matmuls, mid_and_second_matmul)


def _gated_conv(kind, x, seq_len, norm_g, mods, layer, row_fn, mod_rows, w1, cw, cb, w2, tc, tm_pref,
                sub_pref=SUB_ROWS, cast_next=None):
    m, d = x.shape
    parts = 2 if kind == "ffn" else 3
    cparts = 2 if kind == "ffn" else 1
    n_chunks = w2.shape[0] // tc
    assert w1.shape == (parts * n_chunks, d, tc) and cw.shape[1] == cparts * n_chunks * tc
    if seq_len >= tm_pref:
        tm = _tile(seq_len, tm_pref)
        sub_rows = _tile(tm, sub_pref)
        tiles_per_seq, seq_edges = seq_len // tm, False
    else:
        tm = seq_len * _tile(m // seq_len, tm_pref // seq_len)
        sub_rows = seq_len
        tiles_per_seq, seq_edges = 1, True
    assert sub_rows % V7X_SUBLANES_BF16 == 0
    hb = tm // HALO
    n_hblk = m // HALO
    other = 2 * (d * parts * tc * 2 + tc * d * 2) + (tm + 2 * HALO) * d * 2
    n_steps = (m // tm) * n_chunks
    side_cast, side_specs_in, side_specs_out, side_shapes, side_operands = None, [], [], [], []
    if cast_next is not None:
        w_up, w_down, nl = cast_next
        f, fp = w_down.shape[1], n_chunks * tc
        up_rows = min(t for t in range(V7X_SUBLANES_BF16, d + 1, V7X_SUBLANES_BF16)
                      if d % t == 0 and 2 * (d // t) <= 3 * n_steps // 4)
        n_up = 2 * (d // up_rows)
        down_rows = min(t for t in range(V7X_SUBLANES_BF16, fp + 1, V7X_SUBLANES_BF16)
                        if fp % t == 0 and fp // t <= n_steps - n_up)
        n_down = fp // down_rows
        last_down_block = (f - 1) // down_rows
        up_item = lambda i, j: jnp.minimum(i * n_chunks + j, n_up - 1)
        down_item = lambda i, j: jnp.clip(i * n_chunks + j - n_up, 0, n_down - 1)
        side_specs_in = [
            pl.BlockSpec((None, up_rows, f), lambda i, j: (nl, up_item(i, j) // 2, up_item(i, j) % 2)),
            pl.BlockSpec((None, down_rows, d),
                         lambda i, j: (nl, jnp.minimum(down_item(i, j), last_down_block), 0)),
        ]
        side_specs_out = [
            pl.BlockSpec((n_chunks, up_rows, tc), lambda i, j: (up_item(i, j) % 2, up_item(i, j) // 2, 0)),
            pl.BlockSpec((down_rows, d), lambda i, j: (down_item(i, j), 0)),
        ]
        side_shapes = [jax.ShapeDtypeStruct((2 * n_chunks, d, tc), BF16), jax.ShapeDtypeStruct((fp, d), BF16)]
        side_operands = [w_up, w_down]
        side_cast = (n_up, n_down, f)
        other += 2 * (up_rows * f * 4 + up_rows * fp * 2 + down_rows * d * 6)
    x_bufs, out_bufs = _row_tile_buffers(tm * d * 4, other)
    vmem = (x_bufs + out_bufs) * tm * d * 4 + other + VMEM_TEMP_BYTES
    col_spec = lambda rows, p: pl.BlockSpec((rows, tc), lambda i, j: (0, p * n_chunks + j))
    in_specs = [
        pl.BlockSpec((tm, d), lambda i, j: (i, 0), pipeline_mode=pl.Buffered(x_bufs)),
        pl.BlockSpec((HALO, d), lambda i, j: (jnp.maximum(i * hb - 1, 0), 0)),
        pl.BlockSpec((HALO, d), lambda i, j: (jnp.minimum((i + 1) * hb, n_hblk - 1), 0)),
        pl.BlockSpec((1, d), lambda i, j: (0, 0)),
        _mod_spec(layer, lambda i: row_fn(i, tm), d),
    ]
    in_specs += [pl.BlockSpec((None, d, tc), functools.partial(lambda p, i, j: (p * n_chunks + j, 0, 0), p))
                 for p in range(parts)]
    in_specs += [col_spec(3, p) for p in range(cparts)]
    operands = [x, x, x, norm_g, mods] + [w1] * parts + [cw] * cparts
    if kind == "ffn":
        in_specs += [col_spec(1, p) for p in range(cparts)]
        operands += [cb] * cparts
    in_specs.append(pl.BlockSpec((tc, d), lambda i, j: (j, 0)))
    operands.append(w2)
    out_spec = pl.BlockSpec((tm, d), lambda i, j: (i, 0), pipeline_mode=pl.Buffered(out_bufs))
    out_shape = jax.ShapeDtypeStruct((m, d), F32)
    result = pl.pallas_call(
        functools.partial(_gated_conv_kernel, kind=kind, tiles_per_seq=tiles_per_seq, mod_rows=mod_rows,
                          sub_rows=sub_rows, seq_edges=seq_edges, side_cast=side_cast),
        grid=(m // tm, n_chunks),
        in_specs=in_specs + side_specs_in,
        out_specs=[out_spec] + side_specs_out if side_cast else out_spec,
        out_shape=[out_shape] + side_shapes if side_cast else out_shape,
        scratch_shapes=[pltpu.VMEM((tm + 2 * HALO, d), BF16)],
        compiler_params=_compiler_params(("arbitrary" if side_cast else "parallel", "arbitrary"), vmem),
        name=kind,
    )(*operands, *side_operands)
    return tuple(result) if side_cast else result


def _gelu_tanh(x):
    return jax.nn.gelu(x, approximate=True)


def _gmlp_kernel(x_ref, g_ref, mod_ref, win_ref, vg_ref, ws_ref, bs_ref, wout_ref, o_ref,
                 hs_ref, v_ref, ssq_ref, *, n_chunks, group_dim, width, sub_rows):
    j = pl.program_id(1)
    tm = x_ref.shape[0]
    tc = win_ref.shape[1]
    n_sub = tm // sub_rows

    def in_matmul(r):
        return jnp.dot(hs_ref[r * sub_rows:(r + 1) * sub_rows, :], win_ref[...], preferred_element_type=F32)

    def keep_v(r, y):
        rows = slice(r * sub_rows, (r + 1) * sub_rows)
        t = _gelu_tanh(y)
        v_ref[j, rows, :] = t.astype(v_ref.dtype)
        ssq_ref[rows, :] += jnp.sum(t * t, axis=-1, keepdims=True)

    @pl.when(j == 0)
    def _():
        gain = g_ref[...] * (1.0 + mod_ref[1:2, :])
        ssq_ref[...] = jnp.zeros_like(ssq_ref)

        def norm_and_matmul(r):
            _store_norm_mod(hs_ref, [(x_ref, None)], gain, mod_ref[0:1, :], copy_ref=o_ref,
                            row_range=(r * sub_rows, (r + 1) * sub_rows))
            return in_matmul(r)

        _skewed_subtiles(n_sub, norm_and_matmul, keep_v)

    @pl.when((j > 0) & (j < n_chunks))
    def _():
        _skewed_subtiles(n_sub, in_matmul, keep_v)

    @pl.when(j >= n_chunks)
    def _():
        jj = j - n_chunks
        vg = vg_ref[...]
        res_gate = mod_ref[2:3, :]
        bs = bs_ref[...]
        gpc = tc // group_dim
        ws = [ws_ref[jj * gpc + gi] for gi in range(gpc)]

        def gate_and_project(r, y):
            rows = slice(r * sub_rows, (r + 1) * sub_rows)
            inv = lax.rsqrt(ssq_ref[rows, :] * (1.0 / width) + EPS)
            vn = (v_ref[jj, rows, :].astype(F32) * inv * vg).astype(BF16)
            cols = []
            for gi in range(gpc):
                csl = slice(gi * group_dim, (gi + 1) * group_dim)
                cols.append(jnp.concatenate(
                    [jnp.dot(ws[gi], vn[c * GM_CHUNK:(c + 1) * GM_CHUNK, csl], preferred_element_type=F32)
                     + bs[:, csl] for c in range(sub_rows // GM_CHUNK)], axis=0))
            sv = jnp.concatenate(cols, axis=1)
            o_ref[rows, :] += res_gate * jnp.dot((_gelu_tanh(y) * sv).astype(BF16), wout_ref[...],
                                                 preferred_element_type=F32)

        _skewed_subtiles(n_sub, in_matmul, gate_and_project)


def _gmlp(x, norm_g, mods, layer, row_fn, w_in, v_g, w_s, b_s_cols, w_out, tm_pref):
    m, d = x.shape
    width = w_out.shape[0]
    group_dim = width // GM_GROUPS
    tc = w_in.shape[2]
    n_chunks = width // tc
    assert w_in.shape[0] == 2 * n_chunks and tc % group_dim == 0
    tm = _tile(m, tm_pref)
    sub_rows = _tile(tm, SUB_ROWS)
    assert sub_rows % GM_CHUNK == 0
    other = (2 * (d * tc * 2 + tc * d * 2 + GM_CHUNK * tc * 4) + tm * d * 2 + tm * width * 2
             + GM_GROUPS * GM_CHUNK * GM_CHUNK * 2 + tm * V7X_LANES * 4)
    x_bufs, out_bufs = _row_tile_buffers(tm * d * 4, other)
    vmem = (x_bufs + out_bufs) * tm * d * 4 + other + VMEM_TEMP_BYTES
    return pl.pallas_call(
        functools.partial(_gmlp_kernel, n_chunks=n_chunks, group_dim=group_dim, width=width,
                          sub_rows=sub_rows),
        grid=(m // tm, 2 * n_chunks),
        in_specs=[
            pl.BlockSpec((tm, d), lambda i, j: (i, 0), pipeline_mode=pl.Buffered(x_bufs)),
            pl.BlockSpec((1, d), lambda i, j: (0, 0)),
            _mod_spec(layer, lambda i: row_fn(i, tm), d),
            pl.BlockSpec((None, d, tc),
                         lambda i, j: (jnp.where(j < n_chunks, j + n_chunks, j - n_chunks), 0, 0)),
            pl.BlockSpec((1, tc), lambda i, j: (0, jnp.maximum(j - n_chunks, 0))),
            pl.BlockSpec((GM_GROUPS, GM_CHUNK, GM_CHUNK), lambda i, j: (0, 0, 0)),
            pl.BlockSpec((GM_CHUNK, tc), lambda i, j: (0, jnp.maximum(j - n_chunks, 0))),
            pl.BlockSpec((tc, d), lambda i, j: (jnp.maximum(j - n_chunks, 0), 0)),
        ],
        out_specs=pl.BlockSpec((tm, d), lambda i, j: (i, 0), pipeline_mode=pl.Buffered(out_bufs)),
        out_shape=jax.ShapeDtypeStruct((m, d), F32),
        scratch_shapes=[pltpu.VMEM((tm, d), BF16), pltpu.VMEM((n_chunks, tm, tc), BF16), pltpu.VMEM((tm, 1), F32)],
        compiler_params=_compiler_params(("parallel", "arbitrary"), vmem),
        name="gmlp",
    )(x, norm_g, mods, w_in, v_g, w_s, b_s_cols, w_out)


CAST_BLOCK_BYTES = 6 << 20


def _cast_kernel(x_ref, o_ref, *, valid_rows):
    _cast_block(x_ref, o_ref, pl.program_id(0) * x_ref.shape[0], valid_rows)


def _cast_block(x_ref, o_ref, row0, valid_rows):
    rows, f = x_ref.shape
    x = x_ref[...]
    if valid_rows is not None:
        row = row0 + lax.broadcasted_iota(jnp.int32, (rows, 1), 0)
        x = jnp.where(row < valid_rows, x, 0.0)
    x = x.astype(o_ref.dtype)
    if len(o_ref.shape) == 2:
        o_ref[...] = x
        return
    n_chunks, _, tc = o_ref.shape
    for c in range(n_chunks):
        width = max(0, min(tc, f - c * tc))
        if width:
            o_ref[c, :, :width] = x[:, c * tc:c * tc + width]
        if width < tc:
            o_ref[c, :, width:] = jnp.zeros((rows, tc - width), o_ref.dtype)


def _cast_weight(w, idx, parts=1, fp=None, col_chunk=None, rows_out=None, row_block=None):
    _, r, c = w.shape
    f = c // parts
    fp = f if fp is None else fp
    rows_out = r if rows_out is None else rows_out
    assert parts == 1 or f % V7X_LANES == 0
    assert col_chunk is not None or (parts == 1 and fp == f)
    padded_rows = rows_out > r
    if row_block is None:
        row_pref = max(V7X_SUBLANES_BF16, CAST_BLOCK_BYTES // (4 * f))
        row_block = max(t for t in range(V7X_SUBLANES_BF16, row_pref + 1, V7X_SUBLANES_BF16)
                        if rows_out % t == 0)
    assert rows_out % row_block == 0 and row_block % V7X_SUBLANES_BF16 == 0
    assert r % row_block == 0 or (padded_rows and rows_out - row_block < r)
    if col_chunk is None:
        out_spec = pl.BlockSpec((row_block, f), lambda i, p: (i, 0))
        out_shape = (rows_out, f)
    else:
        assert fp % col_chunk == 0 and col_chunk % V7X_LANES == 0 and not padded_rows
        n_chunks = fp // col_chunk
        out_spec = pl.BlockSpec((n_chunks, row_block, col_chunk), lambda i, p: (p, i, 0))
        out_shape = (parts * n_chunks, r, col_chunk)
    return pl.pallas_call(
        functools.partial(_cast_kernel, valid_rows=r if padded_rows else None),
        grid=(rows_out // row_block, parts),
        in_specs=[pl.BlockSpec((None, row_block, f), lambda i, p: (idx, i, p))],
        out_specs=out_spec,
        out_shape=jax.ShapeDtypeStruct(out_shape, BF16),
        compiler_params=_compiler_params(("parallel", "arbitrary"),
                                         2 * row_block * (4 * f + 2 * fp) + VMEM_TEMP_BYTES),
        name="cast_weight",
    )(w)


def _pad_halves(a, f, fp):
    pad = lambda h: jnp.pad(h, ((0, 0), (0, fp - f)))
    return jnp.concatenate([pad(a[:, :f]), pad(a[:, f:])], axis=1)


def _prep_ffn(w_up, conv_w, conv_b, w_down, layer, tc, cast_weights):
    f = w_down.shape[1]
    fp = -(-f // tc) * tc
    cw = _pad_halves(conv_w[layer], f, fp)
    cb = _pad_halves(conv_b[layer][None, :], f, fp)
    if cast_weights is None:
        cast_weights = (_cast_weight(w_up, layer, parts=2, fp=fp, col_chunk=tc),
                        _cast_weight(w_down, layer, rows_out=fp))
    return cast_weights[0], cw, cb, cast_weights[1]


def kernel(x, c, ctx, c_ctx, norm_mix_g, norm_ffn_g, w_ada, b_ada, na_w_qkv, na_q_g, na_k_g, na_rpb, na_w_o,
           gm_w_in, gm_v_g, gm_w_s, gm_b_s, gm_w_out, sc_w_in, sc_conv_w, sc_w_out,
           ffn_w_up, ffn_conv_w, ffn_conv_b, ffn_w_down):
    bn, seq, d = x.shape
    ctx_len = ctx.shape[1]
    depth = w_ada.shape[0]
    head_dim = d // NA_HEADS
    rows = seq // GRID_W
    assert bn + 1 <= COND_ROWS and seq % GRID_W == 0 and ctx_len % GM_CHUNK == 0

    cond = jnp.concatenate([c, c_ctx[None, :], jnp.zeros((COND_ROWS - bn - 1, d), F32)], axis=0)
    mods = _adaln(cond, w_ada, b_ada).reshape(depth, COND_ROWS, N_MOD, d)

    lat_row = lambda i, tm: i // (seq // tm)
    ctx_row = lambda i, tm: bn

    ffn_hidden = ffn_w_down.shape[1]
    ffn_tc = HIDDEN_CHUNK if ffn_hidden >= HIDDEN_CHUNK else V7X_LANES
    sc_tc = _tile(d, HIDDEN_CHUNK)
    gm_width = gm_w_out.shape[1]
    gm_tc = max(gm_width // GM_GROUPS, _tile(gm_width, HIDDEN_CHUNK))
    proj_tn = _tile(d, PROJ_COLS)
    qkv_tn = _tile(d, QKV_COLS)
    lat_tm = ROW_TILE
    ctx_tm = min(ROW_TILE, bn * ctx_len)
    lat_sub = HALO_SUB_ROWS

    xl = x.reshape(bn * seq, d)
    xc = ctx.reshape(bn * ctx_len, d)
    ffn_weights = None
    na_geometry = _na_geometry(rows, min(NA_QROWS, rows))
    na_tables = _na_bias_table(na_rpb * LOG2_E, na_geometry[-1])
    for i in range(depth):
        last = i == depth - 1
        mixer, jx = i % N_MIXERS, i // N_MIXERS
        g_mix = norm_mix_g[i][None, :]
        g_ffn = norm_ffn_g[i][None, :]
        if mixer == 0:
            w_qkv = _cast_weight(na_w_qkv, jx, col_chunk=qkv_tn)
            w_o = _cast_weight(na_w_o, jx, col_chunk=proj_tn)
            q_gain = na_q_g[jx] * (head_dim ** -0.5 * LOG2_E)
            head_gain = jnp.concatenate([jnp.tile(q_gain, NA_HEADS), jnp.tile(na_k_g[jx], NA_HEADS),
                                         jnp.ones((d,), F32)])[None, :]
            qkv = _qkv(xl, g_mix, mods, i, lat_row, w_qkv, head_gain, head_dim)
            qkv_c = _qkv(xc, g_mix, mods, i, ctx_row, w_qkv, head_gain, head_dim).reshape(bn, ctx_len, 3 * d)
            att = _na_attention(qkv, qkv_c, na_tables, jx, na_geometry, bn, rows, d, head_dim)
            xl_new = _proj_res(att, w_o, xl, mods, i, lat_row, 2)
            if not last:
                att_c = _ctx_attention(qkv_c, d, head_dim).reshape(bn * ctx_len, d)
                xc = _proj_res(att_c, w_o, xc, mods, i, ctx_row, 2)
            xl = xl_new
        elif mixer == 1:
            w_in = _cast_weight(gm_w_in, jx, col_chunk=gm_tc)
            w_out = _cast_weight(gm_w_out, jx)
            w_s = gm_w_s[jx].astype(BF16)
            width = w_out.shape[0]
            v_g = gm_v_g[jx][None, :]
            b_s_cols = jnp.repeat(gm_b_s[jx].T, width // GM_GROUPS, axis=1)
            xl = _gmlp(xl, g_mix, mods, i, lat_row, w_in, v_g, w_s, b_s_cols, w_out, lat_tm)
            if not last:
                xc = _gmlp(xc, g_mix, mods, i, ctx_row, w_in, v_g, w_s, b_s_cols, w_out, ctx_tm)
        else:
            w1 = _cast_weight(sc_w_in, jx, parts=3, col_chunk=sc_tc)
            cw, w2 = sc_conv_w[jx], _cast_weight(sc_w_out, jx)
            xl = _gated_conv("sc", xl, seq, g_mix, mods, i, lat_row, (0, 1, 2), w1, cw, None, w2, sc_tc, lat_tm,
                             sub_pref=lat_sub)
            if not last:
                xc = _gated_conv("sc", xc, ctx_len, g_mix, mods, i, ctx_row, (0, 1, 2), w1, cw, None, w2, sc_tc,
                                 ctx_tm)
        w1, cw, cb, w2 = _prep_ffn(ffn_w_up, ffn_conv_w, ffn_conv_b, ffn_w_down, i, ffn_tc, ffn_weights)
        if last:
            xl = _gated_conv("ffn", xl, seq, g_ffn, mods, i, lat_row, (3, 4, 5), w1, cw, cb, w2, ffn_tc, lat_tm,
                             sub_pref=lat_sub)
        else:
            xl, *ffn_weights = _gated_conv("ffn", xl, seq, g_ffn, mods, i, lat_row, (3, 4, 5), w1, cw, cb, w2,
                                           ffn_tc, lat_tm, sub_pref=lat_sub,
                                           cast_next=(ffn_w_up, ffn_w_down, i + 1))
        if not last:
            xc = _gated_conv("ffn", xc, ctx_len, g_ffn, mods, i, ctx_row, (3, 4, 5), w1, cw, cb, w2, ffn_tc,
                             ctx_tm)
    return xl.reshape(bn, seq, d)
```

```python
import functools
import math

import numpy as np
import jax
import jax.numpy as jnp
from jax import lax
from jax.experimental import pallas as pl
from jax.experimental.pallas import tpu as pltpu

GRID_W = 64
NA_HEADS = 16
WIN_H = 8
WIN_W = 16
GM_GROUPS = 16
GM_CHUNK = 128
N_MOD = 6
N_MIXERS = 3
EPS = 1e-6

V7X_VMEM_BYTES = 64 * 1024 * 1024
V7X_SUBLANES_F32 = 8
V7X_SUBLANES_BF16 = 16
V7X_LANES = 128
VMEM_LIMIT_BYTES = V7X_VMEM_BYTES - (4 << 20)
VMEM_TEMP_BYTES = 6 << 20

F32 = jnp.float32
BF16 = jnp.bfloat16

ROW_TILE = 1024
SUB_ROWS = 256
HALO_SUB_ROWS = 512
HIDDEN_CHUNK = 512
PROJ_COLS = 1024
QKV_COLS = 2048
ADALN_COLS = 1024
NA_QROWS = 2
NA_STEP_BLOCKS = 2
COND_ROWS = V7X_SUBLANES_F32


def _compiler_params(semantics, vmem_bytes):
    return pltpu.CompilerParams(dimension_semantics=semantics,
                                vmem_limit_bytes=int(min(vmem_bytes, VMEM_LIMIT_BYTES)))


def _tile(n, pref):
    t = min(n, pref)
    while n % t:
        t -= 1
    return t


def _row_tile_buffers(tile_bytes, other_bytes):
    for x_bufs, out_bufs in ((2, 2), (1, 2), (1, 1)):
        if (x_bufs + out_bufs) * tile_bytes + other_bytes + VMEM_TEMP_BYTES <= VMEM_LIMIT_BYTES:
            break
    return x_bufs, out_bufs


def _norm_mod(x, gain, shift):
    ms = jnp.mean(x * x, axis=-1, keepdims=True)
    return x * lax.rsqrt(ms + EPS) * gain + shift


NORM_CHUNK = 16


def _store_norm_mod(hs_ref, sources, gain, shift, copy_ref=None, row_range=None):
    bounds = np.cumsum([0] + [ref.shape[0] for ref, _ in sources])
    total = int(bounds[-1])
    assert hs_ref.shape[0] == total
    r_lo, r_hi = (0, total) if row_range is None else row_range
    assert r_lo % NORM_CHUNK == 0
    for c0 in range(r_lo, r_hi, NORM_CHUNK):
        c1 = min(c0 + NORM_CHUNK, r_hi)
        pieces = []
        for (ref, zero_pred), b0, b1 in zip(sources, bounds[:-1], bounds[1:]):
            lo, hi = max(c0, int(b0)), min(c1, int(b1))
            if lo >= hi:
                continue
            rows = ref[lo - int(b0):hi - int(b0), :]
            if copy_ref is not None and zero_pred is None:
                copy_ref[lo - int(b0):hi - int(b0), :] = rows
            h = _norm_mod(rows, gain, shift)
            pieces.append(h if zero_pred is None else jnp.where(zero_pred, 0.0, h))
        h = pieces[0] if len(pieces) == 1 else jnp.concatenate(pieces, axis=0)
        hs_ref[c0:c1, :] = h.astype(BF16)


def _adaln_kernel(c_ref, w_ref, b_ref, o_ref):
    c = c_ref[...]
    a = (c * jax.nn.sigmoid(c)).astype(BF16)
    o_ref[...] = jnp.dot(a, w_ref[...].astype(BF16), preferred_element_type=F32) + b_ref[...]


def _adaln(cond, w_ada, b_ada):
    depth, d, n = w_ada.shape
    tn = _tile(n, ADALN_COLS)
    return pl.pallas_call(
        _adaln_kernel,
        grid=(depth, n // tn),
        in_specs=[
            pl.BlockSpec((COND_ROWS, d), lambda l, j: (0, 0)),
            pl.BlockSpec((None, d, tn), lambda l, j: (l, 0, j)),
            pl.BlockSpec((None, 1, tn), lambda l, j: (l, 0, j)),
        ],
        out_specs=pl.BlockSpec((None, COND_ROWS, tn), lambda l, j: (l, 0, j)),
        out_shape=jax.ShapeDtypeStruct((depth, COND_ROWS, n), F32),
        compiler_params=_compiler_params(("arbitrary", "arbitrary"), 3 * d * tn * 4 + VMEM_TEMP_BYTES),
        name="adaln",
    )(cond, w_ada, b_ada.reshape(depth, 1, n))


def _mod_spec(layer, row_fn, width, col_fn=None):
    if col_fn is None:
        return pl.BlockSpec((None, None, N_MOD, width), lambda i, j: (layer, row_fn(i), 0, 0))
    return pl.BlockSpec((None, None, N_MOD, width), lambda i, j: (layer, row_fn(i), 0, col_fn(j)))


def _skewed_subtiles(n_sub, first, second, ahead=1):
    pending = [first(r) for r in range(min(ahead, n_sub))]
    for r in range(n_sub):
        if r + ahead < n_sub:
            pending.append(first(r + ahead))
        second(r, pending.pop(0))


def _proj_res_kernel(a_ref, w_ref, x_ref, mod_ref, o_ref, *, gate_row, sub_rows):
    gate = mod_ref[gate_row:gate_row + 1, :]

    def matmul(r):
        return jnp.dot(a_ref[r * sub_rows:(r + 1) * sub_rows, :], w_ref[...], preferred_element_type=F32)

    def residual(r, y):
        rows = slice(r * sub_rows, (r + 1) * sub_rows)
        o_ref[rows, :] = x_ref[rows, :] + gate * y

    _skewed_subtiles(o_ref.shape[0] // sub_rows, matmul, residual)


def _proj_res(a, w, x, mods, layer, row_fn, gate_row):
    m, k = a.shape
    n_tiles, _, tn = w.shape
    n = n_tiles * tn
    tm = _tile(m, ROW_TILE)
    vmem = 2 * (tm * k * 2 + k * tn * 2 + 2 * tm * tn * 4) + VMEM_TEMP_BYTES
    return pl.pallas_call(
        functools.partial(_proj_res_kernel, gate_row=gate_row, sub_rows=_tile(tm, SUB_ROWS)),
        grid=(m // tm, n // tn),
        in_specs=[
            pl.BlockSpec((tm, k), lambda i, j: (i, 0)),
            pl.BlockSpec((None, k, tn), lambda i, j: (j, 0, 0)),
            pl.BlockSpec((tm, tn), lambda i, j: (i, j)),
            _mod_spec(layer, lambda i: row_fn(i, tm), tn, lambda j: j),
        ],
        out_specs=pl.BlockSpec((tm, tn), lambda i, j: (i, j)),
        out_shape=jax.ShapeDtypeStruct((m, n), F32),
        compiler_params=_compiler_params(("parallel", "arbitrary"), vmem),
        name="proj_res",
    )(a, w, x, mods)


def _qkv_kernel(x_ref, g_ref, mod_ref, w_ref, hg_ref, o_ref, hs_ref, *, n_norm_tiles, head_dim, sub_rows):
    j = pl.program_id(1)
    tm, tn = o_ref.shape
    is_norm_tile = j < n_norm_tiles
    hg = hg_ref[...]

    def matmul(r):
        return jnp.dot(hs_ref[r * sub_rows:(r + 1) * sub_rows, :], w_ref[...], preferred_element_type=F32)

    def head_norm(r, y):
        rows = slice(r * sub_rows, (r + 1) * sub_rows)
        for hh in range(tn // head_dim):
            sl = slice(hh * head_dim, (hh + 1) * head_dim)
            t = y[:, sl]
            inv = lax.rsqrt(jnp.mean(t * t, axis=-1, keepdims=True) + EPS)
            o_ref[rows, sl] = (t * jnp.where(is_norm_tile, inv, 1.0) * hg[:, sl]).astype(o_ref.dtype)

    @pl.when(j == 0)
    def _():
        gain = g_ref[...] * (1.0 + mod_ref[1:2, :])

        def norm_and_matmul(r):
            _store_norm_mod(hs_ref, [(x_ref, None)], gain, mod_ref[0:1, :],
                            row_range=(r * sub_rows, (r + 1) * sub_rows))
            return matmul(r)

        _skewed_subtiles(tm // sub_rows, norm_and_matmul, head_norm)

    @pl.when(j > 0)
    def _():
        _skewed_subtiles(tm // sub_rows, matmul, head_norm)


def _qkv(x, norm_g, mods, layer, row_fn, w_qkv, head_gain, head_dim):
    m, d = x.shape
    n_tiles, _, tn = w_qkv.shape
    n = n_tiles * tn
    tm = _tile(m, ROW_TILE)
    assert (n // 3) % tn == 0 and tn % head_dim == 0
    vmem = 2 * (tm * d * 4 + d * tn * 2 + tm * tn * 2) + tm * d * 2 + 2 * VMEM_TEMP_BYTES
    return pl.pallas_call(
        functools.partial(_qkv_kernel, n_norm_tiles=2 * (n // 3) // tn, head_dim=head_dim,
                          sub_rows=_tile(tm, SUB_ROWS)),
        grid=(m // tm, n // tn),
        in_specs=[
            pl.BlockSpec((tm, d), lambda i, j: (i, 0)),
            pl.BlockSpec((1, d), lambda i, j: (0, 0)),
            _mod_spec(layer, lambda i: row_fn(i, tm), d),
            pl.BlockSpec((None, d, tn), lambda i, j: (j, 0, 0)),
            pl.BlockSpec((1, tn), lambda i, j: (0, j)),
        ],
        out_specs=pl.BlockSpec((tm, tn), lambda i, j: (i, (j + 2 * (n_tiles // 3)) % n_tiles)),
        out_shape=jax.ShapeDtypeStruct((m, n), BF16),
        scratch_shapes=[pltpu.VMEM((tm, d), BF16)],
        compiler_params=_compiler_params(("parallel", "arbitrary"), vmem),
        name="qkv",
    )(x, norm_g, mods, w_qkv, head_gain)


LOG2_E = math.log2(math.e)
QKV_KV_BLOCK = 0
QKV_Q_BLOCK = 2
_NT_DIMS = (((1,), (1,)), ((), ()))


def _na_kernel(*refs, nb, heads, head_dim, step_blocks):
    refs = refs[2:]
    q_ref = refs[0]
    kv_refs = [refs[1 + b * nb:1 + (b + 1) * nb] for b in range(step_blocks)]
    rest = refs[1 + step_blocks * nb:]
    kvc_ref = rest[0]
    tbl_refs = rest[1:1 + step_blocks]
    o_ref = rest[1 + step_blocks]
    qw = q_ref.shape[0] // step_blocks
    d = q_ref.shape[1]

    def scores(u):
        blk, h = divmod(u, heads)
        sl = slice(h * head_dim, (h + 1) * head_dim)
        q = q_ref[blk * qw:(blk + 1) * qw, sl]
        k = jnp.concatenate([r[:, sl] for r in kv_refs[blk]], axis=0)
        s = lax.dot_general(q, k, _NT_DIMS, preferred_element_type=F32) + tbl_refs[blk][h]
        sc = lax.dot_general(q, kvc_ref[:, sl], _NT_DIMS, preferred_element_type=F32)
        return s, sc

    ones_lat = jnp.ones((nb * kv_refs[0][0].shape[0], head_dim), BF16)
    ones_ctx = jnp.ones((kvc_ref.shape[0], head_dim), BF16)

    def attend(u, s_sc):
        blk, h = divmod(u, heads)
        s, sc = s_sc
        sl = slice(h * head_dim, (h + 1) * head_dim)
        vsl = slice(d + h * head_dim, d + (h + 1) * head_dim)
        v = jnp.concatenate([jnp.concatenate([r[:, vsl] for r in kv_refs[blk]], axis=0), ones_lat], axis=1)
        vc = jnp.concatenate([kvc_ref[:, vsl], ones_ctx], axis=1)
        m = jnp.maximum(jnp.max(s, axis=-1, keepdims=True), jnp.max(sc, axis=-1, keepdims=True))
        p = jnp.exp2(s - m).astype(BF16)
        pc = jnp.exp2(sc - m).astype(BF16)
        o = (jnp.dot(p, v, preferred_element_type=F32) + jnp.dot(pc, vc, preferred_element_type=F32))
        o_ref[blk * qw:(blk + 1) * qw, sl] = (
            o[:, :head_dim] * (1.0 / o[:, head_dim:head_dim + 1])).astype(o_ref.dtype)

    _skewed_subtiles(step_blocks * heads, scores, attend, ahead=2)


def _na_geometry(rows, qrows):
    kh = min(WIN_H, rows)
    nb = min(kh + qrows - 1, rows)
    n_blk = rows // qrows
    band0s, keys, types, row_rel = [], {}, [], []
    for blk in range(n_blk):
        r0 = blk * qrows
        q_row = r0 + np.arange(qrows)
        r_start = np.clip(q_row - kh // 2, 0, rows - kh)
        band0 = min(int(np.clip(r0 - kh // 2, 0, rows - kh)), rows - nb)
        k_row = band0 + np.arange(nb)
        row_ok = (k_row[None, :] >= r_start[:, None]) & (k_row[None, :] < r_start[:, None] + kh)
        rel = np.where(row_ok, k_row[None, :] - q_row[:, None] + WIN_H - 1, -1)
        key = rel.tobytes()
        if key not in keys:
            keys[key] = len(row_rel)
            row_rel.append(rel)
        types.append(keys[key])
        band0s.append(band0)
    return nb, n_blk, np.array(band0s, np.int32), np.array(types, np.int32), row_rel


def _na_bias_table(rpb, row_rel):
    layers, heads, n_dr, n_dc = rpb.shape
    period = 2 * GRID_W
    assert n_dc <= period
    r = jnp.pad(rpb, ((0, 0), (0, 0), (0, 0), (0, period - n_dc)))
    skew = jnp.tile(r, (1, 1, 1, GRID_W))[..., :GRID_W * (period - 1)]
    skew = skew.reshape(layers, heads, n_dr, GRID_W, period - 1)
    toep = skew[..., WIN_W - 1:WIN_W - 1 + GRID_W]
    q_col = np.arange(GRID_W)
    c_start = np.clip(q_col - WIN_W // 2, 0, GRID_W - WIN_W)
    col_ok = (q_col[None, :] >= c_start[:, None]) & (q_col[None, :] < c_start[:, None] + WIN_W)
    toep = jnp.where(col_ok, toep, -jnp.inf)
    masked = jnp.full((layers, heads, GRID_W, GRID_W), -jnp.inf, F32)
    tables = []
    for rel in row_rel:
        q_blocks = []
        for qr in range(rel.shape[0]):
            q_blocks.append(jnp.concatenate(
                [toep[:, :, rel[qr, kr]] if rel[qr, kr] >= 0 else masked for kr in range(rel.shape[1])], axis=-1))
        tables.append(jnp.concatenate(q_blocks, axis=-2))
    return jnp.stack(tables, axis=1).astype(F32)


def _na_attention(qkv, qkv_ctx, tables, layer, geometry, bn, rows, d, head_dim):
    nb, n_blk, band0s, types, _ = geometry
    qw, nk = tables.shape[-2:]
    ctx_len = qkv_ctx.shape[1]
    band_tbl = jnp.asarray(band0s)
    type_tbl = jnp.asarray(types)
    kv_view = qkv.reshape(bn * rows, GRID_W, 3 * d)

    spb = _tile(n_blk, NA_STEP_BLOCKS)
    n_steps = n_blk // spb

    def kv_spec(blk, t):
        return pl.BlockSpec((None, GRID_W, 2 * d),
                            lambda b, i, band, typ: (b * rows + band[i * spb + blk] + t, 0, QKV_KV_BLOCK))

    def tbl_spec(blk):
        return pl.BlockSpec((None, None, NA_HEADS, qw, nk),
                            lambda b, i, band, typ: (layer, typ[i * spb + blk], 0, 0, 0))

    in_specs = ([pl.BlockSpec((spb * qw, d), lambda b, i, band, typ: (b * n_steps + i, QKV_Q_BLOCK))]
                + [kv_spec(blk, t) for blk in range(spb) for t in range(nb)]
                + [pl.BlockSpec((None, ctx_len, 2 * d), lambda b, i, band, typ: (b, 0, QKV_KV_BLOCK))]
                + [tbl_spec(blk) for blk in range(spb)])
    vmem = (2 * (spb * (2 * qw * d * 2 + 2 * nk * d * 2 + NA_HEADS * qw * nk * 4) + 2 * ctx_len * d * 2)
            + 2 * VMEM_TEMP_BYTES)
    grid_spec = pltpu.PrefetchScalarGridSpec(
        num_scalar_prefetch=2,
        grid=(bn, n_steps),
        in_specs=in_specs,
        out_specs=pl.BlockSpec((spb * qw, d), lambda b, i, band, typ: (b * n_steps + i, 0)),
    )
    return pl.pallas_call(
        functools.partial(_na_kernel, nb=nb, heads=NA_HEADS, head_dim=head_dim, step_blocks=spb),
        grid_spec=grid_spec,
        out_shape=jax.ShapeDtypeStruct((bn * rows * GRID_W, d), BF16),
        compiler_params=_compiler_params(("parallel", "arbitrary"), vmem),
        name="na_attention",
    )(band_tbl, type_tbl, qkv, *([kv_view] * (spb * nb)), qkv_ctx, *([tables] * spb))


def _ctx_attn_kernel(q_ref, k_ref, v_ref, o_ref, *, heads, head_dim):
    for h in range(heads):
        sl = slice(h * head_dim, (h + 1) * head_dim)
        s = lax.dot_general(q_ref[:, sl], k_ref[:, sl], _NT_DIMS, preferred_element_type=F32)
        m = jnp.max(s, axis=-1, keepdims=True)
        p = jnp.exp2(s - m)
        l = jnp.sum(p, axis=-1, keepdims=True)
        o = jnp.dot(p.astype(BF16), v_ref[:, sl], preferred_element_type=F32)
        o_ref[:, sl] = (o * (1.0 / l)).astype(o_ref.dtype)


def _ctx_attention(qkv_ctx, d, head_dim):
    bn, ctx_len, _ = qkv_ctx.shape
    spec = lambda col: pl.BlockSpec((None, ctx_len, d), lambda b: (b, 0, col))
    return pl.pallas_call(
        functools.partial(_ctx_attn_kernel, heads=NA_HEADS, head_dim=head_dim),
        grid=(bn,),
        in_specs=[spec(QKV_Q_BLOCK), spec(2 * QKV_KV_BLOCK), spec(2 * QKV_KV_BLOCK + 1)],
        out_specs=pl.BlockSpec((None, ctx_len, d), lambda b: (b, 0, 0)),
        out_shape=jax.ShapeDtypeStruct((bn, ctx_len, d), BF16),
        compiler_params=_compiler_params(("arbitrary",), 8 * ctx_len * d * 2 + 2 * VMEM_TEMP_BYTES),
        name="ctx_attention",
    )(qkv_ctx, qkv_ctx, qkv_ctx)


HALO = V7X_SUBLANES_F32


def _shift_rows(u, tm):
    n = u.shape[0]
    prev = pltpu.roll(u, 1, 0)[HALO:HALO + tm]
    nxt = pltpu.roll(u, n - 1, 0)[HALO:HALO + tm]
    return prev, nxt


def _conv3(u, cw, rows, seq_edges):
    prev, nxt = _shift_rows(u, rows)
    if seq_edges:
        row = lax.broadcasted_iota(jnp.int32, (rows, 1), 0)
        prev = jnp.where(row == 0, 0.0, prev)
        nxt = jnp.where(row == rows - 1, 0.0, nxt)
    return prev * cw[0:1] + u[HALO:HALO + rows] * cw[1:2] + nxt * cw[2:3]


def _gated_conv_kernel(*refs, kind, tiles_per_seq, mod_rows, sub_rows, seq_edges, side_cast):
    parts = 2 if kind == "ffn" else 3
    x_ref, xp_ref, xn_ref, g_ref, mod_ref = refs[:5]
    w1_refs = refs[5:5 + parts]
    conv_ref = refs[5 + parts]
    rest = refs[6 + parts:]
    tc = w1_refs[0].shape[1]
    col = lambda p: slice(p * tc, (p + 1) * tc)
    if side_cast is None:
        w2_ref, o_ref, hs_ref = rest
    else:
        w2_ref, up_f32_ref, down_f32_ref, o_ref, up_bf16_ref, down_bf16_ref, hs_ref = rest
    i = pl.program_id(0)
    j = pl.program_id(1)
    tm = x_ref.shape[0]
    r_shift, r_scale, r_gate = mod_rows

    if side_cast is not None:
        n_up, n_down, down_rows = side_cast
        step = i * pl.num_programs(1) + j

        @pl.when(step < n_up)
        def _():
            _cast_block(up_f32_ref, up_bf16_ref, 0, None)

        @pl.when((step >= n_up) & (step < n_up + n_down))
        def _():
            _cast_block(down_f32_ref, down_bf16_ref, (step - n_up) * down_f32_ref.shape[0], down_rows)

    res_gate = mod_ref[r_gate:r_gate + 1, :]
    n_sub = tm // sub_rows

    def first_matmuls(r):
        lhs = hs_ref[r * sub_rows:r * sub_rows + sub_rows + 2 * HALO, :]
        return [jnp.dot(lhs, w[...], preferred_element_type=F32) for w in w1_refs]

    def mid_and_second_matmul(r, u):
        if kind == "ffn":
            gate, up = [_conv3(u[p], conv_ref[0:3, col(p)], sub_rows, seq_edges) + conv_ref[3:4, col(p)]
                        for p in range(2)]
            a = gate * jax.nn.sigmoid(gate) * up
        else:
            a = u[0][HALO:HALO + sub_rows] * _conv3(u[1] * u[2], conv_ref[0:3, :], sub_rows, seq_edges)
        o_ref[r * sub_rows:(r + 1) * sub_rows, :] += res_gate * jnp.dot(a.astype(BF16), w2_ref[...],
                                                                       preferred_element_type=F32)

    @pl.when(j == 0)
    def _():
        gain = g_ref[...] * (1.0 + mod_ref[r_scale:r_scale + 1, :])
        shift = mod_ref[r_shift:r_shift + 1, :]
        pos = i % tiles_per_seq
        sources = [(xp_ref, pos == 0), (x_ref, None), (xn_ref, pos == tiles_per_seq - 1)]
        total = tm + 2 * HALO

        def norm_and_first_matmuls(r):
            lo = 0 if r == 0 else r * sub_rows + 2 * HALO
            hi = total if r == n_sub - 1 else (r + 1) * sub_rows + 2 * HALO
            _store_norm_mod(hs_ref, sources, gain, shift, copy_ref=o_ref, row_range=(lo, hi))
            return first_matmuls(r)

        _skewed_subtiles(n_sub, norm_and_first_matmuls, mid_and_second_matmul)

    @pl.when(j > 0)
    def _():
        _skewed_subtiles(n_sub, first_matmuls, mid_and_second_matmul)


def _gated_conv(kind, x, seq_len, norm_g, mods, layer, row_fn, mod_rows, w1, conv, w2, tc, tm_pref,
                sub_pref=SUB_ROWS, cast_next=None):
    m, d = x.shape
    parts = 2 if kind == "ffn" else 3
    cparts = 2 if kind == "ffn" else 1
    n_chunks = w2.shape[0] // tc
    assert w1.shape == (parts * n_chunks, d, tc) and conv.shape == (n_chunks, CONV_ROWS, cparts * tc)
    if seq_len >= tm_pref:
        tm = _tile(seq_len, tm_pref)
        sub_rows = _tile(tm, sub_pref)
        tiles_per_seq, seq_edges = seq_len // tm, False
    else:
        tm = seq_len * _tile(m // seq_len, tm_pref // seq_len)
        sub_rows = seq_len
        tiles_per_seq, seq_edges = 1, True
    assert sub_rows % V7X_SUBLANES_BF16 == 0
    hb = tm // HALO
    n_hblk = m // HALO
    other = 2 * (d * parts * tc * 2 + tc * d * 2) + (tm + 2 * HALO) * d * 2
    n_steps = (m // tm) * n_chunks
    side_cast, side_specs_in, side_specs_out, side_shapes, side_operands = None, [], [], [], []
    if cast_next is not None:
        w_up, w_down, nl = cast_next
        f, fp = w_down.shape[1], n_chunks * tc
        up_rows = min(t for t in range(V7X_SUBLANES_BF16, d + 1, V7X_SUBLANES_BF16)
                      if d % t == 0 and 2 * (d // t) <= 3 * n_steps // 4)
        n_up = 2 * (d // up_rows)
        down_rows = min(t for t in range(V7X_SUBLANES_BF16, fp + 1, V7X_SUBLANES_BF16)
                        if fp % t == 0 and fp // t <= n_steps - n_up)
        n_down = fp // down_rows
        last_down_block = (f - 1) // down_rows
        up_item = lambda i, j: jnp.minimum(i * n_chunks + j, n_up - 1)
        down_item = lambda i, j: jnp.clip(i * n_chunks + j - n_up, 0, n_down - 1)
        side_specs_in = [
            pl.BlockSpec((None, up_rows, f), lambda i, j: (nl, up_item(i, j) // 2, up_item(i, j) % 2)),
            pl.BlockSpec((None, down_rows, d),
                         lambda i, j: (nl, jnp.minimum(down_item(i, j), last_down_block), 0)),
        ]
        side_specs_out = [
            pl.BlockSpec((n_chunks, up_rows, tc), lambda i, j: (up_item(i, j) % 2, up_item(i, j) // 2, 0)),
            pl.BlockSpec((down_rows, d), lambda i, j: (down_item(i, j), 0)),
        ]
        side_shapes = [jax.ShapeDtypeStruct((2 * n_chunks, d, tc), BF16), jax.ShapeDtypeStruct((fp, d), BF16)]
        side_operands = [w_up, w_down]
        side_cast = (n_up, n_down, f)
        other += 2 * (up_rows * f * 4 + up_rows * fp * 2 + down_rows * d * 6)
    x_bufs, out_bufs = _row_tile_buffers(tm * d * 4, other)
    vmem = (x_bufs + out_bufs) * tm * d * 4 + other + VMEM_TEMP_BYTES
    in_specs = [
        pl.BlockSpec((tm, d), lambda i, j: (i, 0), pipeline_mode=pl.Buffered(x_bufs)),
        pl.BlockSpec((HALO, d), lambda i, j: (jnp.maximum(i * hb - 1, 0), 0)),
        pl.BlockSpec((HALO, d), lambda i, j: (jnp.minimum((i + 1) * hb, n_hblk - 1), 0)),
        pl.BlockSpec((1, d), lambda i, j: (0, 0)),
        _mod_spec(layer, lambda i: row_fn(i, tm), d),
    ]
    in_specs += [pl.BlockSpec((None, d, tc), functools.partial(lambda p, i, j: (p * n_chunks + j, 0, 0), p))
                 for p in range(parts)]
    in_specs += [pl.BlockSpec((None, CONV_ROWS, cparts * tc), lambda i, j: (j, 0, 0)),
                 pl.BlockSpec((tc, d), lambda i, j: (j, 0))]
    operands = [x, x, x, norm_g, mods] + [w1] * parts + [conv, w2]
    out_spec = pl.BlockSpec((tm, d), lambda i, j: (i, 0), pipeline_mode=pl.Buffered(out_bufs))
    out_shape = jax.ShapeDtypeStruct((m, d), F32)
    result = pl.pallas_call(
        functools.partial(_gated_conv_kernel, kind=kind, tiles_per_seq=tiles_per_seq, mod_rows=mod_rows,
                          sub_rows=sub_rows, seq_edges=seq_edges, side_cast=side_cast),
        grid=(m // tm, n_chunks),
        in_specs=in_specs + side_specs_in,
        out_specs=[out_spec] + side_specs_out if side_cast else out_spec,
        out_shape=[out_shape] + side_shapes if side_cast else out_shape,
        scratch_shapes=[pltpu.VMEM((tm + 2 * HALO, d), BF16)],
        compiler_params=_compiler_params(("arbitrary" if side_cast else "parallel", "arbitrary"), vmem),
        name=kind,
    )(*operands, *side_operands)
    return tuple(result) if side_cast else result


def _gelu_tanh(x):
    return jax.nn.gelu(x, approximate=True)


def _gmlp_kernel(x_ref, g_ref, mod_ref, win_ref, vg_ref, ws_ref, bs_ref, wout_ref, o_ref,
                 hs_ref, v_ref, ssq_ref, *, n_chunks, group_dim, width, sub_rows):
    j = pl.program_id(1)
    tm = x_ref.shape[0]
    tc = win_ref.shape[1]
    n_sub = tm // sub_rows

    def in_matmul(r):
        return jnp.dot(hs_ref[r * sub_rows:(r + 1) * sub_rows, :], win_ref[...], preferred_element_type=F32)

    def keep_v(r, y):
        rows = slice(r * sub_rows, (r + 1) * sub_rows)
        t = _gelu_tanh(y)
        v_ref[j, rows, :] = t.astype(v_ref.dtype)
        ssq_ref[rows, :] += jnp.sum(t * t, axis=-1, keepdims=True)

    @pl.when(j == 0)
    def _():
        gain = g_ref[...] * (1.0 + mod_ref[1:2, :])
        ssq_ref[...] = jnp.zeros_like(ssq_ref)

        def norm_and_matmul(r):
            _store_norm_mod(hs_ref, [(x_ref, None)], gain, mod_ref[0:1, :], copy_ref=o_ref,
                            row_range=(r * sub_rows, (r + 1) * sub_rows))
            return in_matmul(r)

        _skewed_subtiles(n_sub, norm_and_matmul, keep_v)

    @pl.when((j > 0) & (j < n_chunks))
    def _():
        _skewed_subtiles(n_sub, in_matmul, keep_v)

    @pl.when(j >= n_chunks)
    def _():
        jj = j - n_chunks
        vg = vg_ref[...]
        res_gate = mod_ref[2:3, :]
        bs = bs_ref[...]
        gpc = tc // group_dim
        ws = [ws_ref[jj * gpc + gi] for gi in range(gpc)]

        def gate_and_project(r, y):
            rows = slice(r * sub_rows, (r + 1) * sub_rows)
            inv = lax.rsqrt(ssq_ref[rows, :] * (1.0 / width) + EPS)
            vn = (v_ref[jj, rows, :].astype(F32) * inv * vg).astype(BF16)
            cols = []
            for gi in range(gpc):
                csl = slice(gi * group_dim, (gi + 1) * group_dim)
                cols.append(jnp.concatenate(
                    [jnp.dot(ws[gi], vn[c * GM_CHUNK:(c + 1) * GM_CHUNK, csl], preferred_element_type=F32)
                     + bs[:, csl] for c in range(sub_rows // GM_CHUNK)], axis=0))
            sv = jnp.concatenate(cols, axis=1)
            o_ref[rows, :] += res_gate * jnp.dot((_gelu_tanh(y) * sv).astype(BF16), wout_ref[...],
                                                 preferred_element_type=F32)

        _skewed_subtiles(n_sub, in_matmul, gate_and_project)


def _gmlp(x, norm_g, mods, layer, row_fn, w_in, v_g, w_s, b_s_cols, w_out, tm_pref):
    m, d = x.shape
    width = w_out.shape[0]
    group_dim = width // GM_GROUPS
    tc = w_in.shape[2]
    n_chunks = width // tc
    assert w_in.shape[0] == 2 * n_chunks and tc % group_dim == 0
    tm = _tile(m, tm_pref)
    sub_rows = _tile(tm, SUB_ROWS)
    assert sub_rows % GM_CHUNK == 0
    other = (2 * (d * tc * 2 + tc * d * 2 + GM_CHUNK * tc * 4) + tm * d * 2 + tm * width * 2
             + GM_GROUPS * GM_CHUNK * GM_CHUNK * 2 + tm * V7X_LANES * 4)
    x_bufs, out_bufs = _row_tile_buffers(tm * d * 4, other)
    vmem = (x_bufs + out_bufs) * tm * d * 4 + other + VMEM_TEMP_BYTES
    return pl.pallas_call(
        functools.partial(_gmlp_kernel, n_chunks=n_chunks, group_dim=group_dim, width=width,
                          sub_rows=sub_rows),
        grid=(m // tm, 2 * n_chunks),
        in_specs=[
            pl.BlockSpec((tm, d), lambda i, j: (i, 0), pipeline_mode=pl.Buffered(x_bufs)),
            pl.BlockSpec((1, d), lambda i, j: (0, 0)),
            _mod_spec(layer, lambda i: row_fn(i, tm), d),
            pl.BlockSpec((None, d, tc),
                         lambda i, j: (jnp.where(j < n_chunks, j + n_chunks, j - n_chunks), 0, 0)),
            pl.BlockSpec((1, tc), lambda i, j: (0, jnp.maximum(j - n_chunks, 0))),
            pl.BlockSpec((GM_GROUPS, GM_CHUNK, GM_CHUNK), lambda i, j: (0, 0, 0)),
            pl.BlockSpec((GM_CHUNK, tc), lambda i, j: (0, jnp.maximum(j - n_chunks, 0))),
            pl.BlockSpec((tc, d), lambda i, j: (jnp.maximum(j - n_chunks, 0), 0)),
        ],
        out_specs=pl.BlockSpec((tm, d), lambda i, j: (i, 0), pipeline_mode=pl.Buffered(out_bufs)),
        out_shape=jax.ShapeDtypeStruct((m, d), F32),
        scratch_shapes=[pltpu.VMEM((tm, d), BF16), pltpu.VMEM((n_chunks, tm, tc), BF16), pltpu.VMEM((tm, 1), F32)],
        compiler_params=_compiler_params(("parallel", "arbitrary"), vmem),
        name="gmlp",
    )(x, norm_g, mods, w_in, v_g, w_s, b_s_cols, w_out)


CAST_BLOCK_BYTES = 6 << 20


def _cast_kernel(x_ref, o_ref, *, valid_rows):
    _cast_block(x_ref, o_ref, pl.program_id(0) * x_ref.shape[0], valid_rows)


def _cast_block(x_ref, o_ref, row0, valid_rows):
    rows, f = x_ref.shape
    x = x_ref[...]
    if valid_rows is not None:
        row = row0 + lax.broadcasted_iota(jnp.int32, (rows, 1), 0)
        x = jnp.where(row < valid_rows, x, 0.0)
    x = x.astype(o_ref.dtype)
    if len(o_ref.shape) == 2:
        o_ref[...] = x
        return
    n_chunks, _, tc = o_ref.shape
    for c in range(n_chunks):
        width = max(0, min(tc, f - c * tc))
        if width:
            o_ref[c, :, :width] = x[:, c * tc:c * tc + width]
        if width < tc:
            o_ref[c, :, width:] = jnp.zeros((rows, tc - width), o_ref.dtype)


def _cast_weight(w, idx, parts=1, fp=None, col_chunk=None, rows_out=None, row_block=None):
    _, r, c = w.shape
    f = c // parts
    fp = f if fp is None else fp
    rows_out = r if rows_out is None else rows_out
    assert parts == 1 or f % V7X_LANES == 0
    assert col_chunk is not None or (parts == 1 and fp == f)
    padded_rows = rows_out > r
    if row_block is None:
        row_pref = max(V7X_SUBLANES_BF16, CAST_BLOCK_BYTES // (4 * f))
        row_block = max(t for t in range(V7X_SUBLANES_BF16, row_pref + 1, V7X_SUBLANES_BF16)
                        if rows_out % t == 0)
    assert rows_out % row_block == 0 and row_block % V7X_SUBLANES_BF16 == 0
    assert r % row_block == 0 or (padded_rows and rows_out - row_block < r)
    if col_chunk is None:
        out_spec = pl.BlockSpec((row_block, f), lambda i, p: (i, 0))
        out_shape = (rows_out, f)
    else:
        assert fp % col_chunk == 0 and col_chunk % V7X_LANES == 0 and not padded_rows
        n_chunks = fp // col_chunk
        out_spec = pl.BlockSpec((n_chunks, row_block, col_chunk), lambda i, p: (p, i, 0))
        out_shape = (parts * n_chunks, r, col_chunk)
    return pl.pallas_call(
        functools.partial(_cast_kernel, valid_rows=r if padded_rows else None),
        grid=(rows_out // row_block, parts),
        in_specs=[pl.BlockSpec((None, row_block, f), lambda i, p: (idx, i, p))],
        out_specs=out_spec,
        out_shape=jax.ShapeDtypeStruct(out_shape, BF16),
        compiler_params=_compiler_params(("parallel", "arbitrary"),
                                         2 * row_block * (4 * f + 2 * fp) + VMEM_TEMP_BYTES),
        name="cast_weight",
    )(w)


def _pad_halves(a, f, fp):
    pad = lambda h: jnp.pad(h, ((0, 0), (0, fp - f)))
    return jnp.concatenate([pad(a[:, :f]), pad(a[:, f:])], axis=1)


CONV_ROWS = V7X_SUBLANES_F32


def _conv_params(taps, bias, cparts, tc):
    width = taps.shape[1]
    rows = [taps, jnp.zeros((1, width), F32) if bias is None else bias,
            jnp.zeros((CONV_ROWS - taps.shape[0] - 1, width), F32)]
    a = jnp.concatenate(rows, axis=0).reshape(CONV_ROWS, cparts, width // (cparts * tc), tc)
    return jnp.transpose(a, (2, 0, 1, 3)).reshape(width // (cparts * tc), CONV_ROWS, cparts * tc)


def _prep_ffn(w_up, conv_w, conv_b, w_down, layer, tc, cast_weights):
    f = w_down.shape[1]
    fp = -(-f // tc) * tc
    conv = _conv_params(_pad_halves(conv_w[layer], f, fp), _pad_halves(conv_b[layer][None, :], f, fp), 2, tc)
    if cast_weights is None:
        cast_weights = (_cast_weight(w_up, layer, parts=2, fp=fp, col_chunk=tc),
                        _cast_weight(w_down, layer, rows_out=fp))
    return cast_weights[0], conv, cast_weights[1]


def kernel(x, c, ctx, c_ctx, norm_mix_g, norm_ffn_g, w_ada, b_ada, na_w_qkv, na_q_g, na_k_g, na_rpb, na_w_o,
           gm_w_in, gm_v_g, gm_w_s, gm_b_s, gm_w_out, sc_w_in, sc_conv_w, sc_w_out,
           ffn_w_up, ffn_conv_w, ffn_conv_b, ffn_w_down):
    bn, seq, d = x.shape
    ctx_len = ctx.shape[1]
    depth = w_ada.shape[0]
    head_dim = d // NA_HEADS
    rows = seq // GRID_W
    assert bn + 1 <= COND_ROWS and seq % GRID_W == 0 and ctx_len % GM_CHUNK == 0

    cond = jnp.concatenate([c, c_ctx[None, :], jnp.zeros((COND_ROWS - bn - 1, d), F32)], axis=0)
    mods = _adaln(cond, w_ada, b_ada).reshape(depth, COND_ROWS, N_MOD, d)

    lat_row = lambda i, tm: i // (seq // tm)
    ctx_row = lambda i, tm: bn

    ffn_hidden = ffn_w_down.shape[1]
    ffn_tc = HIDDEN_CHUNK if ffn_hidden >= HIDDEN_CHUNK else V7X_LANES
    sc_tc = _tile(d, HIDDEN_CHUNK)
    gm_width = gm_w_out.shape[1]
    gm_tc = max(gm_width // GM_GROUPS, _tile(gm_width, HIDDEN_CHUNK))
    proj_tn = _tile(d, PROJ_COLS)
    qkv_tn = _tile(d, QKV_COLS)
    lat_tm = ROW_TILE
    ctx_tm = min(ROW_TILE, bn * ctx_len)
    lat_sub = HALO_SUB_ROWS

    xl = x.reshape(bn * seq, d)
    xc = ctx.reshape(bn * ctx_len, d)
    ffn_weights = None
    na_geometry = _na_geometry(rows, min(NA_QROWS, rows))
    na_tables = _na_bias_table(na_rpb * LOG2_E, na_geometry[-1])
    for i in range(depth):
        last = i == depth - 1
        mixer, jx = i % N_MIXERS, i // N_MIXERS
        g_mix = norm_mix_g[i][None, :]
        g_ffn = norm_ffn_g[i][None, :]
        if mixer == 0:
            w_qkv = _cast_weight(na_w_qkv, jx, col_chunk=qkv_tn)
            w_o = _cast_weight(na_w_o, jx, col_chunk=proj_tn)
            q_gain = na_q_g[jx] * (head_dim ** -0.5 * LOG2_E)
            head_gain = jnp.concatenate([jnp.tile(q_gain, NA_HEADS), jnp.tile(na_k_g[jx], NA_HEADS),
                                         jnp.ones((d,), F32)])[None, :]
            qkv = _qkv(xl, g_mix, mods, i, lat_row, w_qkv, head_gain, head_dim)
            qkv_c = _qkv(xc, g_mix, mods, i, ctx_row, w_qkv, head_gain, head_dim).reshape(bn, ctx_len, 3 * d)
            att = _na_attention(qkv, qkv_c, na_tables, jx, na_geometry, bn, rows, d, head_dim)
            xl_new = _proj_res(att, w_o, xl, mods, i, lat_row, 2)
            if not last:
                att_c = _ctx_attention(qkv_c, d, head_dim).reshape(bn * ctx_len, d)
                xc = _proj_res(att_c, w_o, xc, mods, i, ctx_row, 2)
            xl = xl_new
        elif mixer == 1:
            w_in = _cast_weight(gm_w_in, jx, col_chunk=gm_tc)
            w_out = _cast_weight(gm_w_out, jx)
            w_s = gm_w_s[jx].astype(BF16)
            width = w_out.shape[0]
            v_g = gm_v_g[jx][None, :]
            b_s_cols = jnp.repeat(gm_b_s[jx].T, width // GM_GROUPS, axis=1)
            xl = _gmlp(xl, g_mix, mods, i, lat_row, w_in, v_g, w_s, b_s_cols, w_out, lat_tm)
            if not last:
                xc = _gmlp(xc, g_mix, mods, i, ctx_row, w_in, v_g, w_s, b_s_cols, w_out, ctx_tm)
        else:
            w1 = _cast_weight(sc_w_in, jx, parts=3, col_chunk=sc_tc)
            conv, w2 = _conv_params(sc_conv_w[jx], None, 1, sc_tc), _cast_weight(sc_w_out, jx)
            xl = _gated_conv("sc", xl, seq, g_mix, mods, i, lat_row, (0, 1, 2), w1, conv, w2, sc_tc, lat_tm,
                             sub_pref=lat_sub)
            if not last:
                xc = _gated_conv("sc", xc, ctx_len, g_mix, mods, i, ctx_row, (0, 1, 2), w1, conv, w2, sc_tc,
                                 ctx_tm)
        w1, conv, w2 = _prep_ffn(ffn_w_up, ffn_conv_w, ffn_conv_b, ffn_w_down, i, ffn_tc, ffn_weights)
        if last:
            xl = _gated_conv("ffn", xl, seq, g_ffn, mods, i, lat_row, (3, 4, 5), w1, conv, w2, ffn_tc, lat_tm,
                             sub_pref=lat_sub)
        else:
            xl, *ffn_weights = _gated_conv("ffn", xl, seq, g_ffn, mods, i, lat_row, (3, 4, 5), w1, conv, w2,
                                           ffn_tc, lat_tm, sub_pref=lat_sub,
                                           cast_next=(ffn_w_up, ffn_w_down, i + 1))
        if not last:
            xc = _gated_conv("ffn", xc, ctx_len, g_ffn, mods, i, ctx_row, (3, 4, 5), w1, conv, w2, ffn_tc,
                             ctx_tm)
    return xl.reshape(bn, seq, d)
```

```python
import functools
import math

import numpy as np
import jax
import jax.numpy as jnp
from jax import lax
from jax.experimental import pallas as pl
from jax.experimental.pallas import tpu as pltpu

GRID_W = 64
NA_HEADS = 16
WIN_H = 8
WIN_W = 16
GM_GROUPS = 16
GM_CHUNK = 128
N_MOD = 6
N_MIXERS = 3
EPS = 1e-6

V7X_VMEM_BYTES = 64 * 1024 * 1024
V7X_SUBLANES_F32 = 8
V7X_SUBLANES_BF16 = 16
V7X_LANES = 128
VMEM_LIMIT_BYTES = V7X_VMEM_BYTES - (4 << 20)
VMEM_TEMP_BYTES = 6 << 20

F32 = jnp.float32
BF16 = jnp.bfloat16

ROW_TILE = 1024
SUB_ROWS = 256
HALO_SUB_ROWS = 512
HIDDEN_CHUNK = 512
PROJ_COLS = 1024
QKV_COLS = 2048
ADALN_COLS = 1024
NA_QROWS = 2
NA_STEP_BLOCKS = 2
COND_ROWS = V7X_SUBLANES_F32


def _compiler_params(semantics, vmem_bytes):
    return pltpu.CompilerParams(dimension_semantics=semantics,
                                vmem_limit_bytes=int(min(vmem_bytes, VMEM_LIMIT_BYTES)))


def _tile(n, pref):
    t = min(n, pref)
    while n % t:
        t -= 1
    return t


def _row_tile_buffers(tile_bytes, other_bytes):
    for x_bufs, out_bufs in ((2, 2), (1, 2), (1, 1)):
        if (x_bufs + out_bufs) * tile_bytes + other_bytes + VMEM_TEMP_BYTES <= VMEM_LIMIT_BYTES:
            break
    return x_bufs, out_bufs


def _norm_mod(x, gain, shift):
    ms = jnp.mean(x * x, axis=-1, keepdims=True)
    return x * lax.rsqrt(ms + EPS) * gain + shift


NORM_CHUNK = 16


def _store_norm_mod(hs_ref, sources, gain, shift, copy_ref=None, row_range=None):
    bounds = np.cumsum([0] + [ref.shape[0] for ref, _ in sources])
    total = int(bounds[-1])
    assert hs_ref.shape[0] == total
    r_lo, r_hi = (0, total) if row_range is None else row_range
    assert r_lo % NORM_CHUNK == 0
    for c0 in range(r_lo, r_hi, NORM_CHUNK):
        c1 = min(c0 + NORM_CHUNK, r_hi)
        pieces = []
        for (ref, zero_pred), b0, b1 in zip(sources, bounds[:-1], bounds[1:]):
            lo, hi = max(c0, int(b0)), min(c1, int(b1))
            if lo >= hi:
                continue
            rows = ref[lo - int(b0):hi - int(b0), :]
            if copy_ref is not None and zero_pred is None:
                copy_ref[lo - int(b0):hi - int(b0), :] = rows
            h = _norm_mod(rows, gain, shift)
            pieces.append(h if zero_pred is None else jnp.where(zero_pred, 0.0, h))
        h = pieces[0] if len(pieces) == 1 else jnp.concatenate(pieces, axis=0)
        hs_ref[c0:c1, :] = h.astype(BF16)


def _adaln_kernel(c_ref, w_ref, b_ref, o_ref):
    c = c_ref[...]
    a = (c * jax.nn.sigmoid(c)).astype(BF16)
    o_ref[...] = jnp.dot(a, w_ref[...].astype(BF16), preferred_element_type=F32) + b_ref[...]


def _adaln(cond, w_ada, b_ada):
    depth, d, n = w_ada.shape
    tn = _tile(n, ADALN_COLS)
    return pl.pallas_call(
        _adaln_kernel,
        grid=(depth, n // tn),
        in_specs=[
            pl.BlockSpec((COND_ROWS, d), lambda l, j: (0, 0)),
            pl.BlockSpec((None, d, tn), lambda l, j: (l, 0, j)),
            pl.BlockSpec((None, 1, tn), lambda l, j: (l, 0, j)),
        ],
        out_specs=pl.BlockSpec((None, COND_ROWS, tn), lambda l, j: (l, 0, j)),
        out_shape=jax.ShapeDtypeStruct((depth, COND_ROWS, n), F32),
        compiler_params=_compiler_params(("arbitrary", "arbitrary"), 3 * d * tn * 4 + VMEM_TEMP_BYTES),
        name="adaln",
    )(cond, w_ada, b_ada.reshape(depth, 1, n))


def _mod_spec(layer, row_fn, width, col_fn=None):
    if col_fn is None:
        return pl.BlockSpec((None, None, N_MOD, width), lambda i, j: (layer, row_fn(i), 0, 0))
    return pl.BlockSpec((None, None, N_MOD, width), lambda i, j: (layer, row_fn(i), 0, col_fn(j)))


def _skewed_subtiles(n_sub, first, second, ahead=1):
    pending = [first(r) for r in range(min(ahead, n_sub))]
    for r in range(n_sub):
        if r + ahead < n_sub:
            pending.append(first(r + ahead))
        second(r, pending.pop(0))


def _proj_res_kernel(a_ref, w_ref, x_ref, mod_ref, o_ref, *, gate_row, sub_rows):
    gate = mod_ref[gate_row:gate_row + 1, :]

    def matmul(r):
        return jnp.dot(a_ref[r * sub_rows:(r + 1) * sub_rows, :], w_ref[...], preferred_element_type=F32)

    def residual(r, y):
        rows = slice(r * sub_rows, (r + 1) * sub_rows)
        o_ref[rows, :] = x_ref[rows, :] + gate * y

    _skewed_subtiles(o_ref.shape[0] // sub_rows, matmul, residual)


def _proj_res(a, w, x, mods, layer, row_fn, gate_row):
    m, k = a.shape
    n_tiles, _, tn = w.shape
    n = n_tiles * tn
    tm = _tile(m, ROW_TILE)
    vmem = 2 * (tm * k * 2 + k * tn * 2 + 2 * tm * tn * 4) + VMEM_TEMP_BYTES
    return pl.pallas_call(
        functools.partial(_proj_res_kernel, gate_row=gate_row, sub_rows=_tile(tm, SUB_ROWS)),
        grid=(m // tm, n // tn),
        in_specs=[
            pl.BlockSpec((tm, k), lambda i, j: (i, 0)),
            pl.BlockSpec((None, k, tn), lambda i, j: (j, 0, 0)),
            pl.BlockSpec((tm, tn), lambda i, j: (i, j)),
            _mod_spec(layer, lambda i: row_fn(i, tm), tn, lambda j: j),
        ],
        out_specs=pl.BlockSpec((tm, tn), lambda i, j: (i, j)),
        out_shape=jax.ShapeDtypeStruct((m, n), F32),
        compiler_params=_compiler_params(("parallel", "arbitrary"), vmem),
        name="proj_res",
    )(a, w, x, mods)


def _qkv_kernel(x_ref, g_ref, mod_ref, w_ref, hg_ref, o_ref, hs_ref, *, n_norm_tiles, head_dim, sub_rows):
    j = pl.program_id(1)
    tm, tn = o_ref.shape
    is_norm_tile = j < n_norm_tiles
    hg = hg_ref[...]

    def matmul(r):
        return jnp.dot(hs_ref[r * sub_rows:(r + 1) * sub_rows, :], w_ref[...], preferred_element_type=F32)

    def head_norm(r, y):
        rows = slice(r * sub_rows, (r + 1) * sub_rows)
        for hh in range(tn // head_dim):
            sl = slice(hh * head_dim, (hh + 1) * head_dim)
            t = y[:, sl]
            inv = lax.rsqrt(jnp.mean(t * t, axis=-1, keepdims=True) + EPS)
            o_ref[rows, sl] = (t * jnp.where(is_norm_tile, inv, 1.0) * hg[:, sl]).astype(o_ref.dtype)

    @pl.when(j == 0)
    def _():
        gain = g_ref[...] * (1.0 + mod_ref[1:2, :])

        def norm_and_matmul(r):
            _store_norm_mod(hs_ref, [(x_ref, None)], gain, mod_ref[0:1, :],
                            row_range=(r * sub_rows, (r + 1) * sub_rows))
            return matmul(r)

        _skewed_subtiles(tm // sub_rows, norm_and_matmul, head_norm)

    @pl.when(j > 0)
    def _():
        _skewed_subtiles(tm // sub_rows, matmul, head_norm)


def _qkv(x, norm_g, mods, layer, row_fn, w_qkv, head_gain, head_dim):
    m, d = x.shape
    n_tiles, _, tn = w_qkv.shape
    n = n_tiles * tn
    tm = _tile(m, ROW_TILE)
    assert (n // 3) % tn == 0 and tn % head_dim == 0
    vmem = 2 * (tm * d * 4 + d * tn * 2 + tm * tn * 2) + tm * d * 2 + 2 * VMEM_TEMP_BYTES
    return pl.pallas_call(
        functools.partial(_qkv_kernel, n_norm_tiles=2 * (n // 3) // tn, head_dim=head_dim,
                          sub_rows=_tile(tm, SUB_ROWS)),
        grid=(m // tm, n // tn),
        in_specs=[
            pl.BlockSpec((tm, d), lambda i, j: (i, 0)),
            pl.BlockSpec((1, d), lambda i, j: (0, 0)),
            _mod_spec(layer, lambda i: row_fn(i, tm), d),
            pl.BlockSpec((None, d, tn), lambda i, j: (j, 0, 0)),
            pl.BlockSpec((1, tn), lambda i, j: (0, j)),
        ],
        out_specs=pl.BlockSpec((tm, tn), lambda i, j: (i, (j + 2 * (n_tiles // 3)) % n_tiles)),
        out_shape=jax.ShapeDtypeStruct((m, n), BF16),
        scratch_shapes=[pltpu.VMEM((tm, d), BF16)],
        compiler_params=_compiler_params(("parallel", "arbitrary"), vmem),
        name="qkv",
    )(x, norm_g, mods, w_qkv, head_gain)


LOG2_E = math.log2(math.e)
QKV_KV_BLOCK = 0
QKV_Q_BLOCK = 2
_NT_DIMS = (((1,), (1,)), ((), ()))


def _na_kernel(*refs, nb, heads, head_dim, step_blocks):
    refs = refs[2:]
    q_ref = refs[0]
    kv_refs = refs[1:1 + step_blocks]
    rest = refs[1 + step_blocks:]
    kvc_ref = rest[0]
    tbl_refs = rest[1:1 + step_blocks]
    o_ref = rest[1 + step_blocks]
    qw = q_ref.shape[0] // step_blocks
    d = q_ref.shape[1]

    def scores(u):
        blk, h = divmod(u, heads)
        sl = slice(h * head_dim, (h + 1) * head_dim)
        q = q_ref[blk * qw:(blk + 1) * qw, sl]
        k = kv_refs[blk][:, :, sl].reshape(nk, head_dim)
        s = lax.dot_general(q, k, _NT_DIMS, preferred_element_type=F32) + tbl_refs[blk][h]
        sc = lax.dot_general(q, kvc_ref[:, sl], _NT_DIMS, preferred_element_type=F32)
        return s, sc

    nk = nb * kv_refs[0].shape[1]
    ones_lat = jnp.ones((nk, head_dim), BF16)
    ones_ctx = jnp.ones((kvc_ref.shape[0], head_dim), BF16)

    def attend(u, s_sc):
        blk, h = divmod(u, heads)
        s, sc = s_sc
        sl = slice(h * head_dim, (h + 1) * head_dim)
        vsl = slice(d + h * head_dim, d + (h + 1) * head_dim)
        v = jnp.concatenate([kv_refs[blk][:, :, vsl].reshape(nk, head_dim), ones_lat], axis=1)
        vc = jnp.concatenate([kvc_ref[:, vsl], ones_ctx], axis=1)
        m = jnp.maximum(jnp.max(s, axis=-1, keepdims=True), jnp.max(sc, axis=-1, keepdims=True))
        p = jnp.exp2(s - m).astype(BF16)
        pc = jnp.exp2(sc - m).astype(BF16)
        o = (jnp.dot(p, v, preferred_element_type=F32) + jnp.dot(pc, vc, preferred_element_type=F32))
        o_ref[blk * qw:(blk + 1) * qw, sl] = (
            o[:, :head_dim] * (1.0 / o[:, head_dim:head_dim + 1])).astype(o_ref.dtype)

    _skewed_subtiles(step_blocks * heads, scores, attend, ahead=2)


def _na_geometry(rows, qrows):
    kh = min(WIN_H, rows)
    nb = min(kh + qrows - 1, rows)
    n_blk = rows // qrows
    band0s, keys, types, row_rel = [], {}, [], []
    for blk in range(n_blk):
        r0 = blk * qrows
        q_row = r0 + np.arange(qrows)
        r_start = np.clip(q_row - kh // 2, 0, rows - kh)
        band0 = min(int(np.clip(r0 - kh // 2, 0, rows - kh)), rows - nb)
        k_row = band0 + np.arange(nb)
        row_ok = (k_row[None, :] >= r_start[:, None]) & (k_row[None, :] < r_start[:, None] + kh)
        rel = np.where(row_ok, k_row[None, :] - q_row[:, None] + WIN_H - 1, -1)
        key = rel.tobytes()
        if key not in keys:
            keys[key] = len(row_rel)
            row_rel.append(rel)
        types.append(keys[key])
        band0s.append(band0)
    return nb, n_blk, np.array(band0s, np.int32), np.array(types, np.int32), row_rel


def _na_bias_table(rpb, row_rel):
    layers, heads, n_dr, n_dc = rpb.shape
    period = 2 * GRID_W
    assert n_dc <= period
    r = jnp.pad(rpb, ((0, 0), (0, 0), (0, 0), (0, period - n_dc)))
    skew = jnp.tile(r, (1, 1, 1, GRID_W))[..., :GRID_W * (period - 1)]
    skew = skew.reshape(layers, heads, n_dr, GRID_W, period - 1)
    toep = skew[..., WIN_W - 1:WIN_W - 1 + GRID_W]
    q_col = np.arange(GRID_W)
    c_start = np.clip(q_col - WIN_W // 2, 0, GRID_W - WIN_W)
    col_ok = (q_col[None, :] >= c_start[:, None]) & (q_col[None, :] < c_start[:, None] + WIN_W)
    toep = jnp.where(col_ok, toep, -jnp.inf)
    masked = jnp.full((layers, heads, GRID_W, GRID_W), -jnp.inf, F32)
    tables = []
    for rel in row_rel:
        q_blocks = []
        for qr in range(rel.shape[0]):
            q_blocks.append(jnp.concatenate(
                [toep[:, :, rel[qr, kr]] if rel[qr, kr] >= 0 else masked for kr in range(rel.shape[1])], axis=-1))
        tables.append(jnp.concatenate(q_blocks, axis=-2))
    return jnp.stack(tables, axis=1).astype(F32)


def _na_attention(qkv, qkv_ctx, tables, layer, geometry, bn, rows, d, head_dim):
    nb, n_blk, band0s, types, _ = geometry
    qw, nk = tables.shape[-2:]
    ctx_len = qkv_ctx.shape[1]
    band_tbl = jnp.asarray(band0s)
    type_tbl = jnp.asarray(types)
    kv_view = qkv.reshape(bn * rows, GRID_W, 3 * d)

    spb = _tile(n_blk, NA_STEP_BLOCKS)
    n_steps = n_blk // spb

    def kv_spec(blk):
        return pl.BlockSpec((pl.Element(nb), pl.Element(GRID_W), pl.Element(2 * d)),
                            lambda b, i, band, typ: (b * rows + band[i * spb + blk], 0, QKV_KV_BLOCK * 2 * d))

    def tbl_spec(blk):
        return pl.BlockSpec((None, None, NA_HEADS, qw, nk),
                            lambda b, i, band, typ: (layer, typ[i * spb + blk], 0, 0, 0))

    in_specs = ([pl.BlockSpec((spb * qw, d), lambda b, i, band, typ: (b * n_steps + i, QKV_Q_BLOCK))]
                + [kv_spec(blk) for blk in range(spb)]
                + [pl.BlockSpec((None, ctx_len, 2 * d), lambda b, i, band, typ: (b, 0, QKV_KV_BLOCK))]
                + [tbl_spec(blk) for blk in range(spb)])
    vmem = (2 * (spb * (2 * qw * d * 2 + 2 * nk * d * 2 + NA_HEADS * qw * nk * 4) + 2 * ctx_len * d * 2)
            + 2 * VMEM_TEMP_BYTES)
    grid_spec = pltpu.PrefetchScalarGridSpec(
        num_scalar_prefetch=2,
        grid=(bn, n_steps),
        in_specs=in_specs,
        out_specs=pl.BlockSpec((spb * qw, d), lambda b, i, band, typ: (b * n_steps + i, 0)),
    )
    return pl.pallas_call(
        functools.partial(_na_kernel, nb=nb, heads=NA_HEADS, head_dim=head_dim, step_blocks=spb),
        grid_spec=grid_spec,
        out_shape=jax.ShapeDtypeStruct((bn * rows * GRID_W, d), BF16),
        compiler_params=_compiler_params(("parallel", "arbitrary"), vmem),
        name="na_attention",
    )(band_tbl, type_tbl, qkv, *([kv_view] * spb), qkv_ctx, *([tables] * spb))


def _ctx_attn_kernel(q_ref, k_ref, v_ref, o_ref, *, heads, head_dim):
    for h in range(heads):
        sl = slice(h * head_dim, (h + 1) * head_dim)
        s = lax.dot_general(q_ref[:, sl], k_ref[:, sl], _NT_DIMS, preferred_element_type=F32)
        m = jnp.max(s, axis=-1, keepdims=True)
        p = jnp.exp2(s - m)
        l = jnp.sum(p, axis=-1, keepdims=True)
        o = jnp.dot(p.astype(BF16), v_ref[:, sl], preferred_element_type=F32)
        o_ref[:, sl] = (o * (1.0 / l)).astype(o_ref.dtype)


def _ctx_attention(qkv_ctx, d, head_dim):
    bn, ctx_len, _ = qkv_ctx.shape
    spec = lambda col: pl.BlockSpec((None, ctx_len, d), lambda b: (b, 0, col))
    return pl.pallas_call(
        functools.partial(_ctx_attn_kernel, heads=NA_HEADS, head_dim=head_dim),
        grid=(bn,),
        in_specs=[spec(QKV_Q_BLOCK), spec(2 * QKV_KV_BLOCK), spec(2 * QKV_KV_BLOCK + 1)],
        out_specs=pl.BlockSpec((None, ctx_len, d), lambda b: (b, 0, 0)),
        out_shape=jax.ShapeDtypeStruct((bn, ctx_len, d), BF16),
        compiler_params=_compiler_params(("arbitrary",), 8 * ctx_len * d * 2 + 2 * VMEM_TEMP_BYTES),
        name="ctx_attention",
    )(qkv_ctx, qkv_ctx, qkv_ctx)


HALO = V7X_SUBLANES_F32


def _shift_rows(u, tm):
    n = u.shape[0]
    prev = pltpu.roll(u, 1, 0)[HALO:HALO + tm]
    nxt = pltpu.roll(u, n - 1, 0)[HALO:HALO + tm]
    return prev, nxt


def _conv3(u, cw, rows, seq_edges):
    prev, nxt = _shift_rows(u, rows)
    if seq_edges:
        row = lax.broadcasted_iota(jnp.int32, (rows, 1), 0)
        prev = jnp.where(row == 0, 0.0, prev)
        nxt = jnp.where(row == rows - 1, 0.0, nxt)
    return prev * cw[0:1] + u[HALO:HALO + rows] * cw[1:2] + nxt * cw[2:3]


def _gated_conv_kernel(*refs, kind, tiles_per_seq, mod_rows, sub_rows, seq_edges, side_cast):
    parts = 2 if kind == "ffn" else 3
    x_ref, xp_ref, xn_ref, g_ref, mod_ref = refs[:5]
    w1_refs = refs[5:5 + parts]
    conv_ref = refs[5 + parts]
    rest = refs[6 + parts:]
    tc = w1_refs[0].shape[1]
    col = lambda p: slice(p * tc, (p + 1) * tc)
    if side_cast is None:
        w2_ref, o_ref, hs_ref = rest
    else:
        w2_ref, up_f32_ref, down_f32_ref, o_ref, up_bf16_ref, down_bf16_ref, hs_ref = rest
    i = pl.program_id(0)
    j = pl.program_id(1)
    tm = x_ref.shape[0]
    r_shift, r_scale, r_gate = mod_rows

    if side_cast is not None:
        n_up, n_down, down_rows = side_cast
        step = i * pl.num_programs(1) + j

        @pl.when(step < n_up)
        def _():
            _cast_block(up_f32_ref, up_bf16_ref, 0, None)

        @pl.when((step >= n_up) & (step < n_up + n_down))
        def _():
            _cast_block(down_f32_ref, down_bf16_ref, (step - n_up) * down_f32_ref.shape[0], down_rows)

    res_gate = mod_ref[r_gate:r_gate + 1, :]
    n_sub = tm // sub_rows

    def first_matmuls(r):
        lhs = hs_ref[r * sub_rows:r * sub_rows + sub_rows + 2 * HALO, :]
        return [jnp.dot(lhs, w[...], preferred_element_type=F32) for w in w1_refs]

    def mid_and_second_matmul(r, u):
        if kind == "ffn":
            gate, up = [_conv3(u[p], conv_ref[0:3, col(p)], sub_rows, seq_edges) + conv_ref[3:4, col(p)]
                        for p in range(2)]
            a = gate * jax.nn.sigmoid(gate) * up
        else:
            a = u[0][HALO:HALO + sub_rows] * _conv3(u[1] * u[2], conv_ref[0:3, :], sub_rows, seq_edges)
        o_ref[r * sub_rows:(r + 1) * sub_rows, :] += res_gate * jnp.dot(a.astype(BF16), w2_ref[...],
                                                                       preferred_element_type=F32)

    @pl.when(j == 0)
    def _():
        gain = g_ref[...] * (1.0 + mod_ref[r_scale:r_scale + 1, :])
        shift = mod_ref[r_shift:r_shift + 1, :]
        pos = i % tiles_per_seq
        sources = [(xp_ref, pos == 0), (x_ref, None), (xn_ref, pos == tiles_per_seq - 1)]
        total = tm + 2 * HALO

        def norm_and_first_matmuls(r):
            lo = 0 if r == 0 else r * sub_rows + 2 * HALO
            hi = total if r == n_sub - 1 else (r + 1) * sub_rows + 2 * HALO
            _store_norm_mod(hs_ref, sources, gain, shift, copy_ref=o_ref, row_range=(lo, hi))
            return first_matmuls(r)

        _skewed_subtiles(n_sub, norm_and_first_matmuls, mid_and_second_matmul)

    @pl.when(j > 0)
    def _():
        _skewed_subtiles(n_sub, first_matmuls, mid_and_second_matmul)


def _gated_conv(kind, x, seq_len, norm_g, mods, layer, row_fn, mod_rows, w1, conv, w2, tc, tm_pref,
                sub_pref=SUB_ROWS, cast_next=None):
    m, d = x.shape
    parts = 2 if kind == "ffn" else 3
    cparts = 2 if kind == "ffn" else 1
    n_chunks = w2.shape[0] // tc
    assert w1.shape == (parts * n_chunks, d, tc) and conv.shape == (n_chunks, CONV_ROWS, cparts * tc)
    if seq_len >= tm_pref:
        tm = _tile(seq_len, tm_pref)
        sub_rows = _tile(tm, sub_pref)
        tiles_per_seq, seq_edges = seq_len // tm, False
    else:
        tm = seq_len * _tile(m // seq_len, tm_pref // seq_len)
        sub_rows = seq_len
        tiles_per_seq, seq_edges = 1, True
    assert sub_rows % V7X_SUBLANES_BF16 == 0
    hb = tm // HALO
    n_hblk = m // HALO
    other = 2 * (d * parts * tc * 2 + tc * d * 2) + (tm + 2 * HALO) * d * 2
    n_steps = (m // tm) * n_chunks
    side_cast, side_specs_in, side_specs_out, side_shapes, side_operands = None, [], [], [], []
    if cast_next is not None:
        w_up, w_down, nl = cast_next
        f, fp = w_down.shape[1], n_chunks * tc
        up_rows = min(t for t in range(V7X_SUBLANES_BF16, d + 1, V7X_SUBLANES_BF16)
                      if d % t == 0 and 2 * (d // t) <= 3 * n_steps // 4)
        n_up = 2 * (d // up_rows)
        down_rows = min(t for t in range(V7X_SUBLANES_BF16, fp + 1, V7X_SUBLANES_BF16)
                        if fp % t == 0 and fp // t <= n_steps - n_up)
        n_down = fp // down_rows
        last_down_block = (f - 1) // down_rows
        up_item = lambda i, j: jnp.minimum(i * n_chunks + j, n_up - 1)
        down_item = lambda i, j: jnp.clip(i * n_chunks + j - n_up, 0, n_down - 1)
        side_specs_in = [
            pl.BlockSpec((None, up_rows, f), lambda i, j: (nl, up_item(i, j) // 2, up_item(i, j) % 2)),
            pl.BlockSpec((None, down_rows, d),
                         lambda i, j: (nl, jnp.minimum(down_item(i, j), last_down_block), 0)),
        ]
        side_specs_out = [
            pl.BlockSpec((n_chunks, up_rows, tc), lambda i, j: (up_item(i, j) % 2, up_item(i, j) // 2, 0)),
            pl.BlockSpec((down_rows, d), lambda i, j: (down_item(i, j), 0)),
        ]
        side_shapes = [jax.ShapeDtypeStruct((2 * n_chunks, d, tc), BF16), jax.ShapeDtypeStruct((fp, d), BF16)]
        side_operands = [w_up, w_down]
        side_cast = (n_up, n_down, f)
        other += 2 * (up_rows * f * 4 + up_rows * fp * 2 + down_rows * d * 6)
    x_bufs, out_bufs = _row_tile_buffers(tm * d * 4, other)
    vmem = (x_bufs + out_bufs) * tm * d * 4 + other + VMEM_TEMP_BYTES
    in_specs = [
        pl.BlockSpec((tm, d), lambda i, j: (i, 0), pipeline_mode=pl.Buffered(x_bufs)),
        pl.BlockSpec((HALO, d), lambda i, j: (jnp.maximum(i * hb - 1, 0), 0)),
        pl.BlockSpec((HALO, d), lambda i, j: (jnp.minimum((i + 1) * hb, n_hblk - 1), 0)),
        pl.BlockSpec((1, d), lambda i, j: (0, 0)),
        _mod_spec(layer, lambda i: row_fn(i, tm), d),
    ]
    in_specs += [pl.BlockSpec((None, d, tc), functools.partial(lambda p, i, j: (p * n_chunks + j, 0, 0), p))
                 for p in range(parts)]
    in_specs += [pl.BlockSpec((None, CONV_ROWS, cparts * tc), lambda i, j: (j, 0, 0)),
                 pl.BlockSpec((tc, d), lambda i, j: (j, 0))]
    operands = [x, x, x, norm_g, mods] + [w1] * parts + [conv, w2]
    out_spec = pl.BlockSpec((tm, d), lambda i, j: (i, 0), pipeline_mode=pl.Buffered(out_bufs))
    out_shape = jax.ShapeDtypeStruct((m, d), F32)
    result = pl.pallas_call(
        functools.partial(_gated_conv_kernel, kind=kind, tiles_per_seq=tiles_per_seq, mod_rows=mod_rows,
                          sub_rows=sub_rows, seq_edges=seq_edges, side_cast=side_cast),
        grid=(m // tm, n_chunks),
        in_specs=in_specs + side_specs_in,
        out_specs=[out_spec] + side_specs_out if side_cast else out_spec,
        out_shape=[out_shape] + side_shapes if side_cast else out_shape,
        scratch_shapes=[pltpu.VMEM((tm + 2 * HALO, d), BF16)],
        compiler_params=_compiler_params(("arbitrary" if side_cast else "parallel", "arbitrary"), vmem),
        name=kind,
    )(*operands, *side_operands)
    return tuple(result) if side_cast else result


def _gelu_tanh(x):
    return jax.nn.gelu(x, approximate=True)


def _gmlp_kernel(x_ref, g_ref, mod_ref, win_ref, vg_ref, ws_ref, bs_ref, wout_ref, o_ref,
                 hs_ref, v_ref, ssq_ref, *, n_chunks, group_dim, width, sub_rows):
    j = pl.program_id(1)
    tm = x_ref.shape[0]
    tc = win_ref.shape[1]
    n_sub = tm // sub_rows

    def in_matmul(r):
        return jnp.dot(hs_ref[r * sub_rows:(r + 1) * sub_rows, :], win_ref[...], preferred_element_type=F32)

    def keep_v(r, y):
        rows = slice(r * sub_rows, (r + 1) * sub_rows)
        t = _gelu_tanh(y)
        v_ref[j, rows, :] = t.astype(v_ref.dtype)
        ssq_ref[rows, :] += jnp.sum(t * t, axis=-1, keepdims=True)

    @pl.when(j == 0)
    def _():
        gain = g_ref[...] * (1.0 + mod_ref[1:2, :])
        ssq_ref[...] = jnp.zeros_like(ssq_ref)

        def norm_and_matmul(r):
            _store_norm_mod(hs_ref, [(x_ref, None)], gain, mod_ref[0:1, :], copy_ref=o_ref,
                            row_range=(r * sub_rows, (r + 1) * sub_rows))
            return in_matmul(r)

        _skewed_subtiles(n_sub, norm_and_matmul, keep_v)

    @pl.when((j > 0) & (j < n_chunks))
    def _():
        _skewed_subtiles(n_sub, in_matmul, keep_v)

    @pl.when(j >= n_chunks)
    def _():
        jj = j - n_chunks
        vg = vg_ref[...]
        res_gate = mod_ref[2:3, :]
        bs = bs_ref[...]
        gpc = tc // group_dim
        ws = [ws_ref[jj * gpc + gi] for gi in range(gpc)]

        def gate_and_project(r, y):
            rows = slice(r * sub_rows, (r + 1) * sub_rows)
            inv = lax.rsqrt(ssq_ref[rows, :] * (1.0 / width) + EPS)
            vn = (v_ref[jj, rows, :].astype(F32) * inv * vg).astype(BF16)
            cols = []
            for gi in range(gpc):
                csl = slice(gi * group_dim, (gi + 1) * group_dim)
                cols.append(jnp.concatenate(
                    [jnp.dot(ws[gi], vn[c * GM_CHUNK:(c + 1) * GM_CHUNK, csl], preferred_element_type=F32)
                     + bs[:, csl] for c in range(sub_rows // GM_CHUNK)], axis=0))
            sv = jnp.concatenate(cols, axis=1)
            o_ref[rows, :] += res_gate * jnp.dot((_gelu_tanh(y) * sv).astype(BF16), wout_ref[...],
                                                 preferred_element_type=F32)

        _skewed_subtiles(n_sub, in_matmul, gate_and_project)


def _gmlp(x, norm_g, mods, layer, row_fn, w_in, v_g, w_s, b_s_cols, w_out, tm_pref):
    m, d = x.shape
    width = w_out.shape[0]
    group_dim = width // GM_GROUPS
    tc = w_in.shape[2]
    n_chunks = width // tc
    assert w_in.shape[0] == 2 * n_chunks and tc % group_dim == 0
    tm = _tile(m, tm_pref)
    sub_rows = _tile(tm, SUB_ROWS)
    assert sub_rows % GM_CHUNK == 0
    other = (2 * (d * tc * 2 + tc * d * 2 + GM_CHUNK * tc * 4) + tm * d * 2 + tm * width * 2
             + GM_GROUPS * GM_CHUNK * GM_CHUNK * 2 + tm * V7X_LANES * 4)
    x_bufs, out_bufs = _row_tile_buffers(tm * d * 4, other)
    vmem = (x_bufs + out_bufs) * tm * d * 4 + other + VMEM_TEMP_BYTES
    return pl.pallas_call(
        functools.partial(_gmlp_kernel, n_chunks=n_chunks, group_dim=group_dim, width=width,
                          sub_rows=sub_rows),
        grid=(m // tm, 2 * n_chunks),
        in_specs=[
            pl.BlockSpec((tm, d), lambda i, j: (i, 0), pipeline_mode=pl.Buffered(x_bufs)),
            pl.BlockSpec((1, d), lambda i, j: (0, 0)),
            _mod_spec(layer, lambda i: row_fn(i, tm), d),
            pl.BlockSpec((None, d, tc),
                         lambda i, j: (jnp.where(j < n_chunks, j + n_chunks, j - n_chunks), 0, 0)),
            pl.BlockSpec((1, tc), lambda i, j: (0, jnp.maximum(j - n_chunks, 0))),
            pl.BlockSpec((GM_GROUPS, GM_CHUNK, GM_CHUNK), lambda i, j: (0, 0, 0)),
            pl.BlockSpec((GM_CHUNK, tc), lambda i, j: (0, jnp.maximum(j - n_chunks, 0))),
            pl.BlockSpec((tc, d), lambda i, j: (jnp.maximum(j - n_chunks, 0), 0)),
        ],
        out_specs=pl.BlockSpec((tm, d), lambda i, j: (i, 0), pipeline_mode=pl.Buffered(out_bufs)),
        out_shape=jax.ShapeDtypeStruct((m, d), F32),
        scratch_shapes=[pltpu.VMEM((tm, d), BF16), pltpu.VMEM((n_chunks, tm, tc), BF16), pltpu.VMEM((tm, 1), F32)],
        compiler_params=_compiler_params(("parallel", "arbitrary"), vmem),
        name="gmlp",
    )(x, norm_g, mods, w_in, v_g, w_s, b_s_cols, w_out)


CAST_BLOCK_BYTES = 6 << 20


def _cast_kernel(x_ref, o_ref, *, valid_rows):
    _cast_block(x_ref, o_ref, pl.program_id(0) * x_ref.shape[0], valid_rows)


def _cast_block(x_ref, o_ref, row0, valid_rows):
    rows, f = x_ref.shape
    x = x_ref[...]
    if valid_rows is not None:
        row = row0 + lax.broadcasted_iota(jnp.int32, (rows, 1), 0)
        x = jnp.where(row < valid_rows, x, 0.0)
    x = x.astype(o_ref.dtype)
    if len(o_ref.shape) == 2:
        o_ref[...] = x
        return
    n_chunks, _, tc = o_ref.shape
    for c in range(n_chunks):
        width = max(0, min(tc, f - c * tc))
        if width:
            o_ref[c, :, :width] = x[:, c * tc:c * tc + width]
        if width < tc:
            o_ref[c, :, width:] = jnp.zeros((rows, tc - width), o_ref.dtype)


def _cast_weight(w, idx, parts=1, fp=None, col_chunk=None, rows_out=None, row_block=None):
    _, r, c = w.shape
    f = c // parts
    fp = f if fp is None else fp
    rows_out = r if rows_out is None else rows_out
    assert parts == 1 or f % V7X_LANES == 0
    assert col_chunk is not None or (parts == 1 and fp == f)
    padded_rows = rows_out > r
    if row_block is None:
        row_pref = max(V7X_SUBLANES_BF16, CAST_BLOCK_BYTES // (4 * f))
        row_block = max(t for t in range(V7X_SUBLANES_BF16, row_pref + 1, V7X_SUBLANES_BF16)
                        if rows_out % t == 0)
    assert rows_out % row_block == 0 and row_block % V7X_SUBLANES_BF16 == 0
    assert r % row_block == 0 or (padded_rows and rows_out - row_block < r)
    if col_chunk is None:
        out_spec = pl.BlockSpec((row_block, f), lambda i, p: (i, 0))
        out_shape = (rows_out, f)
    else:
        assert fp % col_chunk == 0 and col_chunk % V7X_LANES == 0 and not padded_rows
        n_chunks = fp // col_chunk
        out_spec = pl.BlockSpec((n_chunks, row_block, col_chunk), lambda i, p: (p, i, 0))
        out_shape = (parts * n_chunks, r, col_chunk)
    return pl.pallas_call(
        functools.partial(_cast_kernel, valid_rows=r if padded_rows else None),
        grid=(rows_out // row_block, parts),
        in_specs=[pl.BlockSpec((None, row_block, f), lambda i, p: (idx, i, p))],
        out_specs=out_spec,
        out_shape=jax.ShapeDtypeStruct(out_shape, BF16),
        compiler_params=_compiler_params(("parallel", "arbitrary"),
                                         2 * row_block * (4 * f + 2 * fp) + VMEM_TEMP_BYTES),
        name="cast_weight",
    )(w)


def _pad_halves(a, f, fp):
    pad = lambda h: jnp.pad(h, ((0, 0), (0, fp - f)))
    return jnp.concatenate([pad(a[:, :f]), pad(a[:, f:])], axis=1)


CONV_ROWS = V7X_SUBLANES_F32


def _conv_params(taps, bias, cparts, tc):
    width = taps.shape[1]
    rows = [taps, jnp.zeros((1, width), F32) if bias is None else bias,
            jnp.zeros((CONV_ROWS - taps.shape[0] - 1, width), F32)]
    a = jnp.concatenate(rows, axis=0).reshape(CONV_ROWS, cparts, width // (cparts * tc), tc)
    return jnp.transpose(a, (2, 0, 1, 3)).reshape(width // (cparts * tc), CONV_ROWS, cparts * tc)


def _prep_ffn(w_up, conv_w, conv_b, w_down, layer, tc, cast_weights):
    f = w_down.shape[1]
    fp = -(-f // tc) * tc
    conv = _conv_params(_pad_halves(conv_w[layer], f, fp), _pad_halves(conv_b[layer][None, :], f, fp), 2, tc)
    if cast_weights is None:
        cast_weights = (_cast_weight(w_up, layer, parts=2, fp=fp, col_chunk=tc),
                        _cast_weight(w_down, layer, rows_out=fp))
    return cast_weights[0], conv, cast_weights[1]


def kernel(x, c, ctx, c_ctx, norm_mix_g, norm_ffn_g, w_ada, b_ada, na_w_qkv, na_q_g, na_k_g, na_rpb, na_w_o,
           gm_w_in, gm_v_g, gm_w_s, gm_b_s, gm_w_out, sc_w_in, sc_conv_w, sc_w_out,
           ffn_w_up, ffn_conv_w, ffn_conv_b, ffn_w_down):
    bn, seq, d = x.shape
    ctx_len = ctx.shape[1]
    depth = w_ada.shape[0]
    head_dim = d // NA_HEADS
    rows = seq // GRID_W
    assert bn + 1 <= COND_ROWS and seq % GRID_W == 0 and ctx_len % GM_CHUNK == 0

    cond = jnp.concatenate([c, c_ctx[None, :], jnp.zeros((COND_ROWS - bn - 1, d), F32)], axis=0)
    mods = _adaln(cond, w_ada, b_ada).reshape(depth, COND_ROWS, N_MOD, d)

    lat_row = lambda i, tm: i // (seq // tm)
    ctx_row = lambda i, tm: bn

    ffn_hidden = ffn_w_down.shape[1]
    ffn_tc = HIDDEN_CHUNK if ffn_hidden >= HIDDEN_CHUNK else V7X_LANES
    sc_tc = _tile(d, HIDDEN_CHUNK)
    gm_width = gm_w_out.shape[1]
    gm_tc = max(gm_width // GM_GROUPS, _tile(gm_width, HIDDEN_CHUNK))
    proj_tn = _tile(d, PROJ_COLS)
    qkv_tn = _tile(d, QKV_COLS)
    lat_tm = ROW_TILE
    ctx_tm = min(ROW_TILE, bn * ctx_len)
    lat_sub = HALO_SUB_ROWS

    xl = x.reshape(bn * seq, d)
    xc = ctx.reshape(bn * ctx_len, d)
    ffn_weights = None
    na_geometry = _na_geometry(rows, min(NA_QROWS, rows))
    na_tables = _na_bias_table(na_rpb * LOG2_E, na_geometry[-1])
    for i in range(depth):
        last = i == depth - 1
        mixer, jx = i % N_MIXERS, i // N_MIXERS
        g_mix = norm_mix_g[i][None, :]
        g_ffn = norm_ffn_g[i][None, :]
        if mixer == 0:
            w_qkv = _cast_weight(na_w_qkv, jx, col_chunk=qkv_tn)
            w_o = _cast_weight(na_w_o, jx, col_chunk=proj_tn)
            q_gain = na_q_g[jx] * (head_dim ** -0.5 * LOG2_E)
            head_gain = jnp.concatenate([jnp.tile(q_gain, NA_HEADS), jnp.tile(na_k_g[jx], NA_HEADS),
                                         jnp.ones((d,), F32)])[None, :]
            qkv = _qkv(xl, g_mix, mods, i, lat_row, w_qkv, head_gain, head_dim)
            qkv_c = _qkv(xc, g_mix, mods, i, ctx_row, w_qkv, head_gain, head_dim).reshape(bn, ctx_len, 3 * d)
            att = _na_attention(qkv, qkv_c, na_tables, jx, na_geometry, bn, rows, d, head_dim)
            xl_new = _proj_res(att, w_o, xl, mods, i, lat_row, 2)
            if not last:
                att_c = _ctx_attention(qkv_c, d, head_dim).reshape(bn * ctx_len, d)
                xc = _proj_res(att_c, w_o, xc, mods, i, ctx_row, 2)
            xl = xl_new
        elif mixer == 1:
            w_in = _cast_weight(gm_w_in, jx, col_chunk=gm_tc)
            w_out = _cast_weight(gm_w_out, jx)
            w_s = gm_w_s[jx].astype(BF16)
            width = w_out.shape[0]
            v_g = gm_v_g[jx][None, :]
            b_s_cols = jnp.repeat(gm_b_s[jx].T, width // GM_GROUPS, axis=1)
            xl = _gmlp(xl, g_mix, mods, i, lat_row, w_in, v_g, w_s, b_s_cols, w_out, lat_tm)
            if not last:
                xc = _gmlp(xc, g_mix, mods, i, ctx_row, w_in, v_g, w_s, b_s_cols, w_out, ctx_tm)
        else:
            w1 = _cast_weight(sc_w_in, jx, parts=3, col_chunk=sc_tc)
            conv, w2 = _conv_params(sc_conv_w[jx], None, 1, sc_tc), _cast_weight(sc_w_out, jx)
            xl = _gated_conv("sc", xl, seq, g_mix, mods, i, lat_row, (0, 1, 2), w1, conv, w2, sc_tc, lat_tm,
                             sub_pref=lat_sub)
            if not last:
                xc = _gated_conv("sc", xc, ctx_len, g_mix, mods, i, ctx_row, (0, 1, 2), w1, conv, w2, sc_tc,
                                 ctx_tm)
        w1, conv, w2 = _prep_ffn(ffn_w_up, ffn_conv_w, ffn_conv_b, ffn_w_down, i, ffn_tc, ffn_weights)
        if last:
            xl = _gated_conv("ffn", xl, seq, g_ffn, mods, i, lat_row, (3, 4, 5), w1, conv, w2, ffn_tc, lat_tm,
                             sub_pref=lat_sub)
        else:
            xl, *ffn_weights = _gated_conv("ffn", xl, seq, g_ffn, mods, i, lat_row, (3, 4, 5), w1, conv, w2,
                                           ffn_tc, lat_tm, sub_pref=lat_sub,
                                           cast_next=(ffn_w_up, ffn_w_down, i + 1))
        if not last:
            xc = _gated_conv("ffn", xc, ctx_len, g_ffn, mods, i, ctx_row, (3, 4, 5), w1, conv, w2, ffn_tc,
                             ctx_tm)
    return xl.reshape(bn, seq, d)
```

```python
import functools
import math

import numpy as np
import jax
import jax.numpy as jnp
from jax import lax
from jax.experimental import pallas as pl
from jax.experimental.pallas import tpu as pltpu

GRID_W = 64
NA_HEADS = 16
WIN_H = 8
WIN_W = 16
GM_GROUPS = 16
GM_CHUNK = 128
N_MOD = 6
N_MIXERS = 3
EPS = 1e-6

V7X_VMEM_BYTES = 64 * 1024 * 1024
V7X_SUBLANES_F32 = 8
V7X_SUBLANES_BF16 = 16
V7X_LANES = 128
VMEM_LIMIT_BYTES = V7X_VMEM_BYTES - (4 << 20)
VMEM_TEMP_BYTES = 6 << 20

F32 = jnp.float32
BF16 = jnp.bfloat16

ROW_TILE = 1024
SUB_ROWS = 256
HALO_SUB_ROWS = 512
HIDDEN_CHUNK = 512
PROJ_COLS = 1024
QKV_COLS = 2048
ADALN_COLS = 1024
NA_QROWS = 2
NA_STEP_BLOCKS = 2
COND_ROWS = V7X_SUBLANES_F32


def _compiler_params(semantics, vmem_bytes):
    return pltpu.CompilerParams(dimension_semantics=semantics,
                                vmem_limit_bytes=int(min(vmem_bytes, VMEM_LIMIT_BYTES)))


def _tile(n, pref):
    t = min(n, pref)
    while n % t:
        t -= 1
    return t


def _row_tile_buffers(tile_bytes, other_bytes):
    for x_bufs, out_bufs in ((2, 2), (1, 2), (1, 1)):
        if (x_bufs + out_bufs) * tile_bytes + other_bytes + VMEM_TEMP_BYTES <= VMEM_LIMIT_BYTES:
            break
    return x_bufs, out_bufs


def _norm_mod(x, gain, shift):
    ms = jnp.mean(x * x, axis=-1, keepdims=True)
    return x * lax.rsqrt(ms + EPS) * gain + shift


NORM_CHUNK = 16


def _store_norm_mod(hs_ref, sources, gain, shift, copy_ref=None, row_range=None):
    bounds = np.cumsum([0] + [ref.shape[0] for ref, _ in sources])
    total = int(bounds[-1])
    assert hs_ref.shape[0] == total
    r_lo, r_hi = (0, total) if row_range is None else row_range
    assert r_lo % NORM_CHUNK == 0
    for c0 in range(r_lo, r_hi, NORM_CHUNK):
        c1 = min(c0 + NORM_CHUNK, r_hi)
        pieces = []
        for (ref, zero_pred), b0, b1 in zip(sources, bounds[:-1], bounds[1:]):
            lo, hi = max(c0, int(b0)), min(c1, int(b1))
            if lo >= hi:
                continue
            rows = ref[lo - int(b0):hi - int(b0), :]
            if copy_ref is not None and zero_pred is None:
                copy_ref[lo - int(b0):hi - int(b0), :] = rows
            h = _norm_mod(rows, gain, shift)
            pieces.append(h if zero_pred is None else jnp.where(zero_pred, 0.0, h))
        h = pieces[0] if len(pieces) == 1 else jnp.concatenate(pieces, axis=0)
        hs_ref[c0:c1, :] = h.astype(BF16)


def _adaln_kernel(c_ref, w_ref, b_ref, o_ref):
    c = c_ref[...]
    a = (c * jax.nn.sigmoid(c)).astype(BF16)
    o_ref[...] = jnp.dot(a, w_ref[...].astype(BF16), preferred_element_type=F32) + b_ref[...]


def _adaln(cond, w_ada, b_ada):
    depth, d, n = w_ada.shape
    tn = _tile(n, ADALN_COLS)
    return pl.pallas_call(
        _adaln_kernel,
        grid=(depth, n // tn),
        in_specs=[
            pl.BlockSpec((COND_ROWS, d), lambda l, j: (0, 0)),
            pl.BlockSpec((None, d, tn), lambda l, j: (l, 0, j)),
            pl.BlockSpec((None, 1, tn), lambda l, j: (l, 0, j)),
        ],
        out_specs=pl.BlockSpec((None, COND_ROWS, tn), lambda l, j: (l, 0, j)),
        out_shape=jax.ShapeDtypeStruct((depth, COND_ROWS, n), F32),
        compiler_params=_compiler_params(("arbitrary", "arbitrary"), 3 * d * tn * 4 + VMEM_TEMP_BYTES),
        name="adaln",
    )(cond, w_ada, b_ada.reshape(depth, 1, n))


def _mod_spec(layer, row_fn, width, col_fn=None):
    if col_fn is None:
        return pl.BlockSpec((None, None, N_MOD, width), lambda i, j: (layer, row_fn(i), 0, 0))
    return pl.BlockSpec((None, None, N_MOD, width), lambda i, j: (layer, row_fn(i), 0, col_fn(j)))


def _skewed_subtiles(n_sub, first, second, ahead=1):
    pending = [first(r) for r in range(min(ahead, n_sub))]
    for r in range(n_sub):
        if r + ahead < n_sub:
            pending.append(first(r + ahead))
        second(r, pending.pop(0))


def _proj_res_kernel(a_ref, w_ref, x_ref, mod_ref, o_ref, *, gate_row, sub_rows):
    gate = mod_ref[gate_row:gate_row + 1, :]

    def matmul(r):
        return jnp.dot(a_ref[r * sub_rows:(r + 1) * sub_rows, :], w_ref[...], preferred_element_type=F32)

    def residual(r, y):
        rows = slice(r * sub_rows, (r + 1) * sub_rows)
        o_ref[rows, :] = x_ref[rows, :] + gate * y

    _skewed_subtiles(o_ref.shape[0] // sub_rows, matmul, residual)


def _proj_res(a, w, x, mods, layer, row_fn, gate_row):
    m, k = a.shape
    n_tiles, _, tn = w.shape
    n = n_tiles * tn
    tm = _tile(m, ROW_TILE)
    vmem = 2 * (tm * k * 2 + k * tn * 2 + 2 * tm * tn * 4) + VMEM_TEMP_BYTES
    return pl.pallas_call(
        functools.partial(_proj_res_kernel, gate_row=gate_row, sub_rows=_tile(tm, SUB_ROWS)),
        grid=(m // tm, n // tn),
        in_specs=[
            pl.BlockSpec((tm, k), lambda i, j: (i, 0)),
            pl.BlockSpec((None, k, tn), lambda i, j: (j, 0, 0)),
            pl.BlockSpec((tm, tn), lambda i, j: (i, j)),
            _mod_spec(layer, lambda i: row_fn(i, tm), tn, lambda j: j),
        ],
        out_specs=pl.BlockSpec((tm, tn), lambda i, j: (i, j)),
        out_shape=jax.ShapeDtypeStruct((m, n), F32),
        compiler_params=_compiler_params(("parallel", "arbitrary"), vmem),
        name="proj_res",
    )(a, w, x, mods)


def _qkv_kernel(x_ref, g_ref, mod_ref, w_ref, hg_ref, o_ref, hs_ref, *, n_norm_tiles, head_dim, sub_rows):
    j = pl.program_id(1)
    tm, tn = o_ref.shape
    is_norm_tile = j < n_norm_tiles
    hg = hg_ref[...]

    def matmul(r):
        return jnp.dot(hs_ref[r * sub_rows:(r + 1) * sub_rows, :], w_ref[...], preferred_element_type=F32)

    def head_norm(r, y):
        rows = slice(r * sub_rows, (r + 1) * sub_rows)
        for hh in range(tn // head_dim):
            sl = slice(hh * head_dim, (hh + 1) * head_dim)
            t = y[:, sl]
            inv = lax.rsqrt(jnp.mean(t * t, axis=-1, keepdims=True) + EPS)
            o_ref[rows, sl] = (t * jnp.where(is_norm_tile, inv, 1.0) * hg[:, sl]).astype(o_ref.dtype)

    @pl.when(j == 0)
    def _():
        gain = g_ref[...] * (1.0 + mod_ref[1:2, :])

        def norm_and_matmul(r):
            _store_norm_mod(hs_ref, [(x_ref, None)], gain, mod_ref[0:1, :],
                            row_range=(r * sub_rows, (r + 1) * sub_rows))
            return matmul(r)

        _skewed_subtiles(tm // sub_rows, norm_and_matmul, head_norm)

    @pl.when(j > 0)
    def _():
        _skewed_subtiles(tm // sub_rows, matmul, head_norm)


def _qkv(x, norm_g, mods, layer, row_fn, w_qkv, head_gain, head_dim):
    m, d = x.shape
    n_tiles, _, tn = w_qkv.shape
    n = n_tiles * tn
    tm = _tile(m, ROW_TILE)
    assert (n // 3) % tn == 0 and tn % head_dim == 0
    vmem = 2 * (tm * d * 4 + d * tn * 2 + tm * tn * 2) + tm * d * 2 + 2 * VMEM_TEMP_BYTES
    return pl.pallas_call(
        functools.partial(_qkv_kernel, n_norm_tiles=2 * (n // 3) // tn, head_dim=head_dim,
                          sub_rows=_tile(tm, SUB_ROWS)),
        grid=(m // tm, n // tn),
        in_specs=[
            pl.BlockSpec((tm, d), lambda i, j: (i, 0)),
            pl.BlockSpec((1, d), lambda i, j: (0, 0)),
            _mod_spec(layer, lambda i: row_fn(i, tm), d),
            pl.BlockSpec((None, d, tn), lambda i, j: (j, 0, 0)),
            pl.BlockSpec((1, tn), lambda i, j: (0, j)),
        ],
        out_specs=pl.BlockSpec((tm, tn), lambda i, j: (i, (j + 2 * (n_tiles // 3)) % n_tiles)),
        out_shape=jax.ShapeDtypeStruct((m, n), BF16),
        scratch_shapes=[pltpu.VMEM((tm, d), BF16)],
        compiler_params=_compiler_params(("parallel", "arbitrary"), vmem),
        name="qkv",
    )(x, norm_g, mods, w_qkv, head_gain)


LOG2_E = math.log2(math.e)
QKV_KV_BLOCK = 0
QKV_Q_BLOCK = 2
_NT_DIMS = (((1,), (1,)), ((), ()))


def _na_kernel(*refs, nb, heads, head_dim, step_blocks):
    off_ref = refs[1]
    refs = refs[3:]
    q_ref = refs[0]
    kv_ref = refs[1]
    kvc_ref = refs[2]
    tbl_refs = refs[3:3 + step_blocks]
    o_ref = refs[3 + step_blocks]
    qw = q_ref.shape[0] // step_blocks
    d = q_ref.shape[1]
    offs = [off_ref[pl.program_id(1) * step_blocks + b] for b in range(step_blocks)]

    def scores(u):
        blk, h = divmod(u, heads)
        sl = slice(h * head_dim, (h + 1) * head_dim)
        q = q_ref[blk * qw:(blk + 1) * qw, sl]
        k = kv_ref[pl.ds(offs[blk], nb), :, sl].reshape(nk, head_dim)
        s = lax.dot_general(q, k, _NT_DIMS, preferred_element_type=F32) + tbl_refs[blk][h]
        sc = lax.dot_general(q, kvc_ref[:, sl], _NT_DIMS, preferred_element_type=F32)
        return s, sc

    nk = nb * kv_ref.shape[1]
    ones_lat = jnp.ones((nk, head_dim), BF16)
    ones_ctx = jnp.ones((kvc_ref.shape[0], head_dim), BF16)

    def attend(u, s_sc):
        blk, h = divmod(u, heads)
        s, sc = s_sc
        sl = slice(h * head_dim, (h + 1) * head_dim)
        vsl = slice(d + h * head_dim, d + (h + 1) * head_dim)
        v = jnp.concatenate([kv_ref[pl.ds(offs[blk], nb), :, vsl].reshape(nk, head_dim), ones_lat], axis=1)
        vc = jnp.concatenate([kvc_ref[:, vsl], ones_ctx], axis=1)
        m = jnp.maximum(jnp.max(s, axis=-1, keepdims=True), jnp.max(sc, axis=-1, keepdims=True))
        p = jnp.exp2(s - m).astype(BF16)
        pc = jnp.exp2(sc - m).astype(BF16)
        o = (jnp.dot(p, v, preferred_element_type=F32) + jnp.dot(pc, vc, preferred_element_type=F32))
        o_ref[blk * qw:(blk + 1) * qw, sl] = (
            o[:, :head_dim] * (1.0 / o[:, head_dim:head_dim + 1])).astype(o_ref.dtype)

    _skewed_subtiles(step_blocks * heads, scores, attend, ahead=2)


def _na_geometry(rows, qrows):
    kh = min(WIN_H, rows)
    nb = min(kh + qrows - 1, rows)
    n_blk = rows // qrows
    band0s, keys, types, row_rel = [], {}, [], []
    for blk in range(n_blk):
        r0 = blk * qrows
        q_row = r0 + np.arange(qrows)
        r_start = np.clip(q_row - kh // 2, 0, rows - kh)
        band0 = min(int(np.clip(r0 - kh // 2, 0, rows - kh)), rows - nb)
        k_row = band0 + np.arange(nb)
        row_ok = (k_row[None, :] >= r_start[:, None]) & (k_row[None, :] < r_start[:, None] + kh)
        rel = np.where(row_ok, k_row[None, :] - q_row[:, None] + WIN_H - 1, -1)
        key = rel.tobytes()
        if key not in keys:
            keys[key] = len(row_rel)
            row_rel.append(rel)
        types.append(keys[key])
        band0s.append(band0)
    return nb, n_blk, np.array(band0s, np.int32), np.array(types, np.int32), row_rel


def _na_bias_table(rpb, row_rel):
    layers, heads, n_dr, n_dc = rpb.shape
    period = 2 * GRID_W
    assert n_dc <= period
    r = jnp.pad(rpb, ((0, 0), (0, 0), (0, 0), (0, period - n_dc)))
    skew = jnp.tile(r, (1, 1, 1, GRID_W))[..., :GRID_W * (period - 1)]
    skew = skew.reshape(layers, heads, n_dr, GRID_W, period - 1)
    toep = skew[..., WIN_W - 1:WIN_W - 1 + GRID_W]
    q_col = np.arange(GRID_W)
    c_start = np.clip(q_col - WIN_W // 2, 0, GRID_W - WIN_W)
    col_ok = (q_col[None, :] >= c_start[:, None]) & (q_col[None, :] < c_start[:, None] + WIN_W)
    toep = jnp.where(col_ok, toep, -jnp.inf)
    masked = jnp.full((layers, heads, GRID_W, GRID_W), -jnp.inf, F32)
    tables = []
    for rel in row_rel:
        q_blocks = []
        for qr in range(rel.shape[0]):
            q_blocks.append(jnp.concatenate(
                [toep[:, :, rel[qr, kr]] if rel[qr, kr] >= 0 else masked for kr in range(rel.shape[1])], axis=-1))
        tables.append(jnp.concatenate(q_blocks, axis=-2))
    return jnp.stack(tables, axis=1).astype(F32)


def _na_attention(qkv, qkv_ctx, tables, layer, geometry, bn, rows, d, head_dim):
    nb, n_blk, band0s, types, _ = geometry
    qw, nk = tables.shape[-2:]
    ctx_len = qkv_ctx.shape[1]
    kv_view = qkv.reshape(bn * rows, GRID_W, 3 * d)

    spb = _tile(n_blk, NA_STEP_BLOCKS)
    n_steps = n_blk // spb
    qrows = qw // GRID_W
    band_rows = min(rows, nb + (spb - 1) * qrows)
    starts = np.minimum(band0s[::spb], rows - band_rows).astype(np.int32)
    offsets = (band0s - np.repeat(starts, spb)).astype(np.int32)
    assert offsets.min() >= 0 and (offsets + nb).max() <= band_rows
    start_tbl, off_tbl, type_tbl = jnp.asarray(starts), jnp.asarray(offsets), jnp.asarray(types)

    def tbl_spec(blk):
        return pl.BlockSpec((None, None, NA_HEADS, qw, nk),
                            lambda b, i, start, off, typ: (layer, typ[i * spb + blk], 0, 0, 0))

    in_specs = ([pl.BlockSpec((spb * qw, d), lambda b, i, start, off, typ: (b * n_steps + i, QKV_Q_BLOCK)),
                 pl.BlockSpec((pl.Element(band_rows), pl.Element(GRID_W), pl.Element(2 * d)),
                              lambda b, i, start, off, typ: (b * rows + start[i], 0, QKV_KV_BLOCK * 2 * d)),
                 pl.BlockSpec((None, ctx_len, 2 * d), lambda b, i, start, off, typ: (b, 0, QKV_KV_BLOCK))]
                + [tbl_spec(blk) for blk in range(spb)])
    vmem = (2 * (spb * (2 * qw * d * 2 + NA_HEADS * qw * nk * 4) + band_rows * GRID_W * 2 * d * 2
                 + 2 * ctx_len * d * 2) + 2 * VMEM_TEMP_BYTES)
    grid_spec = pltpu.PrefetchScalarGridSpec(
        num_scalar_prefetch=3,
        grid=(bn, n_steps),
        in_specs=in_specs,
        out_specs=pl.BlockSpec((spb * qw, d), lambda b, i, start, off, typ: (b * n_steps + i, 0)),
    )
    return pl.pallas_call(
        functools.partial(_na_kernel, nb=nb, heads=NA_HEADS, head_dim=head_dim, step_blocks=spb),
        grid_spec=grid_spec,
        out_shape=jax.ShapeDtypeStruct((bn * rows * GRID_W, d), BF16),
        compiler_params=_compiler_params(("parallel", "arbitrary"), vmem),
        name="na_attention",
    )(start_tbl, off_tbl, type_tbl, qkv, kv_view, qkv_ctx, *([tables] * spb))


def _ctx_attn_kernel(q_ref, k_ref, v_ref, o_ref, *, heads, head_dim):
    for h in range(heads):
        sl = slice(h * head_dim, (h + 1) * head_dim)
        s = lax.dot_general(q_ref[:, sl], k_ref[:, sl], _NT_DIMS, preferred_element_type=F32)
        m = jnp.max(s, axis=-1, keepdims=True)
        p = jnp.exp2(s - m)
        l = jnp.sum(p, axis=-1, keepdims=True)
        o = jnp.dot(p.astype(BF16), v_ref[:, sl], preferred_element_type=F32)
        o_ref[:, sl] = (o * (1.0 / l)).astype(o_ref.dtype)


def _ctx_attention(qkv_ctx, d, head_dim):
    bn, ctx_len, _ = qkv_ctx.shape
    spec = lambda col: pl.BlockSpec((None, ctx_len, d), lambda b: (b, 0, col))
    return pl.pallas_call(
        functools.partial(_ctx_attn_kernel, heads=NA_HEADS, head_dim=head_dim),
        grid=(bn,),
        in_specs=[spec(QKV_Q_BLOCK), spec(2 * QKV_KV_BLOCK), spec(2 * QKV_KV_BLOCK + 1)],
        out_specs=pl.BlockSpec((None, ctx_len, d), lambda b: (b, 0, 0)),
        out_shape=jax.ShapeDtypeStruct((bn, ctx_len, d), BF16),
        compiler_params=_compiler_params(("arbitrary",), 8 * ctx_len * d * 2 + 2 * VMEM_TEMP_BYTES),
        name="ctx_attention",
    )(qkv_ctx, qkv_ctx, qkv_ctx)


HALO = V7X_SUBLANES_F32


def _shift_rows(u, tm):
    n = u.shape[0]
    prev = pltpu.roll(u, 1, 0)[HALO:HALO + tm]
    nxt = pltpu.roll(u, n - 1, 0)[HALO:HALO + tm]
    return prev, nxt


def _conv3(u, cw, rows, seq_edges):
    prev, nxt = _shift_rows(u, rows)
    if seq_edges:
        row = lax.broadcasted_iota(jnp.int32, (rows, 1), 0)
        prev = jnp.where(row == 0, 0.0, prev)
        nxt = jnp.where(row == rows - 1, 0.0, nxt)
    return prev * cw[0:1] + u[HALO:HALO + rows] * cw[1:2] + nxt * cw[2:3]


def _gated_conv_kernel(*refs, kind, tiles_per_seq, mod_rows, sub_rows, seq_edges, side_cast):
    parts = 2 if kind == "ffn" else 3
    x_ref, xp_ref, xn_ref, g_ref, mod_ref = refs[:5]
    w1_refs = refs[5:5 + parts]
    conv_ref = refs[5 + parts]
    rest = refs[6 + parts:]
    tc = w1_refs[0].shape[1]
    col = lambda p: slice(p * tc, (p + 1) * tc)
    if side_cast is None:
        w2_ref, o_ref, hs_ref = rest
    else:
        w2_ref, up_f32_ref, down_f32_ref, o_ref, up_bf16_ref, down_bf16_ref, hs_ref = rest
    i = pl.program_id(0)
    j = pl.program_id(1)
    tm = x_ref.shape[0]
    r_shift, r_scale, r_gate = mod_rows

    if side_cast is not None:
        n_up, n_down, down_rows = side_cast
        step = i * pl.num_programs(1) + j

        @pl.when(step < n_up)
        def _():
            _cast_block(up_f32_ref, up_bf16_ref, 0, None)

        @pl.when((step >= n_up) & (step < n_up + n_down))
        def _():
            _cast_block(down_f32_ref, down_bf16_ref, (step - n_up) * down_f32_ref.shape[0], down_rows)

    res_gate = mod_ref[r_gate:r_gate + 1, :]
    n_sub = tm // sub_rows

    def first_matmuls(r):
        lhs = hs_ref[r * sub_rows:r * sub_rows + sub_rows + 2 * HALO, :]
        return [jnp.dot(lhs, w[...], preferred_element_type=F32) for w in w1_refs]

    def mid_and_second_matmul(r, u):
        if kind == "ffn":
            gate, up = [_conv3(u[p], conv_ref[0:3, col(p)], sub_rows, seq_edges) + conv_ref[3:4, col(p)]
                        for p in range(2)]
            a = gate * jax.nn.sigmoid(gate) * up
        else:
            a = u[0][HALO:HALO + sub_rows] * _conv3(u[1] * u[2], conv_ref[0:3, :], sub_rows, seq_edges)
        o_ref[r * sub_rows:(r + 1) * sub_rows, :] += res_gate * jnp.dot(a.astype(BF16), w2_ref[...],
                                                                       preferred_element_type=F32)

    @pl.when(j == 0)
    def _():
        gain = g_ref[...] * (1.0 + mod_ref[r_scale:r_scale + 1, :])
        shift = mod_ref[r_shift:r_shift + 1, :]
        pos = i % tiles_per_seq
        sources = [(xp_ref, pos == 0), (x_ref, None), (xn_ref, pos == tiles_per_seq - 1)]
        total = tm + 2 * HALO

        def norm_and_first_matmuls(r):
            lo = 0 if r == 0 else r * sub_rows + 2 * HALO
            hi = total if r == n_sub - 1 else (r + 1) * sub_rows + 2 * HALO
            _store_norm_mod(hs_ref, sources, gain, shift, copy_ref=o_ref, row_range=(lo, hi))
            return first_matmuls(r)

        _skewed_subtiles(n_sub, norm_and_first_matmuls, mid_and_second_matmul)

    @pl.when(j > 0)
    def _():
        _skewed_subtiles(n_sub, first_matmuls, mid_and_second_matmul)


def _gated_conv(kind, x, seq_len, norm_g, mods, layer, row_fn, mod_rows, w1, conv, w2, tc, tm_pref,
                sub_pref=SUB_ROWS, cast_next=None):
    m, d = x.shape
    parts = 2 if kind == "ffn" else 3
    cparts = 2 if kind == "ffn" else 1
    n_chunks = w2.shape[0] // tc
    assert w1.shape == (parts * n_chunks, d, tc) and conv.shape == (n_chunks, CONV_ROWS, cparts * tc)
    if seq_len >= tm_pref:
        tm = _tile(seq_len, tm_pref)
        sub_rows = _tile(tm, sub_pref)
        tiles_per_seq, seq_edges = seq_len // tm, False
    else:
        tm = seq_len * _tile(m // seq_len, tm_pref // seq_len)
        sub_rows = seq_len
        tiles_per_seq, seq_edges = 1, True
    assert sub_rows % V7X_SUBLANES_BF16 == 0
    hb = tm // HALO
    n_hblk = m // HALO
    other = 2 * (d * parts * tc * 2 + tc * d * 2) + (tm + 2 * HALO) * d * 2
    n_steps = (m // tm) * n_chunks
    side_cast, side_specs_in, side_specs_out, side_shapes, side_operands = None, [], [], [], []
    if cast_next is not None:
        w_up, w_down, nl = cast_next
        f, fp = w_down.shape[1], n_chunks * tc
        up_rows = min(t for t in range(V7X_SUBLANES_BF16, d + 1, V7X_SUBLANES_BF16)
                      if d % t == 0 and 2 * (d // t) <= 3 * n_steps // 4)
        n_up = 2 * (d // up_rows)
        down_rows = min(t for t in range(V7X_SUBLANES_BF16, fp + 1, V7X_SUBLANES_BF16)
                        if fp % t == 0 and fp // t <= n_steps - n_up)
        n_down = fp // down_rows
        last_down_block = (f - 1) // down_rows
        up_item = lambda i, j: jnp.minimum(i * n_chunks + j, n_up - 1)
        down_item = lambda i, j: jnp.clip(i * n_chunks + j - n_up, 0, n_down - 1)
        side_specs_in = [
            pl.BlockSpec((None, up_rows, f), lambda i, j: (nl, up_item(i, j) // 2, up_item(i, j) % 2)),
            pl.BlockSpec((None, down_rows, d),
                         lambda i, j: (nl, jnp.minimum(down_item(i, j), last_down_block), 0)),
        ]
        side_specs_out = [
            pl.BlockSpec((n_chunks, up_rows, tc), lambda i, j: (up_item(i, j) % 2, up_item(i, j) // 2, 0)),
            pl.BlockSpec((down_rows, d), lambda i, j: (down_item(i, j), 0)),
        ]
        side_shapes = [jax.ShapeDtypeStruct((2 * n_chunks, d, tc), BF16), jax.ShapeDtypeStruct((fp, d), BF16)]
        side_operands = [w_up, w_down]
        side_cast = (n_up, n_down, f)
        other += 2 * (up_rows * f * 4 + up_rows * fp * 2 + down_rows * d * 6)
    x_bufs, out_bufs = _row_tile_buffers(tm * d * 4, other)
    vmem = (x_bufs + out_bufs) * tm * d * 4 + other + VMEM_TEMP_BYTES
    in_specs = [
        pl.BlockSpec((tm, d), lambda i, j: (i, 0), pipeline_mode=pl.Buffered(x_bufs)),
        pl.BlockSpec((HALO, d), lambda i, j: (jnp.maximum(i * hb - 1, 0), 0)),
        pl.BlockSpec((HALO, d), lambda i, j: (jnp.minimum((i + 1) * hb, n_hblk - 1), 0)),
        pl.BlockSpec((1, d), lambda i, j: (0, 0)),
        _mod_spec(layer, lambda i: row_fn(i, tm), d),
    ]
    in_specs += [pl.BlockSpec((None, d, tc), functools.partial(lambda p, i, j: (p * n_chunks + j, 0, 0), p))
                 for p in range(parts)]
    in_specs += [pl.BlockSpec((None, CONV_ROWS, cparts * tc), lambda i, j: (j, 0, 0)),
                 pl.BlockSpec((tc, d), lambda i, j: (j, 0))]
    operands = [x, x, x, norm_g, mods] + [w1] * parts + [conv, w2]
    out_spec = pl.BlockSpec((tm, d), lambda i, j: (i, 0), pipeline_mode=pl.Buffered(out_bufs))
    out_shape = jax.ShapeDtypeStruct((m, d), F32)
    result = pl.pallas_call(
        functools.partial(_gated_conv_kernel, kind=kind, tiles_per_seq=tiles_per_seq, mod_rows=mod_rows,
                          sub_rows=sub_rows, seq_edges=seq_edges, side_cast=side_cast),
        grid=(m // tm, n_chunks),
        in_specs=in_specs + side_specs_in,
        out_specs=[out_spec] + side_specs_out if side_cast else out_spec,
        out_shape=[out_shape] + side_shapes if side_cast else out_shape,
        scratch_shapes=[pltpu.VMEM((tm + 2 * HALO, d), BF16)],
        compiler_params=_compiler_params(("arbitrary" if side_cast else "parallel", "arbitrary"), vmem),
        name=kind,
    )(*operands, *side_operands)
    return tuple(result) if side_cast else result


def _gelu_tanh(x):
    return jax.nn.gelu(x, approximate=True)


def _gmlp_kernel(x_ref, g_ref, mod_ref, win_ref, vg_ref, ws_ref, bs_ref, wout_ref, o_ref,
                 hs_ref, v_ref, ssq_ref, *, n_chunks, group_dim, width, sub_rows):
    j = pl.program_id(1)
    tm = x_ref.shape[0]
    tc = win_ref.shape[1]
    n_sub = tm // sub_rows

    def in_matmul(r):
        return jnp.dot(hs_ref[r * sub_rows:(r + 1) * sub_rows, :], win_ref[...], preferred_element_type=F32)

    def keep_v(r, y):
        rows = slice(r * sub_rows, (r + 1) * sub_rows)
        t = _gelu_tanh(y)
        v_ref[j, rows, :] = t.astype(v_ref.dtype)
        ssq_ref[rows, :] += jnp.sum(t * t, axis=-1, keepdims=True)

    @pl.when(j == 0)
    def _():
        gain = g_ref[...] * (1.0 + mod_ref[1:2, :])
        ssq_ref[...] = jnp.zeros_like(ssq_ref)

        def norm_and_matmul(r):
            _store_norm_mod(hs_ref, [(x_ref, None)], gain, mod_ref[0:1, :], copy_ref=o_ref,
                            row_range=(r * sub_rows, (r + 1) * sub_rows))
            return in_matmul(r)

        _skewed_subtiles(n_sub, norm_and_matmul, keep_v)

    @pl.when((j > 0) & (j < n_chunks))
    def _():
        _skewed_subtiles(n_sub, in_matmul, keep_v)

    @pl.when(j >= n_chunks)
    def _():
        jj = j - n_chunks
        vg = vg_ref[...]
        res_gate = mod_ref[2:3, :]
        bs = bs_ref[...]
        gpc = tc // group_dim
        ws = [ws_ref[jj * gpc + gi] for gi in range(gpc)]

        def gate_and_project(r, y):
            rows = slice(r * sub_rows, (r + 1) * sub_rows)
            inv = lax.rsqrt(ssq_ref[rows, :] * (1.0 / width) + EPS)
            vn = (v_ref[jj, rows, :].astype(F32) * inv * vg).astype(BF16)
            cols = []
            for gi in range(gpc):
                csl = slice(gi * group_dim, (gi + 1) * group_dim)
                cols.append(jnp.concatenate(
                    [jnp.dot(ws[gi], vn[c * GM_CHUNK:(c + 1) * GM_CHUNK, csl], preferred_element_type=F32)
                     + bs[:, csl] for c in range(sub_rows // GM_CHUNK)], axis=0))
            sv = jnp.concatenate(cols, axis=1)
            o_ref[rows, :] += res_gate * jnp.dot((_gelu_tanh(y) * sv).astype(BF16), wout_ref[...],
                                                 preferred_element_type=F32)

        _skewed_subtiles(n_sub, in_matmul, gate_and_project)


def _gmlp(x, norm_g, mods, layer, row_fn, w_in, v_g, w_s, b_s_cols, w_out, tm_pref):
    m, d = x.shape
    width = w_out.shape[0]
    group_dim = width // GM_GROUPS
    tc = w_in.shape[2]
    n_chunks = width // tc
    assert w_in.shape[0] == 2 * n_chunks and tc % group_dim == 0
    tm = _tile(m, tm_pref)
    sub_rows = _tile(tm, SUB_ROWS)
    assert sub_rows % GM_CHUNK == 0
    other = (2 * (d * tc * 2 + tc * d * 2 + GM_CHUNK * tc * 4) + tm * d * 2 + tm * width * 2
             + GM_GROUPS * GM_CHUNK * GM_CHUNK * 2 + tm * V7X_LANES * 4)
    x_bufs, out_bufs = _row_tile_buffers(tm * d * 4, other)
    vmem = (x_bufs + out_bufs) * tm * d * 4 + other + VMEM_TEMP_BYTES
    return pl.pallas_call(
        functools.partial(_gmlp_kernel, n_chunks=n_chunks, group_dim=group_dim, width=width,
                          sub_rows=sub_rows),
        grid=(m // tm, 2 * n_chunks),
        in_specs=[
            pl.BlockSpec((tm, d), lambda i, j: (i, 0), pipeline_mode=pl.Buffered(x_bufs)),
            pl.BlockSpec((1, d), lambda i, j: (0, 0)),
            _mod_spec(layer, lambda i: row_fn(i, tm), d),
            pl.BlockSpec((None, d, tc),
                         lambda i, j: (jnp.where(j < n_chunks, j + n_chunks, j - n_chunks), 0, 0)),
            pl.BlockSpec((1, tc), lambda i, j: (0, jnp.maximum(j - n_chunks, 0))),
            pl.BlockSpec((GM_GROUPS, GM_CHUNK, GM_CHUNK), lambda i, j: (0, 0, 0)),
            pl.BlockSpec((GM_CHUNK, tc), lambda i, j: (0, jnp.maximum(j - n_chunks, 0))),
            pl.BlockSpec((tc, d), lambda i, j: (jnp.maximum(j - n_chunks, 0), 0)),
        ],
        out_specs=pl.BlockSpec((tm, d), lambda i, j: (i, 0), pipeline_mode=pl.Buffered(out_bufs)),
        out_shape=jax.ShapeDtypeStruct((m, d), F32),
        scratch_shapes=[pltpu.VMEM((tm, d), BF16), pltpu.VMEM((n_chunks, tm, tc), BF16), pltpu.VMEM((tm, 1), F32)],
        compiler_params=_compiler_params(("parallel", "arbitrary"), vmem),
        name="gmlp",
    )(x, norm_g, mods, w_in, v_g, w_s, b_s_cols, w_out)


CAST_BLOCK_BYTES = 6 << 20


def _cast_kernel(x_ref, o_ref, *, valid_rows):
    _cast_block(x_ref, o_ref, pl.program_id(0) * x_ref.shape[0], valid_rows)


def _cast_block(x_ref, o_ref, row0, valid_rows):
    rows, f = x_ref.shape
    x = x_ref[...]
    if valid_rows is not None:
        row = row0 + lax.broadcasted_iota(jnp.int32, (rows, 1), 0)
        x = jnp.where(row < valid_rows, x, 0.0)
    x = x.astype(o_ref.dtype)
    if len(o_ref.shape) == 2:
        o_ref[...] = x
        return
    n_chunks, _, tc = o_ref.shape
    for c in range(n_chunks):
        width = max(0, min(tc, f - c * tc))
        if width:
            o_ref[c, :, :width] = x[:, c * tc:c * tc + width]
        if width < tc:
            o_ref[c, :, width:] = jnp.zeros((rows, tc - width), o_ref.dtype)


def _cast_weight(w, idx, parts=1, fp=None, col_chunk=None, rows_out=None, row_block=None):
    _, r, c = w.shape
    f = c // parts
    fp = f if fp is None else fp
    rows_out = r if rows_out is None else rows_out
    assert parts == 1 or f % V7X_LANES == 0
    assert col_chunk is not None or (parts == 1 and fp == f)
    padded_rows = rows_out > r
    if row_block is None:
        row_pref = max(V7X_SUBLANES_BF16, CAST_BLOCK_BYTES // (4 * f))
        row_block = max(t for t in range(V7X_SUBLANES_BF16, row_pref + 1, V7X_SUBLANES_BF16)
                        if rows_out % t == 0)
    assert rows_out % row_block == 0 and row_block % V7X_SUBLANES_BF16 == 0
    assert r % row_block == 0 or (padded_rows and rows_out - row_block < r)
    if col_chunk is None:
        out_spec = pl.BlockSpec((row_block, f), lambda i, p: (i, 0))
        out_shape = (rows_out, f)
    else:
        assert fp % col_chunk == 0 and col_chunk % V7X_LANES == 0 and not padded_rows
        n_chunks = fp // col_chunk
        out_spec = pl.BlockSpec((n_chunks, row_block, col_chunk), lambda i, p: (p, i, 0))
        out_shape = (parts * n_chunks, r, col_chunk)
    return pl.pallas_call(
        functools.partial(_cast_kernel, valid_rows=r if padded_rows else None),
        grid=(rows_out // row_block, parts),
        in_specs=[pl.BlockSpec((None, row_block, f), lambda i, p: (idx, i, p))],
        out_specs=out_spec,
        out_shape=jax.ShapeDtypeStruct(out_shape, BF16),
        compiler_params=_compiler_params(("parallel", "arbitrary"),
                                         2 * row_block * (4 * f + 2 * fp) + VMEM_TEMP_BYTES),
        name="cast_weight",
    )(w)


def _pad_halves(a, f, fp):
    pad = lambda h: jnp.pad(h, ((0, 0), (0, fp - f)))
    return jnp.concatenate([pad(a[:, :f]), pad(a[:, f:])], axis=1)


CONV_ROWS = V7X_SUBLANES_F32


def _conv_params(taps, bias, cparts, tc):
    width = taps.shape[1]
    rows = [taps, jnp.zeros((1, width), F32) if bias is None else bias,
            jnp.zeros((CONV_ROWS - taps.shape[0] - 1, width), F32)]
    a = jnp.concatenate(rows, axis=0).reshape(CONV_ROWS, cparts, width // (cparts * tc), tc)
    return jnp.transpose(a, (2, 0, 1, 3)).reshape(width // (cparts * tc), CONV_ROWS, cparts * tc)


def _prep_ffn(w_up, conv_w, conv_b, w_down, layer, tc, cast_weights):
    f = w_down.shape[1]
    fp = -(-f // tc) * tc
    conv = _conv_params(_pad_halves(conv_w[layer], f, fp), _pad_halves(conv_b[layer][None, :], f, fp), 2, tc)
    if cast_weights is None:
        cast_weights = (_cast_weight(w_up, layer, parts=2, fp=fp, col_chunk=tc),
                        _cast_weight(w_down, layer, rows_out=fp))
    return cast_weights[0], conv, cast_weights[1]


def kernel(x, c, ctx, c_ctx, norm_mix_g, norm_ffn_g, w_ada, b_ada, na_w_qkv, na_q_g, na_k_g, na_rpb, na_w_o,
           gm_w_in, gm_v_g, gm_w_s, gm_b_s, gm_w_out, sc_w_in, sc_conv_w, sc_w_out,
           ffn_w_up, ffn_conv_w, ffn_conv_b, ffn_w_down):
    bn, seq, d = x.shape
    ctx_len = ctx.shape[1]
    depth = w_ada.shape[0]
    head_dim = d // NA_HEADS
    rows = seq // GRID_W
    assert bn + 1 <= COND_ROWS and seq % GRID_W == 0 and ctx_len % GM_CHUNK == 0

    cond = jnp.concatenate([c, c_ctx[None, :], jnp.zeros((COND_ROWS - bn - 1, d), F32)], axis=0)
    mods = _adaln(cond, w_ada, b_ada).reshape(depth, COND_ROWS, N_MOD, d)

    lat_row = lambda i, tm: i // (seq // tm)
    ctx_row = lambda i, tm: bn

    ffn_hidden = ffn_w_down.shape[1]
    ffn_tc = HIDDEN_CHUNK if ffn_hidden >= HIDDEN_CHUNK else V7X_LANES
    sc_tc = _tile(d, HIDDEN_CHUNK)
    gm_width = gm_w_out.shape[1]
    gm_tc = max(gm_width // GM_GROUPS, _tile(gm_width, HIDDEN_CHUNK))
    proj_tn = _tile(d, PROJ_COLS)
    qkv_tn = _tile(d, QKV_COLS)
    lat_tm = ROW_TILE
    ctx_tm = min(ROW_TILE, bn * ctx_len)
    lat_sub = HALO_SUB_ROWS

    xl = x.reshape(bn * seq, d)
    xc = ctx.reshape(bn * ctx_len, d)
    ffn_weights = None
    na_geometry = _na_geometry(rows, min(NA_QROWS, rows))
    na_tables = _na_bias_table(na_rpb * LOG2_E, na_geometry[-1])
    for i in range(depth):
        last = i == depth - 1
        mixer, jx = i % N_MIXERS, i // N_MIXERS
        g_mix = norm_mix_g[i][None, :]
        g_ffn = norm_ffn_g[i][None, :]
        if mixer == 0:
            w_qkv = _cast_weight(na_w_qkv, jx, col_chunk=qkv_tn)
            w_o = _cast_weight(na_w_o, jx, col_chunk=proj_tn)
            q_gain = na_q_g[jx] * (head_dim ** -0.5 * LOG2_E)
            head_gain = jnp.concatenate([jnp.tile(q_gain, NA_HEADS), jnp.tile(na_k_g[jx], NA_HEADS),
                                         jnp.ones((d,), F32)])[None, :]
            qkv = _qkv(xl, g_mix, mods, i, lat_row, w_qkv, head_gain, head_dim)
            qkv_c = _qkv(xc, g_mix, mods, i, ctx_row, w_qkv, head_gain, head_dim).reshape(bn, ctx_len, 3 * d)
            att = _na_attention(qkv, qkv_c, na_tables, jx, na_geometry, bn, rows, d, head_dim)
            xl_new = _proj_res(att, w_o, xl, mods, i, lat_row, 2)
            if not last:
                att_c = _ctx_attention(qkv_c, d, head_dim).reshape(bn * ctx_len, d)
                xc = _proj_res(att_c, w_o, xc, mods, i, ctx_row, 2)
            xl = xl_new
        elif mixer == 1:
            w_in = _cast_weight(gm_w_in, jx, col_chunk=gm_tc)
            w_out = _cast_weight(gm_w_out, jx)
            w_s = gm_w_s[jx].astype(BF16)
            width = w_out.shape[0]
            v_g = gm_v_g[jx][None, :]
            b_s_cols = jnp.repeat(gm_b_s[jx].T, width // GM_GROUPS, axis=1)
            xl = _gmlp(xl, g_mix, mods, i, lat_row, w_in, v_g, w_s, b_s_cols, w_out, lat_tm)
            if not last:
                xc = _gmlp(xc, g_mix, mods, i, ctx_row, w_in, v_g, w_s, b_s_cols, w_out, ctx_tm)
        else:
            w1 = _cast_weight(sc_w_in, jx, parts=3, col_chunk=sc_tc)
            conv, w2 = _conv_params(sc_conv_w[jx], None, 1, sc_tc), _cast_weight(sc_w_out, jx)
            xl = _gated_conv("sc", xl, seq, g_mix, mods, i, lat_row, (0, 1, 2), w1, conv, w2, sc_tc, lat_tm,
                             sub_pref=lat_sub)
            if not last:
                xc = _gated_conv("sc", xc, ctx_len, g_mix, mods, i, ctx_row, (0, 1, 2), w1, conv, w2, sc_tc,
                                 ctx_tm)
        w1, conv, w2 = _prep_ffn(ffn_w_up, ffn_conv_w, ffn_conv_b, ffn_w_down, i, ffn_tc, ffn_weights)
        if last:
            xl = _gated_conv("ffn", xl, seq, g_ffn, mods, i, lat_row, (3, 4, 5), w1, conv, w2, ffn_tc, lat_tm,
                             sub_pref=lat_sub)
        else:
            xl, *ffn_weights = _gated_conv("ffn", xl, seq, g_ffn, mods, i, lat_row, (3, 4, 5), w1, conv, w2,
                                           ffn_tc, lat_tm, sub_pref=lat_sub,
                                           cast_next=(ffn_w_up, ffn_w_down, i + 1))
        if not last:
            xc = _gated_conv("ffn", xc, ctx_len, g_ffn, mods, i, ctx_row, (3, 4, 5), w1, conv, w2, ffn_tc,
                             ctx_tm)
    return xl.reshape(bn, seq, d)
```

```python
import functools
import math

import numpy as np
import jax
import jax.numpy as jnp
from jax import lax
from jax.experimental import pallas as pl
from jax.experimental.pallas import tpu as pltpu

GRID_W = 64
NA_HEADS = 16
WIN_H = 8
WIN_W = 16
GM_GROUPS = 16
GM_CHUNK = 128
N_MOD = 6
N_MIXERS = 3
EPS = 1e-6

V7X_VMEM_BYTES = 64 * 1024 * 1024
V7X_SUBLANES_F32 = 8
V7X_SUBLANES_BF16 = 16
V7X_LANES = 128
VMEM_LIMIT_BYTES = V7X_VMEM_BYTES - (4 << 20)
VMEM_TEMP_BYTES = 6 << 20

F32 = jnp.float32
BF16 = jnp.bfloat16

ROW_TILE = 1024
SUB_ROWS = 256
HALO_SUB_ROWS = 512
HIDDEN_CHUNK = 512
PROJ_COLS = 1024
QKV_COLS = 2048
ADALN_COLS = 1024
NA_QROWS = 2
NA_STEP_BLOCKS = 2
COND_ROWS = V7X_SUBLANES_F32


def _compiler_params(semantics, vmem_bytes):
    return pltpu.CompilerParams(dimension_semantics=semantics,
                                vmem_limit_bytes=int(min(vmem_bytes, VMEM_LIMIT_BYTES)))


def _tile(n, pref):
    t = min(n, pref)
    while n % t:
        t -= 1
    return t


def _row_tile_buffers(tile_bytes, other_bytes):
    for x_bufs, out_bufs in ((2, 2), (1, 2), (1, 1)):
        if (x_bufs + out_bufs) * tile_bytes + other_bytes + VMEM_TEMP_BYTES <= VMEM_LIMIT_BYTES:
            break
    return x_bufs, out_bufs


def _norm_mod(x, gain, shift):
    ms = jnp.mean(x * x, axis=-1, keepdims=True)
    return x * lax.rsqrt(ms + EPS) * gain + shift


NORM_CHUNK = 16


def _store_norm_mod(hs_ref, sources, gain, shift, copy_ref=None, row_range=None):
    bounds = np.cumsum([0] + [ref.shape[0] for ref, _ in sources])
    total = int(bounds[-1])
    assert hs_ref.shape[0] == total
    r_lo, r_hi = (0, total) if row_range is None else row_range
    assert r_lo % NORM_CHUNK == 0
    for c0 in range(r_lo, r_hi, NORM_CHUNK):
        c1 = min(c0 + NORM_CHUNK, r_hi)
        pieces = []
        for (ref, zero_pred), b0, b1 in zip(sources, bounds[:-1], bounds[1:]):
            lo, hi = max(c0, int(b0)), min(c1, int(b1))
            if lo >= hi:
                continue
            rows = ref[lo - int(b0):hi - int(b0), :]
            if copy_ref is not None and zero_pred is None:
                copy_ref[lo - int(b0):hi - int(b0), :] = rows
            h = _norm_mod(rows, gain, shift)
            pieces.append(h if zero_pred is None else jnp.where(zero_pred, 0.0, h))
        h = pieces[0] if len(pieces) == 1 else jnp.concatenate(pieces, axis=0)
        hs_ref[c0:c1, :] = h.astype(BF16)


def _adaln_kernel(c_ref, w_ref, b_ref, o_ref):
    c = c_ref[...]
    a = (c * jax.nn.sigmoid(c)).astype(BF16)
    o_ref[...] = jnp.dot(a, w_ref[...].astype(BF16), preferred_element_type=F32) + b_ref[...]


def _adaln(cond, w_ada, b_ada):
    depth, d, n = w_ada.shape
    tn = _tile(n, ADALN_COLS)
    return pl.pallas_call(
        _adaln_kernel,
        grid=(depth, n // tn),
        in_specs=[
            pl.BlockSpec((COND_ROWS, d), lambda l, j: (0, 0)),
            pl.BlockSpec((None, d, tn), lambda l, j: (l, 0, j)),
            pl.BlockSpec((None, 1, tn), lambda l, j: (l, 0, j)),
        ],
        out_specs=pl.BlockSpec((None, COND_ROWS, tn), lambda l, j: (l, 0, j)),
        out_shape=jax.ShapeDtypeStruct((depth, COND_ROWS, n), F32),
        compiler_params=_compiler_params(("arbitrary", "arbitrary"), 3 * d * tn * 4 + VMEM_TEMP_BYTES),
        name="adaln",
    )(cond, w_ada, b_ada.reshape(depth, 1, n))


def _mod_spec(layer, row_fn, width, col_fn=None):
    if col_fn is None:
        return pl.BlockSpec((None, None, N_MOD, width), lambda i, j: (layer, row_fn(i), 0, 0))
    return pl.BlockSpec((None, None, N_MOD, width), lambda i, j: (layer, row_fn(i), 0, col_fn(j)))


def _skewed_subtiles(n_sub, first, second, ahead=1):
    pending = [first(r) for r in range(min(ahead, n_sub))]
    for r in range(n_sub):
        if r + ahead < n_sub:
            pending.append(first(r + ahead))
        second(r, pending.pop(0))


def _proj_res_kernel(a_ref, w_ref, x_ref, mod_ref, o_ref, *, gate_row, sub_rows):
    gate = mod_ref[gate_row:gate_row + 1, :]

    def matmul(r):
        return jnp.dot(a_ref[r * sub_rows:(r + 1) * sub_rows, :], w_ref[...], preferred_element_type=F32)

    def residual(r, y):
        rows = slice(r * sub_rows, (r + 1) * sub_rows)
        o_ref[rows, :] = x_ref[rows, :] + gate * y

    _skewed_subtiles(o_ref.shape[0] // sub_rows, matmul, residual)


def _proj_res(a, w, x, mods, layer, row_fn, gate_row):
    m, k = a.shape
    n_tiles, _, tn = w.shape
    n = n_tiles * tn
    tm = _tile(m, ROW_TILE)
    vmem = 2 * (tm * k * 2 + k * tn * 2 + 2 * tm * tn * 4) + VMEM_TEMP_BYTES
    return pl.pallas_call(
        functools.partial(_proj_res_kernel, gate_row=gate_row, sub_rows=_tile(tm, SUB_ROWS)),
        grid=(m // tm, n // tn),
        in_specs=[
            pl.BlockSpec((tm, k), lambda i, j: (i, 0)),
            pl.BlockSpec((None, k, tn), lambda i, j: (j, 0, 0)),
            pl.BlockSpec((tm, tn), lambda i, j: (i, j)),
            _mod_spec(layer, lambda i: row_fn(i, tm), tn, lambda j: j),
        ],
        out_specs=pl.BlockSpec((tm, tn), lambda i, j: (i, j)),
        out_shape=jax.ShapeDtypeStruct((m, n), F32),
        compiler_params=_compiler_params(("parallel", "arbitrary"), vmem),
        name="proj_res",
    )(a, w, x, mods)


def _qkv_kernel(x_ref, g_ref, mod_ref, w_ref, hg_ref, o_ref, hs_ref, *, n_norm_tiles, head_dim, sub_rows):
    j = pl.program_id(1)
    tm, tn = o_ref.shape
    is_norm_tile = j < n_norm_tiles
    hg = hg_ref[...]

    def matmul(r):
        return jnp.dot(hs_ref[r * sub_rows:(r + 1) * sub_rows, :], w_ref[...], preferred_element_type=F32)

    def head_norm(r, y):
        rows = slice(r * sub_rows, (r + 1) * sub_rows)
        for hh in range(tn // head_dim):
            sl = slice(hh * head_dim, (hh + 1) * head_dim)
            t = y[:, sl]
            inv = lax.rsqrt(jnp.mean(t * t, axis=-1, keepdims=True) + EPS)
            o_ref[rows, sl] = (t * jnp.where(is_norm_tile, inv, 1.0) * hg[:, sl]).astype(o_ref.dtype)

    @pl.when(j == 0)
    def _():
        gain = g_ref[...] * (1.0 + mod_ref[1:2, :])

        def norm_and_matmul(r):
            _store_norm_mod(hs_ref, [(x_ref, None)], gain, mod_ref[0:1, :],
                            row_range=(r * sub_rows, (r + 1) * sub_rows))
            return matmul(r)

        _skewed_subtiles(tm // sub_rows, norm_and_matmul, head_norm)

    @pl.when(j > 0)
    def _():
        _skewed_subtiles(tm // sub_rows, matmul, head_norm)


def _qkv(x, norm_g, mods, layer, row_fn, w_qkv, head_gain, head_dim):
    m, d = x.shape
    n_tiles, _, tn = w_qkv.shape
    n = n_tiles * tn
    tm = _tile(m, ROW_TILE)
    assert (n // 3) % tn == 0 and tn % head_dim == 0
    vmem = 2 * (tm * d * 4 + d * tn * 2 + tm * tn * 2) + tm * d * 2 + 2 * VMEM_TEMP_BYTES
    return pl.pallas_call(
        functools.partial(_qkv_kernel, n_norm_tiles=2 * (n // 3) // tn, head_dim=head_dim,
                          sub_rows=_tile(tm, SUB_ROWS)),
        grid=(m // tm, n // tn),
        in_specs=[
            pl.BlockSpec((tm, d), lambda i, j: (i, 0)),
            pl.BlockSpec((1, d), lambda i, j: (0, 0)),
            _mod_spec(layer, lambda i: row_fn(i, tm), d),
            pl.BlockSpec((None, d, tn), lambda i, j: (j, 0, 0)),
            pl.BlockSpec((1, tn), lambda i, j: (0, j)),
        ],
        out_specs=pl.BlockSpec((tm, tn), lambda i, j: (i, (j + 2 * (n_tiles // 3)) % n_tiles)),
        out_shape=jax.ShapeDtypeStruct((m, n), BF16),
        scratch_shapes=[pltpu.VMEM((tm, d), BF16)],
        compiler_params=_compiler_params(("parallel", "arbitrary"), vmem),
        name="qkv",
    )(x, norm_g, mods, w_qkv, head_gain)


LOG2_E = math.log2(math.e)
QKV_KV_BLOCK = 0
QKV_Q_BLOCK = 2
_NT_DIMS = (((1,), (1,)), ((), ()))


def _na_kernel(*refs, nb, heads, head_dim, step_blocks):
    refs = refs[2:]
    q_ref = refs[0]
    kv_refs = refs[1:1 + step_blocks]
    rest = refs[1 + step_blocks:]
    kvc_ref = rest[0]
    tbl_refs = rest[1:1 + step_blocks]
    o_ref = rest[1 + step_blocks]
    qw = q_ref.shape[0] // step_blocks
    d = q_ref.shape[1]

    def scores(u):
        blk, h = divmod(u, heads)
        sl = slice(h * head_dim, (h + 1) * head_dim)
        q = q_ref[blk * qw:(blk + 1) * qw, sl]
        k = kv_refs[blk][:, :, sl].reshape(nk, head_dim)
        s = lax.dot_general(q, k, _NT_DIMS, preferred_element_type=F32) + tbl_refs[blk][h]
        sc = lax.dot_general(q, kvc_ref[:, sl], _NT_DIMS, preferred_element_type=F32)
        return s, sc

    nk = nb * kv_refs[0].shape[1]
    ones_lat = jnp.ones((nk, head_dim), BF16)
    ones_ctx = jnp.ones((kvc_ref.shape[0], head_dim), BF16)

    def attend(u, s_sc):
        blk, h = divmod(u, heads)
        s, sc = s_sc
        sl = slice(h * head_dim, (h + 1) * head_dim)
        vsl = slice(d + h * head_dim, d + (h + 1) * head_dim)
        v = jnp.concatenate([kv_refs[blk][:, :, vsl].reshape(nk, head_dim), ones_lat], axis=1)
        vc = jnp.concatenate([kvc_ref[:, vsl], ones_ctx], axis=1)
        m = jnp.maximum(jnp.max(s, axis=-1, keepdims=True), jnp.max(sc, axis=-1, keepdims=True))
        p = jnp.exp2(s - m).astype(BF16)
        pc = jnp.exp2(sc - m).astype(BF16)
        o = (jnp.dot(p, v, preferred_element_type=F32) + jnp.dot(pc, vc, preferred_element_type=F32))
        o_ref[blk * qw:(blk + 1) * qw, sl] = (
            o[:, :head_dim] * (1.0 / o[:, head_dim:head_dim + 1])).astype(o_ref.dtype)

    _skewed_subtiles(step_blocks * heads, scores, attend, ahead=2)


def _na_geometry(rows, qrows):
    kh = min(WIN_H, rows)
    nb = min(kh + qrows - 1, rows)
    n_blk = rows // qrows
    band0s, keys, types, row_rel = [], {}, [], []
    for blk in range(n_blk):
        r0 = blk * qrows
        q_row = r0 + np.arange(qrows)
        r_start = np.clip(q_row - kh // 2, 0, rows - kh)
        band0 = min(int(np.clip(r0 - kh // 2, 0, rows - kh)), rows - nb)
        k_row = band0 + np.arange(nb)
        row_ok = (k_row[None, :] >= r_start[:, None]) & (k_row[None, :] < r_start[:, None] + kh)
        rel = np.where(row_ok, k_row[None, :] - q_row[:, None] + WIN_H - 1, -1)
        key = rel.tobytes()
        if key not in keys:
            keys[key] = len(row_rel)
            row_rel.append(rel)
        types.append(keys[key])
        band0s.append(band0)
    return nb, n_blk, np.array(band0s, np.int32), np.array(types, np.int32), row_rel


def _na_bias_table(rpb, row_rel):
    layers, heads, n_dr, n_dc = rpb.shape
    period = 2 * GRID_W
    assert n_dc <= period
    r = jnp.pad(rpb, ((0, 0), (0, 0), (0, 0), (0, period - n_dc)))
    skew = jnp.tile(r, (1, 1, 1, GRID_W))[..., :GRID_W * (period - 1)]
    skew = skew.reshape(layers, heads, n_dr, GRID_W, period - 1)
    toep = skew[..., WIN_W - 1:WIN_W - 1 + GRID_W]
    q_col = np.arange(GRID_W)
    c_start = np.clip(q_col - WIN_W // 2, 0, GRID_W - WIN_W)
    col_ok = (q_col[None, :] >= c_start[:, None]) & (q_col[None, :] < c_start[:, None] + WIN_W)
    toep = jnp.where(col_ok, toep, -jnp.inf)
    masked = jnp.full((layers, heads, GRID_W, GRID_W), -jnp.inf, F32)
    tables = []
    for rel in row_rel:
        q_blocks = []
        for qr in range(rel.shape[0]):
            q_blocks.append(jnp.concatenate(
                [toep[:, :, rel[qr, kr]] if rel[qr, kr] >= 0 else masked for kr in range(rel.shape[1])], axis=-1))
        tables.append(jnp.concatenate(q_blocks, axis=-2))
    return jnp.stack(tables, axis=1).astype(F32)


def _na_attention(qkv, qkv_ctx, tables, layer, geometry, bn, rows, d, head_dim):
    nb, n_blk, band0s, types, _ = geometry
    qw, nk = tables.shape[-2:]
    ctx_len = qkv_ctx.shape[1]
    band_tbl = jnp.asarray(band0s)
    type_tbl = jnp.asarray(types)
    kv_view = qkv.reshape(bn * rows, GRID_W, 3 * d)

    spb = _tile(n_blk, NA_STEP_BLOCKS)
    n_steps = n_blk // spb

    def kv_spec(blk):
        return pl.BlockSpec((pl.Element(nb), pl.Element(GRID_W), pl.Element(2 * d)),
                            lambda b, i, band, typ: (b * rows + band[i * spb + blk], 0, QKV_KV_BLOCK * 2 * d))

    def tbl_spec(blk):
        return pl.BlockSpec((None, None, NA_HEADS, qw, nk),
                            lambda b, i, band, typ: (layer, typ[i * spb + blk], 0, 0, 0))

    in_specs = ([pl.BlockSpec((spb * qw, d), lambda b, i, band, typ: (b * n_steps + i, QKV_Q_BLOCK))]
                + [kv_spec(blk) for blk in range(spb)]
                + [pl.BlockSpec((None, ctx_len, 2 * d), lambda b, i, band, typ: (b, 0, QKV_KV_BLOCK))]
                + [tbl_spec(blk) for blk in range(spb)])
    vmem = (2 * (spb * (2 * qw * d * 2 + 2 * nk * d * 2 + NA_HEADS * qw * nk * 4) + 2 * ctx_len * d * 2)
            + 2 * VMEM_TEMP_BYTES)
    grid_spec = pltpu.PrefetchScalarGridSpec(
        num_scalar_prefetch=2,
        grid=(bn, n_steps),
        in_specs=in_specs,
        out_specs=pl.BlockSpec((spb * qw, d), lambda b, i, band, typ: (b * n_steps + i, 0)),
    )
    return pl.pallas_call(
        functools.partial(_na_kernel, nb=nb, heads=NA_HEADS, head_dim=head_dim, step_blocks=spb),
        grid_spec=grid_spec,
        out_shape=jax.ShapeDtypeStruct((bn * rows * GRID_W, d), BF16),
        compiler_params=_compiler_params(("parallel", "arbitrary"), vmem),
        name="na_attention",
    )(band_tbl, type_tbl, qkv, *([kv_view] * spb), qkv_ctx, *([tables] * spb))


def _ctx_attn_kernel(q_ref, k_ref, v_ref, o_ref, *, heads, head_dim):
    for h in range(heads):
        sl = slice(h * head_dim, (h + 1) * head_dim)
        s = lax.dot_general(q_ref[:, sl], k_ref[:, sl], _NT_DIMS, preferred_element_type=F32)
        m = jnp.max(s, axis=-1, keepdims=True)
        p = jnp.exp2(s - m)
        l = jnp.sum(p, axis=-1, keepdims=True)
        o = jnp.dot(p.astype(BF16), v_ref[:, sl], preferred_element_type=F32)
        o_ref[:, sl] = (o * (1.0 / l)).astype(o_ref.dtype)


def _ctx_attention(qkv_ctx, d, head_dim):
    bn, ctx_len, _ = qkv_ctx.shape
    spec = lambda col: pl.BlockSpec((None, ctx_len, d), lambda b: (b, 0, col))
    return pl.pallas_call(
        functools.partial(_ctx_attn_kernel, heads=NA_HEADS, head_dim=head_dim),
        grid=(bn,),
        in_specs=[spec(QKV_Q_BLOCK), spec(2 * QKV_KV_BLOCK), spec(2 * QKV_KV_BLOCK + 1)],
        out_specs=pl.BlockSpec((None, ctx_len, d), lambda b: (b, 0, 0)),
        out_shape=jax.ShapeDtypeStruct((bn, ctx_len, d), BF16),
        compiler_params=_compiler_params(("arbitrary",), 8 * ctx_len * d * 2 + 2 * VMEM_TEMP_BYTES),
        name="ctx_attention",
    )(qkv_ctx, qkv_ctx, qkv_ctx)


HALO = V7X_SUBLANES_F32


def _shift_rows(u, tm):
    n = u.shape[0]
    prev = pltpu.roll(u, 1, 0)[HALO:HALO + tm]
    nxt = pltpu.roll(u, n - 1, 0)[HALO:HALO + tm]
    return prev, nxt


def _conv3(u, cw, rows, seq_edges):
    prev, nxt = _shift_rows(u, rows)
    if seq_edges:
        row = lax.broadcasted_iota(jnp.int32, (rows, 1), 0)
        prev = jnp.where(row == 0, 0.0, prev)
        nxt = jnp.where(row == rows - 1, 0.0, nxt)
    return prev * cw[0:1] + u[HALO:HALO + rows] * cw[1:2] + nxt * cw[2:3]


def _gated_conv_kernel(*refs, kind, tiles_per_seq, mod_rows, sub_rows, seq_edges, side_cast):
    parts = 2 if kind == "ffn" else 3
    x_ref, xp_ref, xn_ref, g_ref, mod_ref = refs[:5]
    w1_refs = refs[5:5 + parts]
    conv_ref = refs[5 + parts]
    rest = refs[6 + parts:]
    tc = w1_refs[0].shape[1]
    col = lambda p: slice(p * tc, (p + 1) * tc)
    if side_cast is None:
        w2_ref, o_ref, hs_ref = rest
    else:
        w2_ref, up_f32_ref, down_f32_ref, o_ref, up_bf16_ref, down_bf16_ref, hs_ref = rest
    i = pl.program_id(0)
    j = pl.program_id(1)
    tm = x_ref.shape[0]
    r_shift, r_scale, r_gate = mod_rows

    if side_cast is not None:
        n_up, n_down, down_rows = side_cast
        step = i * pl.num_programs(1) + j

        @pl.when(step < n_up)
        def _():
            _cast_block(up_f32_ref, up_bf16_ref, 0, None)

        @pl.when((step >= n_up) & (step < n_up + n_down))
        def _():
            _cast_block(down_f32_ref, down_bf16_ref, (step - n_up) * down_f32_ref.shape[0], down_rows)

    res_gate = mod_ref[r_gate:r_gate + 1, :]
    n_sub = tm // sub_rows

    def first_matmuls(r):
        lhs = hs_ref[r * sub_rows:r * sub_rows + sub_rows + 2 * HALO, :]
        return [jnp.dot(lhs, w[...], preferred_element_type=F32) for w in w1_refs]

    def mid_and_second_matmul(r, u):
        if kind == "ffn":
            gate, up = [_conv3(u[p], conv_ref[0:3, col(p)], sub_rows, seq_edges) + conv_ref[3:4, col(p)]
                        for p in range(2)]
            a = gate * jax.nn.sigmoid(gate) * up
        else:
            a = u[0][HALO:HALO + sub_rows] * _conv3(u[1] * u[2], conv_ref[0:3, :], sub_rows, seq_edges)
        o_ref[r * sub_rows:(r + 1) * sub_rows, :] += res_gate * jnp.dot(a.astype(BF16), w2_ref[...],
                                                                       preferred_element_type=F32)

    @pl.when(j == 0)
    def _():
        gain = g_ref[...] * (1.0 + mod_ref[r_scale:r_scale + 1, :])
        shift = mod_ref[r_shift:r_shift + 1, :]
        pos = i % tiles_per_seq
        sources = [(xp_ref, pos == 0), (x_ref, None), (xn_ref, pos == tiles_per_seq - 1)]
        total = tm + 2 * HALO

        def norm_and_first_matmuls(r):
            lo = 0 if r == 0 else r * sub_rows + 2 * HALO
            hi = total if r == n_sub - 1 else (r + 1) * sub_rows + 2 * HALO
            _store_norm_mod(hs_ref, sources, gain, shift, copy_ref=o_ref, row_range=(lo, hi))
            return first_matmuls(r)

        _skewed_subtiles(n_sub, norm_and_first_matmuls, mid_and_second_matmul)

    @pl.when(j > 0)
    def _():
        _skewed_subtiles(n_sub, first_matmuls, mid_and_second_matmul)


def _gated_conv(kind, x, seq_len, norm_g, mods, layer, row_fn, mod_rows, w1, conv, w2, tc, tm_pref,
                sub_pref=SUB_ROWS, cast_next=None):
    m, d = x.shape
    parts = 2 if kind == "ffn" else 3
    cparts = 2 if kind == "ffn" else 1
    n_chunks = w2.shape[0] // tc
    conv_all, conv_layer = conv
    assert w1.shape == (parts * n_chunks, d, tc) and conv_all.shape[1:] == (n_chunks, CONV_ROWS, cparts * tc)
    if seq_len >= tm_pref:
        tm = _tile(seq_len, tm_pref)
        sub_rows = _tile(tm, sub_pref)
        tiles_per_seq, seq_edges = seq_len // tm, False
    else:
        tm = seq_len * _tile(m // seq_len, tm_pref // seq_len)
        sub_rows = seq_len
        tiles_per_seq, seq_edges = 1, True
    assert sub_rows % V7X_SUBLANES_BF16 == 0
    hb = tm // HALO
    n_hblk = m // HALO
    other = 2 * (d * parts * tc * 2 + tc * d * 2) + (tm + 2 * HALO) * d * 2
    n_steps = (m // tm) * n_chunks
    side_cast, side_specs_in, side_specs_out, side_shapes, side_operands = None, [], [], [], []
    if cast_next is not None:
        w_up, w_down, nl = cast_next
        f, fp = w_down.shape[1], n_chunks * tc
        up_rows = min(t for t in range(V7X_SUBLANES_BF16, d + 1, V7X_SUBLANES_BF16)
                      if d % t == 0 and 2 * (d // t) <= 3 * n_steps // 4)
        n_up = 2 * (d // up_rows)
        down_rows = min(t for t in range(V7X_SUBLANES_BF16, fp + 1, V7X_SUBLANES_BF16)
                        if fp % t == 0 and fp // t <= n_steps - n_up)
        n_down = fp // down_rows
        last_down_block = (f - 1) // down_rows
        up_item = lambda i, j: jnp.minimum(i * n_chunks + j, n_up - 1)
        down_item = lambda i, j: jnp.clip(i * n_chunks + j - n_up, 0, n_down - 1)
        side_specs_in = [
            pl.BlockSpec((None, up_rows, f), lambda i, j: (nl, up_item(i, j) // 2, up_item(i, j) % 2)),
            pl.BlockSpec((None, down_rows, d),
                         lambda i, j: (nl, jnp.minimum(down_item(i, j), last_down_block), 0)),
        ]
        side_specs_out = [
            pl.BlockSpec((n_chunks, up_rows, tc), lambda i, j: (up_item(i, j) % 2, up_item(i, j) // 2, 0)),
            pl.BlockSpec((down_rows, d), lambda i, j: (down_item(i, j), 0)),
        ]
        side_shapes = [jax.ShapeDtypeStruct((2 * n_chunks, d, tc), BF16), jax.ShapeDtypeStruct((fp, d), BF16)]
        side_operands = [w_up, w_down]
        side_cast = (n_up, n_down, f)
        other += 2 * (up_rows * f * 4 + up_rows * fp * 2 + down_rows * d * 6)
    x_bufs, out_bufs = _row_tile_buffers(tm * d * 4, other)
    vmem = (x_bufs + out_bufs) * tm * d * 4 + other + VMEM_TEMP_BYTES
    in_specs = [
        pl.BlockSpec((tm, d), lambda i, j: (i, 0), pipeline_mode=pl.Buffered(x_bufs)),
        pl.BlockSpec((HALO, d), lambda i, j: (jnp.maximum(i * hb - 1, 0), 0)),
        pl.BlockSpec((HALO, d), lambda i, j: (jnp.minimum((i + 1) * hb, n_hblk - 1), 0)),
        pl.BlockSpec((1, d), lambda i, j: (0, 0)),
        _mod_spec(layer, lambda i: row_fn(i, tm), d),
    ]
    in_specs += [pl.BlockSpec((None, d, tc), functools.partial(lambda p, i, j: (p * n_chunks + j, 0, 0), p))
                 for p in range(parts)]
    in_specs += [pl.BlockSpec((None, None, CONV_ROWS, cparts * tc), lambda i, j: (conv_layer, j, 0, 0)),
                 pl.BlockSpec((tc, d), lambda i, j: (j, 0))]
    operands = [x, x, x, norm_g, mods] + [w1] * parts + [conv_all, w2]
    out_spec = pl.BlockSpec((tm, d), lambda i, j: (i, 0), pipeline_mode=pl.Buffered(out_bufs))
    out_shape = jax.ShapeDtypeStruct((m, d), F32)
    result = pl.pallas_call(
        functools.partial(_gated_conv_kernel, kind=kind, tiles_per_seq=tiles_per_seq, mod_rows=mod_rows,
                          sub_rows=sub_rows, seq_edges=seq_edges, side_cast=side_cast),
        grid=(m // tm, n_chunks),
        in_specs=in_specs + side_specs_in,
        out_specs=[out_spec] + side_specs_out if side_cast else out_spec,
        out_shape=[out_shape] + side_shapes if side_cast else out_shape,
        scratch_shapes=[pltpu.VMEM((tm + 2 * HALO, d), BF16)],
        compiler_params=_compiler_params(("arbitrary" if side_cast else "parallel", "arbitrary"), vmem),
        name=kind,
    )(*operands, *side_operands)
    return tuple(result) if side_cast else result


def _gelu_tanh(x):
    return jax.nn.gelu(x, approximate=True)


def _gmlp_kernel(x_ref, g_ref, mod_ref, win_ref, vg_ref, ws_ref, bs_ref, wout_ref, o_ref,
                 hs_ref, v_ref, ssq_ref, *, n_chunks, group_dim, width, sub_rows):
    j = pl.program_id(1)
    tm = x_ref.shape[0]
    tc = win_ref.shape[1]
    n_sub = tm // sub_rows

    def in_matmul(r):
        return jnp.dot(hs_ref[r * sub_rows:(r + 1) * sub_rows, :], win_ref[...], preferred_element_type=F32)

    def keep_v(r, y):
        rows = slice(r * sub_rows, (r + 1) * sub_rows)
        t = _gelu_tanh(y)
        v_ref[j, rows, :] = t.astype(v_ref.dtype)
        ssq_ref[rows, :] += jnp.sum(t * t, axis=-1, keepdims=True)

    @pl.when(j == 0)
    def _():
        gain = g_ref[...] * (1.0 + mod_ref[1:2, :])
        ssq_ref[...] = jnp.zeros_like(ssq_ref)

        def norm_and_matmul(r):
            _store_norm_mod(hs_ref, [(x_ref, None)], gain, mod_ref[0:1, :], copy_ref=o_ref,
                            row_range=(r * sub_rows, (r + 1) * sub_rows))
            return in_matmul(r)

        _skewed_subtiles(n_sub, norm_and_matmul, keep_v)

    @pl.when((j > 0) & (j < n_chunks))
    def _():
        _skewed_subtiles(n_sub, in_matmul, keep_v)

    @pl.when(j >= n_chunks)
    def _():
        jj = j - n_chunks
        vg = vg_ref[...]
        res_gate = mod_ref[2:3, :]
        bs = bs_ref[...]
        gpc = tc // group_dim
        ws = [ws_ref[jj * gpc + gi] for gi in range(gpc)]

        def gate_and_project(r, y):
            rows = slice(r * sub_rows, (r + 1) * sub_rows)
            inv = lax.rsqrt(ssq_ref[rows, :] * (1.0 / width) + EPS)
            vn = (v_ref[jj, rows, :].astype(F32) * inv * vg).astype(BF16)
            cols = []
            for gi in range(gpc):
                csl = slice(gi * group_dim, (gi + 1) * group_dim)
                cols.append(jnp.concatenate(
                    [jnp.dot(ws[gi], vn[c * GM_CHUNK:(c + 1) * GM_CHUNK, csl], preferred_element_type=F32)
                     + bs[:, csl] for c in range(sub_rows // GM_CHUNK)], axis=0))
            sv = jnp.concatenate(cols, axis=1)
            o_ref[rows, :] += res_gate * jnp.dot((_gelu_tanh(y) * sv).astype(BF16), wout_ref[...],
                                                 preferred_element_type=F32)

        _skewed_subtiles(n_sub, in_matmul, gate_and_project)


def _gmlp(x, norm_g, mods, layer, row_fn, w_in, v_g, w_s, b_s_cols, w_out, tm_pref):
    m, d = x.shape
    width = w_out.shape[0]
    group_dim = width // GM_GROUPS
    tc = w_in.shape[2]
    n_chunks = width // tc
    assert w_in.shape[0] == 2 * n_chunks and tc % group_dim == 0
    tm = _tile(m, tm_pref)
    sub_rows = _tile(tm, SUB_ROWS)
    assert sub_rows % GM_CHUNK == 0
    other = (2 * (d * tc * 2 + tc * d * 2 + GM_CHUNK * tc * 4) + tm * d * 2 + tm * width * 2
             + GM_GROUPS * GM_CHUNK * GM_CHUNK * 2 + tm * V7X_LANES * 4)
    x_bufs, out_bufs = _row_tile_buffers(tm * d * 4, other)
    vmem = (x_bufs + out_bufs) * tm * d * 4 + other + VMEM_TEMP_BYTES
    return pl.pallas_call(
        functools.partial(_gmlp_kernel, n_chunks=n_chunks, group_dim=group_dim, width=width,
                          sub_rows=sub_rows),
        grid=(m // tm, 2 * n_chunks),
        in_specs=[
            pl.BlockSpec((tm, d), lambda i, j: (i, 0), pipeline_mode=pl.Buffered(x_bufs)),
            pl.BlockSpec((1, d), lambda i, j: (0, 0)),
            _mod_spec(layer, lambda i: row_fn(i, tm), d),
            pl.BlockSpec((None, d, tc),
                         lambda i, j: (jnp.where(j < n_chunks, j + n_chunks, j - n_chunks), 0, 0)),
            pl.BlockSpec((1, tc), lambda i, j: (0, jnp.maximum(j - n_chunks, 0))),
            pl.BlockSpec((GM_GROUPS, GM_CHUNK, GM_CHUNK), lambda i, j: (0, 0, 0)),
            pl.BlockSpec((GM_CHUNK, tc), lambda i, j: (0, jnp.maximum(j - n_chunks, 0))),
            pl.BlockSpec((tc, d), lambda i, j: (jnp.maximum(j - n_chunks, 0), 0)),
        ],
        out_specs=pl.BlockSpec((tm, d), lambda i, j: (i, 0), pipeline_mode=pl.Buffered(out_bufs)),
        out_shape=jax.ShapeDtypeStruct((m, d), F32),
        scratch_shapes=[pltpu.VMEM((tm, d), BF16), pltpu.VMEM((n_chunks, tm, tc), BF16), pltpu.VMEM((tm, 1), F32)],
        compiler_params=_compiler_params(("parallel", "arbitrary"), vmem),
        name="gmlp",
    )(x, norm_g, mods, w_in, v_g, w_s, b_s_cols, w_out)


CAST_BLOCK_BYTES = 6 << 20


def _cast_kernel(x_ref, o_ref, *, valid_rows):
    _cast_block(x_ref, o_ref, pl.program_id(0) * x_ref.shape[0], valid_rows)


def _cast_block(x_ref, o_ref, row0, valid_rows):
    rows, f = x_ref.shape
    x = x_ref[...]
    if valid_rows is not None:
        row = row0 + lax.broadcasted_iota(jnp.int32, (rows, 1), 0)
        x = jnp.where(row < valid_rows, x, 0.0)
    x = x.astype(o_ref.dtype)
    if len(o_ref.shape) == 2:
        o_ref[...] = x
        return
    n_chunks, _, tc = o_ref.shape
    for c in range(n_chunks):
        width = max(0, min(tc, f - c * tc))
        if width:
            o_ref[c, :, :width] = x[:, c * tc:c * tc + width]
        if width < tc:
            o_ref[c, :, width:] = jnp.zeros((rows, tc - width), o_ref.dtype)


def _cast_weight(w, idx, parts=1, fp=None, col_chunk=None, rows_out=None, row_block=None):
    _, r, c = w.shape
    f = c // parts
    fp = f if fp is None else fp
    rows_out = r if rows_out is None else rows_out
    assert parts == 1 or f % V7X_LANES == 0
    assert col_chunk is not None or (parts == 1 and fp == f)
    padded_rows = rows_out > r
    if row_block is None:
        row_pref = max(V7X_SUBLANES_BF16, CAST_BLOCK_BYTES // (4 * f))
        row_block = max(t for t in range(V7X_SUBLANES_BF16, row_pref + 1, V7X_SUBLANES_BF16)
                        if rows_out % t == 0)
    assert rows_out % row_block == 0 and row_block % V7X_SUBLANES_BF16 == 0
    assert r % row_block == 0 or (padded_rows and rows_out - row_block < r)
    if col_chunk is None:
        out_spec = pl.BlockSpec((row_block, f), lambda i, p: (i, 0))
        out_shape = (rows_out, f)
    else:
        assert fp % col_chunk == 0 and col_chunk % V7X_LANES == 0 and not padded_rows
        n_chunks = fp // col_chunk
        out_spec = pl.BlockSpec((n_chunks, row_block, col_chunk), lambda i, p: (p, i, 0))
        out_shape = (parts * n_chunks, r, col_chunk)
    return pl.pallas_call(
        functools.partial(_cast_kernel, valid_rows=r if padded_rows else None),
        grid=(rows_out // row_block, parts),
        in_specs=[pl.BlockSpec((None, row_block, f), lambda i, p: (idx, i, p))],
        out_specs=out_spec,
        out_shape=jax.ShapeDtypeStruct(out_shape, BF16),
        compiler_params=_compiler_params(("parallel", "arbitrary"),
                                         2 * row_block * (4 * f + 2 * fp) + VMEM_TEMP_BYTES),
        name="cast_weight",
    )(w)


def _pad_halves(a, f, fp):
    pad = lambda h: jnp.pad(h, [(0, 0)] * (h.ndim - 1) + [(0, fp - f)])
    return jnp.concatenate([pad(a[..., :f]), pad(a[..., f:])], axis=-1)


CONV_ROWS = V7X_SUBLANES_F32


def _conv_params(taps, bias, cparts, tc):
    layers, n_taps, width = taps.shape
    rows = [taps, jnp.zeros((layers, 1, width), F32) if bias is None else bias,
            jnp.zeros((layers, CONV_ROWS - n_taps - 1, width), F32)]
    a = jnp.concatenate(rows, axis=1).reshape(layers, CONV_ROWS, cparts, width // (cparts * tc), tc)
    return jnp.transpose(a, (0, 3, 1, 2, 4)).reshape(layers, width // (cparts * tc), CONV_ROWS, cparts * tc)


def _prep_ffn(w_up, w_down, layer, tc, cast_weights):
    f = w_down.shape[1]
    fp = -(-f // tc) * tc
    if cast_weights is None:
        cast_weights = (_cast_weight(w_up, layer, parts=2, fp=fp, col_chunk=tc),
                        _cast_weight(w_down, layer, rows_out=fp))
    return cast_weights


def kernel(x, c, ctx, c_ctx, norm_mix_g, norm_ffn_g, w_ada, b_ada, na_w_qkv, na_q_g, na_k_g, na_rpb, na_w_o,
           gm_w_in, gm_v_g, gm_w_s, gm_b_s, gm_w_out, sc_w_in, sc_conv_w, sc_w_out,
           ffn_w_up, ffn_conv_w, ffn_conv_b, ffn_w_down):
    bn, seq, d = x.shape
    ctx_len = ctx.shape[1]
    depth = w_ada.shape[0]
    head_dim = d // NA_HEADS
    rows = seq // GRID_W
    assert bn + 1 <= COND_ROWS and seq % GRID_W == 0 and ctx_len % GM_CHUNK == 0

    cond = jnp.concatenate([c, c_ctx[None, :], jnp.zeros((COND_ROWS - bn - 1, d), F32)], axis=0)
    mods = _adaln(cond, w_ada, b_ada).reshape(depth, COND_ROWS, N_MOD, d)

    lat_row = lambda i, tm: i // (seq // tm)
    ctx_row = lambda i, tm: bn

    ffn_hidden = ffn_w_down.shape[1]
    ffn_tc = HIDDEN_CHUNK if ffn_hidden >= HIDDEN_CHUNK else V7X_LANES
    sc_tc = _tile(d, HIDDEN_CHUNK)
    gm_width = gm_w_out.shape[1]
    gm_tc = max(gm_width // GM_GROUPS, _tile(gm_width, HIDDEN_CHUNK))
    proj_tn = _tile(d, PROJ_COLS)
    qkv_tn = _tile(d, QKV_COLS)
    lat_tm = ROW_TILE
    ctx_tm = min(ROW_TILE, bn * ctx_len)
    lat_sub = HALO_SUB_ROWS

    xl = x.reshape(bn * seq, d)
    xc = ctx.reshape(bn * ctx_len, d)
    ffn_weights = None
    na_geometry = _na_geometry(rows, min(NA_QROWS, rows))
    na_tables = _na_bias_table(na_rpb * LOG2_E, na_geometry[-1])
    ffn_fp = -(-ffn_hidden // ffn_tc) * ffn_tc
    ffn_conv_all = _conv_params(_pad_halves(ffn_conv_w, ffn_hidden, ffn_fp),
                                _pad_halves(ffn_conv_b[:, None, :], ffn_hidden, ffn_fp), 2, ffn_tc)
    sc_conv_all = _conv_params(sc_conv_w, None, 1, sc_tc)
    for i in range(depth):
        last = i == depth - 1
        mixer, jx = i % N_MIXERS, i // N_MIXERS
        g_mix = norm_mix_g[i][None, :]
        g_ffn = norm_ffn_g[i][None, :]
        if mixer == 0:
            w_qkv = _cast_weight(na_w_qkv, jx, col_chunk=qkv_tn)
            w_o = _cast_weight(na_w_o, jx, col_chunk=proj_tn)
            q_gain = na_q_g[jx] * (head_dim ** -0.5 * LOG2_E)
            head_gain = jnp.concatenate([jnp.tile(q_gain, NA_HEADS), jnp.tile(na_k_g[jx], NA_HEADS),
                                         jnp.ones((d,), F32)])[None, :]
            qkv = _qkv(xl, g_mix, mods, i, lat_row, w_qkv, head_gain, head_dim)
            qkv_c = _qkv(xc, g_mix, mods, i, ctx_row, w_qkv, head_gain, head_dim).reshape(bn, ctx_len, 3 * d)
            att = _na_attention(qkv, qkv_c, na_tables, jx, na_geometry, bn, rows, d, head_dim)
            xl_new = _proj_res(att, w_o, xl, mods, i, lat_row, 2)
            if not last:
                att_c = _ctx_attention(qkv_c, d, head_dim).reshape(bn * ctx_len, d)
                xc = _proj_res(att_c, w_o, xc, mods, i, ctx_row, 2)
            xl = xl_new
        elif mixer == 1:
            w_in = _cast_weight(gm_w_in, jx, col_chunk=gm_tc)
            w_out = _cast_weight(gm_w_out, jx)
            w_s = gm_w_s[jx].astype(BF16)
            width = w_out.shape[0]
            v_g = gm_v_g[jx][None, :]
            b_s_cols = jnp.repeat(gm_b_s[jx].T, width // GM_GROUPS, axis=1)
            xl = _gmlp(xl, g_mix, mods, i, lat_row, w_in, v_g, w_s, b_s_cols, w_out, lat_tm)
            if not last:
                xc = _gmlp(xc, g_mix, mods, i, ctx_row, w_in, v_g, w_s, b_s_cols, w_out, ctx_tm)
        else:
            w1 = _cast_weight(sc_w_in, jx, parts=3, col_chunk=sc_tc)
            conv, w2 = (sc_conv_all, jx), _cast_weight(sc_w_out, jx)
            xl = _gated_conv("sc", xl, seq, g_mix, mods, i, lat_row, (0, 1, 2), w1, conv, w2, sc_tc, lat_tm,
                             sub_pref=lat_sub)
            if not last:
                xc = _gated_conv("sc", xc, ctx_len, g_mix, mods, i, ctx_row, (0, 1, 2), w1, conv, w2, sc_tc,
                                 ctx_tm)
        w1, w2 = _prep_ffn(ffn_w_up, ffn_w_down, i, ffn_tc, ffn_weights)
        conv = (ffn_conv_all, i)
        if last:
            xl = _gated_conv("ffn", xl, seq, g_ffn, mods, i, lat_row, (3, 4, 5), w1, conv, w2, ffn_tc, lat_tm,
                             sub_pref=lat_sub)
        else:
            xl, *ffn_weights = _gated_conv("ffn", xl, seq, g_ffn, mods, i, lat_row, (3, 4, 5), w1, conv, w2,
                                           ffn_tc, lat_tm, sub_pref=lat_sub,
                                           cast_next=(ffn_w_up, ffn_w_down, i + 1))
        if not last:
            xc = _gated_conv("ffn", xc, ctx_len, g_ffn, mods, i, ctx_row, (3, 4, 5), w1, conv, w2, ffn_tc,
                             ctx_tm)
    return xl.reshape(bn, seq, d)
```
